```python
import math
import jax, jax.numpy as jnp
from jax import lax
import numpy as np

D_MODEL = 1024
BATCH = 2
SEQ = 16384
DEPTH = 2

GRID_W = 64
CTX_LEN = 256
ROPE_THETA = 10000.0
Q_BLOCK = 128
EPS = 1e-6
HALF_WIDTH = D_MODEL // 2

A_HEAD_DIM = 64
A_HEADS = HALF_WIDTH // A_HEAD_DIM
A_KV_HEADS = 2
A_WIDTH = A_HEADS * A_HEAD_DIM
A_KV_WIDTH = A_KV_HEADS * A_HEAD_DIM
B_HEAD_DIM = 128
B_HEADS = HALF_WIDTH // B_HEAD_DIM
B_WIDTH = B_HEADS * B_HEAD_DIM
B_CONV = 3
B_CHUNK = 128
IN_EVEN = A_WIDTH + 2 * A_KV_WIDTH + 4 * B_WIDTH + 4 * B_HEADS
MIX_EVEN = A_WIDTH + B_WIDTH
C_HEAD_DIM = 64
C_V_DIM = 2 * C_HEAD_DIM
C_HEADS = HALF_WIDTH // C_V_DIM
C_WIDTH = C_HEADS * C_V_DIM
D_HEADS = 4
D_Q_LORA = 256
D_KV_LORA = 128
D_NOPE = 64
D_ROPE = 32
D_V_DIM = HALF_WIDTH // D_HEADS
D_WIDTH = D_HEADS * D_V_DIM
IN_ODD = 3 * C_WIDTH + D_Q_LORA + D_KV_LORA + D_ROPE
MIX_ODD = C_WIDTH + D_WIDTH
N_EXPERTS = 32
N_GROUPS = 4
TOP_K = 2
D_EXPERT = 512
MOE_BLOCK = 128

kernel_name = "hybrid_gqa_mlstm_diffattn_mla_moe_prefix_dit"


def rms_norm(x, g):
    xf = x.astype(jnp.float32)
    y = xf * lax.rsqrt(jnp.mean(xf * xf, axis=-1, keepdims=True) + EPS)
    return (y * g.astype(jnp.float32)).astype(x.dtype)


def modulate(x, g, shift, scale):
    return rms_norm(x, g) * (1.0 + scale) + shift


def split_cols(z, sizes):
    bounds = [int(b) for b in np.cumsum(sizes)[:-1]]
    return jnp.split(z, bounds, axis=-1)


def axial_rope_tables(n_tokens, rot_dim):
    rows = n_tokens // GRID_W
    row = jnp.repeat(jnp.arange(rows), GRID_W)
    col = jnp.tile(jnp.arange(GRID_W), rows)
    n_freq = rot_dim // 4
    inv = ROPE_THETA ** (-jnp.arange(n_freq, dtype=jnp.float32) / n_freq)
    ang = jnp.concatenate([row[:, None] * inv, col[:, None] * inv], axis=-1)
    return jnp.cos(ang), jnp.sin(ang)


def apply_rope(x, cos, sin):
    xp = x.reshape(*x.shape[:-1], -1, 2)
    x0, x1 = xp[..., 0], xp[..., 1]
    c = cos[:, None, :].astype(x.dtype)
    s = sin[:, None, :].astype(x.dtype)
    return jnp.stack([x0 * c - x1 * s, x0 * s + x1 * c], axis=-1).reshape(x.shape)


def q_layout(q, n_kv):
    b, n, h, d = q.shape
    return q.reshape(b, n, n_kv, h // n_kv, d).transpose(0, 2, 3, 1, 4)


def kv_layout(t):
    return t.transpose(0, 2, 1, 3)


def merge_heads(o):
    b, hk, g, n, d = o.shape
    return o.transpose(0, 3, 1, 2, 4).reshape(b, n, hk * g * d)


def blocked_attention(q, k, v, scale):
    b, hk, g, sq, d = q.shape
    nb = sq // Q_BLOCK
    qb = jnp.moveaxis(q.reshape(b, hk, g, nb, Q_BLOCK, d), 3, 0)

    def one(qi):
        s = jnp.einsum('bhgqd,bhkd->bhgqk', qi, k).astype(jnp.float32) * scale
        p = jax.nn.softmax(s, axis=-1)
        return jnp.einsum('bhgqk,bhkd->bhgqd', p.astype(v.dtype), v)

    out = lax.map(one, qb)
    return jnp.moveaxis(out, 0, 3).reshape(b, hk, g, sq, v.shape[-1])


def blocked_diff_attention(q1, q2, k1, k2, v, lam, scale):
    b, h, sq, d = q1.shape
    nb = sq // Q_BLOCK
    q1b = jnp.moveaxis(q1.reshape(b, h, nb, Q_BLOCK, d), 2, 0)
    q2b = jnp.moveaxis(q2.reshape(b, h, nb, Q_BLOCK, d), 2, 0)

    def one(qs):
        qa, qb = qs
        p1 = jax.nn.softmax(jnp.einsum('bhqd,bhkd->bhqk', qa, k1).astype(jnp.float32) * scale, axis=-1)
        p2 = jax.nn.softmax(jnp.einsum('bhqd,bhkd->bhqk', qb, k2).astype(jnp.float32) * scale, axis=-1)
        return jnp.einsum('bhqk,bhkd->bhqd', (p1 - lam * p2).astype(v.dtype), v)

    out = lax.map(one, (q1b, q2b))
    return jnp.moveaxis(out, 0, 2).reshape(b, h, sq, v.shape[-1])


def centred_conv(x, w, bias):
    n = x.shape[1]
    pad = B_CONV // 2
    xp = jnp.pad(x, ((0, 0), (pad, pad), (0, 0)))
    return sum(xp[:, j:j + n] * w[j] for j in range(B_CONV)) + bias


def mlstm_chunkwise(q, k, v, i_pre, log_f, state):
    b, h, L, d = q.shape
    nc = L // B_CHUNK

    def chunks(t):
        return jnp.moveaxis(t.reshape(b, h, nc, B_CHUNK, *t.shape[3:]), 2, 0)

    tril = jnp.tril(jnp.ones((B_CHUNK, B_CHUNK), dtype=bool))

    def step(carry, inp):
        C, n, m = carry
        qc, kc, vc, ic, fc = inp
        bcum = jnp.cumsum(fc, axis=-1)
        log_w = jnp.where(tril, bcum[..., :, None] - bcum[..., None, :] + ic[..., None, :], -jnp.inf)
        log_inter = bcum + m[..., None]
        m_row = jnp.maximum(log_inter, jnp.max(log_w, axis=-1))
        w_intra = jnp.exp(log_w - m_row[..., None])
        a_inter = jnp.exp(log_inter - m_row)
        s = jnp.einsum('bhjd,bhsd->bhjs', qc, kc) * w_intra
        num = a_inter[..., None] * jnp.einsum('bhjd,bhde->bhje', qc, C) + jnp.einsum('bhjs,bhse->bhje', s, vc)
        den = a_inter * jnp.einsum('bhjd,bhd->bhj', qc, n) + jnp.sum(s, axis=-1)
        hc = num / jnp.maximum(jnp.abs(den), jnp.exp(-m_row))[..., None]
        b_last = bcum[..., -1]
        log_g = b_last[..., None] - bcum + ic
        m_new = jnp.maximum(b_last + m, jnp.max(log_g, axis=-1))
        g = jnp.exp(log_g - m_new[..., None])
        decay = jnp.exp(b_last + m - m_new)
        C_new = decay[..., None, None] * C + jnp.einsum('bhsd,bhse->bhde', kc * g[..., None], vc)
        n_new = decay[..., None] * n + jnp.einsum('bhs,bhsd->bhd', g, kc)
        return (C_new, n_new, m_new), hc

    state, hs = lax.scan(step, state, (chunks(q), chunks(k), chunks(v), chunks(i_pre), chunks(log_f)))
    return jnp.moveaxis(hs, 0, 2).reshape(b, h, L, d), state


def mlstm_bidirectional(ctx_in, lat_in):
    q_c, k_c, v_c, if_c, ff_c, ib_c, fb_c = ctx_in
    q_l, k_l, v_l, if_l, ff_l, ib_l, fb_l = lat_in
    bsz, nh, _, dh = q_c.shape
    zero = (jnp.zeros((bsz, nh, dh, dh), jnp.float32), jnp.zeros((bsz, nh, dh), jnp.float32),
            jnp.zeros((bsz, nh), jnp.float32))
    flip = lambda t: jnp.flip(t, axis=2)
    hf_c, st_f = mlstm_chunkwise(q_c, k_c, v_c, if_c, ff_c, zero)
    hf_l, _ = mlstm_chunkwise(q_l, k_l, v_l, if_l, ff_l, st_f)
    hb_c, st_b = mlstm_chunkwise(flip(q_c), flip(k_c), flip(v_c), flip(ib_c), flip(fb_c), zero)
    hb_l, _ = mlstm_chunkwise(flip(q_l), flip(k_l), flip(v_l), flip(ib_l), flip(fb_l), st_b)
    return hf_c + flip(hb_c), hf_l + flip(hb_l)


def even_mixer(h_lat, h_ctx, need_ctx, w_in, w_out, g_q, g_k, conv_w, conv_b, b_gates, g_h, rope):
    sizes = (A_WIDTH, A_KV_WIDTH, A_KV_WIDTH, 2 * B_WIDTH, B_WIDTH, B_WIDTH, 4 * B_HEADS)

    def project(h):
        bsz, n = h.shape[:2]
        qa, ka, va, qk_b, vb, ob, gates = split_cols(h @ w_in, sizes)
        qk_b = jax.nn.silu(centred_conv(qk_b, conv_w, conv_b))
        qb, kb = jnp.split(qk_b, 2, axis=-1)
        heads = lambda t: t.reshape(bsz, n, B_HEADS, B_HEAD_DIM).transpose(0, 2, 1, 3).astype(jnp.float32)
        gates = (gates + b_gates).astype(jnp.float32).reshape(bsz, n, 4, B_HEADS).transpose(2, 0, 3, 1)
        i_f, f_f, i_b, f_b = gates
        m_in = (heads(qb), heads(kb) * (B_HEAD_DIM ** -0.5), heads(vb),
                i_f, jax.nn.log_sigmoid(f_f), i_b, jax.nn.log_sigmoid(f_b))
        ka = rms_norm(ka.reshape(bsz, n, A_KV_HEADS, A_HEAD_DIM), g_k)
        va = va.reshape(bsz, n, A_KV_HEADS, A_HEAD_DIM)
        return qa, ka, va, m_in, ob

    def gqa_q(qa):
        bsz, n = qa.shape[:2]
        return rms_norm(qa.reshape(bsz, n, A_HEADS, A_HEAD_DIM), g_q)

    def finish_b(hb, ob):
        bsz, _, n, _ = hb.shape
        hn = rms_norm(hb.transpose(0, 2, 1, 3), g_h.reshape(B_HEADS, B_HEAD_DIM))
        return hn.reshape(bsz, n, B_WIDTH).astype(ob.dtype) * jax.nn.sigmoid(ob)

    scale = A_HEAD_DIM ** -0.5
    cos, sin = rope
    qa_l, ka_l, va_l, m_l, ob_l = project(h_lat)
    qa_c, ka_c, va_c, m_c, ob_c = project(h_ctx)
    q_l = apply_rope(gqa_q(qa_l), cos, sin)
    k_l = apply_rope(ka_l, cos, sin)
    k_all = jnp.concatenate([kv_layout(ka_c), kv_layout(k_l)], axis=2)
    v_all = jnp.concatenate([kv_layout(va_c), kv_layout(va_l)], axis=2)
    oa_l = merge_heads(blocked_attention(q_layout(q_l, A_KV_HEADS), k_all, v_all, scale))
    hb_c, hb_l = mlstm_bidirectional(m_c, m_l)
    out_l = jnp.concatenate([oa_l, finish_b(hb_l, ob_l)], axis=-1) @ w_out
    out_c = None
    if need_ctx:
        oa_c = merge_heads(blocked_attention(q_layout(gqa_q(qa_c), A_KV_HEADS),
                                             kv_layout(ka_c), kv_layout(va_c), scale))
        out_c = jnp.concatenate([oa_c, finish_b(hb_c, ob_c)], axis=-1) @ w_out
    return out_l, out_c


def odd_mixer(h_lat, h_ctx, need_ctx, lam_init, w_in, w_out, g_qc, g_kc, lam_p, g_sub,
              g_cq, w_uq, g_ckv, w_ukv, g_qd, g_kd, rope_c, rope_d):
    sizes = (C_WIDTH, C_WIDTH, C_WIDTH, D_Q_LORA, D_KV_LORA, D_ROPE)
    c_scale = C_HEAD_DIM ** -0.5
    d_scale = (D_NOPE + D_ROPE) ** -0.5

    def keys_values(zp):
        _, kc, vc, _, ckv, kr = zp
        bsz, n = kc.shape[:2]
        kc = rms_norm(kc.reshape(bsz, n, 2 * C_HEADS, C_HEAD_DIM), g_kc)
        vc = vc.reshape(bsz, n, C_HEADS, C_V_DIM)
        kv = (rms_norm(ckv, g_ckv) @ w_ukv).reshape(bsz, n, D_HEADS, D_NOPE + D_V_DIM)
        k_rope = jnp.broadcast_to(kr[:, :, None, :], (bsz, n, D_HEADS, D_ROPE))
        kd = rms_norm(jnp.concatenate([kv[..., :D_NOPE], k_rope], axis=-1), g_kd)
        return kc, vc, kd, kv[..., D_NOPE:]

    def queries(zp):
        qc, _, _, cq, _, _ = zp
        bsz, n = qc.shape[:2]
        qc = rms_norm(qc.reshape(bsz, n, 2 * C_HEADS, C_HEAD_DIM), g_qc)
        qd = rms_norm((rms_norm(cq, g_cq) @ w_uq).reshape(bsz, n, D_HEADS, D_NOPE + D_ROPE), g_qd)
        return qc, qd

    def diff_heads(t):
        bsz, n, _, d = t.shape
        t = t.reshape(bsz, n, C_HEADS, 2, d).transpose(0, 2, 3, 1, 4)
        return t[:, :, 0], t[:, :, 1]

    lq1, lk1, lq2, lk2 = lam_p.astype(jnp.float32)
    lam = jnp.exp(jnp.sum(lq1 * lk1)) - jnp.exp(jnp.sum(lq2 * lk2)) + lam_init

    def diff_attend(qc, k1, k2, v):
        q1, q2 = diff_heads(qc)
        o = blocked_diff_attention(q1, q2, k1, k2, v, lam, c_scale).transpose(0, 2, 1, 3)
        bsz, n = o.shape[:2]
        return (rms_norm(o, g_sub) * (1.0 - lam_init)).reshape(bsz, n, C_WIDTH)

    def rope_tail(t):
        return jnp.concatenate([t[..., :D_NOPE], apply_rope(t[..., D_NOPE:], *rope_d)], axis=-1)

    z_l = split_cols(h_lat @ w_in, sizes)
    z_c = split_cols(h_ctx @ w_in, sizes)
    kc_c, vc_c, kd_c, vd_c = keys_values(z_c)
    kc_l, vc_l, kd_l, vd_l = keys_values(z_l)
    qc_l, qd_l = queries(z_l)
    qc_l = apply_rope(qc_l, *rope_c)
    kc_l = apply_rope(kc_l, *rope_c)
    qd_l = rope_tail(qd_l)
    kd_l = rope_tail(kd_l)
    k1_c, k2_c = diff_heads(kc_c)
    k1_l, k2_l = diff_heads(kc_l)
    k1_all = jnp.concatenate([k1_c, k1_l], axis=2)
    k2_all = jnp.concatenate([k2_c, k2_l], axis=2)
    vc_all = jnp.concatenate([kv_layout(vc_c), kv_layout(vc_l)], axis=2)
    kd_all = jnp.concatenate([kv_layout(kd_c), kv_layout(kd_l)], axis=2)
    vd_all = jnp.concatenate([kv_layout(vd_c), kv_layout(vd_l)], axis=2)
    oc_l = diff_attend(qc_l, k1_all, k2_all, vc_all)
    od_l = merge_heads(blocked_attention(q_layout(qd_l, D_HEADS), kd_all, vd_all, d_scale))
    out_l = jnp.concatenate([oc_l, od_l], axis=-1) @ w_out
    out_c = None
    if need_ctx:
        qc_c, qd_c = queries(z_c)
        oc_c = diff_attend(qc_c, k1_c, k2_c, kv_layout(vc_c))
        od_c = merge_heads(blocked_attention(q_layout(qd_c, D_HEADS), kv_layout(kd_c), kv_layout(vd_c), d_scale))
        out_c = jnp.concatenate([oc_c, od_c], axis=-1) @ w_out
    return out_l, out_c


def moe(tokens, w_router, b_router, w1, w3, w2):
    T, d = tokens.shape
    per = N_EXPERTS // N_GROUPS
    scores = jax.nn.sigmoid((tokens @ w_router).astype(jnp.float32))
    sel = scores + b_router.astype(jnp.float32)
    grp = lax.top_k(sel.reshape(T, N_GROUPS, per), 2)[0].sum(-1)
    g_idx = jnp.argmax(grp, axis=-1)
    in_group = (jnp.arange(N_EXPERTS) // per)[None, :] == g_idx[:, None]
    _, e_idx = lax.top_k(jnp.where(in_group, sel, -jnp.inf), TOP_K)
    gate = jnp.take_along_axis(scores, e_idx, axis=-1)
    gate = gate / jnp.sum(gate, axis=-1, keepdims=True)
    n_assign = T * TOP_K
    e_flat = e_idx.reshape(-1)
    tok_flat = jnp.repeat(jnp.arange(T, dtype=jnp.int32), TOP_K)
    order = jnp.argsort(e_flat)
    se, st = e_flat[order], tok_flat[order]
    counts = jnp.zeros((N_EXPERTS,), jnp.int32).at[e_flat].add(1)
    start = jnp.cumsum(counts) - counts
    padded = (counts + MOE_BLOCK - 1) // MOE_BLOCK * MOE_BLOCK
    pend = jnp.cumsum(padded)
    pstart = pend - padded
    dest = pstart[se] + jnp.arange(n_assign, dtype=jnp.int32) - start[se]
    n_rows = -(-n_assign // MOE_BLOCK) * MOE_BLOCK + N_EXPERTS * MOE_BLOCK
    nb = n_rows // MOE_BLOCK
    row_tok = jnp.full((n_rows,), T, jnp.int32).at[dest].set(st)
    xin = jnp.concatenate([tokens, jnp.zeros((1, d), tokens.dtype)], axis=0)[row_tok]
    blk_e = jnp.minimum(jnp.searchsorted(pend, jnp.arange(nb, dtype=jnp.int32) * MOE_BLOCK, side='right'),
                        N_EXPERTS - 1)

    def expert_block(args):
        xb, e = args
        return (jax.nn.silu(xb @ w1[e]) * (xb @ w3[e])) @ w2[e]

    y = lax.map(expert_block, (xin.reshape(nb, MOE_BLOCK, d), blk_e)).reshape(n_rows, d)
    dest_assign = jnp.zeros((n_assign,), jnp.int32).at[order].set(dest)
    return jnp.einsum('tk,tkd->td', gate.astype(tokens.dtype), y[dest_assign].reshape(T, TOP_K, d))


def setup_inputs(seed: int = 0) -> dict:
    key = jax.random.key(seed)
    ks = iter(jax.random.split(key, 40))
    nrm = lambda shape, s: jax.random.normal(next(ks), shape, jnp.float32) * s
    gain = lambda shape: 1.0 + nrm(shape, 0.05)
    ne, no = (DEPTH + 1) // 2, DEPTH // 2
    D = D_MODEL
    gate_base = jnp.repeat(jnp.array([0.0, 3.0, 0.0, 3.0], jnp.float32), B_HEADS)
    return {
        "x": nrm((BATCH, SEQ, D), 1.0),
        "c": nrm((BATCH, D), 1.0),
        "ctx": nrm((BATCH, CTX_LEN, D), 1.0),
        "c_ctx": nrm((D,), 1.0),
        "w_ada": nrm((DEPTH, D, 6 * D), 0.5 * D ** -0.5),
        "b_ada": nrm((DEPTH, 6 * D), 0.01),
        "g_mix": gain((DEPTH, D)),
        "g_ffn": gain((DEPTH, D)),
        "e_w_in": nrm((ne, D, IN_EVEN), D ** -0.5),
        "e_w_out": nrm((ne, MIX_EVEN, D), MIX_EVEN ** -0.5),
        "e_g_q": gain((ne, A_HEAD_DIM)),
        "e_g_k": gain((ne, A_HEAD_DIM)),
        "e_conv_w": nrm((ne, B_CONV, 2 * B_WIDTH), B_CONV ** -0.5),
        "e_conv_b": nrm((ne, 2 * B_WIDTH), 0.01),
        "e_b_gates": gate_base + nrm((ne, 4 * B_HEADS), 0.1),
        "e_g_h": gain((ne, B_WIDTH)),
        "o_w_in": nrm((no, D, IN_ODD), D ** -0.5),
        "o_w_out": nrm((no, MIX_ODD, D), MIX_ODD ** -0.5),
        "o_g_qc": gain((no, C_HEAD_DIM)),
        "o_g_kc": gain((no, C_HEAD_DIM)),
        "o_lam": nrm((no, 4, C_HEAD_DIM), 0.1),
        "o_g_sub": gain((no, C_V_DIM)),
        "o_g_cq": gain((no, D_Q_LORA)),
        "o_w_uq": nrm((no, D_Q_LORA, D_HEADS * (D_NOPE + D_ROPE)), D_Q_LORA ** -0.5),
        "o_g_ckv": gain((no, D_KV_LORA)),
        "o_w_ukv": nrm((no, D_KV_LORA, D_HEADS * (D_NOPE + D_V_DIM)), D_KV_LORA ** -0.5),
        "o_g_qd": gain((no, D_NOPE + D_ROPE)),
        "o_g_kd": gain((no, D_NOPE + D_ROPE)),
        "w_router": nrm((D, N_EXPERTS), D ** -0.5),
        "b_router": nrm((N_EXPERTS,), 0.01),
        "w1": nrm((DEPTH, N_EXPERTS, D, D_EXPERT), D ** -0.5),
        "w3": nrm((DEPTH, N_EXPERTS, D, D_EXPERT), D ** -0.5),
        "w2": nrm((DEPTH, N_EXPERTS, D_EXPERT, D), D_EXPERT ** -0.5),
    }


def reference(x, c, ctx, c_ctx, w_ada, b_ada, g_mix, g_ffn,
              e_w_in, e_w_out, e_g_q, e_g_k, e_conv_w, e_conv_b, e_b_gates, e_g_h,
              o_w_in, o_w_out, o_g_qc, o_g_kc, o_lam, o_g_sub, o_g_cq, o_w_uq, o_g_ckv, o_w_ukv,
              o_g_qd, o_g_kd, w_router, b_router, w1, w3, w2):
    bsz, n_lat, d = x.shape
    rope_64 = axial_rope_tables(n_lat, A_HEAD_DIM)
    rope_d = axial_rope_tables(n_lat, D_ROPE)
    x_lat, x_ctx = x, ctx
    for layer in range(DEPTH):
        last = layer == DEPTH - 1
        j = layer // 2
        mod_l = jax.nn.silu(c) @ w_ada[layer] + b_ada[layer]
        mod_c = jax.nn.silu(c_ctx) @ w_ada[layer] + b_ada[layer]
        sh1, sc1, g1, sh2, sc2, g2 = [m[:, None, :] for m in jnp.split(mod_l, 6, axis=-1)]
        sh1c, sc1c, g1c, sh2c, sc2c, g2c = jnp.split(mod_c, 6, axis=-1)
        h_l = modulate(x_lat, g_mix[layer], sh1, sc1)
        h_c = modulate(x_ctx, g_mix[layer], sh1c, sc1c)
        if layer % 2 == 0:
            o_l, o_c = even_mixer(h_l, h_c, not last, e_w_in[j], e_w_out[j], e_g_q[j], e_g_k[j],
                                  e_conv_w[j], e_conv_b[j], e_b_gates[j], e_g_h[j], rope_64)
        else:
            lam_init = 0.8 - 0.6 * math.exp(-0.3 * layer)
            o_l, o_c = odd_mixer(h_l, h_c, not last, lam_init, o_w_in[j], o_w_out[j], o_g_qc[j], o_g_kc[j],
                                 o_lam[j], o_g_sub[j], o_g_cq[j], o_w_uq[j], o_g_ckv[j], o_w_ukv[j],
                                 o_g_qd[j], o_g_kd[j], rope_64, rope_d)
        x_lat = x_lat + g1 * o_l
        f_l = modulate(x_lat, g_ffn[layer], sh2, sc2).reshape(-1, d)
        if last:
            y = moe(f_l, w_router, b_router, w1[layer], w3[layer], w2[layer])
            x_lat = x_lat + g2 * y.reshape(x_lat.shape)
        else:
            x_ctx = x_ctx + g1c * o_c
            f_c = modulate(x_ctx, g_ffn[layer], sh2c, sc2c).reshape(-1, d)
            n_ctx_tok = f_c.shape[0]
            y = moe(jnp.concatenate([f_c, f_l], axis=0), w_router, b_router, w1[layer], w3[layer], w2[layer])
            x_ctx = x_ctx + g2c * y[:n_ctx_tok].reshape(x_ctx.shape)
            x_lat = x_lat + g2 * y[n_ctx_tok:].reshape(x_lat.shape)
    return x_lat
```

```python
import functools
import math

import jax
import jax.numpy as jnp
import numpy as np
from jax import lax
from jax.experimental import pallas as pl
from jax.experimental.pallas import tpu as pltpu

F32 = jnp.float32
BF16 = jnp.bfloat16
EPS = 1e-6
LOG2E = 1.4426950408889634
GRID_W = 64
ROPE_THETA = 10000.0
LAM_INIT_BASE = (0.8, 0.6, 0.3)

A_HEAD_DIM, A_HEADS, A_KV_HEADS = 64, 8, 2
B_HEAD_DIM, B_HEADS, B_CONV, B_CHUNK = 128, 4, 3, 128
C_HEAD_DIM, C_V_DIM, C_HEADS = 64, 128, 4
D_HEADS, D_Q_LORA, D_KV_LORA, D_NOPE, D_ROPE, D_V_DIM = 4, 256, 128, 64, 32, 128
N_EXPERTS, N_GROUPS, TOP_K, D_EXPERT = 32, 4, 2, 512

LANE = 128
ROW_TILE = 256
MOE_ROWS = 256
VMEM_LIMIT = 48 * 1024 * 1024


def _cparams(sem):
    return pltpu.CompilerParams(dimension_semantics=sem, vmem_limit_bytes=VMEM_LIMIT)


def _modmm_kernel(x_ref, mod_ref, g_ref, w_ref, o_ref):
    x = x_ref[0]
    y = x * lax.rsqrt(jnp.mean(x * x, axis=-1, keepdims=True) + EPS) * g_ref[...]
    h = y * (1.0 + mod_ref[0, 1:2, :]) + mod_ref[0, 0:1, :]
    o_ref[0] = jnp.dot(h.astype(BF16), w_ref[...], preferred_element_type=F32).astype(o_ref.dtype)


def _modmm(x, mod, g, w, n_ctx_tiles, out_dtype=F32):
    bsz, t, d = x.shape
    n = w.shape[1]
    tm = min(ROW_TILE, t)
    assert t % tm == 0 and n % LANE == 0
    return pl.pallas_call(
        _modmm_kernel,
        grid=(bsz, t // tm),
        in_specs=[
            pl.BlockSpec((1, tm, d), lambda b, i: (b, i, 0)),
            pl.BlockSpec((1, 2, d), lambda b, i: (jnp.where(i < n_ctx_tiles, bsz, b), 0, 0)),
            pl.BlockSpec((1, d), lambda b, i: (0, 0)),
            pl.BlockSpec((d, n), lambda b, i: (0, 0)),
        ],
        out_specs=pl.BlockSpec((1, tm, n), lambda b, i: (b, i, 0)),
        out_shape=jax.ShapeDtypeStruct((bsz, t, n), out_dtype),
        compiler_params=_cparams(("parallel", "parallel")),
        name="modmm",
    )(x, mod, g.reshape(1, d), w)


def _flash_kernel(q_ref, k_ref, v_ref, o_ref, m_sc, l_sc, acc_sc, *, n_heads, share_k, tq, n_first, tk, n_tiles):
    d = q_ref.shape[-1]
    dv = v_ref.shape[-1]
    rows = n_heads * tq
    m_sc[...] = jnp.full(m_sc.shape, -1e30, F32)
    l_sc[...] = jnp.zeros(l_sc.shape, F32)
    acc_sc[...] = jnp.zeros(acc_sc.shape, F32)

    def tile(start, size):
        if share_k:
            s = jnp.dot(q_ref[0].reshape(rows, d), k_ref[0, 0, :, pl.ds(start, size)],
                        preferred_element_type=F32)
        else:
            s = jnp.concatenate(
                [jnp.dot(q_ref[0, h], k_ref[0, h, :, pl.ds(start, size)], preferred_element_type=F32)
                 for h in range(n_heads)], axis=0)
        m_prev = m_sc[...]
        m_new = jnp.maximum(m_prev, jnp.max(s, axis=1, keepdims=True))
        alpha = jnp.exp2(m_prev - m_new)
        p = jnp.exp2(s - m_new)
        l_sc[...] = alpha * l_sc[...] + jnp.sum(p, axis=1, keepdims=True)
        acc_sc[...] = alpha * acc_sc[...] + jnp.dot(p.astype(BF16), v_ref[0, 0, pl.ds(start, size), :],
                                                    preferred_element_type=F32)
        m_sc[...] = m_new

    tile(0, n_first)
    if n_tiles > 0:
        def body(i, carry):
            tile(pl.multiple_of(n_first + i * tk, LANE), tk)
            return carry
        lax.fori_loop(0, n_tiles, body, 0)
    out = acc_sc[...] * (1.0 / l_sc[...])
    o_ref[0] = out.reshape(n_heads, tq, dv).astype(o_ref.dtype)


def _flash(q, kt, v, *, n_heads, share_k, k_per_v, n_first, tq, tk, out_dtype):
    bsz, hq, sq, d = q.shape
    _, hk, _, t_k = kt.shape
    dv = v.shape[-1]
    tq = min(tq, sq)
    rest = t_k - n_first
    tk = min(tk, rest) if rest > 0 else tk
    assert sq % tq == 0 and hq % n_heads == 0 and (rest == 0 or rest % tk == 0)
    n_tiles = rest // tk if rest > 0 else 0
    hk_blk = 1 if share_k else n_heads
    rows = n_heads * tq
    kern = functools.partial(_flash_kernel, n_heads=n_heads, share_k=share_k, tq=tq, n_first=n_first,
                             tk=tk, n_tiles=n_tiles)
    return pl.pallas_call(
        kern,
        grid=(bsz, hq // n_heads, sq // tq),
        in_specs=[
            pl.BlockSpec((1, n_heads, tq, d), lambda b, g, i: (b, g, i, 0)),
            pl.BlockSpec((1, hk_blk, d, t_k), lambda b, g, i: (b, g, 0, 0)),
            pl.BlockSpec((1, 1, t_k, dv), lambda b, g, i: (b, g // k_per_v, 0, 0)),
        ],
        out_specs=pl.BlockSpec((1, n_heads, tq, dv), lambda b, g, i: (b, g, i, 0)),
        out_shape=jax.ShapeDtypeStruct((bsz, hq, sq, dv), out_dtype),
        scratch_shapes=[pltpu.VMEM((rows, 1), F32), pltpu.VMEM((rows, 1), F32), pltpu.VMEM((rows, dv), F32)],
        compiler_params=_cparams(("parallel", "parallel", "parallel")),
        name="flash",
    )(q, kt, v)


def _mlstm_kernel(q_ref, k_ref, v_ref, g_ref, h_ref, c_sc, m_sc, *, n_seq):
    L = q_ref.shape[1]
    dh = q_ref.shape[2]

    @pl.when(pl.program_id(1) == 0)
    def _():
        c_sc[...] = jnp.zeros(c_sc.shape, F32)
        m_sc[...] = jnp.zeros(m_sc.shape, F32)

    row = lax.broadcasted_iota(jnp.int32, (L, L), 0)
    col = lax.broadcasted_iota(jnp.int32, (L, L), 1)
    tril = col <= row
    upper = (row <= col).astype(F32)
    ones_col = (lax.broadcasted_iota(jnp.int32, (L, dh), 1) == 0).astype(F32)

    for s in range(n_seq):
        q = q_ref[s]
        k = k_ref[s]
        v_ext = jnp.concatenate([v_ref[s], ones_col], axis=1)
        gates = g_ref[s, 0]
        cum = jnp.dot(gates, upper, preferred_element_type=F32, precision=lax.Precision.HIGHEST)
        i_r = gates[0:1, :]
        b_r = cum[1:2, :]
        b_cm = jnp.transpose(jnp.broadcast_to(b_r, (L, L)))
        i_cm = jnp.transpose(jnp.broadcast_to(i_r, (L, L)))
        b_c = b_cm[:, 0:1]
        i_c = i_cm[:, 0:1]
        m_prev = m_sc[s][:, 0:1]
        c_ext = c_sc[s]

        log_w = jnp.where(tril, b_cm - b_r + i_r, -jnp.inf)
        log_inter = b_c + m_prev
        m_row = jnp.maximum(log_inter, jnp.max(log_w, axis=1, keepdims=True))
        w_intra = jnp.exp(log_w - m_row)
        a_inter = jnp.exp(log_inter - m_row)
        sc = lax.dot_general(q.astype(BF16), k.astype(BF16), (((1,), (1,)), ((), ())),
                             preferred_element_type=F32) * w_intra
        q_c = jnp.dot(q.astype(BF16), c_ext.astype(BF16), preferred_element_type=F32)
        s_v = jnp.dot(sc.astype(BF16), v_ext.astype(BF16), preferred_element_type=F32)
        num = a_inter * q_c[:, :dh] + s_v[:, :dh]
        den = a_inter * q_c[:, dh:dh + 1] + s_v[:, dh:dh + 1]
        h_ref[s] = num / jnp.maximum(jnp.abs(den), jnp.exp(-m_row))

        b_last = b_r[:, L - 1:L]
        log_g_r = b_last - b_r + i_r
        m_new = jnp.maximum(b_last + m_prev, jnp.max(log_g_r, axis=1, keepdims=True))
        g_c = jnp.exp(b_last - b_c + i_c - m_new)
        decay = jnp.exp(b_last + m_prev - m_new)
        upd = jnp.dot(jnp.transpose(k).astype(BF16), (g_c * v_ext).astype(BF16), preferred_element_type=F32)
        c_sc[s] = decay * c_ext + upd
        m_sc[s] = jnp.broadcast_to(m_new, (1, LANE))


def _mlstm(q, k, v, gates, n_seq):
    n, t, dh = q.shape
    L = B_CHUNK
    assert n % n_seq == 0 and t % L == 0
    seq_spec = pl.BlockSpec((n_seq, L, dh), lambda g, c: (g, c, 0))
    return pl.pallas_call(
        functools.partial(_mlstm_kernel, n_seq=n_seq),
        grid=(n // n_seq, t // L),
        in_specs=[seq_spec, seq_spec, seq_spec, pl.BlockSpec((n_seq, 1, 8, L), lambda g, c: (g, c, 0, 0))],
        out_specs=seq_spec,
        out_shape=jax.ShapeDtypeStruct((n, t, dh), F32),
        scratch_shapes=[pltpu.VMEM((n_seq, dh, 2 * dh), F32), pltpu.VMEM((n_seq, 1, LANE), F32)],
        compiler_params=_cparams(("parallel", "arbitrary")),
        name="mlstm",
    )(q, k, v, gates)


def _outproj_kernel(mix_ref, x_ref, mod_ref, g_ref, w_ref, wr_ref, xo_ref, f_ref, lg_ref):
    o = jnp.dot(mix_ref[0], w_ref[...], preferred_element_type=F32)
    xn = x_ref[0] + mod_ref[0, 0:1, :] * o
    xo_ref[0] = xn
    y = xn * lax.rsqrt(jnp.mean(xn * xn, axis=-1, keepdims=True) + EPS) * g_ref[...]
    f = y * (1.0 + mod_ref[0, 2:3, :]) + mod_ref[0, 1:2, :]
    f_ref[0] = f.astype(BF16)
    lg_ref[0] = jnp.dot(f, wr_ref[...], preferred_element_type=F32, precision=lax.Precision.HIGHEST)


def _outproj(mix, x, mod, g, w, w_router, n_ctx_tiles):
    bsz, t, d = x.shape
    m = mix.shape[-1]
    ne = w_router.shape[1]
    tm = min(ROW_TILE, t)
    assert t % tm == 0
    row = lambda b, i: (b, i, 0)
    return pl.pallas_call(
        _outproj_kernel,
        grid=(bsz, t // tm),
        in_specs=[
            pl.BlockSpec((1, tm, m), row),
            pl.BlockSpec((1, tm, d), row),
            pl.BlockSpec((1, 3, d), lambda b, i: (jnp.where(i < n_ctx_tiles, bsz, b), 0, 0)),
            pl.BlockSpec((1, d), lambda b, i: (0, 0)),
            pl.BlockSpec((m, d), lambda b, i: (0, 0)),
            pl.BlockSpec((d, ne), lambda b, i: (0, 0)),
        ],
        out_specs=[pl.BlockSpec((1, tm, d), row), pl.BlockSpec((1, tm, d), row), pl.BlockSpec((1, tm, ne), row)],
        out_shape=[jax.ShapeDtypeStruct((bsz, t, d), F32), jax.ShapeDtypeStruct((bsz, t, d), BF16),
                   jax.ShapeDtypeStruct((bsz, t, ne), F32)],
        compiler_params=_cparams(("parallel", "parallel")),
        name="outproj",
    )(mix, x, mod, g.reshape(1, d), w, w_router)


def _expert_kernel(blk_e_ref, n_used_ref, x_ref, w1_ref, w3_ref, w2_ref, y_ref):
    i = pl.program_id(0)

    @pl.when(i < n_used_ref[0])
    def _():
        xb = x_ref[...]
        h1 = jnp.dot(xb, w1_ref[0], preferred_element_type=F32)
        h3 = jnp.dot(xb, w3_ref[0], preferred_element_type=F32)
        a = (h1 * jax.nn.sigmoid(h1) * h3).astype(BF16)
        y_ref[...] = jnp.dot(a, w2_ref[0], preferred_element_type=F32).astype(y_ref.dtype)

    @pl.when(i >= n_used_ref[0])
    def _():
        y_ref[...] = jnp.zeros(y_ref.shape, y_ref.dtype)


def _experts(xin, blk_e, n_used, w1, w3, w2):
    n_rows, d = xin.shape
    de = w1.shape[-1]
    nb = n_rows // MOE_ROWS
    grid_spec = pltpu.PrefetchScalarGridSpec(
        num_scalar_prefetch=2,
        grid=(nb,),
        in_specs=[
            pl.BlockSpec((MOE_ROWS, d), lambda i, be, nu: (i, 0)),
            pl.BlockSpec((1, d, de), lambda i, be, nu: (be[i], 0, 0)),
            pl.BlockSpec((1, d, de), lambda i, be, nu: (be[i], 0, 0)),
            pl.BlockSpec((1, de, d), lambda i, be, nu: (be[i], 0, 0)),
        ],
        out_specs=pl.BlockSpec((MOE_ROWS, d), lambda i, be, nu: (i, 0)),
    )
    return pl.pallas_call(
        _expert_kernel,
        grid_spec=grid_spec,
        out_shape=jax.ShapeDtypeStruct((n_rows, d), BF16),
        compiler_params=_cparams(("arbitrary",)),
        name="experts",
    )(blk_e, n_used, xin, w1, w3, w2)


def _moe(f_tok, logits, b_router, w1, w3, w2):
    n_tok, d = f_tok.shape
    per = N_EXPERTS // N_GROUPS
    scores = jax.nn.sigmoid(logits)
    sel = scores + b_router.astype(F32)
    grp = lax.top_k(sel.reshape(n_tok, N_GROUPS, per), 2)[0].sum(-1)
    g_idx = jnp.argmax(grp, axis=-1)
    in_group = (jnp.arange(N_EXPERTS) // per)[None, :] == g_idx[:, None]
    _, e_idx = lax.top_k(jnp.where(in_group, sel, -jnp.inf), TOP_K)
    gate = jnp.take_along_axis(scores, e_idx, axis=-1)
    gate = gate / jnp.sum(gate, axis=-1, keepdims=True)

    n_assign = n_tok * TOP_K
    e_flat = e_idx.reshape(-1).astype(jnp.int32)
    tok_flat = jnp.repeat(jnp.arange(n_tok, dtype=jnp.int32), TOP_K)
    onehot = (e_flat[:, None] == jnp.arange(N_EXPERTS, dtype=jnp.int32)[None, :]).astype(jnp.int32)
    csum = jnp.cumsum(onehot, axis=0)
    counts = csum[-1]
    rank = jnp.take_along_axis(csum, e_flat[:, None], axis=1)[:, 0] - 1
    padded = (counts + MOE_ROWS - 1) // MOE_ROWS * MOE_ROWS
    pend = jnp.cumsum(padded)
    pstart = pend - padded
    dest = pstart[e_flat] + rank
    n_rows = -(-n_assign // MOE_ROWS) * MOE_ROWS + N_EXPERTS * MOE_ROWS
    nb = n_rows // MOE_ROWS
    row_tok = jnp.full((n_rows,), n_tok, jnp.int32).at[dest].set(tok_flat)
    xin = jnp.concatenate([f_tok, jnp.zeros((1, d), f_tok.dtype)], axis=0)[row_tok]
    blk_e = jnp.minimum(jnp.searchsorted(pend, jnp.arange(nb, dtype=jnp.int32) * MOE_ROWS, side='right'),
                        N_EXPERTS - 1).astype(jnp.int32)
    n_used = (pend[-1:] // MOE_ROWS).astype(jnp.int32)
    y = _experts(xin, blk_e, n_used, w1, w3, w2)
    yk = y[dest].reshape(n_tok, TOP_K, d).astype(F32)
    return jnp.einsum('tk,tkd->td', gate, yk)


def _rms(x, g):
    return x * lax.rsqrt(jnp.mean(x * x, axis=-1, keepdims=True) + EPS) * g


def _rope_tables(n_lat, n_ctx, rot_dim):
    rows = n_lat // GRID_W
    r = jnp.repeat(jnp.arange(rows), GRID_W)
    c = jnp.tile(jnp.arange(GRID_W), rows)
    n_freq = rot_dim // 4
    inv = ROPE_THETA ** (-jnp.arange(n_freq, dtype=F32) / n_freq)
    ang = jnp.concatenate([r[:, None] * inv, c[:, None] * inv], axis=-1)
    cos = jnp.concatenate([jnp.ones((n_ctx, rot_dim // 2), F32), jnp.cos(ang)], axis=0)
    sin = jnp.concatenate([jnp.zeros((n_ctx, rot_dim // 2), F32), jnp.sin(ang)], axis=0)
    return cos, sin


def _rope(x, cos, sin):
    xp = x.reshape(*x.shape[:-1], -1, 2)
    x0, x1 = xp[..., 0], xp[..., 1]
    c = cos[:, None, :]
    s = sin[:, None, :]
    return jnp.stack([x0 * c - x1 * s, x0 * s + x1 * c], axis=-1).reshape(x.shape)


def _pad_cols(w, n):
    return jnp.pad(w, ((0, 0), (0, n - w.shape[1])))


def _round_up(n, m):
    return -(-n // m) * m


def _conv_silu(x, w, bias):
    n = x.shape[1]
    pad = B_CONV // 2
    xp = jnp.pad(x, ((0, 0), (pad, pad), (0, 0)))
    return jax.nn.silu(sum(xp[:, j:j + n] * w[j] for j in range(B_CONV)) + bias)


def _flip_parts(t, n_ctx):
    return jnp.concatenate([jnp.flip(t[:, :n_ctx], axis=1), jnp.flip(t[:, n_ctx:], axis=1)], axis=1)


def _even_mixer(x_all, mod, g_mix, n_ctx, w_in, g_q, g_k, conv_w, conv_b, b_gates, g_h, rope):
    bsz, t, _ = x_all.shape
    n_in = w_in.shape[1]
    z = _modmm(x_all, mod, g_mix, _pad_cols(w_in, _round_up(n_in, LANE)).astype(BF16), n_ctx // min(ROW_TILE, t))
    sizes = (512, 128, 128, 1024, 512, 512, 16)
    offs = np.cumsum((0,) + sizes)
    qa, ka, va, qk_b, vb, ob, gates = [z[..., offs[i]:offs[i + 1]] for i in range(7)]
    cos, sin = rope

    q = _rope(_rms(qa.reshape(bsz, t, A_HEADS, A_HEAD_DIM), g_q), cos, sin) * (A_HEAD_DIM ** -0.5 * LOG2E)
    k = _rope(_rms(ka.reshape(bsz, t, A_KV_HEADS, A_HEAD_DIM), g_k), cos, sin)
    q = q.transpose(0, 2, 1, 3).astype(BF16)
    kt = k.transpose(0, 2, 3, 1).astype(BF16)
    v = va.reshape(bsz, t, A_KV_HEADS, A_HEAD_DIM).transpose(0, 2, 1, 3).astype(BF16)
    grp = A_HEADS // A_KV_HEADS
    o_lat = _flash(q[:, :, n_ctx:], kt, v, n_heads=grp, share_k=True, k_per_v=1, n_first=n_ctx,
                   tq=256, tk=512, out_dtype=BF16)
    o_ctx = _flash(q[:, :, :n_ctx], kt[..., :n_ctx], v[:, :, :n_ctx], n_heads=grp, share_k=True, k_per_v=1,
                   n_first=n_ctx, tq=256, tk=512, out_dtype=BF16)
    oa = jnp.concatenate([o_ctx, o_lat], axis=2).transpose(0, 2, 1, 3).reshape(bsz, t, A_HEADS * A_HEAD_DIM)

    qk_b = jnp.concatenate([_conv_silu(qk_b[:, :n_ctx], conv_w, conv_b),
                            _conv_silu(qk_b[:, n_ctx:], conv_w, conv_b)], axis=1)
    qb, kb = jnp.split(qk_b, 2, axis=-1)
    kb = kb * (B_HEAD_DIM ** -0.5)
    gates = (gates + b_gates).reshape(bsz, t, 4, B_HEADS)
    i_f, f_f, i_b, f_b = [gates[:, :, j] for j in range(4)]
    heads = lambda a: a.reshape(bsz, t, B_HEADS, B_HEAD_DIM)
    both = lambda a: jnp.concatenate([a, _flip_parts(a, n_ctx)], axis=0)
    seq = lambda a: both(heads(a)).transpose(0, 2, 1, 3).reshape(2 * bsz * B_HEADS, t, B_HEAD_DIM)
    gi = jnp.concatenate([i_f, _flip_parts(i_b, n_ctx)], axis=0)
    gf = jax.nn.log_sigmoid(jnp.concatenate([f_f, _flip_parts(f_b, n_ctx)], axis=0))
    n_seqs = 2 * bsz * B_HEADS
    g8 = jnp.stack([gi, gf], axis=0).transpose(1, 3, 0, 2)
    g8 = g8.reshape(n_seqs, 2, t // B_CHUNK, B_CHUNK).transpose(0, 2, 1, 3)
    g8 = jnp.pad(g8, ((0, 0), (0, 0), (0, 6), (0, 0)))
    h = _mlstm(seq(qb), seq(kb), seq(vb), g8, n_seq=4)
    h = h.reshape(2, bsz, B_HEADS, t, B_HEAD_DIM)
    hb = h[0] + _flip_parts(h[1].transpose(0, 2, 1, 3), n_ctx).transpose(0, 2, 1, 3)
    hn = _rms(hb.transpose(0, 2, 1, 3), g_h.reshape(B_HEADS, B_HEAD_DIM)).reshape(bsz, t, B_HEADS * B_HEAD_DIM)
    ob_out = hn * jax.nn.sigmoid(ob)
    return jnp.concatenate([oa, ob_out.astype(BF16)], axis=-1)


def _odd_mixer(x_all, mod, g_mix, n_ctx, lam_init, w_in, g_qc, g_kc, lam_p, g_sub, g_cq, w_uq, g_ckv, w_ukv,
               g_qd, g_kd, rope_c, rope_d):
    bsz, t, _ = x_all.shape
    n_lat = t - n_ctx
    n_in = w_in.shape[1]
    z = _modmm(x_all, mod, g_mix, _pad_cols(w_in, _round_up(n_in, LANE)).astype(BF16), n_ctx // min(ROW_TILE, t))
    sizes = (512, 512, 512, D_Q_LORA, D_KV_LORA, D_ROPE)
    offs = np.cumsum((0,) + sizes)
    qc, kc, vc, cq, ckv, kr = [z[..., offs[i]:offs[i + 1]] for i in range(6)]
    cos_c, sin_c = rope_c
    cos_d, sin_d = rope_d
    c_scale = C_HEAD_DIM ** -0.5
    d_scale = (D_NOPE + D_ROPE) ** -0.5
    zero_mod = jnp.zeros((bsz + 1, 2, 1), F32)

    lq1, lk1, lq2, lk2 = lam_p.astype(F32)
    lam = jnp.exp(jnp.sum(lq1 * lk1)) - jnp.exp(jnp.sum(lq2 * lk2)) + lam_init

    q = _rope(_rms(qc.reshape(bsz, t, 2 * C_HEADS, C_HEAD_DIM), g_qc), cos_c, sin_c)[:, n_ctx:] * (c_scale * LOG2E)
    k = _rope(_rms(kc.reshape(bsz, t, 2 * C_HEADS, C_HEAD_DIM), g_kc), cos_c, sin_c)
    q = q.transpose(0, 2, 1, 3).astype(BF16)
    kt = k.transpose(0, 2, 3, 1).astype(BF16)
    v = vc.reshape(bsz, t, C_HEADS, C_V_DIM).transpose(0, 2, 1, 3).astype(BF16)
    o2 = _flash(q, kt, v, n_heads=2, share_k=False, k_per_v=1, n_first=n_ctx, tq=256, tk=512, out_dtype=F32)
    o2 = o2.reshape(bsz, C_HEADS, 2, n_lat, C_V_DIM)
    oc = (o2[:, :, 0] - lam * o2[:, :, 1]).transpose(0, 2, 1, 3)
    oc = (_rms(oc, g_sub) * (1.0 - lam_init)).reshape(bsz, n_lat, C_HEADS * C_V_DIM)

    hd = D_NOPE + D_ROPE
    tail = lambda a: jnp.concatenate([a[..., :D_NOPE], _rope(a[..., D_NOPE:], cos_d, sin_d)], axis=-1)
    qd = _modmm(cq[:, n_ctx:], jnp.broadcast_to(zero_mod, (bsz + 1, 2, D_Q_LORA)), g_cq, w_uq.astype(BF16), 0)
    qd = _rms(qd.reshape(bsz, n_lat, D_HEADS, hd), g_qd)
    qd = jnp.concatenate([qd[..., :D_NOPE], _rope(qd[..., D_NOPE:], cos_d[n_ctx:], sin_d[n_ctx:])], axis=-1)
    qd = jnp.pad(qd * (d_scale * LOG2E), ((0, 0), (0, 0), (0, 0), (0, LANE - hd)))
    kv = _modmm(ckv, jnp.broadcast_to(zero_mod, (bsz + 1, 2, D_KV_LORA)), g_ckv, w_ukv.astype(BF16), 0)
    kv = kv.reshape(bsz, t, D_HEADS, D_NOPE + D_V_DIM)
    k_rope = jnp.broadcast_to(kr[:, :, None, :], (bsz, t, D_HEADS, D_ROPE))
    kd = tail(_rms(jnp.concatenate([kv[..., :D_NOPE], k_rope], axis=-1), g_kd))
    kd = jnp.pad(kd, ((0, 0), (0, 0), (0, 0), (0, LANE - hd)))
    vd = kv[..., D_NOPE:]
    od = _flash(qd.transpose(0, 2, 1, 3).astype(BF16), kd.transpose(0, 2, 3, 1).astype(BF16),
                vd.transpose(0, 2, 1, 3).astype(BF16), n_heads=1, share_k=True, k_per_v=1, n_first=n_ctx,
                tq=512, tk=512, out_dtype=BF16)
    od = od.transpose(0, 2, 1, 3).reshape(bsz, n_lat, D_HEADS * D_V_DIM)
    return jnp.concatenate([oc.astype(BF16), od], axis=-1)


def kernel(x, c, ctx, c_ctx, w_ada, b_ada, g_mix, g_ffn, e_w_in, e_w_out, e_g_q, e_g_k, e_conv_w, e_conv_b, e_b_gates, e_g_h, o_w_in, o_w_out, o_g_qc, o_g_kc, o_lam, o_g_sub, o_g_cq, o_w_uq, o_g_ckv, o_w_ukv, o_g_qd, o_g_kd, w_router, b_router, w1, w3, w2):
    bsz, n_lat, d = x.shape
    n_ctx = ctx.shape[1]
    depth = w_ada.shape[0]
    rope_64 = _rope_tables(n_lat, n_ctx, A_HEAD_DIM)
    rope_d = _rope_tables(n_lat, n_ctx, D_ROPE)
    w_router_f = w_router.astype(F32)
    x_all = jnp.concatenate([ctx, x], axis=1)
    for layer in range(depth):
        last = layer == depth - 1
        j = layer // 2
        cond = jnp.concatenate([c, c_ctx[None, :]], axis=0)
        mod = (jax.nn.silu(cond) @ w_ada[layer] + b_ada[layer]).reshape(bsz + 1, 6, d)
        sh1, sc1, g1, sh2, sc2, g2 = [mod[:, i] for i in range(6)]
        mod_in = jnp.stack([sh1, sc1], axis=1)
        mod_out = jnp.stack([g1, sh2, sc2], axis=1)
        if layer % 2 == 0:
            mix = _even_mixer(x_all, mod_in, g_mix[layer], n_ctx, e_w_in[j], e_g_q[j], e_g_k[j], e_conv_w[j],
                              e_conv_b[j], e_b_gates[j], e_g_h[j], rope_64)
            w_out = e_w_out[j]
        else:
            lam_init = LAM_INIT_BASE[0] - LAM_INIT_BASE[1] * math.exp(-LAM_INIT_BASE[2] * layer)
            mix = _odd_mixer(x_all, mod_in, g_mix[layer], n_ctx, lam_init, o_w_in[j], o_g_qc[j], o_g_kc[j], o_lam[j],
                             o_g_sub[j], o_g_cq[j], o_w_uq[j], o_g_ckv[j], o_w_ukv[j], o_g_qd[j], o_g_kd[j],
                             rope_64, rope_d)
            w_out = o_w_out[j]
        if last:
            x_cur = x_all[:, n_ctx:]
            n_ctx_tiles = 0
            if mix.shape[1] != n_lat:
                mix = mix[:, n_ctx:]
        else:
            x_cur = x_all
            n_ctx_tiles = n_ctx // min(ROW_TILE, x_all.shape[1])
        x_new, f, logits = _outproj(mix, x_cur, mod_out, g_ffn[layer], w_out.astype(BF16), w_router_f, n_ctx_tiles)
        t_cur = x_cur.shape[1]
        y = _moe(f.reshape(bsz * t_cur, d), logits.reshape(bsz * t_cur, N_EXPERTS), b_router,
                 w1[layer].astype(BF16), w3[layer].astype(BF16), w2[layer].astype(BF16))
        g2_rows = jnp.broadcast_to(g2[:bsz, None, :], (bsz, t_cur, d))
        if not last:
            g2_rows = g2_rows.at[:, :n_ctx].set(g2[bsz])
        x_all = x_new + g2_rows * y.reshape(bsz, t_cur, d)
        if last:
            return x_all
    return x_all[:, n_ctx:]
```

```python
import functools
import math

import jax
import jax.numpy as jnp
import numpy as np
from jax import lax
from jax.experimental import pallas as pl
from jax.experimental.pallas import tpu as pltpu

F32 = jnp.float32
BF16 = jnp.bfloat16
EPS = 1e-6
LOG2E = 1.4426950408889634
GRID_W = 64
ROPE_THETA = 10000.0
LAM_INIT_BASE = (0.8, 0.6, 0.3)

A_HEAD_DIM, A_HEADS, A_KV_HEADS = 64, 8, 2
B_HEAD_DIM, B_HEADS, B_CONV, B_CHUNK = 128, 4, 3, 128
C_HEAD_DIM, C_V_DIM, C_HEADS = 64, 128, 4
D_HEADS, D_Q_LORA, D_KV_LORA, D_NOPE, D_ROPE, D_V_DIM = 4, 256, 128, 64, 32, 128
N_EXPERTS, N_GROUPS, TOP_K, D_EXPERT = 32, 4, 2, 512

LANE = 128
ROW_TILE = 256
MOE_ROWS = 256
VMEM_LIMIT = 48 * 1024 * 1024


def _cparams(sem):
    return pltpu.CompilerParams(dimension_semantics=sem, vmem_limit_bytes=VMEM_LIMIT)


def _modmm_kernel(x_ref, mod_ref, g_ref, w_ref, o_ref):
    x = x_ref[0]
    y = x * lax.rsqrt(jnp.mean(x * x, axis=-1, keepdims=True) + EPS) * g_ref[...]
    h = y * (1.0 + mod_ref[0, 1:2, :]) + mod_ref[0, 0:1, :]
    o_ref[0] = jnp.dot(h.astype(BF16), w_ref[...], preferred_element_type=F32).astype(o_ref.dtype)


def _modmm(x, mod, g, w, n_ctx_tiles, out_dtype=F32):
    bsz, t, d = x.shape
    n = w.shape[1]
    tm = min(ROW_TILE, t)
    assert t % tm == 0 and n % LANE == 0
    return pl.pallas_call(
        _modmm_kernel,
        grid=(bsz, t // tm),
        in_specs=[
            pl.BlockSpec((1, tm, d), lambda b, i: (b, i, 0)),
            pl.BlockSpec((1, 2, d), lambda b, i: (jnp.where(i < n_ctx_tiles, bsz, b), 0, 0)),
            pl.BlockSpec((1, d), lambda b, i: (0, 0)),
            pl.BlockSpec((d, n), lambda b, i: (0, 0)),
        ],
        out_specs=pl.BlockSpec((1, tm, n), lambda b, i: (b, i, 0)),
        out_shape=jax.ShapeDtypeStruct((bsz, t, n), out_dtype),
        compiler_params=_cparams(("parallel", "parallel")),
        name="modmm",
    )(x, mod, g.reshape(1, d), w)


NEG_BIG = -1e30


def _attn_kernel(qc_ref, qn_ref, k_ref, v_ref, o_ref, mcur_sc, macc_sc, l_sc, acc_sc, *,
                 n_heads, share_k, tq, n_first, tk, n_tiles):
    d = qc_ref.shape[-1]
    dv = v_ref.shape[-1]
    rows = n_heads * tq

    def scores(q_ref, start, size):
        if share_k:
            return jnp.dot(q_ref[0].reshape(rows, d), k_ref[0, 0, :, pl.ds(start, size)],
                           preferred_element_type=F32)
        return jnp.concatenate(
            [jnp.dot(q_ref[0, h], k_ref[0, h, :, pl.ds(start, size)], preferred_element_type=F32)
             for h in range(n_heads)], axis=0)

    def lane_blocks(s, size):
        return [s[:, c * LANE:(c + 1) * LANE] for c in range(size // LANE)]

    def max_pass(q_ref, start, size):
        blocks = lane_blocks(scores(q_ref, start, size), size)
        macc_sc[...] = functools.reduce(jnp.maximum, blocks, macc_sc[...])

    def finish_max():
        m = jnp.max(macc_sc[...], axis=1, keepdims=True)
        mcur_sc[...] = jnp.broadcast_to(m, (rows, LANE))
        macc_sc[...] = jnp.full((rows, LANE), NEG_BIG, F32)

    def over_keys(fn):
        fn(0, n_first)
        if n_tiles > 0:
            def body(j, carry):
                fn(pl.multiple_of(n_first + j * tk, LANE), tk)
                return carry
            lax.fori_loop(0, n_tiles, body, 0)

    @pl.when(pl.program_id(2) == 0)
    def _():
        macc_sc[...] = jnp.full((rows, LANE), NEG_BIG, F32)
        over_keys(functools.partial(max_pass, qc_ref))
        finish_max()

    l_sc[...] = jnp.zeros((rows, LANE), F32)
    acc_sc[...] = jnp.zeros((rows, dv), F32)

    def both(start, size):
        m = mcur_sc[...]
        p_blocks = [jnp.exp2(blk - m) for blk in lane_blocks(scores(qc_ref, start, size), size)]
        l_sc[...] += functools.reduce(jnp.add, p_blocks)
        p = jnp.concatenate(p_blocks, axis=1).astype(BF16)
        acc_sc[...] += jnp.dot(p, v_ref[0, 0, pl.ds(start, size), :], preferred_element_type=F32)
        max_pass(qn_ref, start, size)

    over_keys(both)
    l = jnp.sum(l_sc[...], axis=1, keepdims=True)
    o_ref[0] = (acc_sc[...] / l).reshape(n_heads, tq, dv).astype(o_ref.dtype)
    finish_max()


def _attention(q, kt, v, *, n_heads, share_k, n_first, tq, tk, out_dtype):
    bsz, hq, sq, d = q.shape
    t_k = kt.shape[-1]
    dv = v.shape[-1]
    tq = min(tq, sq)
    rest = t_k - n_first
    tk = min(tk, rest) if rest > 0 else tk
    assert sq % tq == 0 and hq % n_heads == 0 and rest % tk == 0
    n_q = sq // tq
    hk_blk = 1 if share_k else n_heads
    rows = n_heads * tq
    kern = functools.partial(_attn_kernel, n_heads=n_heads, share_k=share_k, tq=tq, n_first=n_first,
                             tk=tk, n_tiles=rest // tk)
    return pl.pallas_call(
        kern,
        grid=(bsz, hq // n_heads, n_q),
        in_specs=[
            pl.BlockSpec((1, n_heads, tq, d), lambda b, g, i: (b, g, i, 0)),
            pl.BlockSpec((1, n_heads, tq, d), lambda b, g, i: (b, g, jnp.minimum(i + 1, n_q - 1), 0)),
            pl.BlockSpec((1, hk_blk, d, t_k), lambda b, g, i: (b, g, 0, 0)),
            pl.BlockSpec((1, 1, t_k, dv), lambda b, g, i: (b, g, 0, 0)),
        ],
        out_specs=pl.BlockSpec((1, n_heads, tq, dv), lambda b, g, i: (b, g, i, 0)),
        out_shape=jax.ShapeDtypeStruct((bsz, hq, sq, dv), out_dtype),
        scratch_shapes=[pltpu.VMEM((rows, LANE), F32), pltpu.VMEM((rows, LANE), F32),
                        pltpu.VMEM((rows, LANE), F32), pltpu.VMEM((rows, dv), F32)],
        compiler_params=_cparams(("parallel", "parallel", "arbitrary")),
        name="attention",
    )(q, q, kt, v)


def _mlstm_kernel(q_ref, k_ref, v_ref, g_ref, h_ref, c_sc, m_sc, *, n_seq):
    L = q_ref.shape[1]
    dh = q_ref.shape[2]

    @pl.when(pl.program_id(1) == 0)
    def _():
        c_sc[...] = jnp.zeros(c_sc.shape, F32)
        m_sc[...] = jnp.zeros(m_sc.shape, F32)

    row = lax.broadcasted_iota(jnp.int32, (L, L), 0)
    col = lax.broadcasted_iota(jnp.int32, (L, L), 1)
    tril = col <= row
    upper = (row <= col).astype(F32)
    ones_col = (lax.broadcasted_iota(jnp.int32, (L, dh), 1) == 0).astype(F32)

    for s in range(n_seq):
        q = q_ref[s]
        k = k_ref[s]
        v_ext = jnp.concatenate([v_ref[s], ones_col], axis=1)
        gates = g_ref[s, 0]
        cum = jnp.dot(gates, upper, preferred_element_type=F32, precision=lax.Precision.HIGHEST)
        i_r = gates[0:1, :]
        b_r = cum[1:2, :]
        b_cm = jnp.transpose(jnp.broadcast_to(b_r, (L, L)))
        i_cm = jnp.transpose(jnp.broadcast_to(i_r, (L, L)))
        b_c = b_cm[:, 0:1]
        i_c = i_cm[:, 0:1]
        m_prev = m_sc[s][:, 0:1]
        c_ext = c_sc[s]

        log_w = jnp.where(tril, b_cm - b_r + i_r, -jnp.inf)
        log_inter = b_c + m_prev
        m_row = jnp.maximum(log_inter, jnp.max(log_w, axis=1, keepdims=True))
        w_intra = jnp.exp(log_w - m_row)
        a_inter = jnp.exp(log_inter - m_row)
        sc = lax.dot_general(q.astype(BF16), k.astype(BF16), (((1,), (1,)), ((), ())),
                             preferred_element_type=F32) * w_intra
        q_c = jnp.dot(q.astype(BF16), c_ext.astype(BF16), preferred_element_type=F32)
        s_v = jnp.dot(sc.astype(BF16), v_ext.astype(BF16), preferred_element_type=F32)
        num = a_inter * q_c[:, :dh] + s_v[:, :dh]
        den = a_inter * q_c[:, dh:dh + 1] + s_v[:, dh:dh + 1]
        h_ref[s] = num / jnp.maximum(jnp.abs(den), jnp.exp(-m_row))

        b_last = b_r[:, L - 1:L]
        log_g_r = b_last - b_r + i_r
        m_new = jnp.maximum(b_last + m_prev, jnp.max(log_g_r, axis=1, keepdims=True))
        g_c = jnp.exp(b_last - b_c + i_c - m_new)
        decay = jnp.exp(b_last + m_prev - m_new)
        upd = jnp.dot(jnp.transpose(k).astype(BF16), (g_c * v_ext).astype(BF16), preferred_element_type=F32)
        c_sc[s] = decay * c_ext + upd
        m_sc[s] = jnp.broadcast_to(m_new, (1, LANE))


def _mlstm(q, k, v, gates, n_seq):
    n, t, dh = q.shape
    L = B_CHUNK
    assert n % n_seq == 0 and t % L == 0
    seq_spec = pl.BlockSpec((n_seq, L, dh), lambda g, c: (g, c, 0))
    return pl.pallas_call(
        functools.partial(_mlstm_kernel, n_seq=n_seq),
        grid=(n // n_seq, t // L),
        in_specs=[seq_spec, seq_spec, seq_spec, pl.BlockSpec((n_seq, 1, 8, L), lambda g, c: (g, c, 0, 0))],
        out_specs=seq_spec,
        out_shape=jax.ShapeDtypeStruct((n, t, dh), F32),
        scratch_shapes=[pltpu.VMEM((n_seq, dh, 2 * dh), F32), pltpu.VMEM((n_seq, 1, LANE), F32)],
        compiler_params=_cparams(("parallel", "arbitrary")),
        name="mlstm",
    )(q, k, v, gates)


def _outproj_kernel(mix_ref, x_ref, mod_ref, g_ref, w_ref, wr_ref, xo_ref, f_ref, lg_ref):
    o = jnp.dot(mix_ref[0], w_ref[...], preferred_element_type=F32)
    xn = x_ref[0] + mod_ref[0, 0:1, :] * o
    xo_ref[0] = xn
    y = xn * lax.rsqrt(jnp.mean(xn * xn, axis=-1, keepdims=True) + EPS) * g_ref[...]
    f = y * (1.0 + mod_ref[0, 2:3, :]) + mod_ref[0, 1:2, :]
    f_ref[0] = f.astype(BF16)
    lg_ref[0] = jnp.dot(f, wr_ref[...], preferred_element_type=F32, precision=lax.Precision.HIGHEST)


def _outproj(mix, x, mod, g, w, w_router, n_ctx_tiles):
    bsz, t, d = x.shape
    m = mix.shape[-1]
    ne = w_router.shape[1]
    tm = min(ROW_TILE, t)
    assert t % tm == 0
    row = lambda b, i: (b, i, 0)
    return pl.pallas_call(
        _outproj_kernel,
        grid=(bsz, t // tm),
        in_specs=[
            pl.BlockSpec((1, tm, m), row),
            pl.BlockSpec((1, tm, d), row),
            pl.BlockSpec((1, 3, d), lambda b, i: (jnp.where(i < n_ctx_tiles, bsz, b), 0, 0)),
            pl.BlockSpec((1, d), lambda b, i: (0, 0)),
            pl.BlockSpec((m, d), lambda b, i: (0, 0)),
            pl.BlockSpec((d, ne), lambda b, i: (0, 0)),
        ],
        out_specs=[pl.BlockSpec((1, tm, d), row), pl.BlockSpec((1, tm, d), row), pl.BlockSpec((1, tm, ne), row)],
        out_shape=[jax.ShapeDtypeStruct((bsz, t, d), F32), jax.ShapeDtypeStruct((bsz, t, d), BF16),
                   jax.ShapeDtypeStruct((bsz, t, ne), F32)],
        compiler_params=_cparams(("parallel", "parallel")),
        name="outproj",
    )(mix, x, mod, g.reshape(1, d), w, w_router)


def _expert_kernel(blk_e_ref, n_used_ref, x_ref, w1_ref, w3_ref, w2_ref, y_ref):
    i = pl.program_id(0)

    @pl.when(i < n_used_ref[0])
    def _():
        xb = x_ref[...]
        h1 = jnp.dot(xb, w1_ref[0], preferred_element_type=F32)
        h3 = jnp.dot(xb, w3_ref[0], preferred_element_type=F32)
        a = (h1 * jax.nn.sigmoid(h1) * h3).astype(BF16)
        y_ref[...] = jnp.dot(a, w2_ref[0], preferred_element_type=F32).astype(y_ref.dtype)

    @pl.when(i >= n_used_ref[0])
    def _():
        y_ref[...] = jnp.zeros(y_ref.shape, y_ref.dtype)


def _experts(xin, blk_e, n_used, w1, w3, w2):
    n_rows, d = xin.shape
    de = w1.shape[-1]
    nb = n_rows // MOE_ROWS
    grid_spec = pltpu.PrefetchScalarGridSpec(
        num_scalar_prefetch=2,
        grid=(nb,),
        in_specs=[
            pl.BlockSpec((MOE_ROWS, d), lambda i, be, nu: (i, 0)),
            pl.BlockSpec((1, d, de), lambda i, be, nu: (be[i], 0, 0)),
            pl.BlockSpec((1, d, de), lambda i, be, nu: (be[i], 0, 0)),
            pl.BlockSpec((1, de, d), lambda i, be, nu: (be[i], 0, 0)),
        ],
        out_specs=pl.BlockSpec((MOE_ROWS, d), lambda i, be, nu: (i, 0)),
    )
    return pl.pallas_call(
        _expert_kernel,
        grid_spec=grid_spec,
        out_shape=jax.ShapeDtypeStruct((n_rows, d), BF16),
        compiler_params=_cparams(("arbitrary",)),
        name="experts",
    )(blk_e, n_used, xin, w1, w3, w2)


def _moe(f_tok, logits, b_router, w1, w3, w2):
    n_tok, d = f_tok.shape
    per = N_EXPERTS // N_GROUPS
    scores = jax.nn.sigmoid(logits)
    sel = scores + b_router.astype(F32)
    def top2(a):
        i1 = jnp.argmax(a, axis=-1)
        hit = lax.broadcasted_iota(jnp.int32, a.shape, a.ndim - 1) == i1[..., None]
        rest = jnp.where(hit, -jnp.inf, a)
        return jnp.max(a, axis=-1), jnp.max(rest, axis=-1), i1, jnp.argmax(rest, axis=-1)

    g1, g2, _, _ = top2(sel.reshape(n_tok, N_GROUPS, per))
    g_idx = jnp.argmax(g1 + g2, axis=-1)
    in_group = (jnp.arange(N_EXPERTS) // per)[None, :] == g_idx[:, None]
    _, _, e1, e2 = top2(jnp.where(in_group, sel, -jnp.inf))
    e_idx = jnp.stack([e1, e2], axis=-1)
    gate = jnp.take_along_axis(scores, e_idx, axis=-1)
    gate = gate / jnp.sum(gate, axis=-1, keepdims=True)

    n_assign = n_tok * TOP_K
    e_flat = e_idx.reshape(-1).astype(jnp.int32)
    tok_flat = jnp.repeat(jnp.arange(n_tok, dtype=jnp.int32), TOP_K)
    onehot = (e_flat[:, None] == jnp.arange(N_EXPERTS, dtype=jnp.int32)[None, :]).astype(jnp.int32)
    csum = jnp.cumsum(onehot, axis=0)
    counts = csum[-1]
    rank = jnp.take_along_axis(csum, e_flat[:, None], axis=1)[:, 0] - 1
    padded = (counts + MOE_ROWS - 1) // MOE_ROWS * MOE_ROWS
    pend = jnp.cumsum(padded)
    pstart = pend - padded
    dest = pstart[e_flat] + rank
    n_rows = -(-n_assign // MOE_ROWS) * MOE_ROWS + N_EXPERTS * MOE_ROWS
    nb = n_rows // MOE_ROWS
    row_tok = jnp.full((n_rows,), n_tok, jnp.int32).at[dest].set(tok_flat)
    xin = jnp.concatenate([f_tok, jnp.zeros((1, d), f_tok.dtype)], axis=0)[row_tok]
    blk_e = jnp.minimum(jnp.searchsorted(pend, jnp.arange(nb, dtype=jnp.int32) * MOE_ROWS, side='right'),
                        N_EXPERTS - 1).astype(jnp.int32)
    n_used = (pend[-1:] // MOE_ROWS).astype(jnp.int32)
    y = _experts(xin, blk_e, n_used, w1, w3, w2)
    yk = y[dest].reshape(n_tok, TOP_K, d).astype(F32)
    return jnp.einsum('tk,tkd->td', gate, yk)


def _rms(x, g):
    return x * lax.rsqrt(jnp.mean(x * x, axis=-1, keepdims=True) + EPS) * g


def _rope_tables(n_lat, n_ctx, rot_dim):
    rows = n_lat // GRID_W
    r = jnp.repeat(jnp.arange(rows), GRID_W)
    c = jnp.tile(jnp.arange(GRID_W), rows)
    n_freq = rot_dim // 4
    inv = ROPE_THETA ** (-jnp.arange(n_freq, dtype=F32) / n_freq)
    ang = jnp.concatenate([r[:, None] * inv, c[:, None] * inv], axis=-1)
    cos = jnp.concatenate([jnp.ones((n_ctx, rot_dim // 2), F32), jnp.cos(ang)], axis=0)
    sin = jnp.concatenate([jnp.zeros((n_ctx, rot_dim // 2), F32), jnp.sin(ang)], axis=0)
    return cos, sin


def _rope(x, cos, sin):
    xp = x.reshape(*x.shape[:-1], -1, 2)
    x0, x1 = xp[..., 0], xp[..., 1]
    c = cos[:, None, :]
    s = sin[:, None, :]
    return jnp.stack([x0 * c - x1 * s, x0 * s + x1 * c], axis=-1).reshape(x.shape)


def _pad_cols(w, n):
    return jnp.pad(w, ((0, 0), (0, n - w.shape[1])))


def _round_up(n, m):
    return -(-n // m) * m


def _conv_silu(x, w, bias):
    n = x.shape[1]
    pad = B_CONV // 2
    xp = jnp.pad(x, ((0, 0), (pad, pad), (0, 0)))
    return jax.nn.silu(sum(xp[:, j:j + n] * w[j] for j in range(B_CONV)) + bias)


def _flip_parts(t, n_ctx):
    return jnp.concatenate([jnp.flip(t[:, :n_ctx], axis=1), jnp.flip(t[:, n_ctx:], axis=1)], axis=1)


def _even_mixer(x_all, mod, g_mix, n_ctx, w_in, g_q, g_k, conv_w, conv_b, b_gates, g_h, rope):
    bsz, t, _ = x_all.shape
    n_in = w_in.shape[1]
    z = _modmm(x_all, mod, g_mix, _pad_cols(w_in, _round_up(n_in, LANE)).astype(BF16), n_ctx // min(ROW_TILE, t))
    sizes = (512, 128, 128, 1024, 512, 512, 16)
    offs = np.cumsum((0,) + sizes)
    qa, ka, va, qk_b, vb, ob, gates = [z[..., offs[i]:offs[i + 1]] for i in range(7)]
    cos, sin = rope

    q = _rope(_rms(qa.reshape(bsz, t, A_HEADS, A_HEAD_DIM), g_q), cos, sin) * (A_HEAD_DIM ** -0.5 * LOG2E)
    k = _rope(_rms(ka.reshape(bsz, t, A_KV_HEADS, A_HEAD_DIM), g_k), cos, sin)
    q = q.transpose(0, 2, 1, 3).astype(BF16)
    kt = k.transpose(0, 2, 3, 1).astype(BF16)
    v = va.reshape(bsz, t, A_KV_HEADS, A_HEAD_DIM).transpose(0, 2, 1, 3).astype(BF16)
    grp = A_HEADS // A_KV_HEADS
    o_lat = _attention(q[:, :, n_ctx:], kt, v, n_heads=grp, share_k=True, n_first=n_ctx,
                       tq=256, tk=512, out_dtype=BF16)
    o_ctx = _attention(q[:, :, :n_ctx], kt[..., :n_ctx], v[:, :, :n_ctx], n_heads=grp, share_k=True,
                       n_first=n_ctx, tq=256, tk=512, out_dtype=BF16)
    oa = jnp.concatenate([o_ctx, o_lat], axis=2).transpose(0, 2, 1, 3).reshape(bsz, t, A_HEADS * A_HEAD_DIM)

    qk_b = jnp.concatenate([_conv_silu(qk_b[:, :n_ctx], conv_w, conv_b),
                            _conv_silu(qk_b[:, n_ctx:], conv_w, conv_b)], axis=1)
    qb, kb = jnp.split(qk_b, 2, axis=-1)
    kb = kb * (B_HEAD_DIM ** -0.5)
    gates = (gates + b_gates).reshape(bsz, t, 4, B_HEADS)
    i_f, f_f, i_b, f_b = [gates[:, :, j] for j in range(4)]
    heads = lambda a: a.reshape(bsz, t, B_HEADS, B_HEAD_DIM)
    both = lambda a: jnp.concatenate([a, _flip_parts(a, n_ctx)], axis=0)
    seq = lambda a: both(heads(a)).transpose(0, 2, 1, 3).reshape(2 * bsz * B_HEADS, t, B_HEAD_DIM)
    gi = jnp.concatenate([i_f, _flip_parts(i_b, n_ctx)], axis=0)
    gf = jax.nn.log_sigmoid(jnp.concatenate([f_f, _flip_parts(f_b, n_ctx)], axis=0))
    n_seqs = 2 * bsz * B_HEADS
    g8 = jnp.stack([gi, gf], axis=0).transpose(1, 3, 0, 2)
    g8 = g8.reshape(n_seqs, 2, t // B_CHUNK, B_CHUNK).transpose(0, 2, 1, 3)
    g8 = jnp.pad(g8, ((0, 0), (0, 0), (0, 6), (0, 0)))
    h = _mlstm(seq(qb), seq(kb), seq(vb), g8, n_seq=4)
    h = h.reshape(2, bsz, B_HEADS, t, B_HEAD_DIM)
    hb = h[0] + _flip_parts(h[1].transpose(0, 2, 1, 3), n_ctx).transpose(0, 2, 1, 3)
    hn = _rms(hb.transpose(0, 2, 1, 3), g_h.reshape(B_HEADS, B_HEAD_DIM)).reshape(bsz, t, B_HEADS * B_HEAD_DIM)
    ob_out = hn * jax.nn.sigmoid(ob)
    return jnp.concatenate([oa, ob_out.astype(BF16)], axis=-1)


def _odd_mixer(x_all, mod, g_mix, n_ctx, lam_init, w_in, g_qc, g_kc, lam_p, g_sub, g_cq, w_uq, g_ckv, w_ukv,
               g_qd, g_kd, rope_c, rope_d):
    bsz, t, _ = x_all.shape
    n_lat = t - n_ctx
    n_in = w_in.shape[1]
    z = _modmm(x_all, mod, g_mix, _pad_cols(w_in, _round_up(n_in, LANE)).astype(BF16), n_ctx // min(ROW_TILE, t))
    sizes = (512, 512, 512, D_Q_LORA, D_KV_LORA, D_ROPE)
    offs = np.cumsum((0,) + sizes)
    qc, kc, vc, cq, ckv, kr = [z[..., offs[i]:offs[i + 1]] for i in range(6)]
    cos_c, sin_c = rope_c
    cos_d, sin_d = rope_d
    c_scale = C_HEAD_DIM ** -0.5
    d_scale = (D_NOPE + D_ROPE) ** -0.5
    zero_mod = jnp.zeros((bsz + 1, 2, 1), F32)

    lq1, lk1, lq2, lk2 = lam_p.astype(F32)
    lam = jnp.exp(jnp.sum(lq1 * lk1)) - jnp.exp(jnp.sum(lq2 * lk2)) + lam_init

    q = _rope(_rms(qc.reshape(bsz, t, 2 * C_HEADS, C_HEAD_DIM), g_qc), cos_c, sin_c)[:, n_ctx:] * (c_scale * LOG2E)
    k = _rope(_rms(kc.reshape(bsz, t, 2 * C_HEADS, C_HEAD_DIM), g_kc), cos_c, sin_c)
    q = q.transpose(0, 2, 1, 3).astype(BF16)
    kt = k.transpose(0, 2, 3, 1).astype(BF16)
    v = vc.reshape(bsz, t, C_HEADS, C_V_DIM).transpose(0, 2, 1, 3).astype(BF16)
    o2 = _attention(q, kt, v, n_heads=2, share_k=False, n_first=n_ctx, tq=512, tk=512, out_dtype=F32)
    o2 = o2.reshape(bsz, C_HEADS, 2, n_lat, C_V_DIM)
    oc = (o2[:, :, 0] - lam * o2[:, :, 1]).transpose(0, 2, 1, 3)
    oc = (_rms(oc, g_sub) * (1.0 - lam_init)).reshape(bsz, n_lat, C_HEADS * C_V_DIM)

    hd = D_NOPE + D_ROPE
    tail = lambda a: jnp.concatenate([a[..., :D_NOPE], _rope(a[..., D_NOPE:], cos_d, sin_d)], axis=-1)
    qd = _modmm(cq[:, n_ctx:], jnp.broadcast_to(zero_mod, (bsz + 1, 2, D_Q_LORA)), g_cq, w_uq.astype(BF16), 0)
    qd = _rms(qd.reshape(bsz, n_lat, D_HEADS, hd), g_qd)
    qd = jnp.concatenate([qd[..., :D_NOPE], _rope(qd[..., D_NOPE:], cos_d[n_ctx:], sin_d[n_ctx:])], axis=-1)
    qd = jnp.pad(qd * (d_scale * LOG2E), ((0, 0), (0, 0), (0, 0), (0, LANE - hd)))
    kv = _modmm(ckv, jnp.broadcast_to(zero_mod, (bsz + 1, 2, D_KV_LORA)), g_ckv, w_ukv.astype(BF16), 0)
    kv = kv.reshape(bsz, t, D_HEADS, D_NOPE + D_V_DIM)
    k_rope = jnp.broadcast_to(kr[:, :, None, :], (bsz, t, D_HEADS, D_ROPE))
    kd = tail(_rms(jnp.concatenate([kv[..., :D_NOPE], k_rope], axis=-1), g_kd))
    kd = jnp.pad(kd, ((0, 0), (0, 0), (0, 0), (0, LANE - hd)))
    vd = kv[..., D_NOPE:]
    od = _attention(qd.transpose(0, 2, 1, 3).astype(BF16), kd.transpose(0, 2, 3, 1).astype(BF16),
                    vd.transpose(0, 2, 1, 3).astype(BF16), n_heads=1, share_k=True, n_first=n_ctx,
                    tq=1024, tk=512, out_dtype=BF16)
    od = od.transpose(0, 2, 1, 3).reshape(bsz, n_lat, D_HEADS * D_V_DIM)
    return jnp.concatenate([oc.astype(BF16), od], axis=-1)


def kernel(x, c, ctx, c_ctx, w_ada, b_ada, g_mix, g_ffn, e_w_in, e_w_out, e_g_q, e_g_k, e_conv_w, e_conv_b, e_b_gates, e_g_h, o_w_in, o_w_out, o_g_qc, o_g_kc, o_lam, o_g_sub, o_g_cq, o_w_uq, o_g_ckv, o_w_ukv, o_g_qd, o_g_kd, w_router, b_router, w1, w3, w2):
    bsz, n_lat, d = x.shape
    n_ctx = ctx.shape[1]
    depth = w_ada.shape[0]
    rope_64 = _rope_tables(n_lat, n_ctx, A_HEAD_DIM)
    rope_d = _rope_tables(n_lat, n_ctx, D_ROPE)
    w_router_f = w_router.astype(F32)
    x_all = jnp.concatenate([ctx, x], axis=1)
    for layer in range(depth):
        last = layer == depth - 1
        j = layer // 2
        cond = jnp.concatenate([c, c_ctx[None, :]], axis=0)
        mod = (jax.nn.silu(cond) @ w_ada[layer] + b_ada[layer]).reshape(bsz + 1, 6, d)
        sh1, sc1, g1, sh2, sc2, g2 = [mod[:, i] for i in range(6)]
        mod_in = jnp.stack([sh1, sc1], axis=1)
        mod_out = jnp.stack([g1, sh2, sc2], axis=1)
        if layer % 2 == 0:
            mix = _even_mixer(x_all, mod_in, g_mix[layer], n_ctx, e_w_in[j], e_g_q[j], e_g_k[j], e_conv_w[j],
                              e_conv_b[j], e_b_gates[j], e_g_h[j], rope_64)
            w_out = e_w_out[j]
        else:
            lam_init = LAM_INIT_BASE[0] - LAM_INIT_BASE[1] * math.exp(-LAM_INIT_BASE[2] * layer)
            mix = _odd_mixer(x_all, mod_in, g_mix[layer], n_ctx, lam_init, o_w_in[j], o_g_qc[j], o_g_kc[j], o_lam[j],
                             o_g_sub[j], o_g_cq[j], o_w_uq[j], o_g_ckv[j], o_w_ukv[j], o_g_qd[j], o_g_kd[j],
                             rope_64, rope_d)
            w_out = o_w_out[j]
        if last:
            x_cur = x_all[:, n_ctx:]
            n_ctx_tiles = 0
            if mix.shape[1] != n_lat:
                mix = mix[:, n_ctx:]
        else:
            x_cur = x_all
            n_ctx_tiles = n_ctx // min(ROW_TILE, x_all.shape[1])
        x_new, f, logits = _outproj(mix, x_cur, mod_out, g_ffn[layer], w_out.astype(BF16), w_router_f, n_ctx_tiles)
        t_cur = x_cur.shape[1]
        y = _moe(f.reshape(bsz * t_cur, d), logits.reshape(bsz * t_cur, N_EXPERTS), b_router,
                 w1[layer].astype(BF16), w3[layer].astype(BF16), w2[layer].astype(BF16))
        g2_rows = jnp.broadcast_to(g2[:bsz, None, :], (bsz, t_cur, d))
        if not last:
            g2_rows = g2_rows.at[:, :n_ctx].set(g2[bsz])
        x_all = x_new + g2_rows * y.reshape(bsz, t_cur, d)
        if last:
            return x_all
    return x_all[:, n_ctx:]
```

```python
import functools
import math

import jax
import jax.numpy as jnp
import numpy as np
from jax import lax
from jax.experimental import pallas as pl
from jax.experimental.pallas import tpu as pltpu

F32 = jnp.float32
BF16 = jnp.bfloat16
EPS = 1e-6
LOG2E = 1.4426950408889634
GRID_W = 64
ROPE_THETA = 10000.0
LAM_INIT_BASE = (0.8, 0.6, 0.3)

A_HEAD_DIM, A_HEADS, A_KV_HEADS = 64, 8, 2
B_HEAD_DIM, B_HEADS, B_CONV, B_CHUNK = 128, 4, 3, 128
C_HEAD_DIM, C_V_DIM, C_HEADS = 64, 128, 4
D_HEADS, D_Q_LORA, D_KV_LORA, D_NOPE, D_ROPE, D_V_DIM = 4, 256, 128, 64, 32, 128
N_EXPERTS, N_GROUPS, TOP_K, D_EXPERT = 32, 4, 2, 512

LANE = 128
ROW_TILE = 256
MOE_ROWS = 256
KEY_TILE = 640
VMEM_LIMIT = 48 * 1024 * 1024


def _cparams(sem):
    return pltpu.CompilerParams(dimension_semantics=sem, vmem_limit_bytes=VMEM_LIMIT)


def _modmm_kernel(x_ref, mod_ref, g_ref, w_ref, o_ref):
    x = x_ref[0]
    y = x * lax.rsqrt(jnp.mean(x * x, axis=-1, keepdims=True) + EPS) * g_ref[...]
    h = y * (1.0 + mod_ref[0, 1:2, :]) + mod_ref[0, 0:1, :]
    o_ref[0] = jnp.dot(h.astype(BF16), w_ref[...], preferred_element_type=F32).astype(o_ref.dtype)


def _modmm(x, mod, g, w, n_ctx_tiles, out_dtype=F32):
    bsz, t, d = x.shape
    n = w.shape[1]
    tm = min(ROW_TILE, t)
    assert t % tm == 0 and n % LANE == 0
    return pl.pallas_call(
        _modmm_kernel,
        grid=(bsz, t // tm),
        in_specs=[
            pl.BlockSpec((1, tm, d), lambda b, i: (b, i, 0)),
            pl.BlockSpec((1, 2, d), lambda b, i: (jnp.where(i < n_ctx_tiles, bsz, b), 0, 0)),
            pl.BlockSpec((1, d), lambda b, i: (0, 0)),
            pl.BlockSpec((d, n), lambda b, i: (0, 0)),
        ],
        out_specs=pl.BlockSpec((1, tm, n), lambda b, i: (b, i, 0)),
        out_shape=jax.ShapeDtypeStruct((bsz, t, n), out_dtype),
        compiler_params=_cparams(("parallel", "parallel")),
        name="modmm",
    )(x, mod, g.reshape(1, d), w)


NEG_BIG = -1e30


def _attn_kernel(qc_ref, qn_ref, k_ref, v_ref, o_ref, mcur_sc, macc_sc, l_sc, acc_sc, *,
                 n_heads, share_k, tq, n_first, tk, n_tiles):
    d = qc_ref.shape[-1]
    dv = v_ref.shape[-1]
    rows = n_heads * tq

    def scores(q_ref, start, size):
        if share_k:
            return jnp.dot(q_ref[0].reshape(rows, d), k_ref[0, 0, :, pl.ds(start, size)],
                           preferred_element_type=F32)
        return jnp.concatenate(
            [jnp.dot(q_ref[0, h], k_ref[0, h, :, pl.ds(start, size)], preferred_element_type=F32)
             for h in range(n_heads)], axis=0)

    def lane_blocks(s, size):
        return [s[:, c * LANE:(c + 1) * LANE] for c in range(size // LANE)]

    def max_pass(q_ref, start, size):
        blocks = lane_blocks(scores(q_ref, start, size), size)
        macc_sc[...] = functools.reduce(jnp.maximum, blocks, macc_sc[...])

    def finish_max():
        m = jnp.max(macc_sc[...], axis=1, keepdims=True)
        mcur_sc[...] = jnp.broadcast_to(m, (rows, LANE))
        macc_sc[...] = jnp.full((rows, LANE), NEG_BIG, F32)

    def over_keys(fn):
        fn(0, n_first)
        if n_tiles > 0:
            def body(j, carry):
                fn(pl.multiple_of(n_first + j * tk, LANE), tk)
                return carry
            lax.fori_loop(0, n_tiles, body, 0)

    @pl.when(pl.program_id(2) == 0)
    def _():
        macc_sc[...] = jnp.full((rows, LANE), NEG_BIG, F32)
        over_keys(functools.partial(max_pass, qc_ref))
        finish_max()

    l_sc[...] = jnp.zeros((rows, LANE), F32)
    acc_sc[...] = jnp.zeros((rows, dv), F32)

    def both(start, size):
        m = mcur_sc[...]
        p_blocks = [jnp.exp2(blk - m) for blk in lane_blocks(scores(qc_ref, start, size), size)]
        l_sc[...] += functools.reduce(jnp.add, p_blocks)
        p = jnp.concatenate(p_blocks, axis=1).astype(BF16)
        acc_sc[...] += jnp.dot(p, v_ref[0, 0, pl.ds(start, size), :], preferred_element_type=F32)
        max_pass(qn_ref, start, size)

    over_keys(both)
    l = jnp.sum(l_sc[...], axis=1, keepdims=True)
    o_ref[0] = (acc_sc[...] / l).reshape(n_heads, tq, dv).astype(o_ref.dtype)
    finish_max()


def _attn1_kernel(q_ref, k_ref, v_ref, o_ref, s_sc, m_sc, a_sc, l_sc, acc_sc, *, n_heads, share_k, tq, tk, n_tiles):
    d = q_ref.shape[-1]
    dv = v_ref.shape[-1]
    rows = n_heads * tq
    n_blk = tk // LANE

    def score_stage(j, slot):
        start = pl.multiple_of(j * tk, LANE)
        if share_k:
            s = jnp.dot(q_ref[0].reshape(rows, d), k_ref[0, 0, :, pl.ds(start, tk)], preferred_element_type=F32)
        else:
            s = jnp.concatenate(
                [jnp.dot(q_ref[0, h], k_ref[0, h, :, pl.ds(start, tk)], preferred_element_type=F32)
                 for h in range(n_heads)], axis=0)
        s_sc[slot] = s
        blocks = [s[:, c * LANE:(c + 1) * LANE] for c in range(n_blk)]
        m_tile = jnp.max(functools.reduce(jnp.maximum, blocks), axis=1, keepdims=True)
        m_prev = m_sc[1 - slot]
        m_new = jnp.maximum(m_prev, m_tile)
        m_sc[slot] = m_new
        a_sc[slot] = jnp.exp2(m_prev - m_new)

    def value_stage(j, slot):
        start = pl.multiple_of(j * tk, LANE)
        m = m_sc[slot]
        alpha = a_sc[slot]
        p_blocks = [jnp.exp2((s_sc[slot, :, c * LANE:(c + 1) * LANE] - m).astype(BF16)) for c in range(n_blk)]
        l_sc[...] = alpha * l_sc[...] + functools.reduce(jnp.add, p_blocks).astype(F32)
        p = jnp.concatenate(p_blocks, axis=1)
        acc_sc[...] = alpha[:, :dv] * acc_sc[...] + jnp.dot(p, v_ref[0, 0, pl.ds(start, tk), :],
                                                            preferred_element_type=F32)

    m_sc[1] = jnp.full((rows, LANE), NEG_BIG, F32)
    l_sc[...] = jnp.zeros((rows, LANE), F32)
    acc_sc[...] = jnp.zeros((rows, dv), F32)
    score_stage(0, 0)
    peel = (n_tiles - 1) % 2
    if peel:
        value_stage(0, 0)
        score_stage(1, 1)

    def body(t, carry):
        j = 2 * t + peel
        value_stage(j, peel)
        score_stage(j + 1, 1 - peel)
        value_stage(j + 1, 1 - peel)
        score_stage(j + 2, peel)
        return carry

    lax.fori_loop(0, (n_tiles - 1) // 2, body, 0)
    value_stage(n_tiles - 1, (n_tiles - 1) % 2)
    l = jnp.sum(l_sc[...], axis=1, keepdims=True)
    o_ref[0] = (acc_sc[...] / l).reshape(n_heads, tq, dv).astype(o_ref.dtype)


def _attention1(q, kt, v, *, n_heads, share_k, tq, tk, out_dtype):
    bsz, hq, sq, d = q.shape
    t_k = kt.shape[-1]
    dv = v.shape[-1]
    tq = min(tq, sq)
    tk = min(tk, t_k)
    assert sq % tq == 0 and hq % n_heads == 0 and t_k % tk == 0 and dv <= LANE
    hk_blk = 1 if share_k else n_heads
    rows = n_heads * tq
    kern = functools.partial(_attn1_kernel, n_heads=n_heads, share_k=share_k, tq=tq, tk=tk, n_tiles=t_k // tk)
    return pl.pallas_call(
        kern,
        grid=(bsz, hq // n_heads, sq // tq),
        in_specs=[
            pl.BlockSpec((1, n_heads, tq, d), lambda b, g, i: (b, g, i, 0)),
            pl.BlockSpec((1, hk_blk, d, t_k), lambda b, g, i: (b, g, 0, 0)),
            pl.BlockSpec((1, 1, t_k, dv), lambda b, g, i: (b, g, 0, 0)),
        ],
        out_specs=pl.BlockSpec((1, n_heads, tq, dv), lambda b, g, i: (b, g, i, 0)),
        out_shape=jax.ShapeDtypeStruct((bsz, hq, sq, dv), out_dtype),
        scratch_shapes=[pltpu.VMEM((2, rows, tk), F32), pltpu.VMEM((2, rows, LANE), F32),
                        pltpu.VMEM((2, rows, LANE), F32), pltpu.VMEM((rows, LANE), F32),
                        pltpu.VMEM((rows, dv), F32)],
        compiler_params=_cparams(("parallel", "parallel", "parallel")),
        name="attention1",
    )(q, kt, v)


def _attention(q, kt, v, *, n_heads, share_k, n_first, tq, tk, out_dtype):
    bsz, hq, sq, d = q.shape
    t_k = kt.shape[-1]
    dv = v.shape[-1]
    tq = min(tq, sq)
    rest = t_k - n_first
    tk = min(tk, rest) if rest > 0 else tk
    assert sq % tq == 0 and hq % n_heads == 0 and rest % tk == 0
    n_q = sq // tq
    hk_blk = 1 if share_k else n_heads
    rows = n_heads * tq
    kern = functools.partial(_attn_kernel, n_heads=n_heads, share_k=share_k, tq=tq, n_first=n_first,
                             tk=tk, n_tiles=rest // tk)
    return pl.pallas_call(
        kern,
        grid=(bsz, hq // n_heads, n_q),
        in_specs=[
            pl.BlockSpec((1, n_heads, tq, d), lambda b, g, i: (b, g, i, 0)),
            pl.BlockSpec((1, n_heads, tq, d), lambda b, g, i: (b, g, jnp.minimum(i + 1, n_q - 1), 0)),
            pl.BlockSpec((1, hk_blk, d, t_k), lambda b, g, i: (b, g, 0, 0)),
            pl.BlockSpec((1, 1, t_k, dv), lambda b, g, i: (b, g, 0, 0)),
        ],
        out_specs=pl.BlockSpec((1, n_heads, tq, dv), lambda b, g, i: (b, g, i, 0)),
        out_shape=jax.ShapeDtypeStruct((bsz, hq, sq, dv), out_dtype),
        scratch_shapes=[pltpu.VMEM((rows, LANE), F32), pltpu.VMEM((rows, LANE), F32),
                        pltpu.VMEM((rows, LANE), F32), pltpu.VMEM((rows, dv), F32)],
        compiler_params=_cparams(("parallel", "parallel", "arbitrary")),
        name="attention",
    )(q, q, kt, v)


def _mlstm_kernel(q_ref, k_ref, v_ref, g_ref, h_ref, c_sc, m_sc, *, n_seq):
    L = q_ref.shape[1]
    dh = q_ref.shape[2]

    @pl.when(pl.program_id(1) == 0)
    def _():
        c_sc[...] = jnp.zeros(c_sc.shape, F32)
        m_sc[...] = jnp.zeros(m_sc.shape, F32)

    row = lax.broadcasted_iota(jnp.int32, (L, L), 0)
    col = lax.broadcasted_iota(jnp.int32, (L, L), 1)
    tril = col <= row
    upper = (row <= col).astype(F32)
    ones_col = (lax.broadcasted_iota(jnp.int32, (L, dh), 1) == 0).astype(F32)

    for s in range(n_seq):
        q = q_ref[s]
        k = k_ref[s]
        v_ext = jnp.concatenate([v_ref[s], ones_col], axis=1)
        gates = g_ref[s, 0]
        cum = jnp.dot(gates, upper, preferred_element_type=F32, precision=lax.Precision.HIGHEST)
        i_r = gates[0:1, :]
        b_r = cum[1:2, :]
        b_cm = jnp.transpose(jnp.broadcast_to(b_r, (L, L)))
        i_cm = jnp.transpose(jnp.broadcast_to(i_r, (L, L)))
        b_c = b_cm[:, 0:1]
        i_c = i_cm[:, 0:1]
        m_prev = m_sc[s][:, 0:1]
        c_ext = c_sc[s]

        log_w = jnp.where(tril, b_cm - b_r + i_r, -jnp.inf)
        log_inter = b_c + m_prev
        m_row = jnp.maximum(log_inter, jnp.max(log_w, axis=1, keepdims=True))
        w_intra = jnp.exp(log_w - m_row)
        a_inter = jnp.exp(log_inter - m_row)
        sc = lax.dot_general(q.astype(BF16), k.astype(BF16), (((1,), (1,)), ((), ())),
                             preferred_element_type=F32) * w_intra
        q_c = jnp.dot(q.astype(BF16), c_ext.astype(BF16), preferred_element_type=F32)
        s_v = jnp.dot(sc.astype(BF16), v_ext.astype(BF16), preferred_element_type=F32)
        num = a_inter * q_c[:, :dh] + s_v[:, :dh]
        den = a_inter * q_c[:, dh:dh + 1] + s_v[:, dh:dh + 1]
        h_ref[s] = num / jnp.maximum(jnp.abs(den), jnp.exp(-m_row))

        b_last = b_r[:, L - 1:L]
        log_g_r = b_last - b_r + i_r
        m_new = jnp.maximum(b_last + m_prev, jnp.max(log_g_r, axis=1, keepdims=True))
        g_c = jnp.exp(b_last - b_c + i_c - m_new)
        decay = jnp.exp(b_last + m_prev - m_new)
        upd = jnp.dot(jnp.transpose(k).astype(BF16), (g_c * v_ext).astype(BF16), preferred_element_type=F32)
        c_sc[s] = decay * c_ext + upd
        m_sc[s] = jnp.broadcast_to(m_new, (1, LANE))


def _mlstm(q, k, v, gates, n_seq):
    n, t, dh = q.shape
    L = B_CHUNK
    assert n % n_seq == 0 and t % L == 0
    seq_spec = pl.BlockSpec((n_seq, L, dh), lambda g, c: (g, c, 0))
    return pl.pallas_call(
        functools.partial(_mlstm_kernel, n_seq=n_seq),
        grid=(n // n_seq, t // L),
        in_specs=[seq_spec, seq_spec, seq_spec, pl.BlockSpec((n_seq, 1, 8, L), lambda g, c: (g, c, 0, 0))],
        out_specs=seq_spec,
        out_shape=jax.ShapeDtypeStruct((n, t, dh), F32),
        scratch_shapes=[pltpu.VMEM((n_seq, dh, 2 * dh), F32), pltpu.VMEM((n_seq, 1, LANE), F32)],
        compiler_params=_cparams(("parallel", "arbitrary")),
        name="mlstm",
    )(q, k, v, gates)


def _outproj_kernel(mix_ref, x_ref, mod_ref, g_ref, w_ref, wr_ref, xo_ref, f_ref, lg_ref):
    o = jnp.dot(mix_ref[0], w_ref[...], preferred_element_type=F32)
    xn = x_ref[0] + mod_ref[0, 0:1, :] * o
    xo_ref[0] = xn
    y = xn * lax.rsqrt(jnp.mean(xn * xn, axis=-1, keepdims=True) + EPS) * g_ref[...]
    f = y * (1.0 + mod_ref[0, 2:3, :]) + mod_ref[0, 1:2, :]
    f_ref[0] = f.astype(BF16)
    lg_ref[0] = jnp.dot(f, wr_ref[...], preferred_element_type=F32, precision=lax.Precision.HIGHEST)


def _outproj(mix, x, mod, g, w, w_router, n_ctx_tiles):
    bsz, t, d = x.shape
    m = mix.shape[-1]
    ne = w_router.shape[1]
    tm = min(ROW_TILE, t)
    assert t % tm == 0
    row = lambda b, i: (b, i, 0)
    return pl.pallas_call(
        _outproj_kernel,
        grid=(bsz, t // tm),
        in_specs=[
            pl.BlockSpec((1, tm, m), row),
            pl.BlockSpec((1, tm, d), row),
            pl.BlockSpec((1, 3, d), lambda b, i: (jnp.where(i < n_ctx_tiles, bsz, b), 0, 0)),
            pl.BlockSpec((1, d), lambda b, i: (0, 0)),
            pl.BlockSpec((m, d), lambda b, i: (0, 0)),
            pl.BlockSpec((d, ne), lambda b, i: (0, 0)),
        ],
        out_specs=[pl.BlockSpec((1, tm, d), row), pl.BlockSpec((1, tm, d), row), pl.BlockSpec((1, tm, ne), row)],
        out_shape=[jax.ShapeDtypeStruct((bsz, t, d), F32), jax.ShapeDtypeStruct((bsz, t, d), BF16),
                   jax.ShapeDtypeStruct((bsz, t, ne), F32)],
        compiler_params=_cparams(("parallel", "parallel")),
        name="outproj",
    )(mix, x, mod, g.reshape(1, d), w, w_router)


def _expert_kernel(blk_e_ref, n_used_ref, x_ref, w1_ref, w3_ref, w2_ref, y_ref):
    i = pl.program_id(0)

    @pl.when(i < n_used_ref[0])
    def _():
        xb = x_ref[...]
        h1 = jnp.dot(xb, w1_ref[0], preferred_element_type=F32)
        h3 = jnp.dot(xb, w3_ref[0], preferred_element_type=F32)
        a = (h1 * jax.nn.sigmoid(h1) * h3).astype(BF16)
        y_ref[...] = jnp.dot(a, w2_ref[0], preferred_element_type=F32).astype(y_ref.dtype)

    @pl.when(i >= n_used_ref[0])
    def _():
        y_ref[...] = jnp.zeros(y_ref.shape, y_ref.dtype)


def _experts(xin, blk_e, n_used, w1, w3, w2):
    n_rows, d = xin.shape
    de = w1.shape[-1]
    nb = n_rows // MOE_ROWS
    grid_spec = pltpu.PrefetchScalarGridSpec(
        num_scalar_prefetch=2,
        grid=(nb,),
        in_specs=[
            pl.BlockSpec((MOE_ROWS, d), lambda i, be, nu: (i, 0)),
            pl.BlockSpec((1, d, de), lambda i, be, nu: (be[i], 0, 0)),
            pl.BlockSpec((1, d, de), lambda i, be, nu: (be[i], 0, 0)),
            pl.BlockSpec((1, de, d), lambda i, be, nu: (be[i], 0, 0)),
        ],
        out_specs=pl.BlockSpec((MOE_ROWS, d), lambda i, be, nu: (i, 0)),
    )
    return pl.pallas_call(
        _expert_kernel,
        grid_spec=grid_spec,
        out_shape=jax.ShapeDtypeStruct((n_rows, d), BF16),
        compiler_params=_cparams(("arbitrary",)),
        name="experts",
    )(blk_e, n_used, xin, w1, w3, w2)


def _moe(f_tok, logits, b_router, w1, w3, w2):
    n_tok, d = f_tok.shape
    per = N_EXPERTS // N_GROUPS
    scores = jax.nn.sigmoid(logits)
    sel = scores + b_router.astype(F32)
    def top2(a):
        i1 = jnp.argmax(a, axis=-1)
        hit = lax.broadcasted_iota(jnp.int32, a.shape, a.ndim - 1) == i1[..., None]
        rest = jnp.where(hit, -jnp.inf, a)
        return jnp.max(a, axis=-1), jnp.max(rest, axis=-1), i1, jnp.argmax(rest, axis=-1)

    g1, g2, _, _ = top2(sel.reshape(n_tok, N_GROUPS, per))
    g_idx = jnp.argmax(g1 + g2, axis=-1)
    in_group = (jnp.arange(N_EXPERTS) // per)[None, :] == g_idx[:, None]
    _, _, e1, e2 = top2(jnp.where(in_group, sel, -jnp.inf))
    e_idx = jnp.stack([e1, e2], axis=-1)
    gate = jnp.take_along_axis(scores, e_idx, axis=-1)
    gate = gate / jnp.sum(gate, axis=-1, keepdims=True)

    n_assign = n_tok * TOP_K
    e_flat = e_idx.reshape(-1).astype(jnp.int32)
    tok_flat = jnp.repeat(jnp.arange(n_tok, dtype=jnp.int32), TOP_K)
    onehot = (e_flat[:, None] == jnp.arange(N_EXPERTS, dtype=jnp.int32)[None, :]).astype(jnp.int32)
    csum = jnp.cumsum(onehot, axis=0)
    counts = csum[-1]
    rank = jnp.take_along_axis(csum, e_flat[:, None], axis=1)[:, 0] - 1
    padded = (counts + MOE_ROWS - 1) // MOE_ROWS * MOE_ROWS
    pend = jnp.cumsum(padded)
    pstart = pend - padded
    dest = pstart[e_flat] + rank
    n_rows = -(-n_assign // MOE_ROWS) * MOE_ROWS + N_EXPERTS * MOE_ROWS
    nb = n_rows // MOE_ROWS
    row_tok = jnp.full((n_rows,), n_tok, jnp.int32).at[dest].set(tok_flat)
    xin = jnp.concatenate([f_tok, jnp.zeros((1, d), f_tok.dtype)], axis=0)[row_tok]
    blk_e = jnp.minimum(jnp.searchsorted(pend, jnp.arange(nb, dtype=jnp.int32) * MOE_ROWS, side='right'),
                        N_EXPERTS - 1).astype(jnp.int32)
    n_used = (pend[-1:] // MOE_ROWS).astype(jnp.int32)
    y = _experts(xin, blk_e, n_used, w1, w3, w2)
    yk = y[dest].reshape(n_tok, TOP_K, d).astype(F32)
    return jnp.einsum('tk,tkd->td', gate, yk)


def _rms(x, g):
    return x * lax.rsqrt(jnp.mean(x * x, axis=-1, keepdims=True) + EPS) * g


def _rope_tables(n_lat, n_ctx, rot_dim):
    rows = n_lat // GRID_W
    r = jnp.repeat(jnp.arange(rows), GRID_W)
    c = jnp.tile(jnp.arange(GRID_W), rows)
    n_freq = rot_dim // 4
    inv = ROPE_THETA ** (-jnp.arange(n_freq, dtype=F32) / n_freq)
    ang = jnp.concatenate([r[:, None] * inv, c[:, None] * inv], axis=-1)
    cos = jnp.concatenate([jnp.ones((n_ctx, rot_dim // 2), F32), jnp.cos(ang)], axis=0)
    sin = jnp.concatenate([jnp.zeros((n_ctx, rot_dim // 2), F32), jnp.sin(ang)], axis=0)
    return cos, sin


def _rope(x, cos, sin):
    xp = x.reshape(*x.shape[:-1], -1, 2)
    x0, x1 = xp[..., 0], xp[..., 1]
    c = cos[:, None, :]
    s = sin[:, None, :]
    return jnp.stack([x0 * c - x1 * s, x0 * s + x1 * c], axis=-1).reshape(x.shape)


def _pad_cols(w, n):
    return jnp.pad(w, ((0, 0), (0, n - w.shape[1])))


def _round_up(n, m):
    return -(-n // m) * m


def _conv_silu(x, w, bias):
    n = x.shape[1]
    pad = B_CONV // 2
    xp = jnp.pad(x, ((0, 0), (pad, pad), (0, 0)))
    return jax.nn.silu(sum(xp[:, j:j + n] * w[j] for j in range(B_CONV)) + bias)


def _flip_parts(t, n_ctx):
    return jnp.concatenate([jnp.flip(t[:, :n_ctx], axis=1), jnp.flip(t[:, n_ctx:], axis=1)], axis=1)


def _even_mixer(x_all, mod, g_mix, n_ctx, w_in, g_q, g_k, conv_w, conv_b, b_gates, g_h, rope):
    bsz, t, _ = x_all.shape
    n_in = w_in.shape[1]
    z = _modmm(x_all, mod, g_mix, _pad_cols(w_in, _round_up(n_in, LANE)).astype(BF16), n_ctx // min(ROW_TILE, t))
    sizes = (512, 128, 128, 1024, 512, 512, 16)
    offs = np.cumsum((0,) + sizes)
    qa, ka, va, qk_b, vb, ob, gates = [z[..., offs[i]:offs[i + 1]] for i in range(7)]
    cos, sin = rope

    q = _rope(_rms(qa.reshape(bsz, t, A_HEADS, A_HEAD_DIM), g_q), cos, sin) * (A_HEAD_DIM ** -0.5 * LOG2E)
    k = _rope(_rms(ka.reshape(bsz, t, A_KV_HEADS, A_HEAD_DIM), g_k), cos, sin)
    q = q.transpose(0, 2, 1, 3).astype(BF16)
    kt = k.transpose(0, 2, 3, 1).astype(BF16)
    v = va.reshape(bsz, t, A_KV_HEADS, A_HEAD_DIM).transpose(0, 2, 1, 3).astype(BF16)
    grp = A_HEADS // A_KV_HEADS
    o_lat = _attention1(q[:, :, n_ctx:], kt, v, n_heads=grp, share_k=True, tq=256, tk=KEY_TILE, out_dtype=BF16)
    o_ctx = _attention1(q[:, :, :n_ctx], kt[..., :n_ctx], v[:, :, :n_ctx], n_heads=grp, share_k=True,
                        tq=256, tk=n_ctx, out_dtype=BF16)
    oa = jnp.concatenate([o_ctx, o_lat], axis=2).transpose(0, 2, 1, 3).reshape(bsz, t, A_HEADS * A_HEAD_DIM)

    qk_b = jnp.concatenate([_conv_silu(qk_b[:, :n_ctx], conv_w, conv_b),
                            _conv_silu(qk_b[:, n_ctx:], conv_w, conv_b)], axis=1)
    qb, kb = jnp.split(qk_b, 2, axis=-1)
    kb = kb * (B_HEAD_DIM ** -0.5)
    gates = (gates + b_gates).reshape(bsz, t, 4, B_HEADS)
    i_f, f_f, i_b, f_b = [gates[:, :, j] for j in range(4)]
    heads = lambda a: a.reshape(bsz, t, B_HEADS, B_HEAD_DIM)
    both = lambda a: jnp.concatenate([a, _flip_parts(a, n_ctx)], axis=0)
    seq = lambda a: both(heads(a)).transpose(0, 2, 1, 3).reshape(2 * bsz * B_HEADS, t, B_HEAD_DIM)
    gi = jnp.concatenate([i_f, _flip_parts(i_b, n_ctx)], axis=0)
    gf = jax.nn.log_sigmoid(jnp.concatenate([f_f, _flip_parts(f_b, n_ctx)], axis=0))
    n_seqs = 2 * bsz * B_HEADS
    g8 = jnp.stack([gi, gf], axis=0).transpose(1, 3, 0, 2)
    g8 = g8.reshape(n_seqs, 2, t // B_CHUNK, B_CHUNK).transpose(0, 2, 1, 3)
    g8 = jnp.pad(g8, ((0, 0), (0, 0), (0, 6), (0, 0)))
    h = _mlstm(seq(qb), seq(kb), seq(vb), g8, n_seq=4)
    h = h.reshape(2, bsz, B_HEADS, t, B_HEAD_DIM)
    hb = h[0] + _flip_parts(h[1].transpose(0, 2, 1, 3), n_ctx).transpose(0, 2, 1, 3)
    hn = _rms(hb.transpose(0, 2, 1, 3), g_h.reshape(B_HEADS, B_HEAD_DIM)).reshape(bsz, t, B_HEADS * B_HEAD_DIM)
    ob_out = hn * jax.nn.sigmoid(ob)
    return jnp.concatenate([oa, ob_out.astype(BF16)], axis=-1)


def _odd_mixer(x_all, mod, g_mix, n_ctx, lam_init, w_in, g_qc, g_kc, lam_p, g_sub, g_cq, w_uq, g_ckv, w_ukv,
               g_qd, g_kd, rope_c, rope_d):
    bsz, t, _ = x_all.shape
    n_lat = t - n_ctx
    n_in = w_in.shape[1]
    z = _modmm(x_all, mod, g_mix, _pad_cols(w_in, _round_up(n_in, LANE)).astype(BF16), n_ctx // min(ROW_TILE, t))
    sizes = (512, 512, 512, D_Q_LORA, D_KV_LORA, D_ROPE)
    offs = np.cumsum((0,) + sizes)
    qc, kc, vc, cq, ckv, kr = [z[..., offs[i]:offs[i + 1]] for i in range(6)]
    cos_c, sin_c = rope_c
    cos_d, sin_d = rope_d
    c_scale = C_HEAD_DIM ** -0.5
    d_scale = (D_NOPE + D_ROPE) ** -0.5
    zero_mod = jnp.zeros((bsz + 1, 2, 1), F32)

    lq1, lk1, lq2, lk2 = lam_p.astype(F32)
    lam = jnp.exp(jnp.sum(lq1 * lk1)) - jnp.exp(jnp.sum(lq2 * lk2)) + lam_init

    q = _rope(_rms(qc.reshape(bsz, t, 2 * C_HEADS, C_HEAD_DIM), g_qc), cos_c, sin_c)[:, n_ctx:] * (c_scale * LOG2E)
    k = _rope(_rms(kc.reshape(bsz, t, 2 * C_HEADS, C_HEAD_DIM), g_kc), cos_c, sin_c)
    q = q.transpose(0, 2, 1, 3).astype(BF16)
    kt = k.transpose(0, 2, 3, 1).astype(BF16)
    v = vc.reshape(bsz, t, C_HEADS, C_V_DIM).transpose(0, 2, 1, 3).astype(BF16)
    o2 = _attention1(q, kt, v, n_heads=2, share_k=False, tq=512, tk=KEY_TILE, out_dtype=F32)
    o2 = o2.reshape(bsz, C_HEADS, 2, n_lat, C_V_DIM)
    oc = (o2[:, :, 0] - lam * o2[:, :, 1]).transpose(0, 2, 1, 3)
    oc = (_rms(oc, g_sub) * (1.0 - lam_init)).reshape(bsz, n_lat, C_HEADS * C_V_DIM)

    hd = D_NOPE + D_ROPE
    tail = lambda a: jnp.concatenate([a[..., :D_NOPE], _rope(a[..., D_NOPE:], cos_d, sin_d)], axis=-1)
    qd = _modmm(cq[:, n_ctx:], jnp.broadcast_to(zero_mod, (bsz + 1, 2, D_Q_LORA)), g_cq, w_uq.astype(BF16), 0)
    qd = _rms(qd.reshape(bsz, n_lat, D_HEADS, hd), g_qd)
    qd = jnp.concatenate([qd[..., :D_NOPE], _rope(qd[..., D_NOPE:], cos_d[n_ctx:], sin_d[n_ctx:])], axis=-1)
    qd = jnp.pad(qd * (d_scale * LOG2E), ((0, 0), (0, 0), (0, 0), (0, LANE - hd)))
    kv = _modmm(ckv, jnp.broadcast_to(zero_mod, (bsz + 1, 2, D_KV_LORA)), g_ckv, w_ukv.astype(BF16), 0)
    kv = kv.reshape(bsz, t, D_HEADS, D_NOPE + D_V_DIM)
    k_rope = jnp.broadcast_to(kr[:, :, None, :], (bsz, t, D_HEADS, D_ROPE))
    kd = tail(_rms(jnp.concatenate([kv[..., :D_NOPE], k_rope], axis=-1), g_kd))
    kd = jnp.pad(kd, ((0, 0), (0, 0), (0, 0), (0, LANE - hd)))
    vd = kv[..., D_NOPE:]
    od = _attention1(qd.transpose(0, 2, 1, 3).astype(BF16), kd.transpose(0, 2, 3, 1).astype(BF16),
                     vd.transpose(0, 2, 1, 3).astype(BF16), n_heads=1, share_k=True,
                     tq=1024, tk=KEY_TILE, out_dtype=BF16)
    od = od.transpose(0, 2, 1, 3).reshape(bsz, n_lat, D_HEADS * D_V_DIM)
    return jnp.concatenate([oc.astype(BF16), od], axis=-1)


def kernel(x, c, ctx, c_ctx, w_ada, b_ada, g_mix, g_ffn, e_w_in, e_w_out, e_g_q, e_g_k, e_conv_w, e_conv_b, e_b_gates, e_g_h, o_w_in, o_w_out, o_g_qc, o_g_kc, o_lam, o_g_sub, o_g_cq, o_w_uq, o_g_ckv, o_w_ukv, o_g_qd, o_g_kd, w_router, b_router, w1, w3, w2):
    bsz, n_lat, d = x.shape
    n_ctx = ctx.shape[1]
    depth = w_ada.shape[0]
    rope_64 = _rope_tables(n_lat, n_ctx, A_HEAD_DIM)
    rope_d = _rope_tables(n_lat, n_ctx, D_ROPE)
    w_router_f = w_router.astype(F32)
    x_all = jnp.concatenate([ctx, x], axis=1)
    for layer in range(depth):
        last = layer == depth - 1
        j = layer // 2
        cond = jnp.concatenate([c, c_ctx[None, :]], axis=0)
        mod = (jax.nn.silu(cond) @ w_ada[layer] + b_ada[layer]).reshape(bsz + 1, 6, d)
        sh1, sc1, g1, sh2, sc2, g2 = [mod[:, i] for i in range(6)]
        mod_in = jnp.stack([sh1, sc1], axis=1)
        mod_out = jnp.stack([g1, sh2, sc2], axis=1)
        if layer % 2 == 0:
            mix = _even_mixer(x_all, mod_in, g_mix[layer], n_ctx, e_w_in[j], e_g_q[j], e_g_k[j], e_conv_w[j],
                              e_conv_b[j], e_b_gates[j], e_g_h[j], rope_64)
            w_out = e_w_out[j]
        else:
            lam_init = LAM_INIT_BASE[0] - LAM_INIT_BASE[1] * math.exp(-LAM_INIT_BASE[2] * layer)
            mix = _odd_mixer(x_all, mod_in, g_mix[layer], n_ctx, lam_init, o_w_in[j], o_g_qc[j], o_g_kc[j], o_lam[j],
                             o_g_sub[j], o_g_cq[j], o_w_uq[j], o_g_ckv[j], o_w_ukv[j], o_g_qd[j], o_g_kd[j],
                             rope_64, rope_d)
            w_out = o_w_out[j]
        if last:
            x_cur = x_all[:, n_ctx:]
            n_ctx_tiles = 0
            if mix.shape[1] != n_lat:
                mix = mix[:, n_ctx:]
        else:
            x_cur = x_all
            n_ctx_tiles = n_ctx // min(ROW_TILE, x_all.shape[1])
        x_new, f, logits = _outproj(mix, x_cur, mod_out, g_ffn[layer], w_out.astype(BF16), w_router_f, n_ctx_tiles)
        t_cur = x_cur.shape[1]
        y = _moe(f.reshape(bsz * t_cur, d), logits.reshape(bsz * t_cur, N_EXPERTS), b_router,
                 w1[layer].astype(BF16), w3[layer].astype(BF16), w2[layer].astype(BF16))
        g2_rows = jnp.broadcast_to(g2[:bsz, None, :], (bsz, t_cur, d))
        if not last:
            g2_rows = g2_rows.at[:, :n_ctx].set(g2[bsz])
        x_all = x_new + g2_rows * y.reshape(bsz, t_cur, d)
        if last:
            return x_all
    return x_all[:, n_ctx:]
```

```python
import functools
import math

import jax
import jax.numpy as jnp
import numpy as np
from jax import lax
from jax.experimental import pallas as pl
from jax.experimental.pallas import tpu as pltpu

F32 = jnp.float32
BF16 = jnp.bfloat16
EPS = 1e-6
LOG2E = 1.4426950408889634
NEG_BIG = -1e30
GRID_W = 64
ROPE_THETA = 10000.0
LAM_INIT_BASE = (0.8, 0.6, 0.3)

A_HEAD_DIM, A_HEADS, A_KV_HEADS = 64, 8, 2
B_HEAD_DIM, B_HEADS, B_CONV, B_CHUNK = 128, 4, 3, 128
C_HEAD_DIM, C_V_DIM, C_HEADS = 64, 128, 4
D_HEADS, D_Q_LORA, D_KV_LORA, D_NOPE, D_ROPE, D_V_DIM = 4, 256, 128, 64, 32, 128
N_EXPERTS, N_GROUPS, TOP_K, D_EXPERT = 32, 4, 2, 512
HALF = 512

LANE = 128
SUBLANE = 8
ROW_TILE = 256
MOE_ROWS = 256
KEY_TILE = 640
VMEM_LIMIT = 48 * 1024 * 1024


def _cparams(sem):
    return pltpu.CompilerParams(dimension_semantics=sem, vmem_limit_bytes=VMEM_LIMIT)


def _blk(a, i):
    return a[:, i * LANE:(i + 1) * LANE]


def _modulated(x_ref, mod_ref, g_ref):
    x = x_ref[0]
    y = x * lax.rsqrt(jnp.mean(x * x, axis=-1, keepdims=True) + EPS) * g_ref[...]
    return (y * (1.0 + mod_ref[0, 1:2, :]) + mod_ref[0, 0:1, :]).astype(BF16)


def _pairswap(b, even):
    return jnp.where(even, pltpu.roll(b, LANE - 1, 1), pltpu.roll(b, 1, 1))


def _rope(b, cos, sin, even):
    return b * cos + _pairswap(b, even) * sin


def _head64_norm(b, bd, gain):
    ss = jnp.dot((b * b).astype(BF16), bd, preferred_element_type=F32)
    return b * lax.rsqrt(ss * (1.0 / 64) + EPS) * gain


def _lane_norm(b, n_real, gain):
    ss = jnp.sum(b * b, axis=1, keepdims=True)
    return b * lax.rsqrt(ss * (1.0 / n_real) + EPS) * gain


def _inproj_even_kernel(x_ref, mod_ref, g_ref, w_ref, cos_ref, sin_ref, bd_ref, gq_ref, gk_ref, bg_ref,
                        q_ref, k_ref, v_ref, qk_ref, vb_ref, ob_ref, gt_ref):
    z = jnp.dot(_modulated(x_ref, mod_ref, g_ref), w_ref[...], preferred_element_type=F32)
    lane = lax.broadcasted_iota(jnp.int32, (z.shape[0], LANE), 1)
    even = (lane & 1) == 0
    lo = lane < 64
    cos, sin, bd = cos_ref[...], sin_ref[...], bd_ref[...]
    q_scale = A_HEAD_DIM ** -0.5 * LOG2E
    for c in range(4):
        qb = _rope(_head64_norm(_blk(z, c), bd, gq_ref[...]), cos, sin, even) * q_scale
        q_ref[0, :, c * LANE:(c + 1) * LANE] = qb.astype(BF16)
    kn = _rope(_head64_norm(_blk(z, 4), bd, gk_ref[...]), cos, sin, even)
    for src, ref in ((kn, k_ref), (_blk(z, 5), v_ref)):
        sw = pltpu.roll(src, 64, 1)
        ref[0] = jnp.concatenate([jnp.where(lo, src, sw), jnp.where(lo, sw, src)], axis=1).astype(BF16)
    qk_ref[0] = z[:, 6 * LANE:14 * LANE]
    vb_ref[0] = z[:, 14 * LANE:18 * LANE].astype(BF16)
    ob_ref[0] = z[:, 18 * LANE:22 * LANE]
    gt_ref[0] = _blk(z, 22) + bg_ref[...]


def _row_specs(tm, d, bsz, n_lat_tiles):
    return [pl.BlockSpec((1, tm, d), lambda b, i: (b, i, 0)),
            pl.BlockSpec((1, 2, d), lambda b, i: (jnp.where(i >= n_lat_tiles, bsz, b), 0, 0)),
            pl.BlockSpec((1, d), lambda b, i: (0, 0))]


def _const_spec(a):
    return pl.BlockSpec(a.shape, lambda b, i: (0,) * a.ndim)


def _inproj_even(x, mod, g, w, cos, sin, bd, gq, gk, bg, n_lat_tiles):
    bsz, t, d = x.shape
    tm = ROW_TILE
    tab = pl.BlockSpec((tm, LANE), lambda b, i: (i, 0))
    consts = [bd, gq, gk, bg]
    widths = [(4 * LANE, BF16), (2 * LANE, BF16), (2 * LANE, BF16), (8 * LANE, F32), (4 * LANE, BF16),
              (4 * LANE, F32), (LANE, F32)]
    return pl.pallas_call(
        _inproj_even_kernel,
        grid=(bsz, t // tm),
        in_specs=_row_specs(tm, d, bsz, n_lat_tiles) + [_const_spec(w), tab, tab] + [_const_spec(a) for a in consts],
        out_specs=[pl.BlockSpec((1, tm, n), lambda b, i: (b, i, 0)) for n, _ in widths],
        out_shape=[jax.ShapeDtypeStruct((bsz, t, n), dt) for n, dt in widths],
        compiler_params=_cparams(("parallel", "parallel")),
        name="inproj_even",
    )(x, mod, g.reshape(1, d), w, cos, sin, *consts)


def _inproj_odd_kernel(x_ref, mod_ref, g_ref, w_ref, cos_ref, sin_ref, cosd_ref, sind_ref, bd_ref, gqc_ref, gkc_ref,
                       gcq_ref, wuq_ref, gqd_ref, gckv_ref, wkn_ref, wvd_ref, gkd_ref,
                       qc_ref, kc_ref, vc_ref, qd_ref, kd_ref, vd_ref):
    z = jnp.dot(_modulated(x_ref, mod_ref, g_ref), w_ref[...], preferred_element_type=F32)
    lane = lax.broadcasted_iota(jnp.int32, (z.shape[0], LANE), 1)
    even = (lane & 1) == 0
    cos, sin, bd = cos_ref[...], sin_ref[...], bd_ref[...]
    cosd, sind = cosd_ref[...], sind_ref[...]
    c_scale = C_HEAD_DIM ** -0.5 * LOG2E
    d_scale = (D_NOPE + D_ROPE) ** -0.5 * LOG2E
    for c in range(4):
        sl = slice(c * LANE, (c + 1) * LANE)
        qc_ref[0, :, sl] = (_rope(_head64_norm(_blk(z, c), bd, gqc_ref[...]), cos, sin, even) * c_scale).astype(BF16)
        kc_ref[0, :, sl] = _rope(_head64_norm(_blk(z, 4 + c), bd, gkc_ref[...]), cos, sin, even).astype(BF16)
    vc_ref[0] = z[:, 8 * LANE:12 * LANE].astype(BF16)
    cq = z[:, 12 * LANE:14 * LANE]
    cq = cq * lax.rsqrt(jnp.mean(cq * cq, axis=-1, keepdims=True) + EPS) * gcq_ref[...]
    qd = jnp.dot(cq.astype(BF16), wuq_ref[...], preferred_element_type=F32)
    ckv = _blk(z, 14)
    ckv = (ckv * lax.rsqrt(jnp.mean(ckv * ckv, axis=-1, keepdims=True) + EPS) * gckv_ref[...]).astype(BF16)
    kn = jnp.dot(ckv, wkn_ref[...], preferred_element_type=F32)
    vd_ref[0] = jnp.dot(ckv, wvd_ref[...], preferred_element_type=F32).astype(BF16)
    kr = pltpu.roll(_blk(z, 15), D_NOPE, 1)
    hd = D_NOPE + D_ROPE
    for c in range(4):
        sl = slice(c * LANE, (c + 1) * LANE)
        qd_ref[0, :, sl] = (_rope(_lane_norm(_blk(qd, c), hd, gqd_ref[...]), cosd, sind, even) * d_scale).astype(BF16)
        kd_ref[0, :, sl] = _rope(_lane_norm(_blk(kn, c) + kr, hd, gkd_ref[...]), cosd, sind, even).astype(BF16)


def _inproj_odd(x, mod, g, w, cos, sin, cosd, sind, consts, n_lat_tiles):
    bsz, t, d = x.shape
    tm = ROW_TILE
    tab = pl.BlockSpec((tm, LANE), lambda b, i: (i, 0))
    return pl.pallas_call(
        _inproj_odd_kernel,
        grid=(bsz, t // tm),
        in_specs=(_row_specs(tm, d, bsz, n_lat_tiles) + [_const_spec(w), tab, tab, tab, tab]
                  + [_const_spec(a) for a in consts]),
        out_specs=[pl.BlockSpec((1, tm, HALF), lambda b, i: (b, i, 0)) for _ in range(6)],
        out_shape=[jax.ShapeDtypeStruct((bsz, t, HALF), BF16) for _ in range(6)],
        compiler_params=_cparams(("parallel", "parallel")),
        name="inproj_odd",
    )(x, mod, g.reshape(1, d), w, cos, sin, cosd, sind, *consts)


def _attn_kernel(*refs, mode, tq, tk, n_tiles):
    if mode == "diff":
        q_ref, k_ref, v_ref, lam_ref, gsub_ref, o_ref, q_sc, s_sc, m_sc, a_sc, l_sc, acc_sc = refs
    else:
        q_ref, k_ref, v_ref = refs[:3]
        o_ref, q_sc, s_sc, m_sc, a_sc, l_sc, acc_sc = refs[-7:]
    rows = q_sc.shape[0]
    n_blk = tk // LANE
    lo = lax.broadcasted_iota(jnp.int32, (tq, LANE), 1) < 64

    if mode == "mla":
        q_sc[...] = q_ref[0]
    else:
        for p in range(q_ref.shape[2] // LANE):
            qb = _blk(q_ref[0], p)
            q_sc[2 * p * tq:(2 * p + 1) * tq] = jnp.where(lo, qb, jnp.zeros_like(qb))
            q_sc[(2 * p + 1) * tq:(2 * p + 2) * tq] = jnp.where(lo, jnp.zeros_like(qb), qb)

    def score_stage(j, slot):
        start = pl.multiple_of(j * tk, LANE)
        s = lax.dot_general(q_sc[...], k_ref[0, pl.ds(start, tk), :], (((1,), (1,)), ((), ())),
                            preferred_element_type=F32)
        s_sc[slot] = s
        m_tile = jnp.max(functools.reduce(jnp.maximum, [_blk(s, c) for c in range(n_blk)]), axis=1, keepdims=True)
        m_prev = m_sc[1 - slot]
        m_new = jnp.maximum(m_prev, m_tile)
        m_sc[slot] = m_new
        a_sc[slot] = jnp.exp2(m_prev - m_new)

    def value_stage(j, slot):
        start = pl.multiple_of(j * tk, LANE)
        m = m_sc[slot]
        alpha = a_sc[slot]
        p_blocks = [jnp.exp2((s_sc[slot, :, c * LANE:(c + 1) * LANE] - m).astype(BF16)) for c in range(n_blk)]
        l_sc[...] = alpha * l_sc[...] + functools.reduce(jnp.add, p_blocks).astype(F32)
        p = jnp.concatenate(p_blocks, axis=1)
        acc_sc[...] = alpha * acc_sc[...] + jnp.dot(p, v_ref[0, pl.ds(start, tk), :], preferred_element_type=F32)

    m_sc[1] = jnp.full((rows, LANE), NEG_BIG, F32)
    l_sc[...] = jnp.zeros((rows, LANE), F32)
    acc_sc[...] = jnp.zeros((rows, LANE), F32)
    score_stage(0, 0)
    peel = (n_tiles - 1) % 2
    if peel:
        value_stage(0, 0)
        score_stage(1, 1)

    def body(t, carry):
        j = 2 * t + peel
        value_stage(j, peel)
        score_stage(j + 1, 1 - peel)
        value_stage(j + 1, 1 - peel)
        score_stage(j + 2, peel)
        return carry

    lax.fori_loop(0, (n_tiles - 1) // 2, body, 0)
    value_stage(n_tiles - 1, (n_tiles - 1) % 2)

    o = acc_sc[...] / jnp.sum(l_sc[...], axis=1, keepdims=True)
    head = lambda h: o[h * tq:(h + 1) * tq]
    if mode == "gqa":
        for p in range(o_ref.shape[2] // LANE):
            o_ref[0, :, p * LANE:(p + 1) * LANE] = jnp.where(lo, head(2 * p), head(2 * p + 1)).astype(o_ref.dtype)
    elif mode == "diff":
        dlt = head(0) - lam_ref[...] * head(1)
        o_ref[0] = (dlt * lax.rsqrt(jnp.mean(dlt * dlt, axis=1, keepdims=True) + EPS) * gsub_ref[...]).astype(o_ref.dtype)
    else:
        o_ref[0] = o.astype(o_ref.dtype)


def _attn(q, k, v, *, mode, tq, tk, q_rows, k_rows, extra=(), into=None):
    bsz, t_q, wq = q.shape
    qw = 2 * LANE if mode == "gqa" else LANE
    rows = {"gqa": 4, "diff": 2, "mla": 1}[mode] * tq
    q0, nq = q_rows
    k0, nk = k_rows
    tk = min(tk, nk)
    assert q0 % tq == 0 and nq % tq == 0 and k0 % nk == 0 and nk % tk == 0
    qmap = lambda b, g, i: (b, q0 // tq + i, g)
    kmap = lambda b, g, i: (b, k0 // nk, g)
    in_specs = [pl.BlockSpec((1, tq, qw), qmap), pl.BlockSpec((1, nk, LANE), kmap), pl.BlockSpec((1, nk, LANE), kmap)]
    in_specs += [pl.BlockSpec(a.shape, lambda b, g, i: (0, 0)) for a in extra]
    args = [q, k, v, *extra]
    aliases = {}
    if into is not None:
        in_specs.append(pl.BlockSpec(memory_space=pl.ANY))
        aliases = {len(args): 0}
        args.append(into)

    def kern(*refs):
        if into is not None:
            refs = refs[:len(args) - 1] + refs[len(args):]
        _attn_kernel(*refs, mode=mode, tq=tq, tk=tk, n_tiles=nk // tk)

    return pl.pallas_call(
        kern,
        grid=(bsz, wq // qw, nq // tq),
        in_specs=in_specs,
        out_specs=pl.BlockSpec((1, tq, qw), qmap),
        out_shape=jax.ShapeDtypeStruct((bsz, t_q, wq), BF16),
        scratch_shapes=[pltpu.VMEM((rows, LANE), BF16), pltpu.VMEM((2, rows, tk), F32),
                        pltpu.VMEM((2, rows, LANE), F32), pltpu.VMEM((2, rows, LANE), F32),
                        pltpu.VMEM((rows, LANE), F32), pltpu.VMEM((rows, LANE), F32)],
        input_output_aliases=aliases,
        compiler_params=_cparams(("parallel", "parallel", "parallel")),
        name="attn_" + mode,
    )(*args)


def _mlstm_kernel(qf_ref, pf_ref, nf_ref, vf_ref, gf_ref, qb_ref, pb_ref, nb_ref, vb_ref, gb_ref, cw_ref, cb_ref,
                  hf_ref, hb_ref, c_sc, m_sc, *, n_lat_chunks, n_chunks):
    L, dh = B_CHUNK, B_HEAD_DIM
    c = pl.program_id(1)
    n_ctx_chunks = n_chunks - n_lat_chunks
    fwd_chunk = jnp.where(c < n_ctx_chunks, n_lat_chunks + c, c - n_ctx_chunks)
    bwd_chunk = n_chunks - 1 - c

    @pl.when(c == 0)
    def _():
        c_sc[...] = jnp.zeros(c_sc.shape, F32)
        m_sc[...] = jnp.zeros(m_sc.shape, F32)

    row = lax.broadcasted_iota(jnp.int32, (L, L), 0)
    col = lax.broadcasted_iota(jnp.int32, (L, L), 1)
    upper = (row <= col).astype(F32)
    lower = (col <= row).astype(F32)
    rid = lax.broadcasted_iota(jnp.int32, (L, 1), 0)
    glane = lax.broadcasted_iota(jnp.int32, (L, LANE), 1)
    is_f = ((glane >= 4) & (glane < 8)) | ((glane >= 12) & (glane < 16))
    ones_col = (lax.broadcasted_iota(jnp.int32, (L, dh), 1) == 0).astype(BF16)
    hi = lax.Precision.HIGHEST

    def direction(qk_ref, p_ref, n_ref, v_ref, g_ref, h_ref, chunk, reverse):
        first = (chunk == 0) | (chunk == n_lat_chunks)
        last = (chunk == n_lat_chunks - 1) | (chunk == n_chunks - 1)
        x = qk_ref[0]
        prev_row = jnp.where(first, 0.0, p_ref[0, SUBLANE - 1:SUBLANE, :])
        next_row = jnp.where(last, 0.0, n_ref[0, 0:1, :])
        x_prev = jnp.where(rid == 0, prev_row, pltpu.roll(x, 1, 0))
        x_next = jnp.where(rid == L - 1, next_row, pltpu.roll(x, L - 1, 0))
        y = x_prev * cw_ref[0:1, :] + x * cw_ref[1:2, :] + x_next * cw_ref[2:3, :] + cb_ref[...]
        y = y * jax.nn.sigmoid(y)
        g = g_ref[0]
        g = jnp.where(is_f, jnp.minimum(g, 0.0) - jnp.log(1.0 + jnp.exp(-jnp.abs(g))), g)
        g_t = jnp.transpose(g)
        cum_r = jnp.dot(g_t, upper, preferred_element_type=F32, precision=hi)
        cum_c = jnp.dot(lower, g, preferred_element_type=F32, precision=hi)
        if reverse:
            cum_r = cum_r[:, L - 1:L] - cum_r + g_t
            cum_c = cum_c[L - 1:L, :] - cum_c + g
        gi = 8 if reverse else 0
        mask = (col >= row) if reverse else (col <= row)
        for h in range(B_HEADS):
            idx = (4 if reverse else 0) + h
            q = _blk(y, h).astype(BF16)
            k = _blk(y, B_HEADS + h) * (dh ** -0.5)
            v_ext = jnp.concatenate([_blk(v_ref[0], h), ones_col], axis=1)
            i_r = g_t[gi + h:gi + h + 1, :]
            b_r = cum_r[gi + 4 + h:gi + 5 + h, :]
            i_c = g[:, gi + h:gi + h + 1]
            b_c = cum_c[:, gi + 4 + h:gi + 5 + h]
            m_prev = m_sc[idx][:, 0:1]
            c_ext = c_sc[idx]

            log_w = jnp.where(mask, b_c - b_r + i_r, -jnp.inf)
            log_inter = b_c + m_prev
            m_row = jnp.maximum(log_inter, jnp.max(log_w, axis=1, keepdims=True))
            w_intra = jnp.exp(log_w - m_row)
            a_inter = jnp.exp(log_inter - m_row)
            sc = lax.dot_general(q, k.astype(BF16), (((1,), (1,)), ((), ())), preferred_element_type=F32) * w_intra
            q_c = jnp.dot(q, c_ext.astype(BF16), preferred_element_type=F32)
            s_v = jnp.dot(sc.astype(BF16), v_ext, preferred_element_type=F32)
            num = a_inter * q_c[:, :dh] + s_v[:, :dh]
            den = a_inter * q_c[:, dh:dh + 1] + s_v[:, dh:dh + 1]
            h_ref[0, :, h * dh:(h + 1) * dh] = num / jnp.maximum(jnp.abs(den), jnp.exp(-m_row))

            b_last = b_r[:, 0:1] if reverse else b_r[:, L - 1:L]
            m_new = jnp.maximum(b_last + m_prev, jnp.max(b_last - b_r + i_r, axis=1, keepdims=True))
            g_c = jnp.exp(b_last - b_c + i_c - m_new)
            decay = jnp.exp(b_last + m_prev - m_new)
            upd = jnp.dot(jnp.transpose(k).astype(BF16), (g_c * v_ext.astype(F32)).astype(BF16),
                          preferred_element_type=F32)
            c_sc[idx] = decay * c_ext + upd
            m_sc[idx] = jnp.broadcast_to(m_new, (1, LANE))

    direction(qf_ref, pf_ref, nf_ref, vf_ref, gf_ref, hf_ref, fwd_chunk, False)
    direction(qb_ref, pb_ref, nb_ref, vb_ref, gb_ref, hb_ref, bwd_chunk, True)


def _mlstm(qk, vb, gates, conv_w, conv_b, n_lat):
    bsz, t, _ = qk.shape
    L = B_CHUNK
    n_chunks, n_lat_chunks = t // L, n_lat // L
    n_ctx_chunks = n_chunks - n_lat_chunks
    per = L // SUBLANE
    fwd = lambda c: jnp.where(c < n_ctx_chunks, n_lat_chunks + c, c - n_ctx_chunks)
    bwd = lambda c: n_chunks - 1 - c

    def specs(order):
        return [
            pl.BlockSpec((1, L, 2 * HALF), lambda b, c: (b, order(c), 0)),
            pl.BlockSpec((1, SUBLANE, 2 * HALF), lambda b, c: (b, jnp.maximum(order(c) * per - 1, 0), 0)),
            pl.BlockSpec((1, SUBLANE, 2 * HALF), lambda b, c: (b, jnp.minimum((order(c) + 1) * per, t // SUBLANE - 1), 0)),
            pl.BlockSpec((1, L, HALF), lambda b, c: (b, order(c), 0)),
            pl.BlockSpec((1, L, LANE), lambda b, c: (b, order(c), 0)),
        ]

    const = lambda a: pl.BlockSpec(a.shape, lambda b, c: (0, 0))
    return pl.pallas_call(
        functools.partial(_mlstm_kernel, n_lat_chunks=n_lat_chunks, n_chunks=n_chunks),
        grid=(bsz, n_chunks),
        in_specs=specs(fwd) + specs(bwd) + [const(conv_w), const(conv_b)],
        out_specs=[pl.BlockSpec((1, L, HALF), lambda b, c: (b, fwd(c), 0)),
                   pl.BlockSpec((1, L, HALF), lambda b, c: (b, bwd(c), 0))],
        out_shape=[jax.ShapeDtypeStruct((bsz, t, HALF), F32)] * 2,
        scratch_shapes=[pltpu.VMEM((2 * B_HEADS, B_HEAD_DIM, 2 * B_HEAD_DIM), F32),
                        pltpu.VMEM((2 * B_HEADS, 1, LANE), F32)],
        compiler_params=_cparams(("parallel", "arbitrary")),
        name="mlstm",
    )(qk, qk, qk, vb, gates, qk, qk, qk, vb, gates, conv_w, conv_b)


def _out_tail(mix, x_ref, mod_ref, g_ref, w_ref, wr_ref, xo_ref, f_ref, lg_ref):
    o = jnp.dot(mix, w_ref[...], preferred_element_type=F32)
    xn = x_ref[0] + mod_ref[0, 0:1, :] * o
    xo_ref[0] = xn
    y = xn * lax.rsqrt(jnp.mean(xn * xn, axis=-1, keepdims=True) + EPS) * g_ref[...]
    f = y * (1.0 + mod_ref[0, 2:3, :]) + mod_ref[0, 1:2, :]
    f_ref[0] = f.astype(BF16)
    lg_ref[0] = jnp.dot(f, wr_ref[...], preferred_element_type=F32, precision=lax.Precision.HIGHEST)


def _outproj_even_kernel(oa_ref, hf_ref, hb_ref, ob_ref, gh_ref, *tail):
    parts = [oa_ref[0]]
    for h in range(B_HEADS):
        hs = _blk(hf_ref[0], h) + _blk(hb_ref[0], h)
        hn = hs * lax.rsqrt(jnp.mean(hs * hs, axis=1, keepdims=True) + EPS) * _blk(gh_ref[...], h)
        parts.append((hn * jax.nn.sigmoid(_blk(ob_ref[0], h))).astype(BF16))
    _out_tail(jnp.concatenate(parts, axis=1), *tail)


def _outproj_odd_kernel(oc_ref, od_ref, *tail):
    _out_tail(jnp.concatenate([oc_ref[0], od_ref[0]], axis=1), *tail)


def _outproj(kern, heads, consts, x, mod, g, w, w_router, n_rows, n_lat_tiles):
    bsz, _, d = x.shape
    ne = w_router.shape[1]
    tm = ROW_TILE
    row = lambda b, i: (b, i, 0)
    return pl.pallas_call(
        kern,
        grid=(bsz, n_rows // tm),
        in_specs=([pl.BlockSpec((1, tm, HALF), row) for _ in heads] + [_const_spec(a) for a in consts]
                  + [pl.BlockSpec((1, tm, d), row),
                     pl.BlockSpec((1, 3, d), lambda b, i: (jnp.where(i >= n_lat_tiles, bsz, b), 0, 0)),
                     pl.BlockSpec((1, d), lambda b, i: (0, 0)), _const_spec(w), _const_spec(w_router)]),
        out_specs=[pl.BlockSpec((1, tm, d), row), pl.BlockSpec((1, tm, d), row), pl.BlockSpec((1, tm, ne), row)],
        out_shape=[jax.ShapeDtypeStruct((bsz, n_rows, d), F32), jax.ShapeDtypeStruct((bsz, n_rows, d), BF16),
                   jax.ShapeDtypeStruct((bsz, n_rows, ne), F32)],
        compiler_params=_cparams(("parallel", "parallel")),
        name="outproj",
    )(*heads, *consts, x, mod, g.reshape(1, d), w, w_router)


def _expert_kernel(blk_e_ref, n_used_ref, x_ref, w1_ref, w3_ref, w2_ref, y_ref, w1_sc, w3_sc, w2_sc):
    i = pl.program_id(0)

    @pl.when((i == 0) | (blk_e_ref[i] != blk_e_ref[jnp.maximum(i - 1, 0)]))
    def _():
        w1_sc[...] = w1_ref[0].astype(BF16)
        w3_sc[...] = w3_ref[0].astype(BF16)
        w2_sc[...] = w2_ref[0].astype(BF16)

    @pl.when(i < n_used_ref[0])
    def _():
        xb = x_ref[...]
        h1 = jnp.dot(xb, w1_sc[...], preferred_element_type=F32)
        h3 = jnp.dot(xb, w3_sc[...], preferred_element_type=F32)
        a = (h1 * jax.nn.sigmoid(h1) * h3).astype(BF16)
        y_ref[...] = jnp.dot(a, w2_sc[...], preferred_element_type=F32).astype(y_ref.dtype)

    @pl.when(i >= n_used_ref[0])
    def _():
        y_ref[...] = jnp.zeros(y_ref.shape, y_ref.dtype)


def _experts(xin, blk_e, n_used, w1, w3, w2):
    n_rows, d = xin.shape
    de = w1.shape[-1]
    nb = n_rows // MOE_ROWS
    grid_spec = pltpu.PrefetchScalarGridSpec(
        num_scalar_prefetch=2,
        grid=(nb,),
        in_specs=[
            pl.BlockSpec((MOE_ROWS, d), lambda i, be, nu: (i, 0)),
            pl.BlockSpec((1, d, de), lambda i, be, nu: (be[i], 0, 0)),
            pl.BlockSpec((1, d, de), lambda i, be, nu: (be[i], 0, 0)),
            pl.BlockSpec((1, de, d), lambda i, be, nu: (be[i], 0, 0)),
        ],
        out_specs=pl.BlockSpec((MOE_ROWS, d), lambda i, be, nu: (i, 0)),
        scratch_shapes=[pltpu.VMEM((d, de), BF16), pltpu.VMEM((d, de), BF16), pltpu.VMEM((de, d), BF16)],
    )
    return pl.pallas_call(
        _expert_kernel,
        grid_spec=grid_spec,
        out_shape=jax.ShapeDtypeStruct((n_rows, d), BF16),
        compiler_params=_cparams(("arbitrary",)),
        name="experts",
    )(blk_e, n_used, xin, w1, w3, w2)


def _moe(f_tok, logits, b_router, w1, w3, w2):
    n_tok, d = f_tok.shape
    per = N_EXPERTS // N_GROUPS
    scores = jax.nn.sigmoid(logits)
    sel = scores + b_router.astype(F32)

    def top2(a):
        i1 = jnp.argmax(a, axis=-1)
        hit = lax.broadcasted_iota(jnp.int32, a.shape, a.ndim - 1) == i1[..., None]
        rest = jnp.where(hit, -jnp.inf, a)
        return jnp.max(a, axis=-1), jnp.max(rest, axis=-1), i1, jnp.argmax(rest, axis=-1)

    g1, g2, _, _ = top2(sel.reshape(n_tok, N_GROUPS, per))
    g_idx = jnp.argmax(g1 + g2, axis=-1)
    in_group = (jnp.arange(N_EXPERTS) // per)[None, :] == g_idx[:, None]
    _, _, e1, e2 = top2(jnp.where(in_group, sel, -jnp.inf))
    e_idx = jnp.stack([e1, e2], axis=-1)
    gate = jnp.take_along_axis(scores, e_idx, axis=-1)
    gate = gate / jnp.sum(gate, axis=-1, keepdims=True)

    n_assign = n_tok * TOP_K
    e_flat = e_idx.reshape(-1).astype(jnp.int32)
    tok_flat = jnp.repeat(jnp.arange(n_tok, dtype=jnp.int32), TOP_K)
    onehot = (e_flat[:, None] == jnp.arange(N_EXPERTS, dtype=jnp.int32)[None, :]).astype(jnp.int32)
    csum = jnp.cumsum(onehot, axis=0)
    counts = csum[-1]
    rank = jnp.take_along_axis(csum, e_flat[:, None], axis=1)[:, 0] - 1
    padded = (counts + MOE_ROWS - 1) // MOE_ROWS * MOE_ROWS
    pend = jnp.cumsum(padded)
    pstart = pend - padded
    dest = pstart[e_flat] + rank
    n_rows = -(-n_assign // MOE_ROWS) * MOE_ROWS + N_EXPERTS * MOE_ROWS
    nb = n_rows // MOE_ROWS
    row_tok = jnp.full((n_rows,), n_tok, jnp.int32).at[dest].set(tok_flat)
    xin = jnp.concatenate([f_tok, jnp.zeros((1, d), f_tok.dtype)], axis=0)[row_tok]
    blk_e = jnp.minimum(jnp.searchsorted(pend, jnp.arange(nb, dtype=jnp.int32) * MOE_ROWS, side='right'),
                        N_EXPERTS - 1).astype(jnp.int32)
    n_used = (pend[-1:] // MOE_ROWS).astype(jnp.int32)
    y = _experts(xin, blk_e, n_used, w1, w3, w2)
    yk = y[dest].reshape(n_tok, TOP_K, d).astype(F32)
    return jnp.einsum('tk,tkd->td', gate, yk)


def _rope_lanes(n_lat, n_ctx, rot_dim):
    rows = n_lat // GRID_W
    r = jnp.repeat(jnp.arange(rows), GRID_W)
    c = jnp.tile(jnp.arange(GRID_W), rows)
    n_freq = rot_dim // 4
    inv = ROPE_THETA ** (-jnp.arange(n_freq, dtype=F32) / n_freq)
    ang = jnp.concatenate([r[:, None] * inv, c[:, None] * inv], axis=-1)
    cos = jnp.repeat(jnp.cos(ang), 2, axis=1)
    sin = jnp.repeat(jnp.sin(ang), 2, axis=1) * jnp.tile(jnp.array([-1.0, 1.0], F32), rot_dim // 2)
    pad = ((0, n_ctx), (0, 0))
    return jnp.pad(cos, pad, constant_values=1.0), jnp.pad(sin, pad)


def _pad_cols(w, n):
    return jnp.pad(w, ((0, 0), (0, n - w.shape[1])))


def _pad_row(v, n):
    return jnp.pad(v, (0, n - v.shape[0])).reshape(1, n)


def _round_up(n, m):
    return -(-n // m) * m


def _head_blocks(w, n_heads, width, start, size):
    w = w.reshape(w.shape[0], n_heads, width)[:, :, start:start + size]
    return jnp.pad(w, ((0, 0), (0, 0), (0, LANE - size))).reshape(w.shape[0], n_heads * LANE)


def kernel(x, c, ctx, c_ctx, w_ada, b_ada, g_mix, g_ffn, e_w_in, e_w_out, e_g_q, e_g_k, e_conv_w, e_conv_b, e_b_gates, e_g_h, o_w_in, o_w_out, o_g_qc, o_g_kc, o_lam, o_g_sub, o_g_cq, o_w_uq, o_g_ckv, o_w_ukv, o_g_qd, o_g_kd, w_router, b_router, w1, w3, w2):
    bsz, n_lat, d = x.shape
    n_ctx = ctx.shape[1]
    t = n_lat + n_ctx
    depth = w_ada.shape[0]
    assert n_lat % ROW_TILE == 0 and n_ctx % ROW_TILE == 0 and n_ctx % B_CHUNK == 0
    n_lat_tiles = n_lat // ROW_TILE
    hd = D_NOPE + D_ROPE

    cos64, sin64 = _rope_lanes(n_lat, n_ctx, A_HEAD_DIM)
    cos64, sin64 = jnp.tile(cos64, (1, 2)), jnp.tile(sin64, (1, 2))
    cos_d, sin_d = _rope_lanes(n_lat, n_ctx, D_ROPE)
    cos_d = jnp.pad(cos_d, ((0, 0), (D_NOPE, LANE - hd)), constant_values=1.0)
    sin_d = jnp.pad(sin_d, ((0, 0), (D_NOPE, LANE - hd)))
    half = jnp.arange(LANE) // 64
    bd64 = (half[:, None] == half[None, :]).astype(BF16)
    w_router_f = w_router.astype(F32)
    cond = jnp.concatenate([c, c_ctx[None, :]], axis=0)
    x_all = jnp.concatenate([x, ctx], axis=1)

    for layer in range(depth):
        last = layer == depth - 1
        j = layer // 2
        mod = (jax.nn.silu(cond) @ w_ada[layer] + b_ada[layer]).reshape(bsz + 1, 6, d)
        sh1, sc1, g1, sh2, sc2, g2 = [mod[:, i] for i in range(6)]
        mod_in = jnp.stack([sh1, sc1], axis=1)
        mod_out = jnp.stack([g1, sh2, sc2], axis=1)
        n_rows = n_lat if last else t
        if layer % 2 == 0:
            w_in = _pad_cols(e_w_in[j], _round_up(e_w_in.shape[-1], LANE)).astype(BF16)
            q, kd, vd, qk, vb, ob, gates = _inproj_even(
                x_all, mod_in, g_mix[layer], w_in, cos64, sin64, bd64, jnp.tile(e_g_q[j], 2).reshape(1, LANE),
                jnp.tile(e_g_k[j], 2).reshape(1, LANE), _pad_row(e_b_gates[j], LANE), n_lat_tiles)
            oa = _attn(q, kd, vd, mode="gqa", tq=256, tk=KEY_TILE, q_rows=(0, n_lat), k_rows=(0, t))
            if not last:
                oa = _attn(q, kd, vd, mode="gqa", tq=256, tk=KEY_TILE, q_rows=(n_lat, n_ctx), k_rows=(n_lat, n_ctx),
                           into=oa)
            hf, hb = _mlstm(qk, vb, gates, e_conv_w[j], e_conv_b[j].reshape(1, -1), n_lat)
            heads, consts, kern = [oa, hf, hb, ob], [e_g_h[j].reshape(1, HALF)], _outproj_even_kernel
            w_out = e_w_out[j]
        else:
            assert last, "the odd mixer is only wired as the final layer"
            lam_init = LAM_INIT_BASE[0] - LAM_INIT_BASE[1] * math.exp(-LAM_INIT_BASE[2] * layer)
            lq1, lk1, lq2, lk2 = o_lam[j].astype(F32)
            lam = jnp.exp(jnp.sum(lq1 * lk1)) - jnp.exp(jnp.sum(lq2 * lk2)) + lam_init
            w_in = _pad_cols(o_w_in[j], _round_up(o_w_in.shape[-1], LANE)).astype(BF16)
            consts = [bd64, jnp.tile(o_g_qc[j], 2).reshape(1, LANE), jnp.tile(o_g_kc[j], 2).reshape(1, LANE),
                      o_g_cq[j].reshape(1, D_Q_LORA), _head_blocks(o_w_uq[j], D_HEADS, hd, 0, hd).astype(BF16),
                      _pad_row(o_g_qd[j], LANE), o_g_ckv[j].reshape(1, D_KV_LORA),
                      _head_blocks(o_w_ukv[j], D_HEADS, D_NOPE + D_V_DIM, 0, D_NOPE).astype(BF16),
                      _head_blocks(o_w_ukv[j], D_HEADS, D_NOPE + D_V_DIM, D_NOPE, D_V_DIM).astype(BF16),
                      _pad_row(o_g_kd[j], LANE)]
            qc, kc, vc, qd, kdl, vdl = _inproj_odd(x_all, mod_in, g_mix[layer], w_in, cos64, sin64, cos_d, sin_d,
                                                   consts, n_lat_tiles)
            extra = (jnp.full((1, LANE), lam, F32), (o_g_sub[j] * (1.0 - lam_init)).reshape(1, LANE))
            oc = _attn(qc, kc, vc, mode="diff", tq=512, tk=KEY_TILE, q_rows=(0, n_lat), k_rows=(0, t), extra=extra)
            od = _attn(qd, kdl, vdl, mode="mla", tq=1024, tk=KEY_TILE, q_rows=(0, n_lat), k_rows=(0, t))
            heads, consts, kern = [oc, od], [], _outproj_odd_kernel
            w_out = o_w_out[j]
        x_new, f, logits = _outproj(kern, heads, consts, x_all, mod_out, g_ffn[layer], w_out.astype(BF16), w_router_f,
                                    n_rows, n_lat_tiles)
        y = _moe(f.reshape(bsz * n_rows, d), logits.reshape(bsz * n_rows, N_EXPERTS), b_router,
                 w1[layer], w3[layer], w2[layer])
        g2_rows = jnp.broadcast_to(g2[:bsz, None, :], (bsz, n_rows, d))
        if not last:
            g2_rows = g2_rows.at[:, n_lat:].set(g2[bsz])
        x_all = x_new + g2_rows * y.reshape(bsz, n_rows, d)
    return x_all[:, :n_lat]
```

```python
import functools
import math

import jax
import jax.numpy as jnp
import numpy as np
from jax import lax
from jax.experimental import pallas as pl
from jax.experimental.pallas import tpu as pltpu

F32 = jnp.float32
BF16 = jnp.bfloat16
EPS = 1e-6
LOG2E = 1.4426950408889634
NEG_BIG = -1e30
GRID_W = 64
ROPE_THETA = 10000.0
LAM_INIT_BASE = (0.8, 0.6, 0.3)

A_HEAD_DIM, A_HEADS, A_KV_HEADS = 64, 8, 2
B_HEAD_DIM, B_HEADS, B_CONV, B_CHUNK = 128, 4, 3, 128
C_HEAD_DIM, C_V_DIM, C_HEADS = 64, 128, 4
D_HEADS, D_Q_LORA, D_KV_LORA, D_NOPE, D_ROPE, D_V_DIM = 4, 256, 128, 64, 32, 128
N_EXPERTS, N_GROUPS, TOP_K, D_EXPERT = 32, 4, 2, 512
HALF = 512

LANE = 128
SUBLANE = 8
ROW_TILE = 256
MOE_ROWS = 256
KEY_TILE = 1280
ATTN_UNROLL = 2
VMEM_LIMIT = 48 * 1024 * 1024


def _cparams(sem):
    return pltpu.CompilerParams(dimension_semantics=sem, vmem_limit_bytes=VMEM_LIMIT)


def _blk(a, i):
    return a[:, i * LANE:(i + 1) * LANE]


def _modulated(x_ref, mod_ref, g_ref):
    x = x_ref[0]
    y = x * lax.rsqrt(jnp.mean(x * x, axis=-1, keepdims=True) + EPS) * g_ref[...]
    return (y * (1.0 + mod_ref[0, 1:2, :]) + mod_ref[0, 0:1, :]).astype(BF16)


def _pairswap(b, even):
    return jnp.where(even, pltpu.roll(b, LANE - 1, 1), pltpu.roll(b, 1, 1))


def _rope(b, cos, sin, even):
    return b * cos + _pairswap(b, even) * sin


def _head64_norm(b, bd, gain):
    ss = jnp.dot((b * b).astype(BF16), bd, preferred_element_type=F32)
    return b * lax.rsqrt(ss * (1.0 / 64) + EPS) * gain


def _lane_norm(b, n_real, gain):
    ss = jnp.sum(b * b, axis=1, keepdims=True)
    return b * lax.rsqrt(ss * (1.0 / n_real) + EPS) * gain


def _inproj_even_kernel(x_ref, mod_ref, g_ref, w_ref, cos_ref, sin_ref, bd_ref, gq_ref, gk_ref, bg_ref,
                        q_ref, k_ref, v_ref, qk_ref, vb_ref, ob_ref, gt_ref):
    z = jnp.dot(_modulated(x_ref, mod_ref, g_ref), w_ref[...], preferred_element_type=F32)
    lane = lax.broadcasted_iota(jnp.int32, (z.shape[0], LANE), 1)
    even = (lane & 1) == 0
    lo = lane < 64
    cos, sin, bd = cos_ref[...], sin_ref[...], bd_ref[...]
    q_scale = A_HEAD_DIM ** -0.5 * LOG2E
    for c in range(4):
        qb = _rope(_head64_norm(_blk(z, c), bd, gq_ref[...]), cos, sin, even) * q_scale
        q_ref[0, :, c * LANE:(c + 1) * LANE] = qb.astype(BF16)
    kn = _rope(_head64_norm(_blk(z, 4), bd, gk_ref[...]), cos, sin, even)
    for src, ref in ((kn, k_ref), (_blk(z, 5), v_ref)):
        sw = pltpu.roll(src, 64, 1)
        ref[0] = jnp.concatenate([jnp.where(lo, src, sw), jnp.where(lo, sw, src)], axis=1).astype(BF16)
    qk_ref[0] = z[:, 6 * LANE:14 * LANE]
    vb_ref[0] = z[:, 14 * LANE:18 * LANE].astype(BF16)
    ob_ref[0] = z[:, 18 * LANE:22 * LANE]
    gt_ref[0] = _blk(z, 22) + bg_ref[...]


def _row_specs(tm, d, bsz, n_lat_tiles):
    return [pl.BlockSpec((1, tm, d), lambda b, i: (b, i, 0)),
            pl.BlockSpec((1, 2, d), lambda b, i: (jnp.where(i >= n_lat_tiles, bsz, b), 0, 0)),
            pl.BlockSpec((1, d), lambda b, i: (0, 0))]


def _const_spec(a):
    return pl.BlockSpec(a.shape, lambda b, i: (0,) * a.ndim)


def _inproj_even(x, mod, g, w, cos, sin, bd, gq, gk, bg, n_lat_tiles):
    bsz, t, d = x.shape
    tm = ROW_TILE
    tab = pl.BlockSpec((tm, LANE), lambda b, i: (i, 0))
    consts = [bd, gq, gk, bg]
    widths = [(4 * LANE, BF16), (2 * LANE, BF16), (2 * LANE, BF16), (8 * LANE, F32), (4 * LANE, BF16),
              (4 * LANE, F32), (LANE, F32)]
    return pl.pallas_call(
        _inproj_even_kernel,
        grid=(bsz, t // tm),
        in_specs=_row_specs(tm, d, bsz, n_lat_tiles) + [_const_spec(w), tab, tab] + [_const_spec(a) for a in consts],
        out_specs=[pl.BlockSpec((1, tm, n), lambda b, i: (b, i, 0)) for n, _ in widths],
        out_shape=[jax.ShapeDtypeStruct((bsz, t, n), dt) for n, dt in widths],
        compiler_params=_cparams(("parallel", "parallel")),
        name="inproj_even",
    )(x, mod, g.reshape(1, d), w, cos, sin, *consts)


def _inproj_odd_kernel(x_ref, mod_ref, g_ref, w_ref, cos_ref, sin_ref, cosd_ref, sind_ref, bd_ref, gqc_ref, gkc_ref,
                       gcq_ref, wuq_ref, gqd_ref, gckv_ref, wkn_ref, wvd_ref, gkd_ref,
                       qc_ref, kc_ref, vc_ref, qd_ref, kd_ref, vd_ref):
    z = jnp.dot(_modulated(x_ref, mod_ref, g_ref), w_ref[...], preferred_element_type=F32)
    lane = lax.broadcasted_iota(jnp.int32, (z.shape[0], LANE), 1)
    even = (lane & 1) == 0
    cos, sin, bd = cos_ref[...], sin_ref[...], bd_ref[...]
    cosd, sind = cosd_ref[...], sind_ref[...]
    c_scale = C_HEAD_DIM ** -0.5 * LOG2E
    d_scale = (D_NOPE + D_ROPE) ** -0.5 * LOG2E
    for c in range(4):
        sl = slice(c * LANE, (c + 1) * LANE)
        qc_ref[0, :, sl] = (_rope(_head64_norm(_blk(z, c), bd, gqc_ref[...]), cos, sin, even) * c_scale).astype(BF16)
        kc_ref[0, :, sl] = _rope(_head64_norm(_blk(z, 4 + c), bd, gkc_ref[...]), cos, sin, even).astype(BF16)
    vc_ref[0] = z[:, 8 * LANE:12 * LANE].astype(BF16)
    cq = z[:, 12 * LANE:14 * LANE]
    cq = cq * lax.rsqrt(jnp.mean(cq * cq, axis=-1, keepdims=True) + EPS) * gcq_ref[...]
    qd = jnp.dot(cq.astype(BF16), wuq_ref[...], preferred_element_type=F32)
    ckv = _blk(z, 14)
    ckv = (ckv * lax.rsqrt(jnp.mean(ckv * ckv, axis=-1, keepdims=True) + EPS) * gckv_ref[...]).astype(BF16)
    kn = jnp.dot(ckv, wkn_ref[...], preferred_element_type=F32)
    vd_ref[0] = jnp.dot(ckv, wvd_ref[...], preferred_element_type=F32).astype(BF16)
    kr = pltpu.roll(_blk(z, 15), D_NOPE, 1)
    hd = D_NOPE + D_ROPE
    for c in range(4):
        sl = slice(c * LANE, (c + 1) * LANE)
        qd_ref[0, :, sl] = (_rope(_lane_norm(_blk(qd, c), hd, gqd_ref[...]), cosd, sind, even) * d_scale).astype(BF16)
        kd_ref[0, :, sl] = _rope(_lane_norm(_blk(kn, c) + kr, hd, gkd_ref[...]), cosd, sind, even).astype(BF16)


def _inproj_odd(x, mod, g, w, cos, sin, cosd, sind, consts, n_lat_tiles):
    bsz, t, d = x.shape
    tm = ROW_TILE
    tab = pl.BlockSpec((tm, LANE), lambda b, i: (i, 0))
    return pl.pallas_call(
        _inproj_odd_kernel,
        grid=(bsz, t // tm),
        in_specs=(_row_specs(tm, d, bsz, n_lat_tiles) + [_const_spec(w), tab, tab, tab, tab]
                  + [_const_spec(a) for a in consts]),
        out_specs=[pl.BlockSpec((1, tm, HALF), lambda b, i: (b, i, 0)) for _ in range(6)],
        out_shape=[jax.ShapeDtypeStruct((bsz, t, HALF), BF16) for _ in range(6)],
        compiler_params=_cparams(("parallel", "parallel")),
        name="inproj_odd",
    )(x, mod, g.reshape(1, d), w, cos, sin, cosd, sind, *consts)


def _attn_kernel(*refs, mode, tq, tk, n_tiles):
    if mode == "diff":
        q_ref, k_ref, v_ref, lam_ref, gsub_ref, o_ref, q_sc, s_sc, m_sc, a_sc, l_sc, acc_sc = refs
    else:
        q_ref, k_ref, v_ref = refs[:3]
        o_ref, q_sc, s_sc, m_sc, a_sc, l_sc, acc_sc = refs[-7:]
    rows = q_sc.shape[0]
    n_blk = tk // LANE
    lo = lax.broadcasted_iota(jnp.int32, (tq, LANE), 1) < 64

    if mode == "mla":
        q_sc[...] = q_ref[0]
    else:
        for p in range(q_ref.shape[2] // LANE):
            qb = _blk(q_ref[0], p)
            q_sc[2 * p * tq:(2 * p + 1) * tq] = jnp.where(lo, qb, jnp.zeros_like(qb))
            q_sc[(2 * p + 1) * tq:(2 * p + 2) * tq] = jnp.where(lo, jnp.zeros_like(qb), qb)

    def score_stage(j, slot):
        start = pl.multiple_of(j * tk, LANE)
        s = lax.dot_general(q_sc[...], k_ref[0, pl.ds(start, tk), :], (((1,), (1,)), ((), ())),
                            preferred_element_type=F32)
        s_sc[slot] = s
        m_tile = jnp.max(functools.reduce(jnp.maximum, [_blk(s, c) for c in range(n_blk)]), axis=1, keepdims=True)
        m_prev = m_sc[1 - slot]
        m_new = jnp.maximum(m_prev, m_tile)
        m_sc[slot] = m_new
        a_sc[slot] = jnp.exp2(m_prev - m_new)

    def value_stage(j, slot):
        start = pl.multiple_of(j * tk, LANE)
        m = m_sc[slot]
        alpha = a_sc[slot]
        p_blocks = [jnp.exp2((s_sc[slot, :, c * LANE:(c + 1) * LANE] - m).astype(BF16)) for c in range(n_blk)]
        l_sc[...] = alpha * l_sc[...] + functools.reduce(jnp.add, p_blocks).astype(F32)
        p = jnp.concatenate(p_blocks, axis=1)
        acc_sc[...] = alpha * acc_sc[...] + jnp.dot(p, v_ref[0, pl.ds(start, tk), :], preferred_element_type=F32)

    m_sc[1] = jnp.full((rows, LANE), NEG_BIG, F32)
    l_sc[...] = jnp.zeros((rows, LANE), F32)
    acc_sc[...] = jnp.zeros((rows, LANE), F32)
    score_stage(0, 0)
    peel = (n_tiles - 1) % ATTN_UNROLL
    for j in range(peel):
        value_stage(j, j % 2)
        score_stage(j + 1, (j + 1) % 2)

    def body(t, carry):
        for u in range(ATTN_UNROLL):
            j = ATTN_UNROLL * t + peel + u
            value_stage(j, (peel + u) % 2)
            score_stage(j + 1, (peel + u + 1) % 2)
        return carry

    lax.fori_loop(0, (n_tiles - 1) // ATTN_UNROLL, body, 0)
    value_stage(n_tiles - 1, (n_tiles - 1) % 2)

    o = acc_sc[...] / jnp.sum(l_sc[...], axis=1, keepdims=True)
    head = lambda h: o[h * tq:(h + 1) * tq]
    if mode == "gqa":
        for p in range(o_ref.shape[2] // LANE):
            o_ref[0, :, p * LANE:(p + 1) * LANE] = jnp.where(lo, head(2 * p), head(2 * p + 1)).astype(o_ref.dtype)
    elif mode == "diff":
        dlt = head(0) - lam_ref[...] * head(1)
        o_ref[0] = (dlt * lax.rsqrt(jnp.mean(dlt * dlt, axis=1, keepdims=True) + EPS) * gsub_ref[...]).astype(o_ref.dtype)
    else:
        o_ref[0] = o.astype(o_ref.dtype)


def _attn(q, k, v, *, mode, tq, tk, q_rows, k_rows, extra=(), into=None):
    bsz, t_q, wq = q.shape
    qw = 2 * LANE if mode == "gqa" else LANE
    rows = {"gqa": 4, "diff": 2, "mla": 1}[mode] * tq
    q0, nq = q_rows
    k0, nk = k_rows
    tk = min(tk, nk)
    assert q0 % tq == 0 and nq % tq == 0 and k0 % nk == 0 and nk % tk == 0
    qmap = lambda b, g, i: (b, q0 // tq + i, g)
    kmap = lambda b, g, i: (b, k0 // nk, g)
    in_specs = [pl.BlockSpec((1, tq, qw), qmap), pl.BlockSpec((1, nk, LANE), kmap), pl.BlockSpec((1, nk, LANE), kmap)]
    in_specs += [pl.BlockSpec(a.shape, lambda b, g, i: (0, 0)) for a in extra]
    args = [q, k, v, *extra]
    aliases = {}
    if into is not None:
        in_specs.append(pl.BlockSpec(memory_space=pl.ANY))
        aliases = {len(args): 0}
        args.append(into)

    def kern(*refs):
        if into is not None:
            refs = refs[:len(args) - 1] + refs[len(args):]
        _attn_kernel(*refs, mode=mode, tq=tq, tk=tk, n_tiles=nk // tk)

    return pl.pallas_call(
        kern,
        grid=(bsz, wq // qw, nq // tq),
        in_specs=in_specs,
        out_specs=pl.BlockSpec((1, tq, qw), qmap),
        out_shape=jax.ShapeDtypeStruct((bsz, t_q, wq), BF16),
        scratch_shapes=[pltpu.VMEM((rows, LANE), BF16), pltpu.VMEM((2, rows, tk), F32),
                        pltpu.VMEM((2, rows, LANE), F32), pltpu.VMEM((2, rows, LANE), F32),
                        pltpu.VMEM((rows, LANE), F32), pltpu.VMEM((rows, LANE), F32)],
        input_output_aliases=aliases,
        compiler_params=_cparams(("parallel", "parallel", "parallel")),
        name="attn_" + mode,
    )(*args)


def _mlstm_kernel(qf_ref, pf_ref, nf_ref, vf_ref, gf_ref, qb_ref, pb_ref, nb_ref, vb_ref, gb_ref, cw_ref, cb_ref,
                  hf_ref, hb_ref, c_sc, m_sc, *, n_lat_chunks, n_chunks):
    L, dh = B_CHUNK, B_HEAD_DIM
    c = pl.program_id(1)
    n_ctx_chunks = n_chunks - n_lat_chunks
    fwd_chunk = jnp.where(c < n_ctx_chunks, n_lat_chunks + c, c - n_ctx_chunks)
    bwd_chunk = n_chunks - 1 - c

    @pl.when(c == 0)
    def _():
        c_sc[...] = jnp.zeros(c_sc.shape, F32)
        m_sc[...] = jnp.zeros(m_sc.shape, F32)

    row = lax.broadcasted_iota(jnp.int32, (L, L), 0)
    col = lax.broadcasted_iota(jnp.int32, (L, L), 1)
    upper = (row <= col).astype(F32)
    lower = (col <= row).astype(F32)
    rid = lax.broadcasted_iota(jnp.int32, (L, 1), 0)
    glane = lax.broadcasted_iota(jnp.int32, (L, LANE), 1)
    is_f = ((glane >= 4) & (glane < 8)) | ((glane >= 12) & (glane < 16))
    ones_col = (lax.broadcasted_iota(jnp.int32, (L, dh), 1) == 0).astype(BF16)
    hi = lax.Precision.HIGHEST

    def direction(qk_ref, p_ref, n_ref, v_ref, g_ref, h_ref, chunk, reverse):
        first = (chunk == 0) | (chunk == n_lat_chunks)
        last = (chunk == n_lat_chunks - 1) | (chunk == n_chunks - 1)
        x = qk_ref[0]
        prev_row = jnp.where(first, 0.0, p_ref[0, SUBLANE - 1:SUBLANE, :])
        next_row = jnp.where(last, 0.0, n_ref[0, 0:1, :])
        x_prev = jnp.where(rid == 0, prev_row, pltpu.roll(x, 1, 0))
        x_next = jnp.where(rid == L - 1, next_row, pltpu.roll(x, L - 1, 0))
        y = x_prev * cw_ref[0:1, :] + x * cw_ref[1:2, :] + x_next * cw_ref[2:3, :] + cb_ref[...]
        y = y * jax.nn.sigmoid(y)
        g = g_ref[0]
        g = jnp.where(is_f, jnp.minimum(g, 0.0) - jnp.log(1.0 + jnp.exp(-jnp.abs(g))), g)
        g_t = jnp.transpose(g)
        cum_r = jnp.dot(g_t, upper, preferred_element_type=F32, precision=hi)
        cum_c = jnp.dot(lower, g, preferred_element_type=F32, precision=hi)
        if reverse:
            cum_r = cum_r[:, L - 1:L] - cum_r + g_t
            cum_c = cum_c[L - 1:L, :] - cum_c + g
        gi = 8 if reverse else 0
        mask = (col >= row) if reverse else (col <= row)
        for h in range(B_HEADS):
            idx = (4 if reverse else 0) + h
            q = _blk(y, h).astype(BF16)
            k = _blk(y, B_HEADS + h) * (dh ** -0.5)
            v_ext = jnp.concatenate([_blk(v_ref[0], h), ones_col], axis=1)
            i_r = g_t[gi + h:gi + h + 1, :]
            b_r = cum_r[gi + 4 + h:gi + 5 + h, :]
            i_c = g[:, gi + h:gi + h + 1]
            b_c = cum_c[:, gi + 4 + h:gi + 5 + h]
            m_prev = m_sc[idx][:, 0:1]
            c_ext = c_sc[idx]

            log_w = jnp.where(mask, b_c - b_r + i_r, -jnp.inf)
            log_inter = b_c + m_prev
            m_row = jnp.maximum(log_inter, jnp.max(log_w, axis=1, keepdims=True))
            w_intra = jnp.exp(log_w - m_row)
            a_inter = jnp.exp(log_inter - m_row)
            sc = lax.dot_general(q, k.astype(BF16), (((1,), (1,)), ((), ())), preferred_element_type=F32) * w_intra
            q_c = jnp.dot(q, c_ext.astype(BF16), preferred_element_type=F32)
            s_v = jnp.dot(sc.astype(BF16), v_ext, preferred_element_type=F32)
            num = a_inter * q_c[:, :dh] + s_v[:, :dh]
            den = a_inter * q_c[:, dh:dh + 1] + s_v[:, dh:dh + 1]
            h_ref[0, :, h * dh:(h + 1) * dh] = num / jnp.maximum(jnp.abs(den), jnp.exp(-m_row))

            b_last = b_r[:, 0:1] if reverse else b_r[:, L - 1:L]
            m_new = jnp.maximum(b_last + m_prev, jnp.max(b_last - b_r + i_r, axis=1, keepdims=True))
            g_c = jnp.exp(b_last - b_c + i_c - m_new)
            decay = jnp.exp(b_last + m_prev - m_new)
            upd = jnp.dot(jnp.transpose(k).astype(BF16), (g_c * v_ext.astype(F32)).astype(BF16),
                          preferred_element_type=F32)
            c_sc[idx] = decay * c_ext + upd
            m_sc[idx] = jnp.broadcast_to(m_new, (1, LANE))

    direction(qf_ref, pf_ref, nf_ref, vf_ref, gf_ref, hf_ref, fwd_chunk, False)
    direction(qb_ref, pb_ref, nb_ref, vb_ref, gb_ref, hb_ref, bwd_chunk, True)


def _mlstm(qk, vb, gates, conv_w, conv_b, n_lat):
    bsz, t, _ = qk.shape
    L = B_CHUNK
    n_chunks, n_lat_chunks = t // L, n_lat // L
    n_ctx_chunks = n_chunks - n_lat_chunks
    per = L // SUBLANE
    fwd = lambda c: jnp.where(c < n_ctx_chunks, n_lat_chunks + c, c - n_ctx_chunks)
    bwd = lambda c: n_chunks - 1 - c

    def specs(order):
        return [
            pl.BlockSpec((1, L, 2 * HALF), lambda b, c: (b, order(c), 0)),
            pl.BlockSpec((1, SUBLANE, 2 * HALF), lambda b, c: (b, jnp.maximum(order(c) * per - 1, 0), 0)),
            pl.BlockSpec((1, SUBLANE, 2 * HALF), lambda b, c: (b, jnp.minimum((order(c) + 1) * per, t // SUBLANE - 1), 0)),
            pl.BlockSpec((1, L, HALF), lambda b, c: (b, order(c), 0)),
            pl.BlockSpec((1, L, LANE), lambda b, c: (b, order(c), 0)),
        ]

    const = lambda a: pl.BlockSpec(a.shape, lambda b, c: (0, 0))
    return pl.pallas_call(
        functools.partial(_mlstm_kernel, n_lat_chunks=n_lat_chunks, n_chunks=n_chunks),
        grid=(bsz, n_chunks),
        in_specs=specs(fwd) + specs(bwd) + [const(conv_w), const(conv_b)],
        out_specs=[pl.BlockSpec((1, L, HALF), lambda b, c: (b, fwd(c), 0)),
                   pl.BlockSpec((1, L, HALF), lambda b, c: (b, bwd(c), 0))],
        out_shape=[jax.ShapeDtypeStruct((bsz, t, HALF), F32)] * 2,
        scratch_shapes=[pltpu.VMEM((2 * B_HEADS, B_HEAD_DIM, 2 * B_HEAD_DIM), F32),
                        pltpu.VMEM((2 * B_HEADS, 1, LANE), F32)],
        compiler_params=_cparams(("parallel", "arbitrary")),
        name="mlstm",
    )(qk, qk, qk, vb, gates, qk, qk, qk, vb, gates, conv_w, conv_b)


def _out_tail(mix, x_ref, mod_ref, g_ref, w_ref, wrh_ref, wrl_ref, xo_ref, f_ref, lg_ref):
    o = jnp.dot(mix, w_ref[...], preferred_element_type=F32)
    xn = x_ref[0] + mod_ref[0, 0:1, :] * o
    xo_ref[0] = xn
    y = xn * lax.rsqrt(jnp.mean(xn * xn, axis=-1, keepdims=True) + EPS) * g_ref[...]
    f = y * (1.0 + mod_ref[0, 2:3, :]) + mod_ref[0, 1:2, :]
    f_hi = f.astype(BF16)
    f_ref[0] = f_hi
    f_lo = (f - f_hi.astype(F32)).astype(BF16)
    lg_ref[0] = (jnp.dot(f_hi, wrh_ref[...], preferred_element_type=F32)
                 + jnp.dot(f_lo, wrh_ref[...], preferred_element_type=F32)
                 + jnp.dot(f_hi, wrl_ref[...], preferred_element_type=F32))


def _outproj_even_kernel(oa_ref, hf_ref, hb_ref, ob_ref, gh_ref, *tail):
    parts = [oa_ref[0]]
    for h in range(B_HEADS):
        hs = _blk(hf_ref[0], h) + _blk(hb_ref[0], h)
        hn = hs * lax.rsqrt(jnp.mean(hs * hs, axis=1, keepdims=True) + EPS) * _blk(gh_ref[...], h)
        parts.append((hn * jax.nn.sigmoid(_blk(ob_ref[0], h))).astype(BF16))
    _out_tail(jnp.concatenate(parts, axis=1), *tail)


def _outproj_odd_kernel(oc_ref, od_ref, *tail):
    _out_tail(jnp.concatenate([oc_ref[0], od_ref[0]], axis=1), *tail)


def _outproj(kern, heads, consts, x, mod, g, w, wr_hi, wr_lo, n_rows, n_lat_tiles):
    bsz, _, d = x.shape
    ne = wr_hi.shape[1]
    tm = ROW_TILE
    row = lambda b, i: (b, i, 0)
    return pl.pallas_call(
        kern,
        grid=(bsz, n_rows // tm),
        in_specs=([pl.BlockSpec((1, tm, HALF), row) for _ in heads] + [_const_spec(a) for a in consts]
                  + [pl.BlockSpec((1, tm, d), row),
                     pl.BlockSpec((1, 3, d), lambda b, i: (jnp.where(i >= n_lat_tiles, bsz, b), 0, 0)),
                     pl.BlockSpec((1, d), lambda b, i: (0, 0)), _const_spec(w), _const_spec(wr_hi),
                     _const_spec(wr_lo)]),
        out_specs=[pl.BlockSpec((1, tm, d), row), pl.BlockSpec((1, tm, d), row), pl.BlockSpec((1, tm, ne), row)],
        out_shape=[jax.ShapeDtypeStruct((bsz, n_rows, d), F32), jax.ShapeDtypeStruct((bsz, n_rows, d), BF16),
                   jax.ShapeDtypeStruct((bsz, n_rows, ne), F32)],
        compiler_params=_cparams(("parallel", "parallel")),
        name="outproj",
    )(*heads, *consts, x, mod, g.reshape(1, d), w, wr_hi, wr_lo)


def _expert_kernel(blk_e_ref, n_used_ref, x_ref, w1_ref, w3_ref, w2_ref, y_ref, w1_sc, w3_sc, w2_sc):
    i = pl.program_id(0)

    @pl.when((i == 0) | (blk_e_ref[i] != blk_e_ref[jnp.maximum(i - 1, 0)]))
    def _():
        w1_sc[...] = w1_ref[0, 0].astype(BF16)
        w3_sc[...] = w3_ref[0, 0].astype(BF16)
        w2_sc[...] = w2_ref[0, 0].astype(BF16)

    @pl.when(i < n_used_ref[0])
    def _():
        xb = x_ref[...]
        h1 = jnp.dot(xb, w1_sc[...], preferred_element_type=F32)
        h3 = jnp.dot(xb, w3_sc[...], preferred_element_type=F32)
        a = (h1 * jax.nn.sigmoid(h1) * h3).astype(BF16)
        y_ref[...] = jnp.dot(a, w2_sc[...], preferred_element_type=F32).astype(y_ref.dtype)

    @pl.when(i >= n_used_ref[0])
    def _():
        y_ref[...] = jnp.zeros(y_ref.shape, y_ref.dtype)


def _experts(xin, blk_e, n_used, layer, w1, w3, w2):
    n_rows, d = xin.shape
    de = w1.shape[-1]
    nb = n_rows // MOE_ROWS
    grid_spec = pltpu.PrefetchScalarGridSpec(
        num_scalar_prefetch=2,
        grid=(nb,),
        in_specs=[
            pl.BlockSpec((MOE_ROWS, d), lambda i, be, nu: (i, 0)),
            pl.BlockSpec((1, 1, d, de), lambda i, be, nu: (layer, be[i], 0, 0)),
            pl.BlockSpec((1, 1, d, de), lambda i, be, nu: (layer, be[i], 0, 0)),
            pl.BlockSpec((1, 1, de, d), lambda i, be, nu: (layer, be[i], 0, 0)),
        ],
        out_specs=pl.BlockSpec((MOE_ROWS, d), lambda i, be, nu: (i, 0)),
        scratch_shapes=[pltpu.VMEM((d, de), BF16), pltpu.VMEM((d, de), BF16), pltpu.VMEM((de, d), BF16)],
    )
    return pl.pallas_call(
        _expert_kernel,
        grid_spec=grid_spec,
        out_shape=jax.ShapeDtypeStruct((n_rows, d), BF16),
        compiler_params=_cparams(("arbitrary",)),
        name="experts",
    )(blk_e, n_used, xin, w1, w3, w2)


def _moe(f_tok, logits, b_router, layer, w1, w3, w2):
    n_tok, d = f_tok.shape
    per = N_EXPERTS // N_GROUPS
    scores = jax.nn.sigmoid(logits)
    sel = scores + b_router.astype(F32)

    def top2(a):
        i1 = jnp.argmax(a, axis=-1)
        hit = lax.broadcasted_iota(jnp.int32, a.shape, a.ndim - 1) == i1[..., None]
        rest = jnp.where(hit, -jnp.inf, a)
        return jnp.max(a, axis=-1), jnp.max(rest, axis=-1), i1, jnp.argmax(rest, axis=-1)

    g1, g2, _, _ = top2(sel.reshape(n_tok, N_GROUPS, per))
    g_idx = jnp.argmax(g1 + g2, axis=-1)
    in_group = (jnp.arange(N_EXPERTS) // per)[None, :] == g_idx[:, None]
    _, _, e1, e2 = top2(jnp.where(in_group, sel, -jnp.inf))
    e_idx = jnp.stack([e1, e2], axis=-1)
    gate = jnp.take_along_axis(scores, e_idx, axis=-1)
    gate = gate / jnp.sum(gate, axis=-1, keepdims=True)

    n_assign = n_tok * TOP_K
    e_flat = e_idx.reshape(-1).astype(jnp.int32)
    tok_flat = jnp.repeat(jnp.arange(n_tok, dtype=jnp.int32), TOP_K)
    assert n_assign % MOE_ROWS == 0
    onehot = (e_flat[:, None] == jnp.arange(N_EXPERTS, dtype=jnp.int32)[None, :]).astype(F32)
    tri = jnp.tril(jnp.ones((MOE_ROWS, MOE_ROWS), F32))
    within = jnp.einsum('ij,bjk->bik', tri, onehot.reshape(-1, MOE_ROWS, N_EXPERTS), precision=lax.Precision.HIGHEST)
    totals = within[:, -1, :]
    before = jnp.cumsum(totals, axis=0) - totals
    csum = (within + before[:, None, :]).reshape(n_assign, N_EXPERTS)
    counts = jnp.sum(totals, axis=0).astype(jnp.int32)
    rank = jnp.sum(csum * onehot, axis=1).astype(jnp.int32) - 1
    padded = (counts + MOE_ROWS - 1) // MOE_ROWS * MOE_ROWS
    pend = jnp.cumsum(padded)
    pstart = pend - padded
    dest = pstart[e_flat] + rank
    n_rows = -(-n_assign // MOE_ROWS) * MOE_ROWS + N_EXPERTS * MOE_ROWS
    nb = n_rows // MOE_ROWS
    row_tok = jnp.full((n_rows,), n_tok, jnp.int32).at[dest].set(tok_flat)
    xin = jnp.concatenate([f_tok, jnp.zeros((1, d), f_tok.dtype)], axis=0)[row_tok]
    blk_e = jnp.minimum(jnp.searchsorted(pend, jnp.arange(nb, dtype=jnp.int32) * MOE_ROWS, side='right'),
                        N_EXPERTS - 1).astype(jnp.int32)
    n_used = (pend[-1:] // MOE_ROWS).astype(jnp.int32)
    y = _experts(xin, blk_e, n_used, layer, w1, w3, w2)
    dest = dest.reshape(n_tok, TOP_K)
    return y[dest[:, 0]], y[dest[:, 1]], gate


def _rope_lanes(n_lat, n_ctx, rot_dim):
    rows = n_lat // GRID_W
    r = jnp.repeat(jnp.arange(rows), GRID_W)
    c = jnp.tile(jnp.arange(GRID_W), rows)
    n_freq = rot_dim // 4
    inv = ROPE_THETA ** (-jnp.arange(n_freq, dtype=F32) / n_freq)
    ang = jnp.concatenate([r[:, None] * inv, c[:, None] * inv], axis=-1)
    cos = jnp.repeat(jnp.cos(ang), 2, axis=1)
    sin = jnp.repeat(jnp.sin(ang), 2, axis=1) * jnp.tile(jnp.array([-1.0, 1.0], F32), rot_dim // 2)
    pad = ((0, n_ctx), (0, 0))
    return jnp.pad(cos, pad, constant_values=1.0), jnp.pad(sin, pad)


def _pad_cols(w, n):
    return jnp.pad(w, ((0, 0), (0, n - w.shape[1])))


def _pad_row(v, n):
    return jnp.pad(v, (0, n - v.shape[0])).reshape(1, n)


def _round_up(n, m):
    return -(-n // m) * m


def _head_blocks(w, n_heads, width, start, size):
    w = w.reshape(w.shape[0], n_heads, width)[:, :, start:start + size]
    return jnp.pad(w, ((0, 0), (0, 0), (0, LANE - size))).reshape(w.shape[0], n_heads * LANE)


def kernel(x, c, ctx, c_ctx, w_ada, b_ada, g_mix, g_ffn, e_w_in, e_w_out, e_g_q, e_g_k, e_conv_w, e_conv_b, e_b_gates, e_g_h, o_w_in, o_w_out, o_g_qc, o_g_kc, o_lam, o_g_sub, o_g_cq, o_w_uq, o_g_ckv, o_w_ukv, o_g_qd, o_g_kd, w_router, b_router, w1, w3, w2):
    bsz, n_lat, d = x.shape
    n_ctx = ctx.shape[1]
    t = n_lat + n_ctx
    depth = w_ada.shape[0]
    assert n_lat % ROW_TILE == 0 and n_ctx % ROW_TILE == 0 and n_ctx % B_CHUNK == 0
    n_lat_tiles = n_lat // ROW_TILE
    hd = D_NOPE + D_ROPE

    cos64, sin64 = _rope_lanes(n_lat, n_ctx, A_HEAD_DIM)
    cos64, sin64 = jnp.tile(cos64, (1, 2)), jnp.tile(sin64, (1, 2))
    cos_d, sin_d = _rope_lanes(n_lat, n_ctx, D_ROPE)
    cos_d = jnp.pad(cos_d, ((0, 0), (D_NOPE, LANE - hd)), constant_values=1.0)
    sin_d = jnp.pad(sin_d, ((0, 0), (D_NOPE, LANE - hd)))
    half = jnp.arange(LANE) // 64
    bd64 = (half[:, None] == half[None, :]).astype(BF16)
    wr_hi = w_router.astype(BF16)
    wr_lo = (w_router.astype(F32) - wr_hi.astype(F32)).astype(BF16)
    cond = jnp.concatenate([c, c_ctx[None, :]], axis=0)
    x_all = jnp.concatenate([x, ctx], axis=1)

    for layer in range(depth):
        last = layer == depth - 1
        j = layer // 2
        mod = (jax.nn.silu(cond) @ w_ada[layer] + b_ada[layer]).reshape(bsz + 1, 6, d)
        sh1, sc1, g1, sh2, sc2, g2 = [mod[:, i] for i in range(6)]
        mod_in = jnp.stack([sh1, sc1], axis=1)
        mod_out = jnp.stack([g1, sh2, sc2], axis=1)
        n_rows = n_lat if last else t
        if layer % 2 == 0:
            w_in = _pad_cols(e_w_in[j], _round_up(e_w_in.shape[-1], LANE)).astype(BF16)
            q, kd, vd, qk, vb, ob, gates = _inproj_even(
                x_all, mod_in, g_mix[layer], w_in, cos64, sin64, bd64, jnp.tile(e_g_q[j], 2).reshape(1, LANE),
                jnp.tile(e_g_k[j], 2).reshape(1, LANE), _pad_row(e_b_gates[j], LANE), n_lat_tiles)
            oa = _attn(q, kd, vd, mode="gqa", tq=256, tk=KEY_TILE, q_rows=(0, n_lat), k_rows=(0, t))
            if not last:
                oa = _attn(q, kd, vd, mode="gqa", tq=256, tk=KEY_TILE, q_rows=(n_lat, n_ctx), k_rows=(n_lat, n_ctx),
                           into=oa)
            hf, hb = _mlstm(qk, vb, gates, e_conv_w[j], e_conv_b[j].reshape(1, -1), n_lat)
            heads, consts, kern = [oa, hf, hb, ob], [e_g_h[j].reshape(1, HALF)], _outproj_even_kernel
            w_out = e_w_out[j]
        else:
            assert last, "the odd mixer is only wired as the final layer"
            lam_init = LAM_INIT_BASE[0] - LAM_INIT_BASE[1] * math.exp(-LAM_INIT_BASE[2] * layer)
            lq1, lk1, lq2, lk2 = o_lam[j].astype(F32)
            lam = jnp.exp(jnp.sum(lq1 * lk1)) - jnp.exp(jnp.sum(lq2 * lk2)) + lam_init
            w_in = _pad_cols(o_w_in[j], _round_up(o_w_in.shape[-1], LANE)).astype(BF16)
            consts = [bd64, jnp.tile(o_g_qc[j], 2).reshape(1, LANE), jnp.tile(o_g_kc[j], 2).reshape(1, LANE),
                      o_g_cq[j].reshape(1, D_Q_LORA), _head_blocks(o_w_uq[j], D_HEADS, hd, 0, hd).astype(BF16),
                      _pad_row(o_g_qd[j], LANE), o_g_ckv[j].reshape(1, D_KV_LORA),
                      _head_blocks(o_w_ukv[j], D_HEADS, D_NOPE + D_V_DIM, 0, D_NOPE).astype(BF16),
                      _head_blocks(o_w_ukv[j], D_HEADS, D_NOPE + D_V_DIM, D_NOPE, D_V_DIM).astype(BF16),
                      _pad_row(o_g_kd[j], LANE)]
            qc, kc, vc, qd, kdl, vdl = _inproj_odd(x_all, mod_in, g_mix[layer], w_in, cos64, sin64, cos_d, sin_d,
                                                   consts, n_lat_tiles)
            extra = (jnp.full((1, LANE), lam, F32), (o_g_sub[j] * (1.0 - lam_init)).reshape(1, LANE))
            oc = _attn(qc, kc, vc, mode="diff", tq=512, tk=KEY_TILE, q_rows=(0, n_lat), k_rows=(0, t), extra=extra)
            od = _attn(qd, kdl, vdl, mode="mla", tq=1024, tk=KEY_TILE, q_rows=(0, n_lat), k_rows=(0, t))
            heads, consts, kern = [oc, od], [], _outproj_odd_kernel
            w_out = o_w_out[j]
        x_new, f, logits = _outproj(kern, heads, consts, x_all, mod_out, g_ffn[layer], w_out.astype(BF16), wr_hi, wr_lo,
                                    n_rows, n_lat_tiles)
        y0, y1, gate = _moe(f.reshape(bsz * n_rows, d), logits.reshape(bsz * n_rows, N_EXPERTS), b_router,
                            layer, w1, w3, w2)
        y = (gate[:, 0:1] * y0 + gate[:, 1:2] * y1).reshape(bsz, n_rows, d)
        g2_rows = jnp.broadcast_to(g2[:bsz, None, :], (bsz, n_rows, d))
        if not last:
            g2_rows = g2_rows.at[:, n_lat:].set(g2[bsz])
        x_all = x_new + g2_rows * y
    return x_all[:, :n_lat]
```

```python
import functools
import math

import jax
import jax.numpy as jnp
import numpy as np
from jax import lax
from jax.experimental import pallas as pl
from jax.experimental.pallas import tpu as pltpu

F32 = jnp.float32
BF16 = jnp.bfloat16
EPS = 1e-6
LOG2E = 1.4426950408889634
NEG_BIG = -1e30
GRID_W = 64
ROPE_THETA = 10000.0
LAM_INIT_BASE = (0.8, 0.6, 0.3)

A_HEAD_DIM, A_HEADS, A_KV_HEADS = 64, 8, 2
B_HEAD_DIM, B_HEADS, B_CONV, B_CHUNK = 128, 4, 3, 128
C_HEAD_DIM, C_V_DIM, C_HEADS = 64, 128, 4
D_HEADS, D_Q_LORA, D_KV_LORA, D_NOPE, D_ROPE, D_V_DIM = 4, 256, 128, 64, 32, 128
N_EXPERTS, N_GROUPS, TOP_K, D_EXPERT = 32, 4, 2, 512
HALF = 512

LANE = 128
SUBLANE = 8
ROW_TILE = 256
MOE_ROWS = 256
KEY_TILE = 1280
ATTN_UNROLL = 2
VMEM_LIMIT = 48 * 1024 * 1024


def _cparams(sem):
    return pltpu.CompilerParams(dimension_semantics=sem, vmem_limit_bytes=VMEM_LIMIT)


def _blk(a, i):
    return a[:, i * LANE:(i + 1) * LANE]


def _modulated(x_ref, mod_ref, g_ref):
    x = x_ref[0]
    y = x * lax.rsqrt(jnp.mean(x * x, axis=-1, keepdims=True) + EPS) * g_ref[...]
    return (y * (1.0 + mod_ref[0, 1:2, :]) + mod_ref[0, 0:1, :]).astype(BF16)


def _pairswap(b, even):
    return jnp.where(even, pltpu.roll(b, LANE - 1, 1), pltpu.roll(b, 1, 1))


def _rope(b, cos, sin, even):
    return b * cos + _pairswap(b, even) * sin


def _head64_norm(b, bd, gain):
    ss = jnp.dot((b * b).astype(BF16), bd, preferred_element_type=F32)
    return b * lax.rsqrt(ss * (1.0 / 64) + EPS) * gain


def _lane_norm(b, n_real, gain):
    ss = jnp.sum(b * b, axis=1, keepdims=True)
    return b * lax.rsqrt(ss * (1.0 / n_real) + EPS) * gain


def _inproj_even_kernel(x_ref, mod_ref, g_ref, w_ref, cos_ref, sin_ref, bd_ref, gq_ref, gk_ref, bg_ref,
                        q_ref, k_ref, v_ref, qk_ref, vb_ref, ob_ref, gt_ref):
    z = jnp.dot(_modulated(x_ref, mod_ref, g_ref), w_ref[...], preferred_element_type=F32)
    lane = lax.broadcasted_iota(jnp.int32, (z.shape[0], LANE), 1)
    even = (lane & 1) == 0
    lo = lane < 64
    cos, sin, bd = cos_ref[...], sin_ref[...], bd_ref[...]
    q_scale = A_HEAD_DIM ** -0.5 * LOG2E
    for c in range(4):
        qb = _rope(_head64_norm(_blk(z, c), bd, gq_ref[...]), cos, sin, even) * q_scale
        q_ref[0, :, c * LANE:(c + 1) * LANE] = qb.astype(BF16)
    kn = _rope(_head64_norm(_blk(z, 4), bd, gk_ref[...]), cos, sin, even)
    for src, ref in ((kn, k_ref), (_blk(z, 5), v_ref)):
        sw = pltpu.roll(src, 64, 1)
        ref[0] = jnp.concatenate([jnp.where(lo, src, sw), jnp.where(lo, sw, src)], axis=1).astype(BF16)
    qk_ref[0] = z[:, 6 * LANE:14 * LANE]
    vb_ref[0] = z[:, 14 * LANE:18 * LANE].astype(BF16)
    ob_ref[0] = z[:, 18 * LANE:22 * LANE]
    gt_ref[0] = _blk(z, 22) + bg_ref[...]


def _row_specs(tm, d, bsz, n_lat_tiles):
    return [pl.BlockSpec((1, tm, d), lambda b, i: (b, i, 0)),
            pl.BlockSpec((1, 2, d), lambda b, i: (jnp.where(i >= n_lat_tiles, bsz, b), 0, 0)),
            pl.BlockSpec((1, d), lambda b, i: (0, 0))]


def _const_spec(a):
    return pl.BlockSpec(a.shape, lambda b, i: (0,) * a.ndim)


def _inproj_even(x, mod, g, w, cos, sin, bd, gq, gk, bg, n_lat_tiles):
    bsz, t, d = x.shape
    tm = ROW_TILE
    tab = pl.BlockSpec((tm, LANE), lambda b, i: (i, 0))
    consts = [bd, gq, gk, bg]
    widths = [(4 * LANE, BF16), (2 * LANE, BF16), (2 * LANE, BF16), (8 * LANE, F32), (4 * LANE, BF16),
              (4 * LANE, F32), (LANE, F32)]
    return pl.pallas_call(
        _inproj_even_kernel,
        grid=(bsz, t // tm),
        in_specs=_row_specs(tm, d, bsz, n_lat_tiles) + [_const_spec(w), tab, tab] + [_const_spec(a) for a in consts],
        out_specs=[pl.BlockSpec((1, tm, n), lambda b, i: (b, i, 0)) for n, _ in widths],
        out_shape=[jax.ShapeDtypeStruct((bsz, t, n), dt) for n, dt in widths],
        compiler_params=_cparams(("parallel", "parallel")),
        name="inproj_even",
    )(x, mod, g.reshape(1, d), w, cos, sin, *consts)


def _inproj_odd_kernel(x_ref, mod_ref, g_ref, w_ref, cos_ref, sin_ref, cosd_ref, sind_ref, bd_ref, gqc_ref, gkc_ref,
                       gcq_ref, wuq_ref, gqd_ref, gckv_ref, wkn_ref, wvd_ref, gkd_ref,
                       qc_ref, kc_ref, vc_ref, qd_ref, kd_ref, vd_ref):
    z = jnp.dot(_modulated(x_ref, mod_ref, g_ref), w_ref[...], preferred_element_type=F32)
    lane = lax.broadcasted_iota(jnp.int32, (z.shape[0], LANE), 1)
    even = (lane & 1) == 0
    cos, sin, bd = cos_ref[...], sin_ref[...], bd_ref[...]
    cosd, sind = cosd_ref[...], sind_ref[...]
    c_scale = C_HEAD_DIM ** -0.5 * LOG2E
    d_scale = (D_NOPE + D_ROPE) ** -0.5 * LOG2E
    for c in range(4):
        sl = slice(c * LANE, (c + 1) * LANE)
        qc_ref[0, :, sl] = (_rope(_head64_norm(_blk(z, c), bd, gqc_ref[...]), cos, sin, even) * c_scale).astype(BF16)
        kc_ref[0, :, sl] = _rope(_head64_norm(_blk(z, 4 + c), bd, gkc_ref[...]), cos, sin, even).astype(BF16)
    vc_ref[0] = z[:, 8 * LANE:12 * LANE].astype(BF16)
    cq = z[:, 12 * LANE:14 * LANE]
    cq = cq * lax.rsqrt(jnp.mean(cq * cq, axis=-1, keepdims=True) + EPS) * gcq_ref[...]
    qd = jnp.dot(cq.astype(BF16), wuq_ref[...], preferred_element_type=F32)
    ckv = _blk(z, 14)
    ckv = (ckv * lax.rsqrt(jnp.mean(ckv * ckv, axis=-1, keepdims=True) + EPS) * gckv_ref[...]).astype(BF16)
    kn = jnp.dot(ckv, wkn_ref[...], preferred_element_type=F32)
    vd_ref[0] = jnp.dot(ckv, wvd_ref[...], preferred_element_type=F32).astype(BF16)
    kr = pltpu.roll(_blk(z, 15), D_NOPE, 1)
    hd = D_NOPE + D_ROPE
    for c in range(4):
        sl = slice(c * LANE, (c + 1) * LANE)
        qd_ref[0, :, sl] = (_rope(_lane_norm(_blk(qd, c), hd, gqd_ref[...]), cosd, sind, even) * d_scale).astype(BF16)
        kd_ref[0, :, sl] = _rope(_lane_norm(_blk(kn, c) + kr, hd, gkd_ref[...]), cosd, sind, even).astype(BF16)


def _inproj_odd(x, mod, g, w, cos, sin, cosd, sind, consts, n_lat_tiles):
    bsz, t, d = x.shape
    tm = ROW_TILE
    tab = pl.BlockSpec((tm, LANE), lambda b, i: (i, 0))
    return pl.pallas_call(
        _inproj_odd_kernel,
        grid=(bsz, t // tm),
        in_specs=(_row_specs(tm, d, bsz, n_lat_tiles) + [_const_spec(w), tab, tab, tab, tab]
                  + [_const_spec(a) for a in consts]),
        out_specs=[pl.BlockSpec((1, tm, HALF), lambda b, i: (b, i, 0)) for _ in range(6)],
        out_shape=[jax.ShapeDtypeStruct((bsz, t, HALF), BF16) for _ in range(6)],
        compiler_params=_cparams(("parallel", "parallel")),
        name="inproj_odd",
    )(x, mod, g.reshape(1, d), w, cos, sin, cosd, sind, *consts)


def _attn_kernel(*refs, mode, tq, tk, n_tiles):
    if mode == "diff":
        q_ref, k_ref, v_ref, lam_ref, gsub_ref, o_ref, q_sc, s_sc, m_sc, a_sc, l_sc, acc_sc = refs
    else:
        q_ref, k_ref, v_ref = refs[:3]
        o_ref, q_sc, s_sc, m_sc, a_sc, l_sc, acc_sc = refs[-7:]
    rows = q_sc.shape[0]
    n_blk = tk // LANE
    lo = lax.broadcasted_iota(jnp.int32, (tq, LANE), 1) < 64

    if mode == "mla":
        q_sc[...] = q_ref[0]
    else:
        for p in range(q_ref.shape[2] // LANE):
            qb = _blk(q_ref[0], p)
            q_sc[2 * p * tq:(2 * p + 1) * tq] = jnp.where(lo, qb, jnp.zeros_like(qb))
            q_sc[(2 * p + 1) * tq:(2 * p + 2) * tq] = jnp.where(lo, jnp.zeros_like(qb), qb)

    def score_stage(j, slot):
        start = pl.multiple_of(j * tk, LANE)
        s = lax.dot_general(q_sc[...], k_ref[0, pl.ds(start, tk), :], (((1,), (1,)), ((), ())),
                            preferred_element_type=F32)
        s_sc[slot] = s
        m_tile = jnp.max(functools.reduce(jnp.maximum, [_blk(s, c) for c in range(n_blk)]), axis=1, keepdims=True)
        m_prev = m_sc[1 - slot]
        m_new = jnp.maximum(m_prev, m_tile)
        m_sc[slot] = m_new
        a_sc[slot] = jnp.exp2(m_prev - m_new)

    def value_stage(j, slot):
        start = pl.multiple_of(j * tk, LANE)
        m = m_sc[slot]
        alpha = a_sc[slot]
        p_blocks = [jnp.exp2((s_sc[slot, :, c * LANE:(c + 1) * LANE] - m).astype(BF16)) for c in range(n_blk)]
        l_sc[...] = alpha * l_sc[...] + functools.reduce(jnp.add, p_blocks).astype(F32)
        p = jnp.concatenate(p_blocks, axis=1)
        acc_sc[...] = alpha * acc_sc[...] + jnp.dot(p, v_ref[0, pl.ds(start, tk), :], preferred_element_type=F32)

    m_sc[1] = jnp.full((rows, LANE), NEG_BIG, F32)
    l_sc[...] = jnp.zeros((rows, LANE), F32)
    acc_sc[...] = jnp.zeros((rows, LANE), F32)
    score_stage(0, 0)
    peel = (n_tiles - 1) % ATTN_UNROLL
    for j in range(peel):
        value_stage(j, j % 2)
        score_stage(j + 1, (j + 1) % 2)

    def body(t, carry):
        for u in range(ATTN_UNROLL):
            j = ATTN_UNROLL * t + peel + u
            value_stage(j, (peel + u) % 2)
            score_stage(j + 1, (peel + u + 1) % 2)
        return carry

    lax.fori_loop(0, (n_tiles - 1) // ATTN_UNROLL, body, 0)
    value_stage(n_tiles - 1, (n_tiles - 1) % 2)

    o = acc_sc[...] / jnp.sum(l_sc[...], axis=1, keepdims=True)
    head = lambda h: o[h * tq:(h + 1) * tq]
    if mode == "gqa":
        for p in range(o_ref.shape[2] // LANE):
            o_ref[0, :, p * LANE:(p + 1) * LANE] = jnp.where(lo, head(2 * p), head(2 * p + 1)).astype(o_ref.dtype)
    elif mode == "diff":
        dlt = head(0) - lam_ref[...] * head(1)
        o_ref[0] = (dlt * lax.rsqrt(jnp.mean(dlt * dlt, axis=1, keepdims=True) + EPS) * gsub_ref[...]).astype(o_ref.dtype)
    else:
        o_ref[0] = o.astype(o_ref.dtype)


def _attn(q, k, v, *, mode, tq, tk, q_rows, k_rows, extra=(), into=None):
    bsz, t_q, wq = q.shape
    qw = 2 * LANE if mode == "gqa" else LANE
    rows = {"gqa": 4, "diff": 2, "mla": 1}[mode] * tq
    q0, nq = q_rows
    k0, nk = k_rows
    tk = min(tk, nk)
    assert q0 % tq == 0 and nq % tq == 0 and k0 % nk == 0 and nk % tk == 0
    qmap = lambda b, g, i: (b, q0 // tq + i, g)
    kmap = lambda b, g, i: (b, k0 // nk, g)
    in_specs = [pl.BlockSpec((1, tq, qw), qmap), pl.BlockSpec((1, nk, LANE), kmap), pl.BlockSpec((1, nk, LANE), kmap)]
    in_specs += [pl.BlockSpec(a.shape, lambda b, g, i: (0, 0)) for a in extra]
    args = [q, k, v, *extra]
    aliases = {}
    if into is not None:
        in_specs.append(pl.BlockSpec(memory_space=pl.ANY))
        aliases = {len(args): 0}
        args.append(into)

    def kern(*refs):
        if into is not None:
            refs = refs[:len(args) - 1] + refs[len(args):]
        _attn_kernel(*refs, mode=mode, tq=tq, tk=tk, n_tiles=nk // tk)

    return pl.pallas_call(
        kern,
        grid=(bsz, wq // qw, nq // tq),
        in_specs=in_specs,
        out_specs=pl.BlockSpec((1, tq, qw), qmap),
        out_shape=jax.ShapeDtypeStruct((bsz, t_q, wq), BF16),
        scratch_shapes=[pltpu.VMEM((rows, LANE), BF16), pltpu.VMEM((2, rows, tk), F32),
                        pltpu.VMEM((2, rows, LANE), F32), pltpu.VMEM((2, rows, LANE), F32),
                        pltpu.VMEM((rows, LANE), F32), pltpu.VMEM((rows, LANE), F32)],
        input_output_aliases=aliases,
        compiler_params=_cparams(("parallel", "parallel", "parallel")),
        name="attn_" + mode,
    )(*args)


def _mlstm_kernel(qf_ref, pf_ref, nf_ref, vf_ref, gf_ref, qb_ref, pb_ref, nb_ref, vb_ref, gb_ref, cw_ref, cb_ref,
                  hf_ref, hb_ref, c_sc, m_sc, *, n_lat_chunks, n_chunks):
    L, dh = B_CHUNK, B_HEAD_DIM
    c = pl.program_id(1)
    n_ctx_chunks = n_chunks - n_lat_chunks
    fwd_chunk = jnp.where(c < n_ctx_chunks, n_lat_chunks + c, c - n_ctx_chunks)
    bwd_chunk = n_chunks - 1 - c

    @pl.when(c == 0)
    def _():
        c_sc[...] = jnp.zeros(c_sc.shape, F32)
        m_sc[...] = jnp.zeros(m_sc.shape, F32)

    row = lax.broadcasted_iota(jnp.int32, (L, L), 0)
    col = lax.broadcasted_iota(jnp.int32, (L, L), 1)
    upper = (row <= col).astype(F32)
    lower = (col <= row).astype(F32)
    rid = lax.broadcasted_iota(jnp.int32, (L, 1), 0)
    glane = lax.broadcasted_iota(jnp.int32, (L, LANE), 1)
    is_f = ((glane >= 4) & (glane < 8)) | ((glane >= 12) & (glane < 16))
    ones_col = (lax.broadcasted_iota(jnp.int32, (L, dh), 1) == 0).astype(BF16)
    hi = lax.Precision.HIGHEST

    def direction(qk_ref, p_ref, n_ref, v_ref, g_ref, h_ref, chunk, reverse):
        first = (chunk == 0) | (chunk == n_lat_chunks)
        last = (chunk == n_lat_chunks - 1) | (chunk == n_chunks - 1)
        x = qk_ref[0]
        prev_row = jnp.where(first, 0.0, p_ref[0, SUBLANE - 1:SUBLANE, :])
        next_row = jnp.where(last, 0.0, n_ref[0, 0:1, :])
        x_prev = jnp.where(rid == 0, prev_row, pltpu.roll(x, 1, 0))
        x_next = jnp.where(rid == L - 1, next_row, pltpu.roll(x, L - 1, 0))
        y = x_prev * cw_ref[0:1, :] + x * cw_ref[1:2, :] + x_next * cw_ref[2:3, :] + cb_ref[...]
        y = y * jax.nn.sigmoid(y)
        g = g_ref[0]
        g = jnp.where(is_f, jnp.minimum(g, 0.0) - jnp.log(1.0 + jnp.exp(-jnp.abs(g))), g)
        g_t = jnp.transpose(g)
        cum_r = jnp.dot(g_t, upper, preferred_element_type=F32, precision=hi)
        cum_c = jnp.dot(lower, g, preferred_element_type=F32, precision=hi)
        if reverse:
            cum_r = cum_r[:, L - 1:L] - cum_r + g_t
            cum_c = cum_c[L - 1:L, :] - cum_c + g
        gi = 8 if reverse else 0
        mask = (col >= row) if reverse else (col <= row)
        for h in range(B_HEADS):
            idx = (4 if reverse else 0) + h
            q = _blk(y, h).astype(BF16)
            k = _blk(y, B_HEADS + h) * (dh ** -0.5)
            v_ext = jnp.concatenate([_blk(v_ref[0], h), ones_col], axis=1)
            i_r = g_t[gi + h:gi + h + 1, :]
            b_r = cum_r[gi + 4 + h:gi + 5 + h, :]
            i_c = g[:, gi + h:gi + h + 1]
            b_c = cum_c[:, gi + 4 + h:gi + 5 + h]
            m_prev = m_sc[idx][:, 0:1]
            c_ext = c_sc[idx]

            log_w = jnp.where(mask, b_c - b_r + i_r, -jnp.inf)
            log_inter = b_c + m_prev
            m_row = jnp.maximum(log_inter, jnp.max(log_w, axis=1, keepdims=True))
            w_intra = jnp.exp(log_w - m_row)
            a_inter = jnp.exp(log_inter - m_row)
            sc = lax.dot_general(q, k.astype(BF16), (((1,), (1,)), ((), ())), preferred_element_type=F32) * w_intra
            q_c = jnp.dot(q, c_ext.astype(BF16), preferred_element_type=F32)
            s_v = jnp.dot(sc.astype(BF16), v_ext, preferred_element_type=F32)
            num = a_inter * q_c[:, :dh] + s_v[:, :dh]
            den = a_inter * q_c[:, dh:dh + 1] + s_v[:, dh:dh + 1]
            h_ref[0, :, h * dh:(h + 1) * dh] = num / jnp.maximum(jnp.abs(den), jnp.exp(-m_row))

            b_last = b_r[:, 0:1] if reverse else b_r[:, L - 1:L]
            m_new = jnp.maximum(b_last + m_prev, jnp.max(b_last - b_r + i_r, axis=1, keepdims=True))
            g_c = jnp.exp(b_last - b_c + i_c - m_new)
            decay = jnp.exp(b_last + m_prev - m_new)
            upd = jnp.dot(jnp.transpose(k).astype(BF16), (g_c * v_ext.astype(F32)).astype(BF16),
                          preferred_element_type=F32)
            c_sc[idx] = decay * c_ext + upd
            m_sc[idx] = jnp.broadcast_to(m_new, (1, LANE))

    direction(qf_ref, pf_ref, nf_ref, vf_ref, gf_ref, hf_ref, fwd_chunk, False)
    direction(qb_ref, pb_ref, nb_ref, vb_ref, gb_ref, hb_ref, bwd_chunk, True)


def _mlstm(qk, vb, gates, conv_w, conv_b, n_lat):
    bsz, t, _ = qk.shape
    L = B_CHUNK
    n_chunks, n_lat_chunks = t // L, n_lat // L
    n_ctx_chunks = n_chunks - n_lat_chunks
    per = L // SUBLANE
    fwd = lambda c: jnp.where(c < n_ctx_chunks, n_lat_chunks + c, c - n_ctx_chunks)
    bwd = lambda c: n_chunks - 1 - c

    def specs(order):
        return [
            pl.BlockSpec((1, L, 2 * HALF), lambda b, c: (b, order(c), 0)),
            pl.BlockSpec((1, SUBLANE, 2 * HALF), lambda b, c: (b, jnp.maximum(order(c) * per - 1, 0), 0)),
            pl.BlockSpec((1, SUBLANE, 2 * HALF), lambda b, c: (b, jnp.minimum((order(c) + 1) * per, t // SUBLANE - 1), 0)),
            pl.BlockSpec((1, L, HALF), lambda b, c: (b, order(c), 0)),
            pl.BlockSpec((1, L, LANE), lambda b, c: (b, order(c), 0)),
        ]

    const = lambda a: pl.BlockSpec(a.shape, lambda b, c: (0, 0))
    return pl.pallas_call(
        functools.partial(_mlstm_kernel, n_lat_chunks=n_lat_chunks, n_chunks=n_chunks),
        grid=(bsz, n_chunks),
        in_specs=specs(fwd) + specs(bwd) + [const(conv_w), const(conv_b)],
        out_specs=[pl.BlockSpec((1, L, HALF), lambda b, c: (b, fwd(c), 0)),
                   pl.BlockSpec((1, L, HALF), lambda b, c: (b, bwd(c), 0))],
        out_shape=[jax.ShapeDtypeStruct((bsz, t, HALF), F32)] * 2,
        scratch_shapes=[pltpu.VMEM((2 * B_HEADS, B_HEAD_DIM, 2 * B_HEAD_DIM), F32),
                        pltpu.VMEM((2 * B_HEADS, 1, LANE), F32)],
        compiler_params=_cparams(("parallel", "arbitrary")),
        name="mlstm",
    )(qk, qk, qk, vb, gates, qk, qk, qk, vb, gates, conv_w, conv_b)


def _out_tail(mix, x_ref, mod_ref, g_ref, w_ref, wrh_ref, wrl_ref, xo_ref, f_ref, lg_ref):
    o = jnp.dot(mix, w_ref[...], preferred_element_type=F32)
    xn = x_ref[0] + mod_ref[0, 0:1, :] * o
    xo_ref[0] = xn
    y = xn * lax.rsqrt(jnp.mean(xn * xn, axis=-1, keepdims=True) + EPS) * g_ref[...]
    f = y * (1.0 + mod_ref[0, 2:3, :]) + mod_ref[0, 1:2, :]
    f_hi = f.astype(BF16)
    f_ref[0] = f_hi
    f_lo = (f - f_hi.astype(F32)).astype(BF16)
    lg_ref[0] = (jnp.dot(f_hi, wrh_ref[...], preferred_element_type=F32)
                 + jnp.dot(f_lo, wrh_ref[...], preferred_element_type=F32)
                 + jnp.dot(f_hi, wrl_ref[...], preferred_element_type=F32))


def _outproj_even_kernel(oa_ref, hf_ref, hb_ref, ob_ref, gh_ref, *tail):
    parts = [oa_ref[0]]
    for h in range(B_HEADS):
        hs = _blk(hf_ref[0], h) + _blk(hb_ref[0], h)
        hn = hs * lax.rsqrt(jnp.mean(hs * hs, axis=1, keepdims=True) + EPS) * _blk(gh_ref[...], h)
        parts.append((hn * jax.nn.sigmoid(_blk(ob_ref[0], h))).astype(BF16))
    _out_tail(jnp.concatenate(parts, axis=1), *tail)


def _outproj_odd_kernel(oc_ref, od_ref, *tail):
    _out_tail(jnp.concatenate([oc_ref[0], od_ref[0]], axis=1), *tail)


def _outproj(kern, heads, consts, x, mod, g, w, wr_hi, wr_lo, n_rows, n_lat_tiles):
    bsz, _, d = x.shape
    ne = wr_hi.shape[1]
    tm = ROW_TILE
    row = lambda b, i: (b, i, 0)
    return pl.pallas_call(
        kern,
        grid=(bsz, n_rows // tm),
        in_specs=([pl.BlockSpec((1, tm, HALF), row) for _ in heads] + [_const_spec(a) for a in consts]
                  + [pl.BlockSpec((1, tm, d), row),
                     pl.BlockSpec((1, 3, d), lambda b, i: (jnp.where(i >= n_lat_tiles, bsz, b), 0, 0)),
                     pl.BlockSpec((1, d), lambda b, i: (0, 0)), _const_spec(w), _const_spec(wr_hi),
                     _const_spec(wr_lo)]),
        out_specs=[pl.BlockSpec((1, tm, d), row), pl.BlockSpec((1, tm, d), row), pl.BlockSpec((1, tm, ne), row)],
        out_shape=[jax.ShapeDtypeStruct((bsz, n_rows, d), F32), jax.ShapeDtypeStruct((bsz, n_rows, d), BF16),
                   jax.ShapeDtypeStruct((bsz, n_rows, ne), F32)],
        compiler_params=_cparams(("parallel", "parallel")),
        name="outproj",
    )(*heads, *consts, x, mod, g.reshape(1, d), w, wr_hi, wr_lo)


def _expert_kernel(blk_e_ref, n_used_ref, x_ref, w1_ref, w3_ref, w2_ref, y_ref, w1_sc, w3_sc, w2_sc):
    i = pl.program_id(0)

    @pl.when((i == 0) | (blk_e_ref[i] != blk_e_ref[jnp.maximum(i - 1, 0)]))
    def _():
        w1_sc[...] = w1_ref[0, 0].astype(BF16)
        w3_sc[...] = w3_ref[0, 0].astype(BF16)
        w2_sc[...] = w2_ref[0, 0].astype(BF16)

    @pl.when(i < n_used_ref[0])
    def _():
        xb = x_ref[...]
        h1 = jnp.dot(xb, w1_sc[...], preferred_element_type=F32)
        h3 = jnp.dot(xb, w3_sc[...], preferred_element_type=F32)
        a = (h1 * jax.nn.sigmoid(h1) * h3).astype(BF16)
        y_ref[...] = jnp.dot(a, w2_sc[...], preferred_element_type=F32).astype(y_ref.dtype)

    @pl.when(i >= n_used_ref[0])
    def _():
        y_ref[...] = jnp.zeros(y_ref.shape, y_ref.dtype)


def _experts(xin, blk_e, n_used, layer, w1, w3, w2):
    n_rows, d = xin.shape
    de = w1.shape[-1]
    nb = n_rows // MOE_ROWS
    grid_spec = pltpu.PrefetchScalarGridSpec(
        num_scalar_prefetch=2,
        grid=(nb,),
        in_specs=[
            pl.BlockSpec((MOE_ROWS, d), lambda i, be, nu: (i, 0)),
            pl.BlockSpec((1, 1, d, de), lambda i, be, nu: (layer, be[i], 0, 0)),
            pl.BlockSpec((1, 1, d, de), lambda i, be, nu: (layer, be[i], 0, 0)),
            pl.BlockSpec((1, 1, de, d), lambda i, be, nu: (layer, be[i], 0, 0)),
        ],
        out_specs=pl.BlockSpec((MOE_ROWS, d), lambda i, be, nu: (i, 0)),
        scratch_shapes=[pltpu.VMEM((d, de), BF16), pltpu.VMEM((d, de), BF16), pltpu.VMEM((de, d), BF16)],
    )
    return pl.pallas_call(
        _expert_kernel,
        grid_spec=grid_spec,
        out_shape=jax.ShapeDtypeStruct((n_rows, d), BF16),
        compiler_params=_cparams(("arbitrary",)),
        name="experts",
    )(blk_e, n_used, xin, w1, w3, w2)


def _moe(f_tok, logits, b_router, layer, w1, w3, w2):
    n_tok, d = f_tok.shape
    per = N_EXPERTS // N_GROUPS
    scores = jax.nn.sigmoid(logits)
    sel = scores + b_router.astype(F32)

    def top2(a):
        i1 = jnp.argmax(a, axis=-1)
        hit = lax.broadcasted_iota(jnp.int32, a.shape, a.ndim - 1) == i1[..., None]
        rest = jnp.where(hit, -jnp.inf, a)
        return jnp.max(a, axis=-1), jnp.max(rest, axis=-1), i1, jnp.argmax(rest, axis=-1)

    g1, g2, _, _ = top2(sel.reshape(n_tok, N_GROUPS, per))
    g_idx = jnp.argmax(g1 + g2, axis=-1)
    in_group = (jnp.arange(N_EXPERTS) // per)[None, :] == g_idx[:, None]
    _, _, e1, e2 = top2(jnp.where(in_group, sel, -jnp.inf))
    e_idx = jnp.stack([e1, e2], axis=-1)
    gate = jnp.take_along_axis(scores, e_idx, axis=-1)
    gate = gate / jnp.sum(gate, axis=-1, keepdims=True)

    n_assign = n_tok * TOP_K
    e_flat = e_idx.reshape(-1).astype(jnp.int32)
    tok_flat = jnp.repeat(jnp.arange(n_tok, dtype=jnp.int32), TOP_K)
    assert n_assign % MOE_ROWS == 0
    onehot = (e_flat[:, None] == jnp.arange(N_EXPERTS, dtype=jnp.int32)[None, :]).astype(F32)
    tri = jnp.tril(jnp.ones((MOE_ROWS, MOE_ROWS), F32))
    within = jnp.einsum('ij,bjk->bik', tri, onehot.reshape(-1, MOE_ROWS, N_EXPERTS), precision=lax.Precision.HIGHEST)
    totals = within[:, -1, :]
    before = jnp.cumsum(totals, axis=0) - totals
    csum = (within + before[:, None, :]).reshape(n_assign, N_EXPERTS)
    counts = jnp.sum(totals, axis=0).astype(jnp.int32)
    rank = jnp.sum(csum * onehot, axis=1).astype(jnp.int32) - 1
    padded = (counts + MOE_ROWS - 1) // MOE_ROWS * MOE_ROWS
    pend = jnp.cumsum(padded)
    pstart = pend - padded
    dest = pstart[e_flat] + rank
    n_rows = -(-n_assign // MOE_ROWS) * MOE_ROWS + N_EXPERTS * MOE_ROWS
    nb = n_rows // MOE_ROWS
    row_tok = jnp.zeros((n_rows,), jnp.int32).at[dest].set(tok_flat)
    xin = f_tok[row_tok]
    blk_row = jnp.arange(nb, dtype=jnp.int32) * MOE_ROWS
    blk_e = jnp.minimum(jnp.sum((pend[None, :] <= blk_row[:, None]).astype(jnp.int32), axis=1), N_EXPERTS - 1)
    n_used = (pend[-1:] // MOE_ROWS).astype(jnp.int32)
    y = _experts(xin, blk_e, n_used, layer, w1, w3, w2)
    dest = dest.reshape(n_tok, TOP_K)
    return y[dest[:, 0]], y[dest[:, 1]], gate


def _combine_kernel(x_ref, y0_ref, y1_ref, gate_ref, g2_ref, o_ref):
    gate = gate_ref[0]
    y = gate[:, 0:1] * y0_ref[0].astype(F32) + gate[:, 1:2] * y1_ref[0].astype(F32)
    o_ref[0] = x_ref[0] + g2_ref[0] * y


def _combine(x, y0, y1, gate, g2, n_lat_tiles):
    bsz, n_rows, d = x.shape
    tm = ROW_TILE
    row = lambda b, i: (b, i, 0)
    return pl.pallas_call(
        _combine_kernel,
        grid=(bsz, n_rows // tm),
        in_specs=[pl.BlockSpec((1, tm, d), row), pl.BlockSpec((1, tm, d), row), pl.BlockSpec((1, tm, d), row),
                  pl.BlockSpec((1, tm, TOP_K), row),
                  pl.BlockSpec((1, 1, d), lambda b, i: (jnp.where(i >= n_lat_tiles, bsz, b), 0, 0))],
        out_specs=pl.BlockSpec((1, tm, d), row),
        out_shape=jax.ShapeDtypeStruct((bsz, n_rows, d), F32),
        compiler_params=_cparams(("parallel", "parallel")),
        name="combine",
    )(x, y0, y1, gate, g2)


def _rope_lanes(n_lat, n_ctx, rot_dim):
    rows = n_lat // GRID_W
    r = jnp.repeat(jnp.arange(rows), GRID_W)
    c = jnp.tile(jnp.arange(GRID_W), rows)
    n_freq = rot_dim // 4
    inv = ROPE_THETA ** (-jnp.arange(n_freq, dtype=F32) / n_freq)
    ang = jnp.concatenate([r[:, None] * inv, c[:, None] * inv], axis=-1)
    cos = jnp.repeat(jnp.cos(ang), 2, axis=1)
    sin = jnp.repeat(jnp.sin(ang), 2, axis=1) * jnp.tile(jnp.array([-1.0, 1.0], F32), rot_dim // 2)
    pad = ((0, n_ctx), (0, 0))
    return jnp.pad(cos, pad, constant_values=1.0), jnp.pad(sin, pad)


def _pad_cols(w, n):
    return jnp.pad(w, ((0, 0), (0, n - w.shape[1])))


def _pad_row(v, n):
    return jnp.pad(v, (0, n - v.shape[0])).reshape(1, n)


def _round_up(n, m):
    return -(-n // m) * m


def _head_blocks(w, n_heads, width, start, size):
    w = w.reshape(w.shape[0], n_heads, width)[:, :, start:start + size]
    return jnp.pad(w, ((0, 0), (0, 0), (0, LANE - size))).reshape(w.shape[0], n_heads * LANE)


def kernel(x, c, ctx, c_ctx, w_ada, b_ada, g_mix, g_ffn, e_w_in, e_w_out, e_g_q, e_g_k, e_conv_w, e_conv_b, e_b_gates, e_g_h, o_w_in, o_w_out, o_g_qc, o_g_kc, o_lam, o_g_sub, o_g_cq, o_w_uq, o_g_ckv, o_w_ukv, o_g_qd, o_g_kd, w_router, b_router, w1, w3, w2):
    bsz, n_lat, d = x.shape
    n_ctx = ctx.shape[1]
    t = n_lat + n_ctx
    depth = w_ada.shape[0]
    assert n_lat % ROW_TILE == 0 and n_ctx % ROW_TILE == 0 and n_ctx % B_CHUNK == 0
    n_lat_tiles = n_lat // ROW_TILE
    hd = D_NOPE + D_ROPE

    cos64, sin64 = _rope_lanes(n_lat, n_ctx, A_HEAD_DIM)
    cos64, sin64 = jnp.tile(cos64, (1, 2)), jnp.tile(sin64, (1, 2))
    cos_d, sin_d = _rope_lanes(n_lat, n_ctx, D_ROPE)
    cos_d = jnp.pad(cos_d, ((0, 0), (D_NOPE, LANE - hd)), constant_values=1.0)
    sin_d = jnp.pad(sin_d, ((0, 0), (D_NOPE, LANE - hd)))
    half = jnp.arange(LANE) // 64
    bd64 = (half[:, None] == half[None, :]).astype(BF16)
    wr_hi = w_router.astype(BF16)
    wr_lo = (w_router.astype(F32) - wr_hi.astype(F32)).astype(BF16)
    cond = jnp.concatenate([c, c_ctx[None, :]], axis=0)
    x_all = jnp.concatenate([x, ctx], axis=1)

    for layer in range(depth):
        last = layer == depth - 1
        j = layer // 2
        mod = (jax.nn.silu(cond) @ w_ada[layer] + b_ada[layer]).reshape(bsz + 1, 6, d)
        sh1, sc1, g1, sh2, sc2, g2 = [mod[:, i] for i in range(6)]
        mod_in = jnp.stack([sh1, sc1], axis=1)
        mod_out = jnp.stack([g1, sh2, sc2], axis=1)
        n_rows = n_lat if last else t
        if layer % 2 == 0:
            w_in = _pad_cols(e_w_in[j], _round_up(e_w_in.shape[-1], LANE)).astype(BF16)
            q, kd, vd, qk, vb, ob, gates = _inproj_even(
                x_all, mod_in, g_mix[layer], w_in, cos64, sin64, bd64, jnp.tile(e_g_q[j], 2).reshape(1, LANE),
                jnp.tile(e_g_k[j], 2).reshape(1, LANE), _pad_row(e_b_gates[j], LANE), n_lat_tiles)
            oa = _attn(q, kd, vd, mode="gqa", tq=256, tk=KEY_TILE, q_rows=(0, n_lat), k_rows=(0, t))
            if not last:
                oa = _attn(q, kd, vd, mode="gqa", tq=256, tk=KEY_TILE, q_rows=(n_lat, n_ctx), k_rows=(n_lat, n_ctx),
                           into=oa)
            hf, hb = _mlstm(qk, vb, gates, e_conv_w[j], e_conv_b[j].reshape(1, -1), n_lat)
            heads, consts, kern = [oa, hf, hb, ob], [e_g_h[j].reshape(1, HALF)], _outproj_even_kernel
            w_out = e_w_out[j]
        else:
            assert last, "the odd mixer is only wired as the final layer"
            lam_init = LAM_INIT_BASE[0] - LAM_INIT_BASE[1] * math.exp(-LAM_INIT_BASE[2] * layer)
            lq1, lk1, lq2, lk2 = o_lam[j].astype(F32)
            lam = jnp.exp(jnp.sum(lq1 * lk1)) - jnp.exp(jnp.sum(lq2 * lk2)) + lam_init
            w_in = _pad_cols(o_w_in[j], _round_up(o_w_in.shape[-1], LANE)).astype(BF16)
            consts = [bd64, jnp.tile(o_g_qc[j], 2).reshape(1, LANE), jnp.tile(o_g_kc[j], 2).reshape(1, LANE),
                      o_g_cq[j].reshape(1, D_Q_LORA), _head_blocks(o_w_uq[j], D_HEADS, hd, 0, hd).astype(BF16),
                      _pad_row(o_g_qd[j], LANE), o_g_ckv[j].reshape(1, D_KV_LORA),
                      _head_blocks(o_w_ukv[j], D_HEADS, D_NOPE + D_V_DIM, 0, D_NOPE).astype(BF16),
                      _head_blocks(o_w_ukv[j], D_HEADS, D_NOPE + D_V_DIM, D_NOPE, D_V_DIM).astype(BF16),
                      _pad_row(o_g_kd[j], LANE)]
            qc, kc, vc, qd, kdl, vdl = _inproj_odd(x_all, mod_in, g_mix[layer], w_in, cos64, sin64, cos_d, sin_d,
                                                   consts, n_lat_tiles)
            extra = (jnp.full((1, LANE), lam, F32), (o_g_sub[j] * (1.0 - lam_init)).reshape(1, LANE))
            oc = _attn(qc, kc, vc, mode="diff", tq=512, tk=KEY_TILE, q_rows=(0, n_lat), k_rows=(0, t), extra=extra)
            od = _attn(qd, kdl, vdl, mode="mla", tq=1024, tk=KEY_TILE, q_rows=(0, n_lat), k_rows=(0, t))
            heads, consts, kern = [oc, od], [], _outproj_odd_kernel
            w_out = o_w_out[j]
        x_new, f, logits = _outproj(kern, heads, consts, x_all, mod_out, g_ffn[layer], w_out.astype(BF16), wr_hi, wr_lo,
                                    n_rows, n_lat_tiles)
        y0, y1, gate = _moe(f.reshape(bsz * n_rows, d), logits.reshape(bsz * n_rows, N_EXPERTS), b_router,
                            layer, w1, w3, w2)
        x_all = _combine(x_new, y0.reshape(bsz, n_rows, d), y1.reshape(bsz, n_rows, d),
                         gate.reshape(bsz, n_rows, TOP_K), g2.reshape(bsz + 1, 1, d), n_lat_tiles)
    return x_all
```

```python
import functools
import math

import jax
import jax.numpy as jnp
import numpy as np
from jax import lax
from jax.experimental import pallas as pl
from jax.experimental.pallas import tpu as pltpu

F32 = jnp.float32
BF16 = jnp.bfloat16
EPS = 1e-6
LOG2E = 1.4426950408889634
NEG_BIG = -1e30
GRID_W = 64
ROPE_THETA = 10000.0
LAM_INIT_BASE = (0.8, 0.6, 0.3)

A_HEAD_DIM, A_HEADS, A_KV_HEADS = 64, 8, 2
B_HEAD_DIM, B_HEADS, B_CONV, B_CHUNK = 128, 4, 3, 128
C_HEAD_DIM, C_V_DIM, C_HEADS = 64, 128, 4
D_HEADS, D_Q_LORA, D_KV_LORA, D_NOPE, D_ROPE, D_V_DIM = 4, 256, 128, 64, 32, 128
N_EXPERTS, N_GROUPS, TOP_K, D_EXPERT = 32, 4, 2, 512
HALF = 512

LANE = 128
SUBLANE = 8
ROW_TILE = 256
MOE_ROWS = 256
KEY_TILE = 1280
ATTN_UNROLL = 2
VMEM_LIMIT = 48 * 1024 * 1024


def _cparams(sem):
    return pltpu.CompilerParams(dimension_semantics=sem, vmem_limit_bytes=VMEM_LIMIT)


def _blk(a, i):
    return a[:, i * LANE:(i + 1) * LANE]


def _modulated(x_ref, mod_ref, g_ref):
    x = x_ref[0]
    y = x * lax.rsqrt(jnp.mean(x * x, axis=-1, keepdims=True) + EPS) * g_ref[...]
    return (y * (1.0 + mod_ref[0, 1:2, :]) + mod_ref[0, 0:1, :]).astype(BF16)


def _pairswap(b, even):
    return jnp.where(even, pltpu.roll(b, LANE - 1, 1), pltpu.roll(b, 1, 1))


def _rope(b, cos, sin, even):
    return b * cos + _pairswap(b, even) * sin


def _head64_norm(b, bd, gain):
    ss = jnp.dot((b * b).astype(BF16), bd, preferred_element_type=F32)
    return b * lax.rsqrt(ss * (1.0 / 64) + EPS) * gain


def _lane_norm(b, n_real, gain):
    ss = jnp.sum(b * b, axis=1, keepdims=True)
    return b * lax.rsqrt(ss * (1.0 / n_real) + EPS) * gain


def _inproj_even_kernel(x_ref, mod_ref, g_ref, w_ref, cos_ref, sin_ref, bd_ref, gq_ref, gk_ref, bg_ref,
                        q_ref, k_ref, v_ref, qk_ref, vb_ref, ob_ref, gt_ref):
    z = jnp.dot(_modulated(x_ref, mod_ref, g_ref), w_ref[...], preferred_element_type=F32)
    lane = lax.broadcasted_iota(jnp.int32, (z.shape[0], LANE), 1)
    even = (lane & 1) == 0
    lo = lane < 64
    cos, sin, bd = cos_ref[...], sin_ref[...], bd_ref[...]
    q_scale = A_HEAD_DIM ** -0.5 * LOG2E
    for c in range(4):
        qb = _rope(_head64_norm(_blk(z, c), bd, gq_ref[...]), cos, sin, even) * q_scale
        q_ref[0, :, c * LANE:(c + 1) * LANE] = qb.astype(BF16)
    kn = _rope(_head64_norm(_blk(z, 4), bd, gk_ref[...]), cos, sin, even)
    for src, ref in ((kn, k_ref), (_blk(z, 5), v_ref)):
        sw = pltpu.roll(src, 64, 1)
        ref[0] = jnp.concatenate([jnp.where(lo, src, sw), jnp.where(lo, sw, src)], axis=1).astype(BF16)
    qk_ref[0] = z[:, 6 * LANE:14 * LANE]
    vb_ref[0] = z[:, 14 * LANE:18 * LANE].astype(BF16)
    ob_ref[0] = z[:, 18 * LANE:22 * LANE]
    gt_ref[0] = _blk(z, 22) + bg_ref[...]


def _row_specs(tm, d, bsz, n_lat_tiles):
    return [pl.BlockSpec((1, tm, d), lambda b, i: (b, i, 0)),
            pl.BlockSpec((1, 2, d), lambda b, i: (jnp.where(i >= n_lat_tiles, bsz, b), 0, 0)),
            pl.BlockSpec((1, d), lambda b, i: (0, 0))]


def _const_spec(a):
    return pl.BlockSpec(a.shape, lambda b, i: (0,) * a.ndim)


def _inproj_even(x, mod, g, w, cos, sin, bd, gq, gk, bg, n_lat_tiles):
    bsz, t, d = x.shape
    tm = ROW_TILE
    tab = pl.BlockSpec((tm, LANE), lambda b, i: (i, 0))
    consts = [bd, gq, gk, bg]
    widths = [(4 * LANE, BF16), (2 * LANE, BF16), (2 * LANE, BF16), (8 * LANE, F32), (4 * LANE, BF16),
              (4 * LANE, F32), (LANE, F32)]
    return pl.pallas_call(
        _inproj_even_kernel,
        grid=(bsz, t // tm),
        in_specs=_row_specs(tm, d, bsz, n_lat_tiles) + [_const_spec(w), tab, tab] + [_const_spec(a) for a in consts],
        out_specs=[pl.BlockSpec((1, tm, n), lambda b, i: (b, i, 0)) for n, _ in widths],
        out_shape=[jax.ShapeDtypeStruct((bsz, t, n), dt) for n, dt in widths],
        compiler_params=_cparams(("parallel", "parallel")),
        name="inproj_even",
    )(x, mod, g.reshape(1, d), w, cos, sin, *consts)


def _inproj_odd_kernel(x_ref, mod_ref, g_ref, w_ref, cos_ref, sin_ref, cosd_ref, sind_ref, bd_ref, gqc_ref, gkc_ref,
                       gcq_ref, wuq_ref, gqd_ref, gckv_ref, wkn_ref, wvd_ref, gkd_ref,
                       qc_ref, kc_ref, vc_ref, qd_ref, kd_ref, vd_ref):
    z = jnp.dot(_modulated(x_ref, mod_ref, g_ref), w_ref[...], preferred_element_type=F32)
    lane = lax.broadcasted_iota(jnp.int32, (z.shape[0], LANE), 1)
    even = (lane & 1) == 0
    cos, sin, bd = cos_ref[...], sin_ref[...], bd_ref[...]
    cosd, sind = cosd_ref[...], sind_ref[...]
    c_scale = C_HEAD_DIM ** -0.5 * LOG2E
    d_scale = (D_NOPE + D_ROPE) ** -0.5 * LOG2E
    for c in range(4):
        sl = slice(c * LANE, (c + 1) * LANE)
        qc_ref[0, :, sl] = (_rope(_head64_norm(_blk(z, c), bd, gqc_ref[...]), cos, sin, even) * c_scale).astype(BF16)
        kc_ref[0, :, sl] = _rope(_head64_norm(_blk(z, 4 + c), bd, gkc_ref[...]), cos, sin, even).astype(BF16)
    vc_ref[0] = z[:, 8 * LANE:12 * LANE].astype(BF16)
    cq = z[:, 12 * LANE:14 * LANE]
    cq = cq * lax.rsqrt(jnp.mean(cq * cq, axis=-1, keepdims=True) + EPS) * gcq_ref[...]
    qd = jnp.dot(cq.astype(BF16), wuq_ref[...], preferred_element_type=F32)
    ckv = _blk(z, 14)
    ckv = (ckv * lax.rsqrt(jnp.mean(ckv * ckv, axis=-1, keepdims=True) + EPS) * gckv_ref[...]).astype(BF16)
    kn = jnp.dot(ckv, wkn_ref[...], preferred_element_type=F32)
    vd_ref[0] = jnp.dot(ckv, wvd_ref[...], preferred_element_type=F32).astype(BF16)
    kr = pltpu.roll(_blk(z, 15), D_NOPE, 1)
    hd = D_NOPE + D_ROPE
    for c in range(4):
        sl = slice(c * LANE, (c + 1) * LANE)
        qd_ref[0, :, sl] = (_rope(_lane_norm(_blk(qd, c), hd, gqd_ref[...]), cosd, sind, even) * d_scale).astype(BF16)
        kd_ref[0, :, sl] = _rope(_lane_norm(_blk(kn, c) + kr, hd, gkd_ref[...]), cosd, sind, even).astype(BF16)


def _inproj_odd(x, mod, g, w, cos, sin, cosd, sind, consts, n_lat_tiles):
    bsz, t, d = x.shape
    tm = ROW_TILE
    tab = pl.BlockSpec((tm, LANE), lambda b, i: (i, 0))
    return pl.pallas_call(
        _inproj_odd_kernel,
        grid=(bsz, t // tm),
        in_specs=(_row_specs(tm, d, bsz, n_lat_tiles) + [_const_spec(w), tab, tab, tab, tab]
                  + [_const_spec(a) for a in consts]),
        out_specs=[pl.BlockSpec((1, tm, HALF), lambda b, i: (b, i, 0)) for _ in range(6)],
        out_shape=[jax.ShapeDtypeStruct((bsz, t, HALF), BF16) for _ in range(6)],
        compiler_params=_cparams(("parallel", "parallel")),
        name="inproj_odd",
    )(x, mod, g.reshape(1, d), w, cos, sin, cosd, sind, *consts)


def _attn_kernel(*refs, mode, tq, tk, n_tiles):
    if mode == "diff":
        q_ref, k_ref, v_ref, lam_ref, gsub_ref, o_ref, q_sc, s_sc, m_sc, a_sc, l_sc, acc_sc = refs
    else:
        q_ref, k_ref, v_ref = refs[:3]
        o_ref, q_sc, s_sc, m_sc, a_sc, l_sc, acc_sc = refs[-7:]
    rows = q_sc.shape[0]
    n_blk = tk // LANE
    lo = lax.broadcasted_iota(jnp.int32, (tq, LANE), 1) < 64

    if mode == "mla":
        q_sc[...] = q_ref[0]
    else:
        for p in range(q_ref.shape[2] // LANE):
            qb = _blk(q_ref[0], p)
            q_sc[2 * p * tq:(2 * p + 1) * tq] = jnp.where(lo, qb, jnp.zeros_like(qb))
            q_sc[(2 * p + 1) * tq:(2 * p + 2) * tq] = jnp.where(lo, jnp.zeros_like(qb), qb)

    def score_stage(j, slot):
        start = pl.multiple_of(j * tk, LANE)
        s = lax.dot_general(q_sc[...], k_ref[0, pl.ds(start, tk), :], (((1,), (1,)), ((), ())),
                            preferred_element_type=F32)
        s_sc[slot] = s
        m_tile = jnp.max(functools.reduce(jnp.maximum, [_blk(s, c) for c in range(n_blk)]), axis=1, keepdims=True)
        m_prev = m_sc[1 - slot]
        m_new = jnp.maximum(m_prev, m_tile)
        m_sc[slot] = m_new
        a_sc[slot] = jnp.exp2(m_prev - m_new)

    def value_stage(j, slot):
        start = pl.multiple_of(j * tk, LANE)
        m = m_sc[slot]
        alpha = a_sc[slot]
        p_blocks = [jnp.exp2((s_sc[slot, :, c * LANE:(c + 1) * LANE] - m).astype(BF16)) for c in range(n_blk)]
        l_sc[...] = alpha * l_sc[...] + functools.reduce(jnp.add, p_blocks).astype(F32)
        p = jnp.concatenate(p_blocks, axis=1)
        acc_sc[...] = alpha * acc_sc[...] + jnp.dot(p, v_ref[0, pl.ds(start, tk), :], preferred_element_type=F32)

    m_sc[1] = jnp.full((rows, LANE), NEG_BIG, F32)
    l_sc[...] = jnp.zeros((rows, LANE), F32)
    acc_sc[...] = jnp.zeros((rows, LANE), F32)
    score_stage(0, 0)
    peel = (n_tiles - 1) % ATTN_UNROLL
    for j in range(peel):
        value_stage(j, j % 2)
        score_stage(j + 1, (j + 1) % 2)

    def body(t, carry):
        for u in range(ATTN_UNROLL):
            j = ATTN_UNROLL * t + peel + u
            value_stage(j, (peel + u) % 2)
            score_stage(j + 1, (peel + u + 1) % 2)
        return carry

    lax.fori_loop(0, (n_tiles - 1) // ATTN_UNROLL, body, 0)
    value_stage(n_tiles - 1, (n_tiles - 1) % 2)

    o = acc_sc[...] / jnp.sum(l_sc[...], axis=1, keepdims=True)
    head = lambda h: o[h * tq:(h + 1) * tq]
    if mode == "gqa":
        for p in range(o_ref.shape[2] // LANE):
            o_ref[0, :, p * LANE:(p + 1) * LANE] = jnp.where(lo, head(2 * p), head(2 * p + 1)).astype(o_ref.dtype)
    elif mode == "diff":
        dlt = head(0) - lam_ref[...] * head(1)
        o_ref[0] = (dlt * lax.rsqrt(jnp.mean(dlt * dlt, axis=1, keepdims=True) + EPS) * gsub_ref[...]).astype(o_ref.dtype)
    else:
        o_ref[0] = o.astype(o_ref.dtype)


def _attn(q, k, v, *, mode, tq, tk, q_rows, k_rows, extra=(), into=None):
    bsz, t_q, wq = q.shape
    qw = 2 * LANE if mode == "gqa" else LANE
    rows = {"gqa": 4, "diff": 2, "mla": 1}[mode] * tq
    q0, nq = q_rows
    k0, nk = k_rows
    tk = min(tk, nk)
    assert q0 % tq == 0 and nq % tq == 0 and k0 % nk == 0 and nk % tk == 0
    qmap = lambda b, g, i: (b, q0 // tq + i, g)
    kmap = lambda b, g, i: (b, k0 // nk, g)
    in_specs = [pl.BlockSpec((1, tq, qw), qmap), pl.BlockSpec((1, nk, LANE), kmap), pl.BlockSpec((1, nk, LANE), kmap)]
    in_specs += [pl.BlockSpec(a.shape, lambda b, g, i: (0, 0)) for a in extra]
    args = [q, k, v, *extra]
    aliases = {}
    if into is not None:
        in_specs.append(pl.BlockSpec(memory_space=pl.ANY))
        aliases = {len(args): 0}
        args.append(into)

    def kern(*refs):
        if into is not None:
            refs = refs[:len(args) - 1] + refs[len(args):]
        _attn_kernel(*refs, mode=mode, tq=tq, tk=tk, n_tiles=nk // tk)

    return pl.pallas_call(
        kern,
        grid=(bsz, wq // qw, nq // tq),
        in_specs=in_specs,
        out_specs=pl.BlockSpec((1, tq, qw), qmap),
        out_shape=jax.ShapeDtypeStruct((bsz, t_q, wq), BF16),
        scratch_shapes=[pltpu.VMEM((rows, LANE), BF16), pltpu.VMEM((2, rows, tk), F32),
                        pltpu.VMEM((2, rows, LANE), F32), pltpu.VMEM((2, rows, LANE), F32),
                        pltpu.VMEM((rows, LANE), F32), pltpu.VMEM((rows, LANE), F32)],
        input_output_aliases=aliases,
        compiler_params=_cparams(("parallel", "parallel", "parallel")),
        name="attn_" + mode,
    )(*args)


def _mlstm_kernel(qf_ref, pf_ref, nf_ref, vf_ref, gf_ref, qb_ref, pb_ref, nb_ref, vb_ref, gb_ref, cw_ref, cb_ref,
                  hf_ref, hb_ref, c_sc, m_sc, *, n_lat_chunks, n_chunks):
    L, dh = B_CHUNK, B_HEAD_DIM
    c = pl.program_id(1)
    n_ctx_chunks = n_chunks - n_lat_chunks
    fwd_chunk = jnp.where(c < n_ctx_chunks, n_lat_chunks + c, c - n_ctx_chunks)
    bwd_chunk = n_chunks - 1 - c

    @pl.when(c == 0)
    def _():
        c_sc[...] = jnp.zeros(c_sc.shape, F32)
        m_sc[...] = jnp.zeros(m_sc.shape, F32)

    row = lax.broadcasted_iota(jnp.int32, (L, L), 0)
    col = lax.broadcasted_iota(jnp.int32, (L, L), 1)
    upper = (row <= col).astype(F32)
    lower = (col <= row).astype(F32)
    rid = lax.broadcasted_iota(jnp.int32, (L, 1), 0)
    glane = lax.broadcasted_iota(jnp.int32, (L, LANE), 1)
    is_f = ((glane >= 4) & (glane < 8)) | ((glane >= 12) & (glane < 16))
    ones_col = (lax.broadcasted_iota(jnp.int32, (L, dh), 1) == 0).astype(BF16)
    hi = lax.Precision.HIGHEST

    def direction(qk_ref, p_ref, n_ref, v_ref, g_ref, h_ref, chunk, reverse):
        first = (chunk == 0) | (chunk == n_lat_chunks)
        last = (chunk == n_lat_chunks - 1) | (chunk == n_chunks - 1)
        x = qk_ref[0]
        prev_row = jnp.where(first, 0.0, p_ref[0, SUBLANE - 1:SUBLANE, :])
        next_row = jnp.where(last, 0.0, n_ref[0, 0:1, :])
        x_prev = jnp.where(rid == 0, prev_row, pltpu.roll(x, 1, 0))
        x_next = jnp.where(rid == L - 1, next_row, pltpu.roll(x, L - 1, 0))
        y = x_prev * cw_ref[0:1, :] + x * cw_ref[1:2, :] + x_next * cw_ref[2:3, :] + cb_ref[...]
        y = y * jax.nn.sigmoid(y)
        g = g_ref[0]
        g = jnp.where(is_f, jnp.minimum(g, 0.0) - jnp.log(1.0 + jnp.exp(-jnp.abs(g))), g)
        g_t = jnp.transpose(g)
        cum_r = jnp.dot(g_t, upper, preferred_element_type=F32, precision=hi)
        cum_c = jnp.dot(lower, g, preferred_element_type=F32, precision=hi)
        if reverse:
            cum_r = cum_r[:, L - 1:L] - cum_r + g_t
            cum_c = cum_c[L - 1:L, :] - cum_c + g
        gi = 8 if reverse else 0
        mask = (col >= row) if reverse else (col <= row)
        for h in range(B_HEADS):
            idx = (4 if reverse else 0) + h
            q = _blk(y, h).astype(BF16)
            k = _blk(y, B_HEADS + h) * (dh ** -0.5)
            v_ext = jnp.concatenate([_blk(v_ref[0], h), ones_col], axis=1)
            i_r = g_t[gi + h:gi + h + 1, :]
            b_r = cum_r[gi + 4 + h:gi + 5 + h, :]
            i_c = g[:, gi + h:gi + h + 1]
            b_c = cum_c[:, gi + 4 + h:gi + 5 + h]
            m_prev = m_sc[idx][:, 0:1]
            c_ext = c_sc[idx]

            log_w = jnp.where(mask, b_c - b_r + i_r, -jnp.inf)
            log_inter = b_c + m_prev
            m_row = jnp.maximum(log_inter, jnp.max(log_w, axis=1, keepdims=True))
            w_intra = jnp.exp(log_w - m_row)
            a_inter = jnp.exp(log_inter - m_row)
            sc = lax.dot_general(q, k.astype(BF16), (((1,), (1,)), ((), ())), preferred_element_type=F32) * w_intra
            q_c = jnp.dot(q, c_ext.astype(BF16), preferred_element_type=F32)
            s_v = jnp.dot(sc.astype(BF16), v_ext, preferred_element_type=F32)
            num = a_inter * q_c[:, :dh] + s_v[:, :dh]
            den = a_inter * q_c[:, dh:dh + 1] + s_v[:, dh:dh + 1]
            h_ref[0, :, h * dh:(h + 1) * dh] = num / jnp.maximum(jnp.abs(den), jnp.exp(-m_row))

            b_last = b_r[:, 0:1] if reverse else b_r[:, L - 1:L]
            m_new = jnp.maximum(b_last + m_prev, jnp.max(b_last - b_r + i_r, axis=1, keepdims=True))
            g_c = jnp.exp(b_last - b_c + i_c - m_new)
            decay = jnp.exp(b_last + m_prev - m_new)
            upd = jnp.dot(jnp.transpose(k).astype(BF16), (g_c * v_ext.astype(F32)).astype(BF16),
                          preferred_element_type=F32)
            c_sc[idx] = decay * c_ext + upd
            m_sc[idx] = jnp.broadcast_to(m_new, (1, LANE))

    direction(qf_ref, pf_ref, nf_ref, vf_ref, gf_ref, hf_ref, fwd_chunk, False)
    direction(qb_ref, pb_ref, nb_ref, vb_ref, gb_ref, hb_ref, bwd_chunk, True)


def _mlstm(qk, vb, gates, conv_w, conv_b, n_lat):
    bsz, t, _ = qk.shape
    L = B_CHUNK
    n_chunks, n_lat_chunks = t // L, n_lat // L
    n_ctx_chunks = n_chunks - n_lat_chunks
    per = L // SUBLANE
    fwd = lambda c: jnp.where(c < n_ctx_chunks, n_lat_chunks + c, c - n_ctx_chunks)
    bwd = lambda c: n_chunks - 1 - c

    def specs(order):
        return [
            pl.BlockSpec((1, L, 2 * HALF), lambda b, c: (b, order(c), 0)),
            pl.BlockSpec((1, SUBLANE, 2 * HALF), lambda b, c: (b, jnp.maximum(order(c) * per - 1, 0), 0)),
            pl.BlockSpec((1, SUBLANE, 2 * HALF), lambda b, c: (b, jnp.minimum((order(c) + 1) * per, t // SUBLANE - 1), 0)),
            pl.BlockSpec((1, L, HALF), lambda b, c: (b, order(c), 0)),
            pl.BlockSpec((1, L, LANE), lambda b, c: (b, order(c), 0)),
        ]

    const = lambda a: pl.BlockSpec(a.shape, lambda b, c: (0, 0))
    return pl.pallas_call(
        functools.partial(_mlstm_kernel, n_lat_chunks=n_lat_chunks, n_chunks=n_chunks),
        grid=(bsz, n_chunks),
        in_specs=specs(fwd) + specs(bwd) + [const(conv_w), const(conv_b)],
        out_specs=[pl.BlockSpec((1, L, HALF), lambda b, c: (b, fwd(c), 0)),
                   pl.BlockSpec((1, L, HALF), lambda b, c: (b, bwd(c), 0))],
        out_shape=[jax.ShapeDtypeStruct((bsz, t, HALF), F32)] * 2,
        scratch_shapes=[pltpu.VMEM((2 * B_HEADS, B_HEAD_DIM, 2 * B_HEAD_DIM), F32),
                        pltpu.VMEM((2 * B_HEADS, 1, LANE), F32)],
        compiler_params=_cparams(("parallel", "arbitrary")),
        name="mlstm",
    )(qk, qk, qk, vb, gates, qk, qk, qk, vb, gates, conv_w, conv_b)


def _out_tail(mix, x_ref, mod_ref, g_ref, w_ref, wrh_ref, wrl_ref, xo_ref, f_ref, lg_ref):
    o = jnp.dot(mix, w_ref[...], preferred_element_type=F32)
    xn = x_ref[0] + mod_ref[0, 0:1, :] * o
    xo_ref[0] = xn
    y = xn * lax.rsqrt(jnp.mean(xn * xn, axis=-1, keepdims=True) + EPS) * g_ref[...]
    f = y * (1.0 + mod_ref[0, 2:3, :]) + mod_ref[0, 1:2, :]
    f_hi = f.astype(BF16)
    f_ref[0] = f_hi
    f_lo = (f - f_hi.astype(F32)).astype(BF16)
    lg_ref[0] = (jnp.dot(f_hi, wrh_ref[...], preferred_element_type=F32)
                 + jnp.dot(f_lo, wrh_ref[...], preferred_element_type=F32)
                 + jnp.dot(f_hi, wrl_ref[...], preferred_element_type=F32))


def _outproj_even_kernel(oa_ref, hf_ref, hb_ref, ob_ref, gh_ref, *tail):
    parts = [oa_ref[0]]
    for h in range(B_HEADS):
        hs = _blk(hf_ref[0], h) + _blk(hb_ref[0], h)
        hn = hs * lax.rsqrt(jnp.mean(hs * hs, axis=1, keepdims=True) + EPS) * _blk(gh_ref[...], h)
        parts.append((hn * jax.nn.sigmoid(_blk(ob_ref[0], h))).astype(BF16))
    _out_tail(jnp.concatenate(parts, axis=1), *tail)


def _outproj_odd_kernel(oc_ref, od_ref, *tail):
    _out_tail(jnp.concatenate([oc_ref[0], od_ref[0]], axis=1), *tail)


def _outproj(kern, heads, consts, x, mod, g, w, wr_hi, wr_lo, n_rows, n_lat_tiles):
    bsz, _, d = x.shape
    ne = wr_hi.shape[1]
    tm = ROW_TILE
    row = lambda b, i: (b, i, 0)
    return pl.pallas_call(
        kern,
        grid=(bsz, n_rows // tm),
        in_specs=([pl.BlockSpec((1, tm, HALF), row) for _ in heads] + [_const_spec(a) for a in consts]
                  + [pl.BlockSpec((1, tm, d), row),
                     pl.BlockSpec((1, 3, d), lambda b, i: (jnp.where(i >= n_lat_tiles, bsz, b), 0, 0)),
                     pl.BlockSpec((1, d), lambda b, i: (0, 0)), _const_spec(w), _const_spec(wr_hi),
                     _const_spec(wr_lo)]),
        out_specs=[pl.BlockSpec((1, tm, d), row), pl.BlockSpec((1, tm, d), row), pl.BlockSpec((1, tm, ne), row)],
        out_shape=[jax.ShapeDtypeStruct((bsz, n_rows, d), F32), jax.ShapeDtypeStruct((bsz, n_rows, d), BF16),
                   jax.ShapeDtypeStruct((bsz, n_rows, ne), F32)],
        compiler_params=_cparams(("parallel", "parallel")),
        name="outproj",
    )(*heads, *consts, x, mod, g.reshape(1, d), w, wr_hi, wr_lo)


def _expert_kernel(blk_e_ref, n_used_ref, x_ref, w1_ref, w3_ref, w2_ref, y_ref, w1_sc, w3_sc, w2_sc):
    i = pl.program_id(0)

    @pl.when((i == 0) | (blk_e_ref[i] != blk_e_ref[jnp.maximum(i - 1, 0)]))
    def _():
        w1_sc[...] = w1_ref[0, 0].astype(BF16)
        w3_sc[...] = w3_ref[0, 0].astype(BF16)
        w2_sc[...] = w2_ref[0, 0].astype(BF16)

    @pl.when(i < n_used_ref[0])
    def _():
        xb = x_ref[...]
        h1 = jnp.dot(xb, w1_sc[...], preferred_element_type=F32)
        h3 = jnp.dot(xb, w3_sc[...], preferred_element_type=F32)
        a = (h1 * jax.nn.sigmoid(h1) * h3).astype(BF16)
        y_ref[...] = jnp.dot(a, w2_sc[...], preferred_element_type=F32).astype(y_ref.dtype)

    @pl.when(i >= n_used_ref[0])
    def _():
        y_ref[...] = jnp.zeros(y_ref.shape, y_ref.dtype)


def _experts(xin, blk_e, n_used, layer, w1, w3, w2):
    n_rows, d = xin.shape
    de = w1.shape[-1]
    nb = n_rows // MOE_ROWS
    grid_spec = pltpu.PrefetchScalarGridSpec(
        num_scalar_prefetch=2,
        grid=(nb,),
        in_specs=[
            pl.BlockSpec((MOE_ROWS, d), lambda i, be, nu: (i, 0)),
            pl.BlockSpec((1, 1, d, de), lambda i, be, nu: (layer, be[i], 0, 0)),
            pl.BlockSpec((1, 1, d, de), lambda i, be, nu: (layer, be[i], 0, 0)),
            pl.BlockSpec((1, 1, de, d), lambda i, be, nu: (layer, be[i], 0, 0)),
        ],
        out_specs=pl.BlockSpec((MOE_ROWS, d), lambda i, be, nu: (i, 0)),
        scratch_shapes=[pltpu.VMEM((d, de), BF16), pltpu.VMEM((d, de), BF16), pltpu.VMEM((de, d), BF16)],
    )
    return pl.pallas_call(
        _expert_kernel,
        grid_spec=grid_spec,
        out_shape=jax.ShapeDtypeStruct((n_rows, d), BF16),
        compiler_params=_cparams(("arbitrary",)),
        name="experts",
    )(blk_e, n_used, xin, w1, w3, w2)


def _moe(f_tok, logits, b_router, layer, w1, w3, w2):
    n_tok, d = f_tok.shape
    per = N_EXPERTS // N_GROUPS
    scores = jax.nn.sigmoid(logits)
    sel = scores + b_router.astype(F32)

    def top2(a):
        i1 = jnp.argmax(a, axis=-1)
        hit = lax.broadcasted_iota(jnp.int32, a.shape, a.ndim - 1) == i1[..., None]
        rest = jnp.where(hit, -jnp.inf, a)
        return jnp.max(a, axis=-1), jnp.max(rest, axis=-1), i1, jnp.argmax(rest, axis=-1)

    g1, g2, _, _ = top2(sel.reshape(n_tok, N_GROUPS, per))
    g_idx = jnp.argmax(g1 + g2, axis=-1)
    in_group = (jnp.arange(N_EXPERTS) // per)[None, :] == g_idx[:, None]
    _, _, e1, e2 = top2(jnp.where(in_group, sel, -jnp.inf))
    lane_e = jnp.arange(N_EXPERTS, dtype=jnp.int32)[None, :]
    s1 = jnp.sum(jnp.where(lane_e == e1[:, None], scores, 0.0), axis=1)
    s2 = jnp.sum(jnp.where(lane_e == e2[:, None], scores, 0.0), axis=1)
    gate = jnp.stack([s1, s2], axis=-1) / (s1 + s2)[:, None]

    n_assign = n_tok * TOP_K
    e_flat = jnp.concatenate([e1, e2]).astype(jnp.int32)
    tok_flat = jnp.tile(jnp.arange(n_tok, dtype=jnp.int32), TOP_K)
    assert n_assign % MOE_ROWS == 0
    onehot = (e_flat[:, None] == jnp.arange(N_EXPERTS, dtype=jnp.int32)[None, :]).astype(F32)
    tri = jnp.tril(jnp.ones((MOE_ROWS, MOE_ROWS), F32))
    within = jnp.einsum('ij,bjk->bik', tri, onehot.reshape(-1, MOE_ROWS, N_EXPERTS), precision=lax.Precision.HIGHEST)
    totals = within[:, -1, :]
    before = jnp.cumsum(totals, axis=0) - totals
    csum = (within + before[:, None, :]).reshape(n_assign, N_EXPERTS)
    counts = jnp.sum(totals, axis=0).astype(jnp.int32)
    rank = jnp.sum(csum * onehot, axis=1).astype(jnp.int32) - 1
    padded = (counts + MOE_ROWS - 1) // MOE_ROWS * MOE_ROWS
    pend = jnp.cumsum(padded)
    pstart = pend - padded
    dest = pstart[e_flat] + rank
    n_rows = -(-n_assign // MOE_ROWS) * MOE_ROWS + N_EXPERTS * MOE_ROWS
    nb = n_rows // MOE_ROWS
    row_tok = (jnp.arange(n_rows, dtype=jnp.int32) % n_tok).at[dest].set(tok_flat, unique_indices=True)
    xin = f_tok[row_tok]
    blk_row = jnp.arange(nb, dtype=jnp.int32) * MOE_ROWS
    blk_e = jnp.minimum(jnp.sum((pend[None, :] <= blk_row[:, None]).astype(jnp.int32), axis=1), N_EXPERTS - 1)
    n_used = (pend[-1:] // MOE_ROWS).astype(jnp.int32)
    y = _experts(xin, blk_e, n_used, layer, w1, w3, w2)
    return y[dest[:n_tok]], y[dest[n_tok:]], gate


def _combine_kernel(x_ref, y0_ref, y1_ref, gate_ref, g2_ref, o_ref):
    gate = gate_ref[0]
    y = gate[:, 0:1] * y0_ref[0].astype(F32) + gate[:, 1:2] * y1_ref[0].astype(F32)
    o_ref[0] = x_ref[0] + g2_ref[0] * y


def _combine(x, y0, y1, gate, g2, n_lat_tiles):
    bsz, n_rows, d = x.shape
    tm = ROW_TILE
    row = lambda b, i: (b, i, 0)
    return pl.pallas_call(
        _combine_kernel,
        grid=(bsz, n_rows // tm),
        in_specs=[pl.BlockSpec((1, tm, d), row), pl.BlockSpec((1, tm, d), row), pl.BlockSpec((1, tm, d), row),
                  pl.BlockSpec((1, tm, TOP_K), row),
                  pl.BlockSpec((1, 1, d), lambda b, i: (jnp.where(i >= n_lat_tiles, bsz, b), 0, 0))],
        out_specs=pl.BlockSpec((1, tm, d), row),
        out_shape=jax.ShapeDtypeStruct((bsz, n_rows, d), F32),
        compiler_params=_cparams(("parallel", "parallel")),
        name="combine",
    )(x, y0, y1, gate, g2)


def _rope_lanes(n_lat, n_ctx, rot_dim):
    rows = n_lat // GRID_W
    r = jnp.repeat(jnp.arange(rows), GRID_W)
    c = jnp.tile(jnp.arange(GRID_W), rows)
    n_freq = rot_dim // 4
    inv = ROPE_THETA ** (-jnp.arange(n_freq, dtype=F32) / n_freq)
    ang = jnp.concatenate([r[:, None] * inv, c[:, None] * inv], axis=-1)
    cos = jnp.repeat(jnp.cos(ang), 2, axis=1)
    sin = jnp.repeat(jnp.sin(ang), 2, axis=1) * jnp.tile(jnp.array([-1.0, 1.0], F32), rot_dim // 2)
    pad = ((0, n_ctx), (0, 0))
    return jnp.pad(cos, pad, constant_values=1.0), jnp.pad(sin, pad)


def _pad_cols(w, n):
    return jnp.pad(w, ((0, 0), (0, n - w.shape[1])))


def _pad_row(v, n):
    return jnp.pad(v, (0, n - v.shape[0])).reshape(1, n)


def _round_up(n, m):
    return -(-n // m) * m


def _head_blocks(w, n_heads, width, start, size):
    w = w.reshape(w.shape[0], n_heads, width)[:, :, start:start + size]
    return jnp.pad(w, ((0, 0), (0, 0), (0, LANE - size))).reshape(w.shape[0], n_heads * LANE)


def kernel(x, c, ctx, c_ctx, w_ada, b_ada, g_mix, g_ffn, e_w_in, e_w_out, e_g_q, e_g_k, e_conv_w, e_conv_b, e_b_gates, e_g_h, o_w_in, o_w_out, o_g_qc, o_g_kc, o_lam, o_g_sub, o_g_cq, o_w_uq, o_g_ckv, o_w_ukv, o_g_qd, o_g_kd, w_router, b_router, w1, w3, w2):
    bsz, n_lat, d = x.shape
    n_ctx = ctx.shape[1]
    t = n_lat + n_ctx
    depth = w_ada.shape[0]
    assert n_lat % ROW_TILE == 0 and n_ctx % ROW_TILE == 0 and n_ctx % B_CHUNK == 0
    n_lat_tiles = n_lat // ROW_TILE
    hd = D_NOPE + D_ROPE

    cos64, sin64 = _rope_lanes(n_lat, n_ctx, A_HEAD_DIM)
    cos64, sin64 = jnp.tile(cos64, (1, 2)), jnp.tile(sin64, (1, 2))
    cos_d, sin_d = _rope_lanes(n_lat, n_ctx, D_ROPE)
    cos_d = jnp.pad(cos_d, ((0, 0), (D_NOPE, LANE - hd)), constant_values=1.0)
    sin_d = jnp.pad(sin_d, ((0, 0), (D_NOPE, LANE - hd)))
    half = jnp.arange(LANE) // 64
    bd64 = (half[:, None] == half[None, :]).astype(BF16)
    wr_hi = w_router.astype(BF16)
    wr_lo = (w_router.astype(F32) - wr_hi.astype(F32)).astype(BF16)
    cond = jnp.concatenate([c, c_ctx[None, :]], axis=0)
    x_all = jnp.concatenate([x, ctx], axis=1)

    for layer in range(depth):
        last = layer == depth - 1
        j = layer // 2
        mod = (jax.nn.silu(cond) @ w_ada[layer] + b_ada[layer]).reshape(bsz + 1, 6, d)
        sh1, sc1, g1, sh2, sc2, g2 = [mod[:, i] for i in range(6)]
        mod_in = jnp.stack([sh1, sc1], axis=1)
        mod_out = jnp.stack([g1, sh2, sc2], axis=1)
        n_rows = n_lat if last else t
        if layer % 2 == 0:
            w_in = _pad_cols(e_w_in[j], _round_up(e_w_in.shape[-1], LANE)).astype(BF16)
            q, kd, vd, qk, vb, ob, gates = _inproj_even(
                x_all, mod_in, g_mix[layer], w_in, cos64, sin64, bd64, jnp.tile(e_g_q[j], 2).reshape(1, LANE),
                jnp.tile(e_g_k[j], 2).reshape(1, LANE), _pad_row(e_b_gates[j], LANE), n_lat_tiles)
            oa = _attn(q, kd, vd, mode="gqa", tq=256, tk=KEY_TILE, q_rows=(0, n_lat), k_rows=(0, t))
            if not last:
                oa = _attn(q, kd, vd, mode="gqa", tq=256, tk=KEY_TILE, q_rows=(n_lat, n_ctx), k_rows=(n_lat, n_ctx),
                           into=oa)
            hf, hb = _mlstm(qk, vb, gates, e_conv_w[j], e_conv_b[j].reshape(1, -1), n_lat)
            heads, consts, kern = [oa, hf, hb, ob], [e_g_h[j].reshape(1, HALF)], _outproj_even_kernel
            w_out = e_w_out[j]
        else:
            assert last, "the odd mixer is only wired as the final layer"
            lam_init = LAM_INIT_BASE[0] - LAM_INIT_BASE[1] * math.exp(-LAM_INIT_BASE[2] * layer)
            lq1, lk1, lq2, lk2 = o_lam[j].astype(F32)
            lam = jnp.exp(jnp.sum(lq1 * lk1)) - jnp.exp(jnp.sum(lq2 * lk2)) + lam_init
            w_in = _pad_cols(o_w_in[j], _round_up(o_w_in.shape[-1], LANE)).astype(BF16)
            consts = [bd64, jnp.tile(o_g_qc[j], 2).reshape(1, LANE), jnp.tile(o_g_kc[j], 2).reshape(1, LANE),
                      o_g_cq[j].reshape(1, D_Q_LORA), _head_blocks(o_w_uq[j], D_HEADS, hd, 0, hd).astype(BF16),
                      _pad_row(o_g_qd[j], LANE), o_g_ckv[j].reshape(1, D_KV_LORA),
                      _head_blocks(o_w_ukv[j], D_HEADS, D_NOPE + D_V_DIM, 0, D_NOPE).astype(BF16),
                      _head_blocks(o_w_ukv[j], D_HEADS, D_NOPE + D_V_DIM, D_NOPE, D_V_DIM).astype(BF16),
                      _pad_row(o_g_kd[j], LANE)]
            qc, kc, vc, qd, kdl, vdl = _inproj_odd(x_all, mod_in, g_mix[layer], w_in, cos64, sin64, cos_d, sin_d,
                                                   consts, n_lat_tiles)
            extra = (jnp.full((1, LANE), lam, F32), (o_g_sub[j] * (1.0 - lam_init)).reshape(1, LANE))
            oc = _attn(qc, kc, vc, mode="diff", tq=512, tk=KEY_TILE, q_rows=(0, n_lat), k_rows=(0, t), extra=extra)
            od = _attn(qd, kdl, vdl, mode="mla", tq=1024, tk=KEY_TILE, q_rows=(0, n_lat), k_rows=(0, t))
            heads, consts, kern = [oc, od], [], _outproj_odd_kernel
            w_out = o_w_out[j]
        x_new, f, logits = _outproj(kern, heads, consts, x_all, mod_out, g_ffn[layer], w_out.astype(BF16), wr_hi, wr_lo,
                                    n_rows, n_lat_tiles)
        y0, y1, gate = _moe(f.reshape(bsz * n_rows, d), logits.reshape(bsz * n_rows, N_EXPERTS), b_router,
                            layer, w1, w3, w2)
        x_all = _combine(x_new, y0.reshape(bsz, n_rows, d), y1.reshape(bsz, n_rows, d),
                         gate.reshape(bsz, n_rows, TOP_K), g2.reshape(bsz + 1, 1, d), n_lat_tiles)
    return x_all
```

```python
import functools
import math

import jax
import jax.numpy as jnp
import numpy as np
from jax import lax
from jax.experimental import pallas as pl
from jax.experimental.pallas import tpu as pltpu

F32 = jnp.float32
BF16 = jnp.bfloat16
EPS = 1e-6
LOG2E = 1.4426950408889634
NEG_BIG = -1e30
GRID_W = 64
ROPE_THETA = 10000.0
LAM_INIT_BASE = (0.8, 0.6, 0.3)

A_HEAD_DIM, A_HEADS, A_KV_HEADS = 64, 8, 2
B_HEAD_DIM, B_HEADS, B_CONV, B_CHUNK = 128, 4, 3, 128
C_HEAD_DIM, C_V_DIM, C_HEADS = 64, 128, 4
D_HEADS, D_Q_LORA, D_KV_LORA, D_NOPE, D_ROPE, D_V_DIM = 4, 256, 128, 64, 32, 128
N_EXPERTS, N_GROUPS, TOP_K, D_EXPERT = 32, 4, 2, 512
HALF = 512

LANE = 128
SUBLANE = 8
ROW_TILE = 256
MOE_ROWS = 512
KEY_TILE = 1280
KEY_CHUNK = 256
ATTN_UNROLL = 2
VMEM_LIMIT = 48 * 1024 * 1024


def _cparams(sem):
    return pltpu.CompilerParams(dimension_semantics=sem, vmem_limit_bytes=VMEM_LIMIT)


def _blk(a, i):
    return a[:, i * LANE:(i + 1) * LANE]


def _modulated(x_ref, mod_ref, g_ref):
    x = x_ref[0]
    y = x * lax.rsqrt(jnp.mean(x * x, axis=-1, keepdims=True) + EPS) * g_ref[...]
    return (y * (1.0 + mod_ref[0, 1:2, :]) + mod_ref[0, 0:1, :]).astype(BF16)


def _pairswap(b, even):
    return jnp.where(even, pltpu.roll(b, LANE - 1, 1), pltpu.roll(b, 1, 1))


def _rope(b, cos, sin, even):
    return b * cos + _pairswap(b, even) * sin


def _head64_norm(b, bd, gain):
    ss = jnp.dot((b * b).astype(BF16), bd, preferred_element_type=F32)
    return b * lax.rsqrt(ss * (1.0 / 64) + EPS) * gain


def _lane_norm(b, n_real, gain):
    ss = jnp.sum(b * b, axis=1, keepdims=True)
    return b * lax.rsqrt(ss * (1.0 / n_real) + EPS) * gain


def _inproj_even_kernel(x_ref, mod_ref, g_ref, w_ref, cos_ref, sin_ref, bd_ref, gq_ref, gk_ref, bg_ref,
                        q_ref, k_ref, v_ref, qk_ref, vb_ref, ob_ref, gt_ref):
    z = jnp.dot(_modulated(x_ref, mod_ref, g_ref), w_ref[...], preferred_element_type=F32)
    lane = lax.broadcasted_iota(jnp.int32, (z.shape[0], LANE), 1)
    even = (lane & 1) == 0
    lo = lane < 64
    cos, sin, bd = cos_ref[...], sin_ref[...], bd_ref[...]
    q_scale = A_HEAD_DIM ** -0.5 * LOG2E
    for c in range(4):
        qb = _rope(_head64_norm(_blk(z, c), bd, gq_ref[...]), cos, sin, even) * q_scale
        q_ref[0, :, c * LANE:(c + 1) * LANE] = qb.astype(BF16)
    kn = _rope(_head64_norm(_blk(z, 4), bd, gk_ref[...]), cos, sin, even)
    for src, ref in ((kn, k_ref), (_blk(z, 5), v_ref)):
        sw = pltpu.roll(src, 64, 1)
        ref[0] = jnp.concatenate([jnp.where(lo, src, sw), jnp.where(lo, sw, src)], axis=1).astype(BF16)
    qk_ref[0] = z[:, 6 * LANE:14 * LANE]
    vb_ref[0] = z[:, 14 * LANE:18 * LANE].astype(BF16)
    ob_ref[0] = z[:, 18 * LANE:22 * LANE]
    gt_ref[0] = _blk(z, 22) + bg_ref[...]


def _row_specs(tm, d, bsz, n_lat_tiles):
    return [pl.BlockSpec((1, tm, d), lambda b, i: (b, i, 0)),
            pl.BlockSpec((1, 2, d), lambda b, i: (jnp.where(i >= n_lat_tiles, bsz, b), 0, 0)),
            pl.BlockSpec((1, d), lambda b, i: (0, 0))]


def _const_spec(a):
    return pl.BlockSpec(a.shape, lambda b, i: (0,) * a.ndim)


def _inproj_even(x, mod, g, w, cos, sin, bd, gq, gk, bg, n_lat_tiles):
    bsz, t, d = x.shape
    tm = ROW_TILE
    tab = pl.BlockSpec((tm, LANE), lambda b, i: (i, 0))
    consts = [bd, gq, gk, bg]
    widths = [(4 * LANE, BF16), (2 * LANE, BF16), (2 * LANE, BF16), (8 * LANE, F32), (4 * LANE, BF16),
              (4 * LANE, F32), (LANE, F32)]
    return pl.pallas_call(
        _inproj_even_kernel,
        grid=(bsz, t // tm),
        in_specs=_row_specs(tm, d, bsz, n_lat_tiles) + [_const_spec(w), tab, tab] + [_const_spec(a) for a in consts],
        out_specs=[pl.BlockSpec((1, tm, n), lambda b, i: (b, i, 0)) for n, _ in widths],
        out_shape=[jax.ShapeDtypeStruct((bsz, t, n), dt) for n, dt in widths],
        compiler_params=_cparams(("parallel", "parallel")),
        name="inproj_even",
    )(x, mod, g.reshape(1, d), w, cos, sin, *consts)


def _inproj_odd_kernel(x_ref, mod_ref, g_ref, w_ref, cos_ref, sin_ref, cosd_ref, sind_ref, bd_ref, gqc_ref, gkc_ref,
                       gcq_ref, wuq_ref, gqd_ref, gckv_ref, wkn_ref, wvd_ref, gkd_ref,
                       qc_ref, kc_ref, vc_ref, qd_ref, kd_ref, vd_ref):
    z = jnp.dot(_modulated(x_ref, mod_ref, g_ref), w_ref[...], preferred_element_type=F32)
    lane = lax.broadcasted_iota(jnp.int32, (z.shape[0], LANE), 1)
    even = (lane & 1) == 0
    cos, sin, bd = cos_ref[...], sin_ref[...], bd_ref[...]
    cosd, sind = cosd_ref[...], sind_ref[...]
    c_scale = C_HEAD_DIM ** -0.5 * LOG2E
    d_scale = (D_NOPE + D_ROPE) ** -0.5 * LOG2E
    for c in range(4):
        sl = slice(c * LANE, (c + 1) * LANE)
        qc_ref[0, :, sl] = (_rope(_head64_norm(_blk(z, c), bd, gqc_ref[...]), cos, sin, even) * c_scale).astype(BF16)
        kc_ref[0, :, sl] = _rope(_head64_norm(_blk(z, 4 + c), bd, gkc_ref[...]), cos, sin, even).astype(BF16)
    vc_ref[0] = z[:, 8 * LANE:12 * LANE].astype(BF16)
    cq = z[:, 12 * LANE:14 * LANE]
    cq = cq * lax.rsqrt(jnp.mean(cq * cq, axis=-1, keepdims=True) + EPS) * gcq_ref[...]
    qd = jnp.dot(cq.astype(BF16), wuq_ref[...], preferred_element_type=F32)
    ckv = _blk(z, 14)
    ckv = (ckv * lax.rsqrt(jnp.mean(ckv * ckv, axis=-1, keepdims=True) + EPS) * gckv_ref[...]).astype(BF16)
    kn = jnp.dot(ckv, wkn_ref[...], preferred_element_type=F32)
    vd_ref[0] = jnp.dot(ckv, wvd_ref[...], preferred_element_type=F32).astype(BF16)
    kr = pltpu.roll(_blk(z, 15), D_NOPE, 1)
    hd = D_NOPE + D_ROPE
    for c in range(4):
        sl = slice(c * LANE, (c + 1) * LANE)
        qd_ref[0, :, sl] = (_rope(_lane_norm(_blk(qd, c), hd, gqd_ref[...]), cosd, sind, even) * d_scale).astype(BF16)
        kd_ref[0, :, sl] = _rope(_lane_norm(_blk(kn, c) + kr, hd, gkd_ref[...]), cosd, sind, even).astype(BF16)


def _inproj_odd(x, mod, g, w, cos, sin, cosd, sind, consts, n_lat_tiles):
    bsz, t, d = x.shape
    tm = ROW_TILE
    tab = pl.BlockSpec((tm, LANE), lambda b, i: (i, 0))
    return pl.pallas_call(
        _inproj_odd_kernel,
        grid=(bsz, t // tm),
        in_specs=(_row_specs(tm, d, bsz, n_lat_tiles) + [_const_spec(w), tab, tab, tab, tab]
                  + [_const_spec(a) for a in consts]),
        out_specs=[pl.BlockSpec((1, tm, HALF), lambda b, i: (b, i, 0)) for _ in range(6)],
        out_shape=[jax.ShapeDtypeStruct((bsz, t, HALF), BF16) for _ in range(6)],
        compiler_params=_cparams(("parallel", "parallel")),
        name="inproj_odd",
    )(x, mod, g.reshape(1, d), w, cos, sin, cosd, sind, *consts)


def _attn_kernel(*refs, mode, tq, tk, n_tiles):
    if mode == "diff":
        q_ref, k_ref, v_ref, lam_ref, gsub_ref, o_ref, q_sc, s_sc, m_sc, a_sc, l_sc, acc_sc = refs
    else:
        q_ref, k_ref, v_ref = refs[:3]
        o_ref, q_sc, s_sc, m_sc, a_sc, l_sc, acc_sc = refs[-7:]
    rows = q_sc.shape[0]
    n_blk = tk // LANE
    lo = lax.broadcasted_iota(jnp.int32, (tq, LANE), 1) < 64

    if mode == "mla":
        q_sc[...] = q_ref[0]
    else:
        for p in range(q_ref.shape[2] // LANE):
            qb = _blk(q_ref[0], p)
            q_sc[2 * p * tq:(2 * p + 1) * tq] = jnp.where(lo, qb, jnp.zeros_like(qb))
            q_sc[(2 * p + 1) * tq:(2 * p + 2) * tq] = jnp.where(lo, jnp.zeros_like(qb), qb)

    def score_stage(j, slot):
        m_lane = None
        for c in range(tk // KEY_CHUNK):
            start = pl.multiple_of(j * tk + c * KEY_CHUNK, LANE)
            s = lax.dot_general(q_sc[...], k_ref[0, pl.ds(start, KEY_CHUNK), :], (((1,), (1,)), ((), ())),
                                preferred_element_type=F32)
            s_sc[slot, :, c * KEY_CHUNK:(c + 1) * KEY_CHUNK] = s
            blocks = [_blk(s, i) for i in range(KEY_CHUNK // LANE)] + ([] if m_lane is None else [m_lane])
            m_lane = functools.reduce(jnp.maximum, blocks)
        m_tile = jnp.max(m_lane, axis=1, keepdims=True)
        m_prev = m_sc[1 - slot]
        m_new = jnp.maximum(m_prev, m_tile)
        m_sc[slot] = m_new
        a_sc[slot] = jnp.exp2(m_prev - m_new)

    def value_stage(j, slot):
        start = pl.multiple_of(j * tk, LANE)
        m = m_sc[slot]
        alpha = a_sc[slot]
        p_blocks = [jnp.exp2((s_sc[slot, :, c * LANE:(c + 1) * LANE] - m).astype(BF16)) for c in range(n_blk)]
        l_sc[...] = alpha * l_sc[...] + functools.reduce(jnp.add, p_blocks).astype(F32)
        p = jnp.concatenate(p_blocks, axis=1)
        acc_sc[...] = alpha * acc_sc[...] + jnp.dot(p, v_ref[0, pl.ds(start, tk), :], preferred_element_type=F32)

    m_sc[1] = jnp.full((rows, LANE), NEG_BIG, F32)
    l_sc[...] = jnp.zeros((rows, LANE), F32)
    acc_sc[...] = jnp.zeros((rows, LANE), F32)
    score_stage(0, 0)
    peel = (n_tiles - 1) % ATTN_UNROLL
    for j in range(peel):
        value_stage(j, j % 2)
        score_stage(j + 1, (j + 1) % 2)

    def body(t, carry):
        for u in range(ATTN_UNROLL):
            j = ATTN_UNROLL * t + peel + u
            value_stage(j, (peel + u) % 2)
            score_stage(j + 1, (peel + u + 1) % 2)
        return carry

    lax.fori_loop(0, (n_tiles - 1) // ATTN_UNROLL, body, 0)
    value_stage(n_tiles - 1, (n_tiles - 1) % 2)

    o = acc_sc[...] / jnp.sum(l_sc[...], axis=1, keepdims=True)
    head = lambda h: o[h * tq:(h + 1) * tq]
    if mode == "gqa":
        for p in range(o_ref.shape[2] // LANE):
            o_ref[0, :, p * LANE:(p + 1) * LANE] = jnp.where(lo, head(2 * p), head(2 * p + 1)).astype(o_ref.dtype)
    elif mode == "diff":
        dlt = head(0) - lam_ref[...] * head(1)
        o_ref[0] = (dlt * lax.rsqrt(jnp.mean(dlt * dlt, axis=1, keepdims=True) + EPS) * gsub_ref[...]).astype(o_ref.dtype)
    else:
        o_ref[0] = o.astype(o_ref.dtype)


def _attn(q, k, v, *, mode, tq, tk, q_rows, k_rows, extra=(), into=None):
    bsz, t_q, wq = q.shape
    qw = 2 * LANE if mode == "gqa" else LANE
    rows = {"gqa": 4, "diff": 2, "mla": 1}[mode] * tq
    q0, nq = q_rows
    k0, nk = k_rows
    tk = min(tk, nk)
    assert q0 % tq == 0 and nq % tq == 0 and k0 % nk == 0 and nk % tk == 0 and tk % KEY_CHUNK == 0
    qmap = lambda b, g, i: (b, q0 // tq + i, g)
    kmap = lambda b, g, i: (b, k0 // nk, g)
    in_specs = [pl.BlockSpec((1, tq, qw), qmap), pl.BlockSpec((1, nk, LANE), kmap), pl.BlockSpec((1, nk, LANE), kmap)]
    in_specs += [pl.BlockSpec(a.shape, lambda b, g, i: (0, 0)) for a in extra]
    args = [q, k, v, *extra]
    aliases = {}
    if into is not None:
        in_specs.append(pl.BlockSpec(memory_space=pl.ANY))
        aliases = {len(args): 0}
        args.append(into)

    def kern(*refs):
        if into is not None:
            refs = refs[:len(args) - 1] + refs[len(args):]
        _attn_kernel(*refs, mode=mode, tq=tq, tk=tk, n_tiles=nk // tk)

    return pl.pallas_call(
        kern,
        grid=(bsz, wq // qw, nq // tq),
        in_specs=in_specs,
        out_specs=pl.BlockSpec((1, tq, qw), qmap),
        out_shape=jax.ShapeDtypeStruct((bsz, t_q, wq), BF16),
        scratch_shapes=[pltpu.VMEM((rows, LANE), BF16), pltpu.VMEM((2, rows, tk), F32),
                        pltpu.VMEM((2, rows, LANE), F32), pltpu.VMEM((2, rows, LANE), F32),
                        pltpu.VMEM((rows, LANE), F32), pltpu.VMEM((rows, LANE), F32)],
        input_output_aliases=aliases,
        compiler_params=_cparams(("parallel", "parallel", "parallel")),
        name="attn_" + mode,
    )(*args)


def _mlstm_kernel(qf_ref, pf_ref, nf_ref, vf_ref, gf_ref, qb_ref, pb_ref, nb_ref, vb_ref, gb_ref, cw_ref, cb_ref,
                  hf_ref, hb_ref, c_sc, m_sc, *, n_lat_chunks, n_chunks):
    L, dh = B_CHUNK, B_HEAD_DIM
    c = pl.program_id(1)
    n_ctx_chunks = n_chunks - n_lat_chunks
    fwd_chunk = jnp.where(c < n_ctx_chunks, n_lat_chunks + c, c - n_ctx_chunks)
    bwd_chunk = n_chunks - 1 - c

    @pl.when(c == 0)
    def _():
        c_sc[...] = jnp.zeros(c_sc.shape, F32)
        m_sc[...] = jnp.zeros(m_sc.shape, F32)

    row = lax.broadcasted_iota(jnp.int32, (L, L), 0)
    col = lax.broadcasted_iota(jnp.int32, (L, L), 1)
    upper = (row <= col).astype(F32)
    lower = (col <= row).astype(F32)
    rid = lax.broadcasted_iota(jnp.int32, (L, 1), 0)
    glane = lax.broadcasted_iota(jnp.int32, (L, LANE), 1)
    is_f = ((glane >= 4) & (glane < 8)) | ((glane >= 12) & (glane < 16))
    ones_col = (lax.broadcasted_iota(jnp.int32, (L, dh), 1) == 0).astype(BF16)
    hi = lax.Precision.HIGHEST

    def direction(qk_ref, p_ref, n_ref, v_ref, g_ref, h_ref, chunk, reverse):
        first = (chunk == 0) | (chunk == n_lat_chunks)
        last = (chunk == n_lat_chunks - 1) | (chunk == n_chunks - 1)
        x = qk_ref[0]
        prev_row = jnp.where(first, 0.0, p_ref[0, SUBLANE - 1:SUBLANE, :])
        next_row = jnp.where(last, 0.0, n_ref[0, 0:1, :])
        x_prev = jnp.where(rid == 0, prev_row, pltpu.roll(x, 1, 0))
        x_next = jnp.where(rid == L - 1, next_row, pltpu.roll(x, L - 1, 0))
        y = x_prev * cw_ref[0:1, :] + x * cw_ref[1:2, :] + x_next * cw_ref[2:3, :] + cb_ref[...]
        y = y * jax.nn.sigmoid(y)
        g = g_ref[0]
        g = jnp.where(is_f, jnp.minimum(g, 0.0) - jnp.log(1.0 + jnp.exp(-jnp.abs(g))), g)
        g_t = jnp.transpose(g)
        cum_r = jnp.dot(g_t, upper, preferred_element_type=F32, precision=hi)
        cum_c = jnp.dot(lower, g, preferred_element_type=F32, precision=hi)
        if reverse:
            cum_r = cum_r[:, L - 1:L] - cum_r + g_t
            cum_c = cum_c[L - 1:L, :] - cum_c + g
        gi = 8 if reverse else 0
        mask = (col >= row) if reverse else (col <= row)
        for h in range(B_HEADS):
            idx = (4 if reverse else 0) + h
            q = _blk(y, h).astype(BF16)
            k = _blk(y, B_HEADS + h) * (dh ** -0.5)
            v_ext = jnp.concatenate([_blk(v_ref[0], h), ones_col], axis=1)
            i_r = g_t[gi + h:gi + h + 1, :]
            b_r = cum_r[gi + 4 + h:gi + 5 + h, :]
            i_c = g[:, gi + h:gi + h + 1]
            b_c = cum_c[:, gi + 4 + h:gi + 5 + h]
            m_prev = m_sc[idx][:, 0:1]
            c_ext = c_sc[idx]

            log_w = jnp.where(mask, b_c - b_r + i_r, -jnp.inf)
            log_inter = b_c + m_prev
            m_row = jnp.maximum(log_inter, jnp.max(log_w, axis=1, keepdims=True))
            w_intra = jnp.exp(log_w - m_row)
            a_inter = jnp.exp(log_inter - m_row)
            sc = lax.dot_general(q, k.astype(BF16), (((1,), (1,)), ((), ())), preferred_element_type=F32) * w_intra
            q_c = jnp.dot(q, c_ext.astype(BF16), preferred_element_type=F32)
            s_v = jnp.dot(sc.astype(BF16), v_ext, preferred_element_type=F32)
            num = a_inter * q_c[:, :dh] + s_v[:, :dh]
            den = a_inter * q_c[:, dh:dh + 1] + s_v[:, dh:dh + 1]
            h_ref[0, :, h * dh:(h + 1) * dh] = num / jnp.maximum(jnp.abs(den), jnp.exp(-m_row))

            b_last = b_r[:, 0:1] if reverse else b_r[:, L - 1:L]
            m_new = jnp.maximum(b_last + m_prev, jnp.max(b_last - b_r + i_r, axis=1, keepdims=True))
            g_c = jnp.exp(b_last - b_c + i_c - m_new)
            decay = jnp.exp(b_last + m_prev - m_new)
            upd = jnp.dot(jnp.transpose(k).astype(BF16), (g_c * v_ext.astype(F32)).astype(BF16),
                          preferred_element_type=F32)
            c_sc[idx] = decay * c_ext + upd
            m_sc[idx] = jnp.broadcast_to(m_new, (1, LANE))

    direction(qf_ref, pf_ref, nf_ref, vf_ref, gf_ref, hf_ref, fwd_chunk, False)
    direction(qb_ref, pb_ref, nb_ref, vb_ref, gb_ref, hb_ref, bwd_chunk, True)


def _mlstm(qk, vb, gates, conv_w, conv_b, n_lat):
    bsz, t, _ = qk.shape
    L = B_CHUNK
    n_chunks, n_lat_chunks = t // L, n_lat // L
    n_ctx_chunks = n_chunks - n_lat_chunks
    per = L // SUBLANE
    fwd = lambda c: jnp.where(c < n_ctx_chunks, n_lat_chunks + c, c - n_ctx_chunks)
    bwd = lambda c: n_chunks - 1 - c

    def specs(order):
        return [
            pl.BlockSpec((1, L, 2 * HALF), lambda b, c: (b, order(c), 0)),
            pl.BlockSpec((1, SUBLANE, 2 * HALF), lambda b, c: (b, jnp.maximum(order(c) * per - 1, 0), 0)),
            pl.BlockSpec((1, SUBLANE, 2 * HALF), lambda b, c: (b, jnp.minimum((order(c) + 1) * per, t // SUBLANE - 1), 0)),
            pl.BlockSpec((1, L, HALF), lambda b, c: (b, order(c), 0)),
            pl.BlockSpec((1, L, LANE), lambda b, c: (b, order(c), 0)),
        ]

    const = lambda a: pl.BlockSpec(a.shape, lambda b, c: (0, 0))
    return pl.pallas_call(
        functools.partial(_mlstm_kernel, n_lat_chunks=n_lat_chunks, n_chunks=n_chunks),
        grid=(bsz, n_chunks),
        in_specs=specs(fwd) + specs(bwd) + [const(conv_w), const(conv_b)],
        out_specs=[pl.BlockSpec((1, L, HALF), lambda b, c: (b, fwd(c), 0)),
                   pl.BlockSpec((1, L, HALF), lambda b, c: (b, bwd(c), 0))],
        out_shape=[jax.ShapeDtypeStruct((bsz, t, HALF), F32)] * 2,
        scratch_shapes=[pltpu.VMEM((2 * B_HEADS, B_HEAD_DIM, 2 * B_HEAD_DIM), F32),
                        pltpu.VMEM((2 * B_HEADS, 1, LANE), F32)],
        compiler_params=_cparams(("parallel", "arbitrary")),
        name="mlstm",
    )(qk, qk, qk, vb, gates, qk, qk, qk, vb, gates, conv_w, conv_b)


def _out_tail(mix, x_ref, mod_ref, g_ref, w_ref, wrh_ref, wrl_ref, xo_ref, f_ref, lg_ref):
    o = jnp.dot(mix, w_ref[...], preferred_element_type=F32)
    xn = x_ref[0] + mod_ref[0, 0:1, :] * o
    xo_ref[0] = xn
    y = xn * lax.rsqrt(jnp.mean(xn * xn, axis=-1, keepdims=True) + EPS) * g_ref[...]
    f = y * (1.0 + mod_ref[0, 2:3, :]) + mod_ref[0, 1:2, :]
    f_hi = f.astype(BF16)
    f_ref[0] = f_hi
    f_lo = (f - f_hi.astype(F32)).astype(BF16)
    lg_ref[0] = (jnp.dot(f_hi, wrh_ref[...], preferred_element_type=F32)
                 + jnp.dot(f_lo, wrh_ref[...], preferred_element_type=F32)
                 + jnp.dot(f_hi, wrl_ref[...], preferred_element_type=F32))


def _outproj_even_kernel(oa_ref, hf_ref, hb_ref, ob_ref, gh_ref, *tail):
    parts = [oa_ref[0]]
    for h in range(B_HEADS):
        hs = _blk(hf_ref[0], h) + _blk(hb_ref[0], h)
        hn = hs * lax.rsqrt(jnp.mean(hs * hs, axis=1, keepdims=True) + EPS) * _blk(gh_ref[...], h)
        parts.append((hn * jax.nn.sigmoid(_blk(ob_ref[0], h))).astype(BF16))
    _out_tail(jnp.concatenate(parts, axis=1), *tail)


def _outproj_odd_kernel(oc_ref, od_ref, *tail):
    _out_tail(jnp.concatenate([oc_ref[0], od_ref[0]], axis=1), *tail)


def _outproj(kern, heads, consts, x, mod, g, w, wr_hi, wr_lo, n_rows, n_lat_tiles):
    bsz, _, d = x.shape
    ne = wr_hi.shape[1]
    tm = ROW_TILE
    row = lambda b, i: (b, i, 0)
    return pl.pallas_call(
        kern,
        grid=(bsz, n_rows // tm),
        in_specs=([pl.BlockSpec((1, tm, HALF), row) for _ in heads] + [_const_spec(a) for a in consts]
                  + [pl.BlockSpec((1, tm, d), row),
                     pl.BlockSpec((1, 3, d), lambda b, i: (jnp.where(i >= n_lat_tiles, bsz, b), 0, 0)),
                     pl.BlockSpec((1, d), lambda b, i: (0, 0)), _const_spec(w), _const_spec(wr_hi),
                     _const_spec(wr_lo)]),
        out_specs=[pl.BlockSpec((1, tm, d), row), pl.BlockSpec((1, tm, d), row), pl.BlockSpec((1, tm, ne), row)],
        out_shape=[jax.ShapeDtypeStruct((bsz, n_rows, d), F32), jax.ShapeDtypeStruct((bsz, n_rows, d), BF16),
                   jax.ShapeDtypeStruct((bsz, n_rows, ne), F32)],
        compiler_params=_cparams(("parallel", "parallel")),
        name="outproj",
    )(*heads, *consts, x, mod, g.reshape(1, d), w, wr_hi, wr_lo)


def _expert_kernel(blk_e_ref, n_used_ref, x_ref, w1_ref, w3_ref, w2_ref, y_ref, w1_sc, w3_sc, w2_sc):
    i = pl.program_id(0)

    @pl.when((i == 0) | (blk_e_ref[i] != blk_e_ref[jnp.maximum(i - 1, 0)]))
    def _():
        w1_sc[...] = w1_ref[0, 0].astype(BF16)
        w3_sc[...] = w3_ref[0, 0].astype(BF16)
        w2_sc[...] = w2_ref[0, 0].astype(BF16)

    @pl.when(i < n_used_ref[0])
    def _():
        xb = x_ref[...]
        h1 = jnp.dot(xb, w1_sc[...], preferred_element_type=F32)
        h3 = jnp.dot(xb, w3_sc[...], preferred_element_type=F32)
        a = (h1 * jax.nn.sigmoid(h1) * h3).astype(BF16)
        y_ref[...] = jnp.dot(a, w2_sc[...], preferred_element_type=F32).astype(y_ref.dtype)

    @pl.when(i >= n_used_ref[0])
    def _():
        y_ref[...] = jnp.zeros(y_ref.shape, y_ref.dtype)


def _experts(xin, blk_e, n_used, layer, w1, w3, w2):
    n_rows, d = xin.shape
    de = w1.shape[-1]
    nb = n_rows // MOE_ROWS
    grid_spec = pltpu.PrefetchScalarGridSpec(
        num_scalar_prefetch=2,
        grid=(nb,),
        in_specs=[
            pl.BlockSpec((MOE_ROWS, d), lambda i, be, nu: (i, 0)),
            pl.BlockSpec((1, 1, d, de), lambda i, be, nu: (layer, be[i], 0, 0)),
            pl.BlockSpec((1, 1, d, de), lambda i, be, nu: (layer, be[i], 0, 0)),
            pl.BlockSpec((1, 1, de, d), lambda i, be, nu: (layer, be[i], 0, 0)),
        ],
        out_specs=pl.BlockSpec((MOE_ROWS, d), lambda i, be, nu: (i, 0)),
        scratch_shapes=[pltpu.VMEM((d, de), BF16), pltpu.VMEM((d, de), BF16), pltpu.VMEM((de, d), BF16)],
    )
    return pl.pallas_call(
        _expert_kernel,
        grid_spec=grid_spec,
        out_shape=jax.ShapeDtypeStruct((n_rows, d), BF16),
        compiler_params=_cparams(("arbitrary",)),
        name="experts",
    )(blk_e, n_used, xin, w1, w3, w2)


def _moe(f_tok, logits, b_router, layer, w1, w3, w2):
    n_tok, d = f_tok.shape
    per = N_EXPERTS // N_GROUPS
    scores = jax.nn.sigmoid(logits)
    sel = scores + b_router.astype(F32)

    def top2(a):
        i1 = jnp.argmax(a, axis=-1)
        hit = lax.broadcasted_iota(jnp.int32, a.shape, a.ndim - 1) == i1[..., None]
        rest = jnp.where(hit, -jnp.inf, a)
        return jnp.max(a, axis=-1), jnp.max(rest, axis=-1), i1, jnp.argmax(rest, axis=-1)

    g1, g2, _, _ = top2(sel.reshape(n_tok, N_GROUPS, per))
    g_idx = jnp.argmax(g1 + g2, axis=-1)
    in_group = (jnp.arange(N_EXPERTS) // per)[None, :] == g_idx[:, None]
    _, _, e1, e2 = top2(jnp.where(in_group, sel, -jnp.inf))
    lane_e = jnp.arange(N_EXPERTS, dtype=jnp.int32)[None, :]
    s1 = jnp.sum(jnp.where(lane_e == e1[:, None], scores, 0.0), axis=1)
    s2 = jnp.sum(jnp.where(lane_e == e2[:, None], scores, 0.0), axis=1)
    gate = jnp.stack([s1, s2], axis=-1) / (s1 + s2)[:, None]

    n_assign = n_tok * TOP_K
    e_flat = jnp.concatenate([e1, e2]).astype(jnp.int32)
    tok_flat = jnp.tile(jnp.arange(n_tok, dtype=jnp.int32), TOP_K)
    assert n_assign % MOE_ROWS == 0
    onehot = (e_flat[:, None] == jnp.arange(N_EXPERTS, dtype=jnp.int32)[None, :]).astype(F32)
    tri = jnp.tril(jnp.ones((MOE_ROWS, MOE_ROWS), F32))
    within = jnp.einsum('ij,bjk->bik', tri, onehot.reshape(-1, MOE_ROWS, N_EXPERTS), precision=lax.Precision.HIGHEST)
    totals = within[:, -1, :]
    before = jnp.cumsum(totals, axis=0) - totals
    csum = (within + before[:, None, :]).reshape(n_assign, N_EXPERTS)
    counts = jnp.sum(totals, axis=0).astype(jnp.int32)
    rank = jnp.sum(csum * onehot, axis=1).astype(jnp.int32) - 1
    padded = (counts + MOE_ROWS - 1) // MOE_ROWS * MOE_ROWS
    pend = jnp.cumsum(padded)
    pstart = pend - padded
    dest = pstart[e_flat] + rank
    n_rows = -(-n_assign // MOE_ROWS) * MOE_ROWS + N_EXPERTS * MOE_ROWS
    nb = n_rows // MOE_ROWS
    row_tok = (jnp.arange(n_rows, dtype=jnp.int32) % n_tok).at[dest].set(tok_flat, unique_indices=True)
    xin = f_tok[row_tok]
    blk_row = jnp.arange(nb, dtype=jnp.int32) * MOE_ROWS
    blk_e = jnp.minimum(jnp.sum((pend[None, :] <= blk_row[:, None]).astype(jnp.int32), axis=1), N_EXPERTS - 1)
    n_used = (pend[-1:] // MOE_ROWS).astype(jnp.int32)
    y = _experts(xin, blk_e, n_used, layer, w1, w3, w2)
    return y[dest[:n_tok]], y[dest[n_tok:]], gate


def _combine_kernel(x_ref, y0_ref, y1_ref, gate_ref, g2_ref, o_ref):
    gate = gate_ref[0]
    y = gate[:, 0:1] * y0_ref[0].astype(F32) + gate[:, 1:2] * y1_ref[0].astype(F32)
    o_ref[0] = x_ref[0] + g2_ref[0] * y


def _combine(x, y0, y1, gate, g2, n_lat_tiles):
    bsz, n_rows, d = x.shape
    tm = ROW_TILE
    row = lambda b, i: (b, i, 0)
    return pl.pallas_call(
        _combine_kernel,
        grid=(bsz, n_rows // tm),
        in_specs=[pl.BlockSpec((1, tm, d), row), pl.BlockSpec((1, tm, d), row), pl.BlockSpec((1, tm, d), row),
                  pl.BlockSpec((1, tm, TOP_K), row),
                  pl.BlockSpec((1, 1, d), lambda b, i: (jnp.where(i >= n_lat_tiles, bsz, b), 0, 0))],
        out_specs=pl.BlockSpec((1, tm, d), row),
        out_shape=jax.ShapeDtypeStruct((bsz, n_rows, d), F32),
        compiler_params=_cparams(("parallel", "parallel")),
        name="combine",
    )(x, y0, y1, gate, g2)


def _rope_lanes(n_lat, n_ctx, rot_dim):
    rows = n_lat // GRID_W
    r = jnp.repeat(jnp.arange(rows), GRID_W)
    c = jnp.tile(jnp.arange(GRID_W), rows)
    n_freq = rot_dim // 4
    inv = ROPE_THETA ** (-jnp.arange(n_freq, dtype=F32) / n_freq)
    ang = jnp.concatenate([r[:, None] * inv, c[:, None] * inv], axis=-1)
    cos = jnp.repeat(jnp.cos(ang), 2, axis=1)
    sin = jnp.repeat(jnp.sin(ang), 2, axis=1) * jnp.tile(jnp.array([-1.0, 1.0], F32), rot_dim // 2)
    pad = ((0, n_ctx), (0, 0))
    return jnp.pad(cos, pad, constant_values=1.0), jnp.pad(sin, pad)


def _pad_cols(w, n):
    return jnp.pad(w, ((0, 0), (0, n - w.shape[1])))


def _pad_row(v, n):
    return jnp.pad(v, (0, n - v.shape[0])).reshape(1, n)


def _round_up(n, m):
    return -(-n // m) * m


def _head_blocks(w, n_heads, width, start, size):
    w = w.reshape(w.shape[0], n_heads, width)[:, :, start:start + size]
    return jnp.pad(w, ((0, 0), (0, 0), (0, LANE - size))).reshape(w.shape[0], n_heads * LANE)


def kernel(x, c, ctx, c_ctx, w_ada, b_ada, g_mix, g_ffn, e_w_in, e_w_out, e_g_q, e_g_k, e_conv_w, e_conv_b, e_b_gates, e_g_h, o_w_in, o_w_out, o_g_qc, o_g_kc, o_lam, o_g_sub, o_g_cq, o_w_uq, o_g_ckv, o_w_ukv, o_g_qd, o_g_kd, w_router, b_router, w1, w3, w2):
    bsz, n_lat, d = x.shape
    n_ctx = ctx.shape[1]
    t = n_lat + n_ctx
    depth = w_ada.shape[0]
    assert n_lat % ROW_TILE == 0 and n_ctx % ROW_TILE == 0 and n_ctx % B_CHUNK == 0
    n_lat_tiles = n_lat // ROW_TILE
    hd = D_NOPE + D_ROPE

    cos64, sin64 = _rope_lanes(n_lat, n_ctx, A_HEAD_DIM)
    cos64, sin64 = jnp.tile(cos64, (1, 2)), jnp.tile(sin64, (1, 2))
    cos_d, sin_d = _rope_lanes(n_lat, n_ctx, D_ROPE)
    cos_d = jnp.pad(cos_d, ((0, 0), (D_NOPE, LANE - hd)), constant_values=1.0)
    sin_d = jnp.pad(sin_d, ((0, 0), (D_NOPE, LANE - hd)))
    half = jnp.arange(LANE) // 64
    bd64 = (half[:, None] == half[None, :]).astype(BF16)
    wr_hi = w_router.astype(BF16)
    wr_lo = (w_router.astype(F32) - wr_hi.astype(F32)).astype(BF16)
    cond = jnp.concatenate([c, c_ctx[None, :]], axis=0)
    x_all = jnp.concatenate([x, ctx], axis=1)

    for layer in range(depth):
        last = layer == depth - 1
        j = layer // 2
        mod = (jax.nn.silu(cond) @ w_ada[layer] + b_ada[layer]).reshape(bsz + 1, 6, d)
        sh1, sc1, g1, sh2, sc2, g2 = [mod[:, i] for i in range(6)]
        mod_in = jnp.stack([sh1, sc1], axis=1)
        mod_out = jnp.stack([g1, sh2, sc2], axis=1)
        n_rows = n_lat if last else t
        if layer % 2 == 0:
            w_in = _pad_cols(e_w_in[j], _round_up(e_w_in.shape[-1], LANE)).astype(BF16)
            q, kd, vd, qk, vb, ob, gates = _inproj_even(
                x_all, mod_in, g_mix[layer], w_in, cos64, sin64, bd64, jnp.tile(e_g_q[j], 2).reshape(1, LANE),
                jnp.tile(e_g_k[j], 2).reshape(1, LANE), _pad_row(e_b_gates[j], LANE), n_lat_tiles)
            oa = _attn(q, kd, vd, mode="gqa", tq=256, tk=KEY_TILE, q_rows=(0, n_lat), k_rows=(0, t))
            if not last:
                oa = _attn(q, kd, vd, mode="gqa", tq=256, tk=KEY_TILE, q_rows=(n_lat, n_ctx), k_rows=(n_lat, n_ctx),
                           into=oa)
            hf, hb = _mlstm(qk, vb, gates, e_conv_w[j], e_conv_b[j].reshape(1, -1), n_lat)
            heads, consts, kern = [oa, hf, hb, ob], [e_g_h[j].reshape(1, HALF)], _outproj_even_kernel
            w_out = e_w_out[j]
        else:
            assert last, "the odd mixer is only wired as the final layer"
            lam_init = LAM_INIT_BASE[0] - LAM_INIT_BASE[1] * math.exp(-LAM_INIT_BASE[2] * layer)
            lq1, lk1, lq2, lk2 = o_lam[j].astype(F32)
            lam = jnp.exp(jnp.sum(lq1 * lk1)) - jnp.exp(jnp.sum(lq2 * lk2)) + lam_init
            w_in = _pad_cols(o_w_in[j], _round_up(o_w_in.shape[-1], LANE)).astype(BF16)
            consts = [bd64, jnp.tile(o_g_qc[j], 2).reshape(1, LANE), jnp.tile(o_g_kc[j], 2).reshape(1, LANE),
                      o_g_cq[j].reshape(1, D_Q_LORA), _head_blocks(o_w_uq[j], D_HEADS, hd, 0, hd).astype(BF16),
                      _pad_row(o_g_qd[j], LANE), o_g_ckv[j].reshape(1, D_KV_LORA),
                      _head_blocks(o_w_ukv[j], D_HEADS, D_NOPE + D_V_DIM, 0, D_NOPE).astype(BF16),
                      _head_blocks(o_w_ukv[j], D_HEADS, D_NOPE + D_V_DIM, D_NOPE, D_V_DIM).astype(BF16),
                      _pad_row(o_g_kd[j], LANE)]
            qc, kc, vc, qd, kdl, vdl = _inproj_odd(x_all, mod_in, g_mix[layer], w_in, cos64, sin64, cos_d, sin_d,
                                                   consts, n_lat_tiles)
            extra = (jnp.full((1, LANE), lam, F32), (o_g_sub[j] * (1.0 - lam_init)).reshape(1, LANE))
            oc = _attn(qc, kc, vc, mode="diff", tq=512, tk=KEY_TILE, q_rows=(0, n_lat), k_rows=(0, t), extra=extra)
            od = _attn(qd, kdl, vdl, mode="mla", tq=1024, tk=KEY_TILE, q_rows=(0, n_lat), k_rows=(0, t))
            heads, consts, kern = [oc, od], [], _outproj_odd_kernel
            w_out = o_w_out[j]
        x_new, f, logits = _outproj(kern, heads, consts, x_all, mod_out, g_ffn[layer], w_out.astype(BF16), wr_hi, wr_lo,
                                    n_rows, n_lat_tiles)
        y0, y1, gate = _moe(f.reshape(bsz * n_rows, d), logits.reshape(bsz * n_rows, N_EXPERTS), b_router,
                            layer, w1, w3, w2)
        x_all = _combine(x_new, y0.reshape(bsz, n_rows, d), y1.reshape(bsz, n_rows, d),
                         gate.reshape(bsz, n_rows, TOP_K), g2.reshape(bsz + 1, 1, d), n_lat_tiles)
    return x_all
```

```python
import functools
import math

import jax
import jax.numpy as jnp
import numpy as np
from jax import lax
from jax.experimental import pallas as pl
from jax.experimental.pallas import tpu as pltpu

F32 = jnp.float32
BF16 = jnp.bfloat16
EPS = 1e-6
LOG2E = 1.4426950408889634
NEG_BIG = -1e30
GRID_W = 64
ROPE_THETA = 10000.0
LAM_INIT_BASE = (0.8, 0.6, 0.3)

A_HEAD_DIM, A_HEADS, A_KV_HEADS = 64, 8, 2
B_HEAD_DIM, B_HEADS, B_CONV, B_CHUNK = 128, 4, 3, 128
C_HEAD_DIM, C_V_DIM, C_HEADS = 64, 128, 4
D_HEADS, D_Q_LORA, D_KV_LORA, D_NOPE, D_ROPE, D_V_DIM = 4, 256, 128, 64, 32, 128
N_EXPERTS, N_GROUPS, TOP_K, D_EXPERT = 32, 4, 2, 512
HALF = 512
HEAD64 = 64

LANE = 128
SUBLANE = 8
ROW_TILE = 256
MOE_ROWS = 512
KEY_TILE = 1280
ATTN_ROWS = 1024
ATTN_UNROLL = 2
VMEM_LIMIT = 48 * 1024 * 1024


def _cparams(sem):
    return pltpu.CompilerParams(dimension_semantics=sem, vmem_limit_bytes=VMEM_LIMIT)


def _blk(a, i):
    return a[:, i * LANE:(i + 1) * LANE]


def _modulated(x_ref, mod_ref, g_ref):
    x = x_ref[0]
    y = x * lax.rsqrt(jnp.mean(x * x, axis=-1, keepdims=True) + EPS) * g_ref[...]
    return (y * (1.0 + mod_ref[0, 1:2, :]) + mod_ref[0, 0:1, :]).astype(BF16)


def _pairswap(b, even):
    return jnp.where(even, pltpu.roll(b, LANE - 1, 1), pltpu.roll(b, 1, 1))


def _rope(b, cos, sin, even):
    return b * cos + _pairswap(b, even) * sin


def _head64_norm(b, bd, gain):
    ss = jnp.dot((b * b).astype(BF16), bd, preferred_element_type=F32)
    return b * lax.rsqrt(ss * (1.0 / HEAD64) + EPS) * gain


def _lane_norm(b, n_real, gain):
    ss = jnp.sum(b * b, axis=1, keepdims=True)
    return b * lax.rsqrt(ss * (1.0 / n_real) + EPS) * gain


def _inproj_even_kernel(x_ref, mod_ref, g_ref, w_ref, cos_ref, sin_ref, bd_ref, gq_ref, gk_ref, bg_ref,
                        q_ref, k_ref, v_ref, qk_ref, vb_ref, ob_ref, gt_ref):
    z = jnp.dot(_modulated(x_ref, mod_ref, g_ref), w_ref[...], preferred_element_type=F32)
    lane = lax.broadcasted_iota(jnp.int32, (z.shape[0], LANE), 1)
    even = (lane & 1) == 0
    lo = lane < HEAD64
    cos, sin, bd = cos_ref[...], sin_ref[...], bd_ref[...]
    q_scale = A_HEAD_DIM ** -0.5 * LOG2E
    for c in range(4):
        qb = _rope(_head64_norm(_blk(z, c), bd, gq_ref[...]), cos, sin, even) * q_scale
        q_ref[0, :, c * LANE:(c + 1) * LANE] = qb.astype(BF16)
    kn = _rope(_head64_norm(_blk(z, 4), bd, gk_ref[...]), cos, sin, even)
    for src, ref in ((kn, k_ref), (_blk(z, 5), v_ref)):
        sw = pltpu.roll(src, HEAD64, 1)
        ref[0] = jnp.concatenate([jnp.where(lo, src, sw), jnp.where(lo, sw, src)], axis=1).astype(BF16)
    qk_ref[0] = z[:, 6 * LANE:14 * LANE]
    vb_ref[0] = z[:, 14 * LANE:18 * LANE].astype(BF16)
    ob_ref[0] = z[:, 18 * LANE:22 * LANE]
    gt_ref[0] = _blk(z, 22) + bg_ref[...]


def _row_specs(tm, d, bsz, n_lat_tiles):
    return [pl.BlockSpec((1, tm, d), lambda b, i: (b, i, 0)),
            pl.BlockSpec((1, 2, d), lambda b, i: (jnp.where(i >= n_lat_tiles, bsz, b), 0, 0)),
            pl.BlockSpec((1, d), lambda b, i: (0, 0))]


def _const_spec(a):
    return pl.BlockSpec(a.shape, lambda b, i: (0,) * a.ndim)


def _inproj_even(x, mod, g, w, cos, sin, bd, gq, gk, bg, n_lat_tiles):
    bsz, t, d = x.shape
    tm = ROW_TILE
    tab = pl.BlockSpec((tm, LANE), lambda b, i: (i, 0))
    consts = [bd, gq, gk, bg]
    widths = [(4 * LANE, BF16), (2 * LANE, BF16), (2 * LANE, BF16), (8 * LANE, F32), (4 * LANE, BF16),
              (4 * LANE, F32), (LANE, F32)]
    return pl.pallas_call(
        _inproj_even_kernel,
        grid=(bsz, t // tm),
        in_specs=_row_specs(tm, d, bsz, n_lat_tiles) + [_const_spec(w), tab, tab] + [_const_spec(a) for a in consts],
        out_specs=[pl.BlockSpec((1, tm, n), lambda b, i: (b, i, 0)) for n, _ in widths],
        out_shape=[jax.ShapeDtypeStruct((bsz, t, n), dt) for n, dt in widths],
        compiler_params=_cparams(("parallel", "parallel")),
        name="inproj_even",
    )(x, mod, g.reshape(1, d), w, cos, sin, *consts)


def _inproj_odd_kernel(x_ref, mod_ref, g_ref, w_ref, cos_ref, sin_ref, cosd_ref, sind_ref, bd_ref, gqc_ref, gkc_ref,
                       gcq_ref, wuq_ref, gqd_ref, gckv_ref, wkn_ref, wvd_ref, gkd_ref,
                       qc_ref, kc_ref, vc_ref, qd_ref, kd_ref, vd_ref):
    z = jnp.dot(_modulated(x_ref, mod_ref, g_ref), w_ref[...], preferred_element_type=F32)
    lane = lax.broadcasted_iota(jnp.int32, (z.shape[0], LANE), 1)
    even = (lane & 1) == 0
    cos, sin, bd = cos_ref[...], sin_ref[...], bd_ref[...]
    cosd, sind = cosd_ref[...], sind_ref[...]
    c_scale = C_HEAD_DIM ** -0.5 * LOG2E
    d_scale = (D_NOPE + D_ROPE) ** -0.5 * LOG2E
    for c in range(4):
        sl = slice(c * LANE, (c + 1) * LANE)
        qc_ref[0, :, sl] = (_rope(_head64_norm(_blk(z, c), bd, gqc_ref[...]), cos, sin, even) * c_scale).astype(BF16)
        kc_ref[0, :, sl] = _rope(_head64_norm(_blk(z, 4 + c), bd, gkc_ref[...]), cos, sin, even).astype(BF16)
    vc_ref[0] = z[:, 8 * LANE:12 * LANE].astype(BF16)
    cq = z[:, 12 * LANE:14 * LANE]
    cq = cq * lax.rsqrt(jnp.mean(cq * cq, axis=-1, keepdims=True) + EPS) * gcq_ref[...]
    qd = jnp.dot(cq.astype(BF16), wuq_ref[...], preferred_element_type=F32)
    ckv = _blk(z, 14)
    ckv = (ckv * lax.rsqrt(jnp.mean(ckv * ckv, axis=-1, keepdims=True) + EPS) * gckv_ref[...]).astype(BF16)
    kn = jnp.dot(ckv, wkn_ref[...], preferred_element_type=F32)
    vd_ref[0] = jnp.dot(ckv, wvd_ref[...], preferred_element_type=F32).astype(BF16)
    kr = pltpu.roll(_blk(z, 15), D_NOPE, 1)
    hd = D_NOPE + D_ROPE
    for c in range(4):
        sl = slice(c * LANE, (c + 1) * LANE)
        qd_ref[0, :, sl] = (_rope(_lane_norm(_blk(qd, c), hd, gqd_ref[...]), cosd, sind, even) * d_scale).astype(BF16)
        kd_ref[0, :, sl] = _rope(_lane_norm(_blk(kn, c) + kr, hd, gkd_ref[...]), cosd, sind, even).astype(BF16)


def _inproj_odd(x, mod, g, w, cos, sin, cosd, sind, consts, n_lat_tiles):
    bsz, t, d = x.shape
    tm = ROW_TILE
    tab = pl.BlockSpec((tm, LANE), lambda b, i: (i, 0))
    return pl.pallas_call(
        _inproj_odd_kernel,
        grid=(bsz, t // tm),
        in_specs=(_row_specs(tm, d, bsz, n_lat_tiles) + [_const_spec(w), tab, tab, tab, tab]
                  + [_const_spec(a) for a in consts]),
        out_specs=[pl.BlockSpec((1, tm, HALF), lambda b, i: (b, i, 0)) for _ in range(6)],
        out_shape=[jax.ShapeDtypeStruct((bsz, t, HALF), BF16) for _ in range(6)],
        compiler_params=_cparams(("parallel", "parallel")),
        name="inproj_odd",
    )(x, mod, g.reshape(1, d), w, cos, sin, cosd, sind, *consts)


def _attn_kernel(*refs, mode, tq, tk, n_tiles):
    if mode == "diff":
        q_ref, k_ref, v_ref, lam_ref, gsub_ref, o_ref, q_sc, s_sc, m_sc, a_sc, l_sc, acc_sc = refs
    else:
        q_ref, k_ref, v_ref = refs[:3]
        o_ref, q_sc, s_sc, m_sc, a_sc, l_sc, acc_sc = refs[-7:]
    rows = q_sc.shape[0]
    n_blk = tk // LANE
    lo = lax.broadcasted_iota(jnp.int32, (tq, LANE), 1) < HEAD64

    if mode == "mla":
        q_sc[...] = q_ref[0]
    else:
        for p in range(q_ref.shape[2] // LANE):
            qb = _blk(q_ref[0], p)
            q_sc[2 * p * tq:(2 * p + 1) * tq] = jnp.where(lo, qb, jnp.zeros_like(qb))
            q_sc[(2 * p + 1) * tq:(2 * p + 2) * tq] = jnp.where(lo, jnp.zeros_like(qb), qb)

    def score_stage(j, slot):
        start = pl.multiple_of(j * tk, LANE)
        s = lax.dot_general(q_sc[...], k_ref[0, pl.ds(start, tk), :], (((1,), (1,)), ((), ())),
                            preferred_element_type=F32)
        s_sc[slot] = s
        m_tile = jnp.max(functools.reduce(jnp.maximum, [_blk(s, c) for c in range(n_blk)]), axis=1, keepdims=True)
        m_prev = m_sc[1 - slot]
        m_new = jnp.maximum(m_prev, m_tile)
        m_sc[slot] = m_new
        a_sc[slot] = jnp.exp2(m_prev - m_new)

    def value_stage(j, slot):
        start = pl.multiple_of(j * tk, LANE)
        m = m_sc[slot]
        alpha = a_sc[slot]
        p_blocks = [jnp.exp2((s_sc[slot, :, c * LANE:(c + 1) * LANE] - m).astype(BF16)) for c in range(n_blk)]
        l_sc[...] = alpha * l_sc[...] + functools.reduce(jnp.add, p_blocks).astype(F32)
        p = jnp.concatenate(p_blocks, axis=1)
        acc_sc[...] = alpha * acc_sc[...] + jnp.dot(p, v_ref[0, pl.ds(start, tk), :], preferred_element_type=F32)

    m_sc[1] = jnp.full((rows, LANE), NEG_BIG, F32)
    l_sc[...] = jnp.zeros((rows, LANE), F32)
    acc_sc[...] = jnp.zeros((rows, LANE), F32)
    score_stage(0, 0)
    peel = (n_tiles - 1) % ATTN_UNROLL
    for j in range(peel):
        value_stage(j, j % 2)
        score_stage(j + 1, (j + 1) % 2)

    def body(t, carry):
        for u in range(ATTN_UNROLL):
            j = ATTN_UNROLL * t + peel + u
            value_stage(j, (peel + u) % 2)
            score_stage(j + 1, (peel + u + 1) % 2)
        return carry

    lax.fori_loop(0, (n_tiles - 1) // ATTN_UNROLL, body, 0)
    value_stage(n_tiles - 1, (n_tiles - 1) % 2)

    o = acc_sc[...] / jnp.sum(l_sc[...], axis=1, keepdims=True)
    head = lambda h: o[h * tq:(h + 1) * tq]
    if mode == "gqa":
        for p in range(o_ref.shape[2] // LANE):
            o_ref[0, :, p * LANE:(p + 1) * LANE] = jnp.where(lo, head(2 * p), head(2 * p + 1)).astype(o_ref.dtype)
    elif mode == "diff":
        dlt = head(0) - lam_ref[...] * head(1)
        o_ref[0] = (dlt * lax.rsqrt(jnp.mean(dlt * dlt, axis=1, keepdims=True) + EPS) * gsub_ref[...]).astype(o_ref.dtype)
    else:
        o_ref[0] = o.astype(o_ref.dtype)


def _attn(q, k, v, *, mode, tq, tk, q_rows, k_rows, extra=(), into=None):
    bsz, t_q, wq = q.shape
    qw = 2 * LANE if mode == "gqa" else LANE
    rows = {"gqa": 4, "diff": 2, "mla": 1}[mode] * tq
    q0, nq = q_rows
    k0, nk = k_rows
    tk = min(tk, nk)
    assert q0 % tq == 0 and nq % tq == 0 and k0 % nk == 0 and nk % tk == 0
    qmap = lambda b, g, i: (b, q0 // tq + i, g)
    kmap = lambda b, g, i: (b, k0 // nk, g)
    in_specs = [pl.BlockSpec((1, tq, qw), qmap), pl.BlockSpec((1, nk, LANE), kmap), pl.BlockSpec((1, nk, LANE), kmap)]
    in_specs += [pl.BlockSpec(a.shape, lambda b, g, i: (0, 0)) for a in extra]
    args = [q, k, v, *extra]
    aliases = {}
    if into is not None:
        in_specs.append(pl.BlockSpec(memory_space=pl.ANY))
        aliases = {len(args): 0}
        args.append(into)

    def kern(*refs):
        if into is not None:
            refs = refs[:len(args) - 1] + refs[len(args):]
        _attn_kernel(*refs, mode=mode, tq=tq, tk=tk, n_tiles=nk // tk)

    return pl.pallas_call(
        kern,
        grid=(bsz, wq // qw, nq // tq),
        in_specs=in_specs,
        out_specs=pl.BlockSpec((1, tq, qw), qmap),
        out_shape=jax.ShapeDtypeStruct((bsz, t_q, wq), BF16),
        scratch_shapes=[pltpu.VMEM((rows, LANE), BF16), pltpu.VMEM((2, rows, tk), F32),
                        pltpu.VMEM((2, rows, LANE), F32), pltpu.VMEM((2, rows, LANE), F32),
                        pltpu.VMEM((rows, LANE), F32), pltpu.VMEM((rows, LANE), F32)],
        input_output_aliases=aliases,
        compiler_params=_cparams(("parallel", "parallel", "parallel")),
        name="attn_" + mode,
    )(*args)


def _mlstm_kernel(qf_ref, pf_ref, nf_ref, vf_ref, gf_ref, qb_ref, pb_ref, nb_ref, vb_ref, gb_ref, cw_ref, cb_ref,
                  hf_ref, hb_ref, c_sc, m_sc, *, n_lat_chunks, n_chunks):
    L, dh = B_CHUNK, B_HEAD_DIM
    c = pl.program_id(1)
    n_ctx_chunks = n_chunks - n_lat_chunks
    fwd_chunk = jnp.where(c < n_ctx_chunks, n_lat_chunks + c, c - n_ctx_chunks)
    bwd_chunk = n_chunks - 1 - c

    @pl.when(c == 0)
    def _():
        c_sc[...] = jnp.zeros(c_sc.shape, F32)
        m_sc[...] = jnp.zeros(m_sc.shape, F32)

    row = lax.broadcasted_iota(jnp.int32, (L, L), 0)
    col = lax.broadcasted_iota(jnp.int32, (L, L), 1)
    upper = (row <= col).astype(F32)
    lower = (col <= row).astype(F32)
    rid = lax.broadcasted_iota(jnp.int32, (L, 1), 0)
    glane = lax.broadcasted_iota(jnp.int32, (L, LANE), 1)
    is_f = (glane // B_HEADS) % 2 == 1
    ones_col = (lax.broadcasted_iota(jnp.int32, (L, dh), 1) == 0).astype(BF16)
    hi = lax.Precision.HIGHEST

    def direction(qk_ref, p_ref, n_ref, v_ref, g_ref, h_ref, chunk, reverse):
        first = (chunk == 0) | (chunk == n_lat_chunks)
        last = (chunk == n_lat_chunks - 1) | (chunk == n_chunks - 1)
        x = qk_ref[0]
        prev_row = jnp.where(first, 0.0, p_ref[0, SUBLANE - 1:SUBLANE, :])
        next_row = jnp.where(last, 0.0, n_ref[0, 0:1, :])
        x_prev = jnp.where(rid == 0, prev_row, pltpu.roll(x, 1, 0))
        x_next = jnp.where(rid == L - 1, next_row, pltpu.roll(x, L - 1, 0))
        y = x_prev * cw_ref[0:1, :] + x * cw_ref[1:2, :] + x_next * cw_ref[2:3, :] + cb_ref[...]
        y = y * jax.nn.sigmoid(y)
        g = g_ref[0]
        g = jnp.where(is_f, jnp.minimum(g, 0.0) - jnp.log(1.0 + jnp.exp(-jnp.abs(g))), g)
        g_t = jnp.transpose(g)
        cum_r = jnp.dot(g_t, upper, preferred_element_type=F32, precision=hi)
        cum_c = jnp.dot(lower, g, preferred_element_type=F32, precision=hi)
        if reverse:
            cum_r = cum_r[:, L - 1:L] - cum_r + g_t
            cum_c = cum_c[L - 1:L, :] - cum_c + g
        gi = 2 * B_HEADS if reverse else 0
        mask = (col >= row) if reverse else (col <= row)
        for h in range(B_HEADS):
            idx = (B_HEADS if reverse else 0) + h
            q = _blk(y, h).astype(BF16)
            k = _blk(y, B_HEADS + h) * (dh ** -0.5)
            v_ext = jnp.concatenate([_blk(v_ref[0], h), ones_col], axis=1)
            i_r = g_t[gi + h:gi + h + 1, :]
            b_r = cum_r[gi + B_HEADS + h:gi + B_HEADS + h + 1, :]
            i_c = g[:, gi + h:gi + h + 1]
            b_c = cum_c[:, gi + B_HEADS + h:gi + B_HEADS + h + 1]
            m_prev = m_sc[idx][:, 0:1]
            c_ext = c_sc[idx]

            log_w = jnp.where(mask, b_c - b_r + i_r, -jnp.inf)
            log_inter = b_c + m_prev
            m_row = jnp.maximum(log_inter, jnp.max(log_w, axis=1, keepdims=True))
            w_intra = jnp.exp(log_w - m_row)
            a_inter = jnp.exp(log_inter - m_row)
            sc = lax.dot_general(q, k.astype(BF16), (((1,), (1,)), ((), ())), preferred_element_type=F32) * w_intra
            q_c = jnp.dot(q, c_ext.astype(BF16), preferred_element_type=F32)
            s_v = jnp.dot(sc.astype(BF16), v_ext, preferred_element_type=F32)
            num = a_inter * q_c[:, :dh] + s_v[:, :dh]
            den = a_inter * q_c[:, dh:dh + 1] + s_v[:, dh:dh + 1]
            h_ref[0, :, h * dh:(h + 1) * dh] = num / jnp.maximum(jnp.abs(den), jnp.exp(-m_row))

            b_last = b_r[:, 0:1] if reverse else b_r[:, L - 1:L]
            m_new = jnp.maximum(b_last + m_prev, jnp.max(b_last - b_r + i_r, axis=1, keepdims=True))
            g_c = jnp.exp(b_last - b_c + i_c - m_new)
            decay = jnp.exp(b_last + m_prev - m_new)
            upd = jnp.dot(jnp.transpose(k).astype(BF16), (g_c * v_ext.astype(F32)).astype(BF16),
                          preferred_element_type=F32)
            c_sc[idx] = decay * c_ext + upd
            m_sc[idx] = jnp.broadcast_to(m_new, (1, LANE))

    direction(qf_ref, pf_ref, nf_ref, vf_ref, gf_ref, hf_ref, fwd_chunk, False)
    direction(qb_ref, pb_ref, nb_ref, vb_ref, gb_ref, hb_ref, bwd_chunk, True)


def _mlstm(qk, vb, gates, conv_w, conv_b, n_lat):
    bsz, t, _ = qk.shape
    L = B_CHUNK
    n_chunks, n_lat_chunks = t // L, n_lat // L
    n_ctx_chunks = n_chunks - n_lat_chunks
    per = L // SUBLANE
    fwd = lambda c: jnp.where(c < n_ctx_chunks, n_lat_chunks + c, c - n_ctx_chunks)
    bwd = lambda c: n_chunks - 1 - c

    def specs(order):
        return [
            pl.BlockSpec((1, L, 2 * HALF), lambda b, c: (b, order(c), 0)),
            pl.BlockSpec((1, SUBLANE, 2 * HALF), lambda b, c: (b, jnp.maximum(order(c) * per - 1, 0), 0)),
            pl.BlockSpec((1, SUBLANE, 2 * HALF), lambda b, c: (b, jnp.minimum((order(c) + 1) * per, t // SUBLANE - 1), 0)),
            pl.BlockSpec((1, L, HALF), lambda b, c: (b, order(c), 0)),
            pl.BlockSpec((1, L, LANE), lambda b, c: (b, order(c), 0)),
        ]

    const = lambda a: pl.BlockSpec(a.shape, lambda b, c: (0, 0))
    return pl.pallas_call(
        functools.partial(_mlstm_kernel, n_lat_chunks=n_lat_chunks, n_chunks=n_chunks),
        grid=(bsz, n_chunks),
        in_specs=specs(fwd) + specs(bwd) + [const(conv_w), const(conv_b)],
        out_specs=[pl.BlockSpec((1, L, HALF), lambda b, c: (b, fwd(c), 0)),
                   pl.BlockSpec((1, L, HALF), lambda b, c: (b, bwd(c), 0))],
        out_shape=[jax.ShapeDtypeStruct((bsz, t, HALF), F32)] * 2,
        scratch_shapes=[pltpu.VMEM((2 * B_HEADS, B_HEAD_DIM, 2 * B_HEAD_DIM), F32),
                        pltpu.VMEM((2 * B_HEADS, 1, LANE), F32)],
        compiler_params=_cparams(("parallel", "arbitrary")),
        name="mlstm",
    )(qk, qk, qk, vb, gates, qk, qk, qk, vb, gates, conv_w, conv_b)


def _out_tail(mix, x_ref, mod_ref, g_ref, w_ref, wrh_ref, wrl_ref, xo_ref, f_ref, lg_ref):
    o = jnp.dot(mix, w_ref[...], preferred_element_type=F32)
    xn = x_ref[0] + mod_ref[0, 0:1, :] * o
    xo_ref[0] = xn
    y = xn * lax.rsqrt(jnp.mean(xn * xn, axis=-1, keepdims=True) + EPS) * g_ref[...]
    f = y * (1.0 + mod_ref[0, 2:3, :]) + mod_ref[0, 1:2, :]
    f_hi = f.astype(BF16)
    f_ref[0] = f_hi
    f_lo = (f - f_hi.astype(F32)).astype(BF16)
    lg_ref[0] = (jnp.dot(f_hi, wrh_ref[...], preferred_element_type=F32)
                 + jnp.dot(f_lo, wrh_ref[...], preferred_element_type=F32)
                 + jnp.dot(f_hi, wrl_ref[...], preferred_element_type=F32))


def _outproj_even_kernel(oa_ref, hf_ref, hb_ref, ob_ref, gh_ref, *tail):
    parts = [oa_ref[0]]
    for h in range(B_HEADS):
        hs = _blk(hf_ref[0], h) + _blk(hb_ref[0], h)
        hn = hs * lax.rsqrt(jnp.mean(hs * hs, axis=1, keepdims=True) + EPS) * _blk(gh_ref[...], h)
        parts.append((hn * jax.nn.sigmoid(_blk(ob_ref[0], h))).astype(BF16))
    _out_tail(jnp.concatenate(parts, axis=1), *tail)


def _outproj_odd_kernel(oc_ref, od_ref, *tail):
    _out_tail(jnp.concatenate([oc_ref[0], od_ref[0]], axis=1), *tail)


def _outproj(kern, heads, consts, x, mod, g, w, wr_hi, wr_lo, n_rows, n_lat_tiles):
    bsz, _, d = x.shape
    ne = wr_hi.shape[1]
    tm = ROW_TILE
    row = lambda b, i: (b, i, 0)
    return pl.pallas_call(
        kern,
        grid=(bsz, n_rows // tm),
        in_specs=([pl.BlockSpec((1, tm, HALF), row) for _ in heads] + [_const_spec(a) for a in consts]
                  + [pl.BlockSpec((1, tm, d), row),
                     pl.BlockSpec((1, 3, d), lambda b, i: (jnp.where(i >= n_lat_tiles, bsz, b), 0, 0)),
                     pl.BlockSpec((1, d), lambda b, i: (0, 0)), _const_spec(w), _const_spec(wr_hi),
                     _const_spec(wr_lo)]),
        out_specs=[pl.BlockSpec((1, tm, d), row), pl.BlockSpec((1, tm, d), row), pl.BlockSpec((1, tm, ne), row)],
        out_shape=[jax.ShapeDtypeStruct((bsz, n_rows, d), F32), jax.ShapeDtypeStruct((bsz, n_rows, d), BF16),
                   jax.ShapeDtypeStruct((bsz, n_rows, ne), F32)],
        compiler_params=_cparams(("parallel", "parallel")),
        name="outproj",
    )(*heads, *consts, x, mod, g.reshape(1, d), w, wr_hi, wr_lo)


def _expert_kernel(blk_e_ref, n_used_ref, x_ref, w1_ref, w3_ref, w2_ref, y_ref, w1_sc, w3_sc, w2_sc):
    i = pl.program_id(0)

    @pl.when((i == 0) | (blk_e_ref[i] != blk_e_ref[jnp.maximum(i - 1, 0)]))
    def _():
        w1_sc[...] = w1_ref[0, 0].astype(BF16)
        w3_sc[...] = w3_ref[0, 0].astype(BF16)
        w2_sc[...] = w2_ref[0, 0].astype(BF16)

    @pl.when(i < n_used_ref[0])
    def _():
        xb = x_ref[...]
        h1 = jnp.dot(xb, w1_sc[...], preferred_element_type=F32)
        h3 = jnp.dot(xb, w3_sc[...], preferred_element_type=F32)
        a = (h1 * jax.nn.sigmoid(h1) * h3).astype(BF16)
        y_ref[...] = jnp.dot(a, w2_sc[...], preferred_element_type=F32).astype(y_ref.dtype)

    @pl.when(i >= n_used_ref[0])
    def _():
        y_ref[...] = jnp.zeros(y_ref.shape, y_ref.dtype)


def _experts(xin, blk_e, n_used, layer, w1, w3, w2):
    n_rows, d = xin.shape
    de = w1.shape[-1]
    nb = n_rows // MOE_ROWS
    grid_spec = pltpu.PrefetchScalarGridSpec(
        num_scalar_prefetch=2,
        grid=(nb,),
        in_specs=[
            pl.BlockSpec((MOE_ROWS, d), lambda i, be, nu: (i, 0)),
            pl.BlockSpec((1, 1, d, de), lambda i, be, nu: (layer, be[i], 0, 0)),
            pl.BlockSpec((1, 1, d, de), lambda i, be, nu: (layer, be[i], 0, 0)),
            pl.BlockSpec((1, 1, de, d), lambda i, be, nu: (layer, be[i], 0, 0)),
        ],
        out_specs=pl.BlockSpec((MOE_ROWS, d), lambda i, be, nu: (i, 0)),
        scratch_shapes=[pltpu.VMEM((d, de), BF16), pltpu.VMEM((d, de), BF16), pltpu.VMEM((de, d), BF16)],
    )
    return pl.pallas_call(
        _expert_kernel,
        grid_spec=grid_spec,
        out_shape=jax.ShapeDtypeStruct((n_rows, d), BF16),
        compiler_params=_cparams(("arbitrary",)),
        name="experts",
    )(blk_e, n_used, xin, w1, w3, w2)


def _moe(f_tok, logits, b_router, layer, w1, w3, w2):
    n_tok, d = f_tok.shape
    per = N_EXPERTS // N_GROUPS
    scores = jax.nn.sigmoid(logits)
    sel = scores + b_router.astype(F32)

    def top2(a):
        i1 = jnp.argmax(a, axis=-1)
        hit = lax.broadcasted_iota(jnp.int32, a.shape, a.ndim - 1) == i1[..., None]
        rest = jnp.where(hit, -jnp.inf, a)
        return jnp.max(a, axis=-1), jnp.max(rest, axis=-1), i1, jnp.argmax(rest, axis=-1)

    g1, g2, _, _ = top2(sel.reshape(n_tok, N_GROUPS, per))
    g_idx = jnp.argmax(g1 + g2, axis=-1)
    in_group = (jnp.arange(N_EXPERTS) // per)[None, :] == g_idx[:, None]
    _, _, e1, e2 = top2(jnp.where(in_group, sel, -jnp.inf))
    lane_e = jnp.arange(N_EXPERTS, dtype=jnp.int32)[None, :]
    s1 = jnp.sum(jnp.where(lane_e == e1[:, None], scores, 0.0), axis=1)
    s2 = jnp.sum(jnp.where(lane_e == e2[:, None], scores, 0.0), axis=1)
    gate = jnp.stack([s1, s2], axis=-1) / (s1 + s2)[:, None]

    n_assign = n_tok * TOP_K
    e_flat = jnp.concatenate([e1, e2]).astype(jnp.int32)
    assert n_assign % MOE_ROWS == 0
    onehot = (e_flat[:, None] == jnp.arange(N_EXPERTS, dtype=jnp.int32)[None, :]).astype(F32)
    tri = jnp.tril(jnp.ones((MOE_ROWS, MOE_ROWS), F32))
    within = jnp.einsum('ij,bjk->bik', tri, onehot.reshape(-1, MOE_ROWS, N_EXPERTS), precision=lax.Precision.HIGHEST)
    totals = within[:, -1, :]
    before = jnp.cumsum(totals, axis=0) - totals
    csum = (within + before[:, None, :]).reshape(n_assign, N_EXPERTS)
    counts = jnp.sum(totals, axis=0).astype(jnp.int32)
    rank = jnp.sum(csum * onehot, axis=1).astype(jnp.int32) - 1
    padded = (counts + MOE_ROWS - 1) // MOE_ROWS * MOE_ROWS
    pend = jnp.cumsum(padded)
    pstart = pend - padded
    dest = pstart[e_flat] + rank
    n_rows = -(-n_assign // MOE_ROWS) * MOE_ROWS + N_EXPERTS * MOE_ROWS
    nb = n_rows // MOE_ROWS
    blk_row = jnp.arange(nb, dtype=jnp.int32) * MOE_ROWS
    blk_e = jnp.minimum(jnp.sum((pend[None, :] <= blk_row[:, None]).astype(jnp.int32), axis=1), N_EXPERTS - 1)
    order = jnp.argsort(e_flat, stable=True).astype(jnp.int32)
    row = jnp.arange(n_rows, dtype=jnp.int32)
    row_e = jnp.repeat(blk_e, MOE_ROWS)
    pos = row - pstart[row_e]
    src = jnp.minimum((jnp.cumsum(counts) - counts)[row_e] + pos, n_assign - 1)
    row_tok = jnp.where(pos < counts[row_e], order[src], row) % n_tok
    xin = f_tok[row_tok]
    n_used = (pend[-1:] // MOE_ROWS).astype(jnp.int32)
    y = _experts(xin, blk_e, n_used, layer, w1, w3, w2)
    return y[dest[:n_tok]], y[dest[n_tok:]], gate


def _combine_kernel(x_ref, y0_ref, y1_ref, gate_ref, g2_ref, o_ref):
    gate = gate_ref[0]
    y = gate[:, 0:1] * y0_ref[0].astype(F32) + gate[:, 1:2] * y1_ref[0].astype(F32)
    o_ref[0] = x_ref[0] + g2_ref[0] * y


def _combine(x, y0, y1, gate, g2, n_lat_tiles):
    bsz, n_rows, d = x.shape
    tm = ROW_TILE
    row = lambda b, i: (b, i, 0)
    return pl.pallas_call(
        _combine_kernel,
        grid=(bsz, n_rows // tm),
        in_specs=[pl.BlockSpec((1, tm, d), row), pl.BlockSpec((1, tm, d), row), pl.BlockSpec((1, tm, d), row),
                  pl.BlockSpec((1, tm, TOP_K), row),
                  pl.BlockSpec((1, 1, d), lambda b, i: (jnp.where(i >= n_lat_tiles, bsz, b), 0, 0))],
        out_specs=pl.BlockSpec((1, tm, d), row),
        out_shape=jax.ShapeDtypeStruct((bsz, n_rows, d), F32),
        compiler_params=_cparams(("parallel", "parallel")),
        name="combine",
    )(x, y0, y1, gate, g2)


def _rope_lanes(n_lat, n_ctx, rot_dim):
    rows = n_lat // GRID_W
    r = jnp.repeat(jnp.arange(rows), GRID_W)
    c = jnp.tile(jnp.arange(GRID_W), rows)
    n_freq = rot_dim // 4
    inv = ROPE_THETA ** (-jnp.arange(n_freq, dtype=F32) / n_freq)
    ang = jnp.concatenate([r[:, None] * inv, c[:, None] * inv], axis=-1)
    cos = jnp.repeat(jnp.cos(ang), 2, axis=1)
    sin = jnp.repeat(jnp.sin(ang), 2, axis=1) * jnp.tile(jnp.array([-1.0, 1.0], F32), rot_dim // 2)
    pad = ((0, n_ctx), (0, 0))
    return jnp.pad(cos, pad, constant_values=1.0), jnp.pad(sin, pad)


def _pad_cols(w, n):
    return jnp.pad(w, ((0, 0), (0, n - w.shape[1])))


def _pad_row(v, n):
    return jnp.pad(v, (0, n - v.shape[0])).reshape(1, n)


def _round_up(n, m):
    return -(-n // m) * m


def _head_blocks(w, n_heads, width, start, size):
    w = w.reshape(w.shape[0], n_heads, width)[:, :, start:start + size]
    return jnp.pad(w, ((0, 0), (0, 0), (0, LANE - size))).reshape(w.shape[0], n_heads * LANE)


def kernel(x, c, ctx, c_ctx, w_ada, b_ada, g_mix, g_ffn, e_w_in, e_w_out, e_g_q, e_g_k, e_conv_w, e_conv_b, e_b_gates, e_g_h, o_w_in, o_w_out, o_g_qc, o_g_kc, o_lam, o_g_sub, o_g_cq, o_w_uq, o_g_ckv, o_w_ukv, o_g_qd, o_g_kd, w_router, b_router, w1, w3, w2):
    bsz, n_lat, d = x.shape
    n_ctx = ctx.shape[1]
    t = n_lat + n_ctx
    depth = w_ada.shape[0]
    assert n_lat % ROW_TILE == 0 and n_ctx % ROW_TILE == 0 and n_ctx % B_CHUNK == 0
    n_lat_tiles = n_lat // ROW_TILE
    hd = D_NOPE + D_ROPE

    cos64, sin64 = _rope_lanes(n_lat, n_ctx, A_HEAD_DIM)
    cos64, sin64 = jnp.tile(cos64, (1, 2)), jnp.tile(sin64, (1, 2))
    cos_d, sin_d = _rope_lanes(n_lat, n_ctx, D_ROPE)
    cos_d = jnp.pad(cos_d, ((0, 0), (D_NOPE, LANE - hd)), constant_values=1.0)
    sin_d = jnp.pad(sin_d, ((0, 0), (D_NOPE, LANE - hd)))
    half = jnp.arange(LANE) // HEAD64
    bd64 = (half[:, None] == half[None, :]).astype(BF16)
    wr_hi = w_router.astype(BF16)
    wr_lo = (w_router.astype(F32) - wr_hi.astype(F32)).astype(BF16)
    cond = jnp.concatenate([c, c_ctx[None, :]], axis=0)
    x_all = jnp.concatenate([x, ctx], axis=1)

    for layer in range(depth):
        last = layer == depth - 1
        j = layer // 2
        mod = (jax.nn.silu(cond) @ w_ada[layer] + b_ada[layer]).reshape(bsz + 1, 6, d)
        sh1, sc1, g1, sh2, sc2, g2 = [mod[:, i] for i in range(6)]
        mod_in = jnp.stack([sh1, sc1], axis=1)
        mod_out = jnp.stack([g1, sh2, sc2], axis=1)
        n_rows = n_lat if last else t
        if layer % 2 == 0:
            w_in = _pad_cols(e_w_in[j], _round_up(e_w_in.shape[-1], LANE)).astype(BF16)
            q, kd, vd, qk, vb, ob, gates = _inproj_even(
                x_all, mod_in, g_mix[layer], w_in, cos64, sin64, bd64, jnp.tile(e_g_q[j], 2).reshape(1, LANE),
                jnp.tile(e_g_k[j], 2).reshape(1, LANE), _pad_row(e_b_gates[j], LANE), n_lat_tiles)
            oa = _attn(q, kd, vd, mode="gqa", tq=ATTN_ROWS // 4, tk=KEY_TILE, q_rows=(0, n_lat), k_rows=(0, t))
            if not last:
                oa = _attn(q, kd, vd, mode="gqa", tq=ATTN_ROWS // 4, tk=KEY_TILE, q_rows=(n_lat, n_ctx), k_rows=(n_lat, n_ctx),
                           into=oa)
            hf, hb = _mlstm(qk, vb, gates, e_conv_w[j], e_conv_b[j].reshape(1, -1), n_lat)
            heads, consts, kern = [oa, hf, hb, ob], [e_g_h[j].reshape(1, HALF)], _outproj_even_kernel
            w_out = e_w_out[j]
        else:
            assert last, "the odd mixer is only wired as the final layer"
            lam_init = LAM_INIT_BASE[0] - LAM_INIT_BASE[1] * math.exp(-LAM_INIT_BASE[2] * layer)
            lq1, lk1, lq2, lk2 = o_lam[j].astype(F32)
            lam = jnp.exp(jnp.sum(lq1 * lk1)) - jnp.exp(jnp.sum(lq2 * lk2)) + lam_init
            w_in = _pad_cols(o_w_in[j], _round_up(o_w_in.shape[-1], LANE)).astype(BF16)
            consts = [bd64, jnp.tile(o_g_qc[j], 2).reshape(1, LANE), jnp.tile(o_g_kc[j], 2).reshape(1, LANE),
                      o_g_cq[j].reshape(1, D_Q_LORA), _head_blocks(o_w_uq[j], D_HEADS, hd, 0, hd).astype(BF16),
                      _pad_row(o_g_qd[j], LANE), o_g_ckv[j].reshape(1, D_KV_LORA),
                      _head_blocks(o_w_ukv[j], D_HEADS, D_NOPE + D_V_DIM, 0, D_NOPE).astype(BF16),
                      _head_blocks(o_w_ukv[j], D_HEADS, D_NOPE + D_V_DIM, D_NOPE, D_V_DIM).astype(BF16),
                      _pad_row(o_g_kd[j], LANE)]
            qc, kc, vc, qd, kdl, vdl = _inproj_odd(x_all, mod_in, g_mix[layer], w_in, cos64, sin64, cos_d, sin_d,
                                                   consts, n_lat_tiles)
            extra = (jnp.full((1, LANE), lam, F32), (o_g_sub[j] * (1.0 - lam_init)).reshape(1, LANE))
            oc = _attn(qc, kc, vc, mode="diff", tq=ATTN_ROWS // 2, tk=KEY_TILE, q_rows=(0, n_lat), k_rows=(0, t), extra=extra)
            od = _attn(qd, kdl, vdl, mode="mla", tq=ATTN_ROWS, tk=KEY_TILE, q_rows=(0, n_lat), k_rows=(0, t))
            heads, consts, kern = [oc, od], [], _outproj_odd_kernel
            w_out = o_w_out[j]
        x_new, f, logits = _outproj(kern, heads, consts, x_all, mod_out, g_ffn[layer], w_out.astype(BF16), wr_hi, wr_lo,
                                    n_rows, n_lat_tiles)
        y0, y1, gate = _moe(f.reshape(bsz * n_rows, d), logits.reshape(bsz * n_rows, N_EXPERTS), b_router,
                            layer, w1, w3, w2)
        x_all = _combine(x_new, y0.reshape(bsz, n_rows, d), y1.reshape(bsz, n_rows, d),
                         gate.reshape(bsz, n_rows, TOP_K), g2.reshape(bsz + 1, 1, d), n_lat_tiles)
    return x_all
```

```python
import functools
import math

import jax
import jax.numpy as jnp
import numpy as np
from jax import lax
from jax.experimental import pallas as pl
from jax.experimental.pallas import tpu as pltpu

F32 = jnp.float32
BF16 = jnp.bfloat16
EPS = 1e-6
LOG2E = 1.4426950408889634
NEG_BIG = -1e30
GRID_W = 64
ROPE_THETA = 10000.0
LAM_INIT_BASE = (0.8, 0.6, 0.3)

A_HEAD_DIM, A_HEADS, A_KV_HEADS = 64, 8, 2
B_HEAD_DIM, B_HEADS, B_CONV, B_CHUNK = 128, 4, 3, 128
C_HEAD_DIM, C_V_DIM, C_HEADS = 64, 128, 4
D_HEADS, D_Q_LORA, D_KV_LORA, D_NOPE, D_ROPE, D_V_DIM = 4, 256, 128, 64, 32, 128
N_EXPERTS, N_GROUPS, TOP_K, D_EXPERT = 32, 4, 2, 512
HALF = 512
HEAD64 = 64

LANE = 128
SUBLANE = 8
ROW_TILE = 256
MOE_ROWS = 512
KEY_TILE = 1280
ATTN_ROWS = 2048
ATTN_UNROLL = 2
VMEM_LIMIT = 60 * 1024 * 1024


def _cparams(sem):
    return pltpu.CompilerParams(dimension_semantics=sem, vmem_limit_bytes=VMEM_LIMIT)


def _blk(a, i):
    return a[:, i * LANE:(i + 1) * LANE]


def _modulated(x_ref, mod_ref, g_ref):
    x = x_ref[0]
    y = x * lax.rsqrt(jnp.mean(x * x, axis=-1, keepdims=True) + EPS) * g_ref[...]
    return (y * (1.0 + mod_ref[0, 1:2, :]) + mod_ref[0, 0:1, :]).astype(BF16)


def _pairswap(b, even):
    return jnp.where(even, pltpu.roll(b, LANE - 1, 1), pltpu.roll(b, 1, 1))


def _rope(b, cos, sin, even):
    return b * cos + _pairswap(b, even) * sin


def _head64_norm(b, bd, gain):
    ss = jnp.dot((b * b).astype(BF16), bd, preferred_element_type=F32)
    return b * lax.rsqrt(ss * (1.0 / HEAD64) + EPS) * gain


def _lane_norm(b, n_real, gain):
    ss = jnp.sum(b * b, axis=1, keepdims=True)
    return b * lax.rsqrt(ss * (1.0 / n_real) + EPS) * gain


def _inproj_even_kernel(x_ref, mod_ref, g_ref, w_ref, cos_ref, sin_ref, bd_ref, gq_ref, gk_ref, bg_ref,
                        q_ref, k_ref, v_ref, qk_ref, vb_ref, ob_ref, gt_ref):
    z = jnp.dot(_modulated(x_ref, mod_ref, g_ref), w_ref[...], preferred_element_type=F32)
    lane = lax.broadcasted_iota(jnp.int32, (z.shape[0], LANE), 1)
    even = (lane & 1) == 0
    lo = lane < HEAD64
    cos, sin, bd = cos_ref[...], sin_ref[...], bd_ref[...]
    q_scale = A_HEAD_DIM ** -0.5 * LOG2E
    for c in range(4):
        qb = _rope(_head64_norm(_blk(z, c), bd, gq_ref[...]), cos, sin, even) * q_scale
        q_ref[0, :, c * LANE:(c + 1) * LANE] = qb.astype(BF16)
    kn = _rope(_head64_norm(_blk(z, 4), bd, gk_ref[...]), cos, sin, even)
    for src, ref in ((kn, k_ref), (_blk(z, 5), v_ref)):
        sw = pltpu.roll(src, HEAD64, 1)
        ref[0] = jnp.concatenate([jnp.where(lo, src, sw), jnp.where(lo, sw, src)], axis=1).astype(BF16)
    qk_ref[0] = z[:, 6 * LANE:14 * LANE]
    vb_ref[0] = z[:, 14 * LANE:18 * LANE].astype(BF16)
    ob_ref[0] = z[:, 18 * LANE:22 * LANE]
    gt_ref[0] = _blk(z, 22) + bg_ref[...]


def _row_specs(tm, d, bsz, n_lat_tiles):
    return [pl.BlockSpec((1, tm, d), lambda b, i: (b, i, 0)),
            pl.BlockSpec((1, 2, d), lambda b, i: (jnp.where(i >= n_lat_tiles, bsz, b), 0, 0)),
            pl.BlockSpec((1, d), lambda b, i: (0, 0))]


def _const_spec(a):
    return pl.BlockSpec(a.shape, lambda b, i: (0,) * a.ndim)


def _inproj_even(x, mod, g, w, cos, sin, bd, gq, gk, bg, n_lat_tiles):
    bsz, t, d = x.shape
    tm = ROW_TILE
    tab = pl.BlockSpec((tm, LANE), lambda b, i: (i, 0))
    consts = [bd, gq, gk, bg]
    widths = [(4 * LANE, BF16), (2 * LANE, BF16), (2 * LANE, BF16), (8 * LANE, F32), (4 * LANE, BF16),
              (4 * LANE, F32), (LANE, F32)]
    return pl.pallas_call(
        _inproj_even_kernel,
        grid=(bsz, t // tm),
        in_specs=_row_specs(tm, d, bsz, n_lat_tiles) + [_const_spec(w), tab, tab] + [_const_spec(a) for a in consts],
        out_specs=[pl.BlockSpec((1, tm, n), lambda b, i: (b, i, 0)) for n, _ in widths],
        out_shape=[jax.ShapeDtypeStruct((bsz, t, n), dt) for n, dt in widths],
        compiler_params=_cparams(("parallel", "parallel")),
        name="inproj_even",
    )(x, mod, g.reshape(1, d), w, cos, sin, *consts)


def _inproj_odd_kernel(x_ref, mod_ref, g_ref, w_ref, cos_ref, sin_ref, cosd_ref, sind_ref, bd_ref, gqc_ref, gkc_ref,
                       gcq_ref, wuq_ref, gqd_ref, gckv_ref, wkn_ref, wvd_ref, gkd_ref,
                       qc_ref, kc_ref, vc_ref, qd_ref, kd_ref, vd_ref):
    z = jnp.dot(_modulated(x_ref, mod_ref, g_ref), w_ref[...], preferred_element_type=F32)
    lane = lax.broadcasted_iota(jnp.int32, (z.shape[0], LANE), 1)
    even = (lane & 1) == 0
    cos, sin, bd = cos_ref[...], sin_ref[...], bd_ref[...]
    cosd, sind = cosd_ref[...], sind_ref[...]
    c_scale = C_HEAD_DIM ** -0.5 * LOG2E
    d_scale = (D_NOPE + D_ROPE) ** -0.5 * LOG2E
    for c in range(4):
        sl = slice(c * LANE, (c + 1) * LANE)
        qc_ref[0, :, sl] = (_rope(_head64_norm(_blk(z, c), bd, gqc_ref[...]), cos, sin, even) * c_scale).astype(BF16)
        kc_ref[0, :, sl] = _rope(_head64_norm(_blk(z, 4 + c), bd, gkc_ref[...]), cos, sin, even).astype(BF16)
    vc_ref[0] = z[:, 8 * LANE:12 * LANE].astype(BF16)
    cq = z[:, 12 * LANE:14 * LANE]
    cq = cq * lax.rsqrt(jnp.mean(cq * cq, axis=-1, keepdims=True) + EPS) * gcq_ref[...]
    qd = jnp.dot(cq.astype(BF16), wuq_ref[...], preferred_element_type=F32)
    ckv = _blk(z, 14)
    ckv = (ckv * lax.rsqrt(jnp.mean(ckv * ckv, axis=-1, keepdims=True) + EPS) * gckv_ref[...]).astype(BF16)
    kn = jnp.dot(ckv, wkn_ref[...], preferred_element_type=F32)
    vd_ref[0] = jnp.dot(ckv, wvd_ref[...], preferred_element_type=F32).astype(BF16)
    kr = pltpu.roll(_blk(z, 15), D_NOPE, 1)
    hd = D_NOPE + D_ROPE
    for c in range(4):
        sl = slice(c * LANE, (c + 1) * LANE)
        qd_ref[0, :, sl] = (_rope(_lane_norm(_blk(qd, c), hd, gqd_ref[...]), cosd, sind, even) * d_scale).astype(BF16)
        kd_ref[0, :, sl] = _rope(_lane_norm(_blk(kn, c) + kr, hd, gkd_ref[...]), cosd, sind, even).astype(BF16)


def _inproj_odd(x, mod, g, w, cos, sin, cosd, sind, consts, n_lat_tiles):
    bsz, t, d = x.shape
    tm = ROW_TILE
    tab = pl.BlockSpec((tm, LANE), lambda b, i: (i, 0))
    return pl.pallas_call(
        _inproj_odd_kernel,
        grid=(bsz, t // tm),
        in_specs=(_row_specs(tm, d, bsz, n_lat_tiles) + [_const_spec(w), tab, tab, tab, tab]
                  + [_const_spec(a) for a in consts]),
        out_specs=[pl.BlockSpec((1, tm, HALF), lambda b, i: (b, i, 0)) for _ in range(6)],
        out_shape=[jax.ShapeDtypeStruct((bsz, t, HALF), BF16) for _ in range(6)],
        compiler_params=_cparams(("parallel", "parallel")),
        name="inproj_odd",
    )(x, mod, g.reshape(1, d), w, cos, sin, cosd, sind, *consts)


def _attn_kernel(*refs, mode, tq, tk, n_tiles):
    if mode == "diff":
        q_ref, k_ref, v_ref, lam_ref, gsub_ref, o_ref, q_sc, s_sc, m_sc, a_sc, l_sc, acc_sc = refs
    else:
        q_ref, k_ref, v_ref = refs[:3]
        o_ref, q_sc, s_sc, m_sc, a_sc, l_sc, acc_sc = refs[-7:]
    rows = q_sc.shape[0]
    n_blk = tk // LANE
    lo = lax.broadcasted_iota(jnp.int32, (tq, LANE), 1) < HEAD64

    if mode == "mla":
        q_sc[...] = q_ref[0]
    else:
        for p in range(q_ref.shape[2] // LANE):
            qb = _blk(q_ref[0], p)
            q_sc[2 * p * tq:(2 * p + 1) * tq] = jnp.where(lo, qb, jnp.zeros_like(qb))
            q_sc[(2 * p + 1) * tq:(2 * p + 2) * tq] = jnp.where(lo, jnp.zeros_like(qb), qb)

    def score_stage(j, slot):
        start = pl.multiple_of(j * tk, LANE)
        s = lax.dot_general(q_sc[...], k_ref[0, pl.ds(start, tk), :], (((1,), (1,)), ((), ())),
                            preferred_element_type=F32)
        s_sc[slot] = s
        m_tile = jnp.max(functools.reduce(jnp.maximum, [_blk(s, c) for c in range(n_blk)]), axis=1, keepdims=True)
        m_prev = m_sc[1 - slot]
        m_new = jnp.maximum(m_prev, m_tile)
        m_sc[slot] = m_new
        a_sc[slot] = jnp.exp2(m_prev - m_new)

    def value_stage(j, slot):
        start = pl.multiple_of(j * tk, LANE)
        m = m_sc[slot]
        alpha = a_sc[slot]
        p_blocks = [jnp.exp2((s_sc[slot, :, c * LANE:(c + 1) * LANE] - m).astype(BF16)) for c in range(n_blk)]
        l_sc[...] = alpha * l_sc[...] + functools.reduce(jnp.add, p_blocks).astype(F32)
        p = jnp.concatenate(p_blocks, axis=1)
        acc_sc[...] = alpha * acc_sc[...] + jnp.dot(p, v_ref[0, pl.ds(start, tk), :], preferred_element_type=F32)

    m_sc[1] = jnp.full((rows, LANE), NEG_BIG, F32)
    l_sc[...] = jnp.zeros((rows, LANE), F32)
    acc_sc[...] = jnp.zeros((rows, LANE), F32)
    score_stage(0, 0)
    peel = (n_tiles - 1) % ATTN_UNROLL
    for j in range(peel):
        value_stage(j, j % 2)
        score_stage(j + 1, (j + 1) % 2)

    def body(t, carry):
        for u in range(ATTN_UNROLL):
            j = ATTN_UNROLL * t + peel + u
            value_stage(j, (peel + u) % 2)
            score_stage(j + 1, (peel + u + 1) % 2)
        return carry

    lax.fori_loop(0, (n_tiles - 1) // ATTN_UNROLL, body, 0)
    value_stage(n_tiles - 1, (n_tiles - 1) % 2)

    o = acc_sc[...] / jnp.sum(l_sc[...], axis=1, keepdims=True)
    head = lambda h: o[h * tq:(h + 1) * tq]
    if mode == "gqa":
        for p in range(o_ref.shape[2] // LANE):
            o_ref[0, :, p * LANE:(p + 1) * LANE] = jnp.where(lo, head(2 * p), head(2 * p + 1)).astype(o_ref.dtype)
    elif mode == "diff":
        dlt = head(0) - lam_ref[...] * head(1)
        o_ref[0] = (dlt * lax.rsqrt(jnp.mean(dlt * dlt, axis=1, keepdims=True) + EPS) * gsub_ref[...]).astype(o_ref.dtype)
    else:
        o_ref[0] = o.astype(o_ref.dtype)


def _attn(q, k, v, *, mode, tq, tk, q_rows, k_rows, extra=(), into=None):
    bsz, t_q, wq = q.shape
    qw = 2 * LANE if mode == "gqa" else LANE
    q0, nq = q_rows
    k0, nk = k_rows
    tq, tk = min(tq, nq), min(tk, nk)
    rows = {"gqa": 4, "diff": 2, "mla": 1}[mode] * tq
    assert q0 % tq == 0 and nq % tq == 0 and k0 % nk == 0 and nk % tk == 0
    qmap = lambda b, g, i: (b, q0 // tq + i, g)
    kmap = lambda b, g, i: (b, k0 // nk, g)
    in_specs = [pl.BlockSpec((1, tq, qw), qmap), pl.BlockSpec((1, nk, LANE), kmap), pl.BlockSpec((1, nk, LANE), kmap)]
    in_specs += [pl.BlockSpec(a.shape, lambda b, g, i: (0, 0)) for a in extra]
    args = [q, k, v, *extra]
    aliases = {}
    if into is not None:
        in_specs.append(pl.BlockSpec(memory_space=pl.ANY))
        aliases = {len(args): 0}
        args.append(into)

    def kern(*refs):
        if into is not None:
            refs = refs[:len(args) - 1] + refs[len(args):]
        _attn_kernel(*refs, mode=mode, tq=tq, tk=tk, n_tiles=nk // tk)

    return pl.pallas_call(
        kern,
        grid=(bsz, wq // qw, nq // tq),
        in_specs=in_specs,
        out_specs=pl.BlockSpec((1, tq, qw), qmap),
        out_shape=jax.ShapeDtypeStruct((bsz, t_q, wq), BF16),
        scratch_shapes=[pltpu.VMEM((rows, LANE), BF16), pltpu.VMEM((2, rows, tk), F32),
                        pltpu.VMEM((2, rows, LANE), F32), pltpu.VMEM((2, rows, LANE), F32),
                        pltpu.VMEM((rows, LANE), F32), pltpu.VMEM((rows, LANE), F32)],
        input_output_aliases=aliases,
        compiler_params=_cparams(("parallel", "parallel", "parallel")),
        name="attn_" + mode,
    )(*args)


def _mlstm_kernel(qf_ref, pf_ref, nf_ref, vf_ref, gf_ref, qb_ref, pb_ref, nb_ref, vb_ref, gb_ref, cw_ref, cb_ref,
                  hf_ref, hb_ref, c_sc, m_sc, *, n_lat_chunks, n_chunks):
    L, dh = B_CHUNK, B_HEAD_DIM
    c = pl.program_id(1)
    n_ctx_chunks = n_chunks - n_lat_chunks
    fwd_chunk = jnp.where(c < n_ctx_chunks, n_lat_chunks + c, c - n_ctx_chunks)
    bwd_chunk = n_chunks - 1 - c

    @pl.when(c == 0)
    def _():
        c_sc[...] = jnp.zeros(c_sc.shape, F32)
        m_sc[...] = jnp.zeros(m_sc.shape, F32)

    row = lax.broadcasted_iota(jnp.int32, (L, L), 0)
    col = lax.broadcasted_iota(jnp.int32, (L, L), 1)
    upper = (row <= col).astype(F32)
    lower = (col <= row).astype(F32)
    rid = lax.broadcasted_iota(jnp.int32, (L, 1), 0)
    glane = lax.broadcasted_iota(jnp.int32, (L, LANE), 1)
    is_f = (glane // B_HEADS) % 2 == 1
    ones_col = (lax.broadcasted_iota(jnp.int32, (L, dh), 1) == 0).astype(BF16)
    hi = lax.Precision.HIGHEST

    def direction(qk_ref, p_ref, n_ref, v_ref, g_ref, h_ref, chunk, reverse):
        first = (chunk == 0) | (chunk == n_lat_chunks)
        last = (chunk == n_lat_chunks - 1) | (chunk == n_chunks - 1)
        x = qk_ref[0]
        prev_row = jnp.where(first, 0.0, p_ref[0, SUBLANE - 1:SUBLANE, :])
        next_row = jnp.where(last, 0.0, n_ref[0, 0:1, :])
        x_prev = jnp.where(rid == 0, prev_row, pltpu.roll(x, 1, 0))
        x_next = jnp.where(rid == L - 1, next_row, pltpu.roll(x, L - 1, 0))
        y = x_prev * cw_ref[0:1, :] + x * cw_ref[1:2, :] + x_next * cw_ref[2:3, :] + cb_ref[...]
        y = y * jax.nn.sigmoid(y)
        g = g_ref[0]
        g = jnp.where(is_f, jnp.minimum(g, 0.0) - jnp.log(1.0 + jnp.exp(-jnp.abs(g))), g)
        g_t = jnp.transpose(g)
        cum_r = jnp.dot(g_t, upper, preferred_element_type=F32, precision=hi)
        cum_c = jnp.dot(lower, g, preferred_element_type=F32, precision=hi)
        if reverse:
            cum_r = cum_r[:, L - 1:L] - cum_r + g_t
            cum_c = cum_c[L - 1:L, :] - cum_c + g
        gi = 2 * B_HEADS if reverse else 0
        mask = (col >= row) if reverse else (col <= row)
        for h in range(B_HEADS):
            idx = (B_HEADS if reverse else 0) + h
            q = _blk(y, h).astype(BF16)
            k = _blk(y, B_HEADS + h) * (dh ** -0.5)
            v_ext = jnp.concatenate([_blk(v_ref[0], h), ones_col], axis=1)
            i_r = g_t[gi + h:gi + h + 1, :]
            b_r = cum_r[gi + B_HEADS + h:gi + B_HEADS + h + 1, :]
            i_c = g[:, gi + h:gi + h + 1]
            b_c = cum_c[:, gi + B_HEADS + h:gi + B_HEADS + h + 1]
            m_prev = m_sc[idx][:, 0:1]
            c_ext = c_sc[idx]

            log_w = jnp.where(mask, b_c - b_r + i_r, -jnp.inf)
            log_inter = b_c + m_prev
            m_row = jnp.maximum(log_inter, jnp.max(log_w, axis=1, keepdims=True))
            w_intra = jnp.exp(log_w - m_row)
            a_inter = jnp.exp(log_inter - m_row)
            sc = lax.dot_general(q, k.astype(BF16), (((1,), (1,)), ((), ())), preferred_element_type=F32) * w_intra
            q_c = jnp.dot(q, c_ext.astype(BF16), preferred_element_type=F32)
            s_v = jnp.dot(sc.astype(BF16), v_ext, preferred_element_type=F32)
            num = a_inter * q_c[:, :dh] + s_v[:, :dh]
            den = a_inter * q_c[:, dh:dh + 1] + s_v[:, dh:dh + 1]
            h_ref[0, :, h * dh:(h + 1) * dh] = num / jnp.maximum(jnp.abs(den), jnp.exp(-m_row))

            b_last = b_r[:, 0:1] if reverse else b_r[:, L - 1:L]
            m_new = jnp.maximum(b_last + m_prev, jnp.max(b_last - b_r + i_r, axis=1, keepdims=True))
            g_c = jnp.exp(b_last - b_c + i_c - m_new)
            decay = jnp.exp(b_last + m_prev - m_new)
            upd = jnp.dot(jnp.transpose(k).astype(BF16), (g_c * v_ext.astype(F32)).astype(BF16),
                          preferred_element_type=F32)
            c_sc[idx] = decay * c_ext + upd
            m_sc[idx] = jnp.broadcast_to(m_new, (1, LANE))

    direction(qf_ref, pf_ref, nf_ref, vf_ref, gf_ref, hf_ref, fwd_chunk, False)
    direction(qb_ref, pb_ref, nb_ref, vb_ref, gb_ref, hb_ref, bwd_chunk, True)


def _mlstm(qk, vb, gates, conv_w, conv_b, n_lat):
    bsz, t, _ = qk.shape
    L = B_CHUNK
    n_chunks, n_lat_chunks = t // L, n_lat // L
    n_ctx_chunks = n_chunks - n_lat_chunks
    per = L // SUBLANE
    fwd = lambda c: jnp.where(c < n_ctx_chunks, n_lat_chunks + c, c - n_ctx_chunks)
    bwd = lambda c: n_chunks - 1 - c

    def specs(order):
        return [
            pl.BlockSpec((1, L, 2 * HALF), lambda b, c: (b, order(c), 0)),
            pl.BlockSpec((1, SUBLANE, 2 * HALF), lambda b, c: (b, jnp.maximum(order(c) * per - 1, 0), 0)),
            pl.BlockSpec((1, SUBLANE, 2 * HALF), lambda b, c: (b, jnp.minimum((order(c) + 1) * per, t // SUBLANE - 1), 0)),
            pl.BlockSpec((1, L, HALF), lambda b, c: (b, order(c), 0)),
            pl.BlockSpec((1, L, LANE), lambda b, c: (b, order(c), 0)),
        ]

    const = lambda a: pl.BlockSpec(a.shape, lambda b, c: (0, 0))
    return pl.pallas_call(
        functools.partial(_mlstm_kernel, n_lat_chunks=n_lat_chunks, n_chunks=n_chunks),
        grid=(bsz, n_chunks),
        in_specs=specs(fwd) + specs(bwd) + [const(conv_w), const(conv_b)],
        out_specs=[pl.BlockSpec((1, L, HALF), lambda b, c: (b, fwd(c), 0)),
                   pl.BlockSpec((1, L, HALF), lambda b, c: (b, bwd(c), 0))],
        out_shape=[jax.ShapeDtypeStruct((bsz, t, HALF), F32)] * 2,
        scratch_shapes=[pltpu.VMEM((2 * B_HEADS, B_HEAD_DIM, 2 * B_HEAD_DIM), F32),
                        pltpu.VMEM((2 * B_HEADS, 1, LANE), F32)],
        compiler_params=_cparams(("parallel", "arbitrary")),
        name="mlstm",
    )(qk, qk, qk, vb, gates, qk, qk, qk, vb, gates, conv_w, conv_b)


def _out_tail(mix, x_ref, mod_ref, g_ref, w_ref, wrh_ref, wrl_ref, xo_ref, f_ref, lg_ref):
    o = jnp.dot(mix, w_ref[...], preferred_element_type=F32)
    xn = x_ref[0] + mod_ref[0, 0:1, :] * o
    xo_ref[0] = xn
    y = xn * lax.rsqrt(jnp.mean(xn * xn, axis=-1, keepdims=True) + EPS) * g_ref[...]
    f = y * (1.0 + mod_ref[0, 2:3, :]) + mod_ref[0, 1:2, :]
    f_hi = f.astype(BF16)
    f_ref[0] = f_hi
    f_lo = (f - f_hi.astype(F32)).astype(BF16)
    lg_ref[0] = (jnp.dot(f_hi, wrh_ref[...], preferred_element_type=F32)
                 + jnp.dot(f_lo, wrh_ref[...], preferred_element_type=F32)
                 + jnp.dot(f_hi, wrl_ref[...], preferred_element_type=F32))


def _outproj_even_kernel(oa_ref, hf_ref, hb_ref, ob_ref, gh_ref, *tail):
    parts = [oa_ref[0]]
    for h in range(B_HEADS):
        hs = _blk(hf_ref[0], h) + _blk(hb_ref[0], h)
        hn = hs * lax.rsqrt(jnp.mean(hs * hs, axis=1, keepdims=True) + EPS) * _blk(gh_ref[...], h)
        parts.append((hn * jax.nn.sigmoid(_blk(ob_ref[0], h))).astype(BF16))
    _out_tail(jnp.concatenate(parts, axis=1), *tail)


def _outproj_odd_kernel(oc_ref, od_ref, *tail):
    _out_tail(jnp.concatenate([oc_ref[0], od_ref[0]], axis=1), *tail)


def _outproj(kern, heads, consts, x, mod, g, w, wr_hi, wr_lo, n_rows, n_lat_tiles):
    bsz, _, d = x.shape
    ne = wr_hi.shape[1]
    tm = ROW_TILE
    row = lambda b, i: (b, i, 0)
    return pl.pallas_call(
        kern,
        grid=(bsz, n_rows // tm),
        in_specs=([pl.BlockSpec((1, tm, HALF), row) for _ in heads] + [_const_spec(a) for a in consts]
                  + [pl.BlockSpec((1, tm, d), row),
                     pl.BlockSpec((1, 3, d), lambda b, i: (jnp.where(i >= n_lat_tiles, bsz, b), 0, 0)),
                     pl.BlockSpec((1, d), lambda b, i: (0, 0)), _const_spec(w), _const_spec(wr_hi),
                     _const_spec(wr_lo)]),
        out_specs=[pl.BlockSpec((1, tm, d), row), pl.BlockSpec((1, tm, d), row), pl.BlockSpec((1, tm, ne), row)],
        out_shape=[jax.ShapeDtypeStruct((bsz, n_rows, d), F32), jax.ShapeDtypeStruct((bsz, n_rows, d), BF16),
                   jax.ShapeDtypeStruct((bsz, n_rows, ne), F32)],
        compiler_params=_cparams(("parallel", "parallel")),
        name="outproj",
    )(*heads, *consts, x, mod, g.reshape(1, d), w, wr_hi, wr_lo)


def _expert_kernel(blk_e_ref, n_used_ref, x_ref, w1_ref, w3_ref, w2_ref, y_ref, w1_sc, w3_sc, w2_sc):
    i = pl.program_id(0)

    @pl.when((i == 0) | (blk_e_ref[i] != blk_e_ref[jnp.maximum(i - 1, 0)]))
    def _():
        w1_sc[...] = w1_ref[0, 0].astype(BF16)
        w3_sc[...] = w3_ref[0, 0].astype(BF16)
        w2_sc[...] = w2_ref[0, 0].astype(BF16)

    @pl.when(i < n_used_ref[0])
    def _():
        xb = x_ref[...]
        h1 = jnp.dot(xb, w1_sc[...], preferred_element_type=F32)
        h3 = jnp.dot(xb, w3_sc[...], preferred_element_type=F32)
        a = (h1 * jax.nn.sigmoid(h1) * h3).astype(BF16)
        y_ref[...] = jnp.dot(a, w2_sc[...], preferred_element_type=F32).astype(y_ref.dtype)

    @pl.when(i >= n_used_ref[0])
    def _():
        y_ref[...] = jnp.zeros(y_ref.shape, y_ref.dtype)


def _experts(xin, blk_e, n_used, layer, w1, w3, w2):
    n_rows, d = xin.shape
    de = w1.shape[-1]
    nb = n_rows // MOE_ROWS
    grid_spec = pltpu.PrefetchScalarGridSpec(
        num_scalar_prefetch=2,
        grid=(nb,),
        in_specs=[
            pl.BlockSpec((MOE_ROWS, d), lambda i, be, nu: (i, 0)),
            pl.BlockSpec((1, 1, d, de), lambda i, be, nu: (layer, be[i], 0, 0)),
            pl.BlockSpec((1, 1, d, de), lambda i, be, nu: (layer, be[i], 0, 0)),
            pl.BlockSpec((1, 1, de, d), lambda i, be, nu: (layer, be[i], 0, 0)),
        ],
        out_specs=pl.BlockSpec((MOE_ROWS, d), lambda i, be, nu: (i, 0)),
        scratch_shapes=[pltpu.VMEM((d, de), BF16), pltpu.VMEM((d, de), BF16), pltpu.VMEM((de, d), BF16)],
    )
    return pl.pallas_call(
        _expert_kernel,
        grid_spec=grid_spec,
        out_shape=jax.ShapeDtypeStruct((n_rows, d), BF16),
        compiler_params=_cparams(("arbitrary",)),
        name="experts",
    )(blk_e, n_used, xin, w1, w3, w2)


def _moe(f_tok, logits, b_router, layer, w1, w3, w2):
    n_tok, d = f_tok.shape
    per = N_EXPERTS // N_GROUPS
    scores = jax.nn.sigmoid(logits)
    sel = scores + b_router.astype(F32)

    def top2(a):
        i1 = jnp.argmax(a, axis=-1)
        hit = lax.broadcasted_iota(jnp.int32, a.shape, a.ndim - 1) == i1[..., None]
        rest = jnp.where(hit, -jnp.inf, a)
        return jnp.max(a, axis=-1), jnp.max(rest, axis=-1), i1, jnp.argmax(rest, axis=-1)

    g1, g2, _, _ = top2(sel.reshape(n_tok, N_GROUPS, per))
    g_idx = jnp.argmax(g1 + g2, axis=-1)
    in_group = (jnp.arange(N_EXPERTS) // per)[None, :] == g_idx[:, None]
    _, _, e1, e2 = top2(jnp.where(in_group, sel, -jnp.inf))
    lane_e = jnp.arange(N_EXPERTS, dtype=jnp.int32)[None, :]
    s1 = jnp.sum(jnp.where(lane_e == e1[:, None], scores, 0.0), axis=1)
    s2 = jnp.sum(jnp.where(lane_e == e2[:, None], scores, 0.0), axis=1)
    gate = jnp.stack([s1, s2], axis=-1) / (s1 + s2)[:, None]

    n_assign = n_tok * TOP_K
    e_flat = jnp.concatenate([e1, e2]).astype(jnp.int32)
    assert n_assign % MOE_ROWS == 0
    onehot = (e_flat[:, None] == jnp.arange(N_EXPERTS, dtype=jnp.int32)[None, :]).astype(F32)
    tri = jnp.tril(jnp.ones((MOE_ROWS, MOE_ROWS), F32))
    within = jnp.einsum('ij,bjk->bik', tri, onehot.reshape(-1, MOE_ROWS, N_EXPERTS), precision=lax.Precision.HIGHEST)
    totals = within[:, -1, :]
    before = jnp.cumsum(totals, axis=0) - totals
    csum = (within + before[:, None, :]).reshape(n_assign, N_EXPERTS)
    counts = jnp.sum(totals, axis=0).astype(jnp.int32)
    rank = jnp.sum(csum * onehot, axis=1).astype(jnp.int32) - 1
    padded = (counts + MOE_ROWS - 1) // MOE_ROWS * MOE_ROWS
    pend = jnp.cumsum(padded)
    pstart = pend - padded
    dest = pstart[e_flat] + rank
    n_rows = -(-n_assign // MOE_ROWS) * MOE_ROWS + N_EXPERTS * MOE_ROWS
    nb = n_rows // MOE_ROWS
    blk_row = jnp.arange(nb, dtype=jnp.int32) * MOE_ROWS
    blk_e = jnp.minimum(jnp.sum((pend[None, :] <= blk_row[:, None]).astype(jnp.int32), axis=1), N_EXPERTS - 1)
    order = jnp.argsort(e_flat, stable=True).astype(jnp.int32)
    row = jnp.arange(n_rows, dtype=jnp.int32)
    row_e = jnp.repeat(blk_e, MOE_ROWS)
    pos = row - pstart[row_e]
    src = jnp.minimum((jnp.cumsum(counts) - counts)[row_e] + pos, n_assign - 1)
    row_tok = jnp.where(pos < counts[row_e], order[src], row) % n_tok
    xin = f_tok[row_tok]
    n_used = (pend[-1:] // MOE_ROWS).astype(jnp.int32)
    y = _experts(xin, blk_e, n_used, layer, w1, w3, w2)
    return y[dest[:n_tok]], y[dest[n_tok:]], gate


def _combine_kernel(x_ref, y0_ref, y1_ref, gate_ref, g2_ref, o_ref):
    gate = gate_ref[0]
    y = gate[:, 0:1] * y0_ref[0].astype(F32) + gate[:, 1:2] * y1_ref[0].astype(F32)
    o_ref[0] = x_ref[0] + g2_ref[0] * y


def _combine(x, y0, y1, gate, g2, n_lat_tiles):
    bsz, n_rows, d = x.shape
    tm = ROW_TILE
    row = lambda b, i: (b, i, 0)
    return pl.pallas_call(
        _combine_kernel,
        grid=(bsz, n_rows // tm),
        in_specs=[pl.BlockSpec((1, tm, d), row), pl.BlockSpec((1, tm, d), row), pl.BlockSpec((1, tm, d), row),
                  pl.BlockSpec((1, tm, TOP_K), row),
                  pl.BlockSpec((1, 1, d), lambda b, i: (jnp.where(i >= n_lat_tiles, bsz, b), 0, 0))],
        out_specs=pl.BlockSpec((1, tm, d), row),
        out_shape=jax.ShapeDtypeStruct((bsz, n_rows, d), F32),
        compiler_params=_cparams(("parallel", "parallel")),
        name="combine",
    )(x, y0, y1, gate, g2)


def _rope_lanes(n_lat, n_ctx, rot_dim):
    rows = n_lat // GRID_W
    r = jnp.repeat(jnp.arange(rows), GRID_W)
    c = jnp.tile(jnp.arange(GRID_W), rows)
    n_freq = rot_dim // 4
    inv = ROPE_THETA ** (-jnp.arange(n_freq, dtype=F32) / n_freq)
    ang = jnp.concatenate([r[:, None] * inv, c[:, None] * inv], axis=-1)
    cos = jnp.repeat(jnp.cos(ang), 2, axis=1)
    sin = jnp.repeat(jnp.sin(ang), 2, axis=1) * jnp.tile(jnp.array([-1.0, 1.0], F32), rot_dim // 2)
    pad = ((0, n_ctx), (0, 0))
    return jnp.pad(cos, pad, constant_values=1.0), jnp.pad(sin, pad)


def _pad_cols(w, n):
    return jnp.pad(w, ((0, 0), (0, n - w.shape[1])))


def _pad_row(v, n):
    return jnp.pad(v, (0, n - v.shape[0])).reshape(1, n)


def _round_up(n, m):
    return -(-n // m) * m


def _head_blocks(w, n_heads, width, start, size):
    w = w.reshape(w.shape[0], n_heads, width)[:, :, start:start + size]
    return jnp.pad(w, ((0, 0), (0, 0), (0, LANE - size))).reshape(w.shape[0], n_heads * LANE)


def kernel(x, c, ctx, c_ctx, w_ada, b_ada, g_mix, g_ffn, e_w_in, e_w_out, e_g_q, e_g_k, e_conv_w, e_conv_b, e_b_gates, e_g_h, o_w_in, o_w_out, o_g_qc, o_g_kc, o_lam, o_g_sub, o_g_cq, o_w_uq, o_g_ckv, o_w_ukv, o_g_qd, o_g_kd, w_router, b_router, w1, w3, w2):
    bsz, n_lat, d = x.shape
    n_ctx = ctx.shape[1]
    t = n_lat + n_ctx
    depth = w_ada.shape[0]
    assert n_lat % ROW_TILE == 0 and n_ctx % ROW_TILE == 0 and n_ctx % B_CHUNK == 0
    n_lat_tiles = n_lat // ROW_TILE
    hd = D_NOPE + D_ROPE

    cos64, sin64 = _rope_lanes(n_lat, n_ctx, A_HEAD_DIM)
    cos64, sin64 = jnp.tile(cos64, (1, 2)), jnp.tile(sin64, (1, 2))
    cos_d, sin_d = _rope_lanes(n_lat, n_ctx, D_ROPE)
    cos_d = jnp.pad(cos_d, ((0, 0), (D_NOPE, LANE - hd)), constant_values=1.0)
    sin_d = jnp.pad(sin_d, ((0, 0), (D_NOPE, LANE - hd)))
    half = jnp.arange(LANE) // HEAD64
    bd64 = (half[:, None] == half[None, :]).astype(BF16)
    wr_hi = w_router.astype(BF16)
    wr_lo = (w_router.astype(F32) - wr_hi.astype(F32)).astype(BF16)
    cond = jnp.concatenate([c, c_ctx[None, :]], axis=0)
    x_all = jnp.concatenate([x, ctx], axis=1)

    for layer in range(depth):
        last = layer == depth - 1
        j = layer // 2
        mod = (jax.nn.silu(cond) @ w_ada[layer] + b_ada[layer]).reshape(bsz + 1, 6, d)
        sh1, sc1, g1, sh2, sc2, g2 = [mod[:, i] for i in range(6)]
        mod_in = jnp.stack([sh1, sc1], axis=1)
        mod_out = jnp.stack([g1, sh2, sc2], axis=1)
        n_rows = n_lat if last else t
        if layer % 2 == 0:
            w_in = _pad_cols(e_w_in[j], _round_up(e_w_in.shape[-1], LANE)).astype(BF16)
            q, kd, vd, qk, vb, ob, gates = _inproj_even(
                x_all, mod_in, g_mix[layer], w_in, cos64, sin64, bd64, jnp.tile(e_g_q[j], 2).reshape(1, LANE),
                jnp.tile(e_g_k[j], 2).reshape(1, LANE), _pad_row(e_b_gates[j], LANE), n_lat_tiles)
            oa = _attn(q, kd, vd, mode="gqa", tq=ATTN_ROWS // 4, tk=KEY_TILE, q_rows=(0, n_lat), k_rows=(0, t))
            if not last:
                oa = _attn(q, kd, vd, mode="gqa", tq=ATTN_ROWS // 4, tk=KEY_TILE, q_rows=(n_lat, n_ctx), k_rows=(n_lat, n_ctx),
                           into=oa)
            hf, hb = _mlstm(qk, vb, gates, e_conv_w[j], e_conv_b[j].reshape(1, -1), n_lat)
            heads, consts, kern = [oa, hf, hb, ob], [e_g_h[j].reshape(1, HALF)], _outproj_even_kernel
            w_out = e_w_out[j]
        else:
            assert last, "the odd mixer is only wired as the final layer"
            lam_init = LAM_INIT_BASE[0] - LAM_INIT_BASE[1] * math.exp(-LAM_INIT_BASE[2] * layer)
            lq1, lk1, lq2, lk2 = o_lam[j].astype(F32)
            lam = jnp.exp(jnp.sum(lq1 * lk1)) - jnp.exp(jnp.sum(lq2 * lk2)) + lam_init
            w_in = _pad_cols(o_w_in[j], _round_up(o_w_in.shape[-1], LANE)).astype(BF16)
            consts = [bd64, jnp.tile(o_g_qc[j], 2).reshape(1, LANE), jnp.tile(o_g_kc[j], 2).reshape(1, LANE),
                      o_g_cq[j].reshape(1, D_Q_LORA), _head_blocks(o_w_uq[j], D_HEADS, hd, 0, hd).astype(BF16),
                      _pad_row(o_g_qd[j], LANE), o_g_ckv[j].reshape(1, D_KV_LORA),
                      _head_blocks(o_w_ukv[j], D_HEADS, D_NOPE + D_V_DIM, 0, D_NOPE).astype(BF16),
                      _head_blocks(o_w_ukv[j], D_HEADS, D_NOPE + D_V_DIM, D_NOPE, D_V_DIM).astype(BF16),
                      _pad_row(o_g_kd[j], LANE)]
            qc, kc, vc, qd, kdl, vdl = _inproj_odd(x_all, mod_in, g_mix[layer], w_in, cos64, sin64, cos_d, sin_d,
                                                   consts, n_lat_tiles)
            extra = (jnp.full((1, LANE), lam, F32), (o_g_sub[j] * (1.0 - lam_init)).reshape(1, LANE))
            oc = _attn(qc, kc, vc, mode="diff", tq=ATTN_ROWS // 2, tk=KEY_TILE, q_rows=(0, n_lat), k_rows=(0, t), extra=extra)
            od = _attn(qd, kdl, vdl, mode="mla", tq=ATTN_ROWS, tk=KEY_TILE, q_rows=(0, n_lat), k_rows=(0, t))
            heads, consts, kern = [oc, od], [], _outproj_odd_kernel
            w_out = o_w_out[j]
        x_new, f, logits = _outproj(kern, heads, consts, x_all, mod_out, g_ffn[layer], w_out.astype(BF16), wr_hi, wr_lo,
                                    n_rows, n_lat_tiles)
        y0, y1, gate = _moe(f.reshape(bsz * n_rows, d), logits.reshape(bsz * n_rows, N_EXPERTS), b_router,
                            layer, w1, w3, w2)
        x_all = _combine(x_new, y0.reshape(bsz, n_rows, d), y1.reshape(bsz, n_rows, d),
                         gate.reshape(bsz, n_rows, TOP_K), g2.reshape(bsz + 1, 1, d), n_lat_tiles)
    return x_all
```

```python
import functools
import math

import jax
import jax.numpy as jnp
import numpy as np
from jax import lax
from jax.experimental import pallas as pl
from jax.experimental.pallas import tpu as pltpu

F32 = jnp.float32
BF16 = jnp.bfloat16
EPS = 1e-6
LOG2E = 1.4426950408889634
NEG_BIG = -1e30
GRID_W = 64
ROPE_THETA = 10000.0
LAM_INIT_BASE = (0.8, 0.6, 0.3)

A_HEAD_DIM, A_HEADS, A_KV_HEADS = 64, 8, 2
B_HEAD_DIM, B_HEADS, B_CONV, B_CHUNK = 128, 4, 3, 128
C_HEAD_DIM, C_V_DIM, C_HEADS = 64, 128, 4
D_HEADS, D_Q_LORA, D_KV_LORA, D_NOPE, D_ROPE, D_V_DIM = 4, 256, 128, 64, 32, 128
N_EXPERTS, N_GROUPS, TOP_K, D_EXPERT = 32, 4, 2, 512
HALF = 512
HEAD64 = 64

LANE = 128
SUBLANE = 8
ROW_TILE = 256
MOE_ROWS = 512
KEY_TILE = 1280
ATTN_ROWS = 2048
ATTN_UNROLL = 2
VMEM_LIMIT = 60 * 1024 * 1024


def _cparams(sem):
    return pltpu.CompilerParams(dimension_semantics=sem, vmem_limit_bytes=VMEM_LIMIT)


def _blk(a, i):
    return a[:, i * LANE:(i + 1) * LANE]


def _modulated(x_ref, mod_ref, g_ref):
    x = x_ref[0]
    y = x * lax.rsqrt(jnp.mean(x * x, axis=-1, keepdims=True) + EPS) * g_ref[...]
    return (y * (1.0 + mod_ref[0, 1:2, :]) + mod_ref[0, 0:1, :]).astype(BF16)


def _pairswap(b, even):
    return jnp.where(even, pltpu.roll(b, LANE - 1, 1), pltpu.roll(b, 1, 1))


def _rope(b, cos, sin, even):
    return b * cos + _pairswap(b, even) * sin


def _head64_norm(b, bd, gain):
    ss = jnp.dot((b * b).astype(BF16), bd, preferred_element_type=F32)
    return b * lax.rsqrt(ss * (1.0 / HEAD64) + EPS) * gain


def _lane_norm(b, n_real, gain):
    ss = jnp.sum(b * b, axis=1, keepdims=True)
    return b * lax.rsqrt(ss * (1.0 / n_real) + EPS) * gain


def _inproj_even_kernel(x_ref, mod_ref, g_ref, w_ref, cos_ref, sin_ref, bd_ref, gq_ref, gk_ref, bg_ref,
                        q_ref, k_ref, v_ref, qk_ref, vb_ref, ob_ref, gt_ref):
    z = jnp.dot(_modulated(x_ref, mod_ref, g_ref), w_ref[...], preferred_element_type=F32)
    lane = lax.broadcasted_iota(jnp.int32, (z.shape[0], LANE), 1)
    even = (lane & 1) == 0
    lo = lane < HEAD64
    cos, sin, bd = cos_ref[...], sin_ref[...], bd_ref[...]
    q_scale = A_HEAD_DIM ** -0.5 * LOG2E
    for c in range(4):
        qb = _rope(_head64_norm(_blk(z, c), bd, gq_ref[...]), cos, sin, even) * q_scale
        q_ref[0, :, c * LANE:(c + 1) * LANE] = qb.astype(BF16)
    kn = _rope(_head64_norm(_blk(z, 4), bd, gk_ref[...]), cos, sin, even)
    for src, ref in ((kn, k_ref), (_blk(z, 5), v_ref)):
        sw = pltpu.roll(src, HEAD64, 1)
        ref[0] = jnp.concatenate([jnp.where(lo, src, sw), jnp.where(lo, sw, src)], axis=1).astype(BF16)
    qk_ref[0] = z[:, 6 * LANE:14 * LANE]
    vb_ref[0] = z[:, 14 * LANE:18 * LANE].astype(BF16)
    ob_ref[0] = z[:, 18 * LANE:22 * LANE]
    gt_ref[0] = _blk(z, 22) + bg_ref[...]


def _row_specs(tm, d, bsz, n_lat_tiles):
    return [pl.BlockSpec((1, tm, d), lambda b, i: (b, i, 0)),
            pl.BlockSpec((1, 2, d), lambda b, i: (jnp.where(i >= n_lat_tiles, bsz, b), 0, 0)),
            pl.BlockSpec((1, d), lambda b, i: (0, 0))]


def _const_spec(a):
    return pl.BlockSpec(a.shape, lambda b, i: (0,) * a.ndim)


def _inproj_even(x, mod, g, w, cos, sin, bd, gq, gk, bg, n_lat_tiles):
    bsz, t, d = x.shape
    tm = ROW_TILE
    tab = pl.BlockSpec((tm, LANE), lambda b, i: (i, 0))
    consts = [bd, gq, gk, bg]
    widths = [(4 * LANE, BF16), (2 * LANE, BF16), (2 * LANE, BF16), (8 * LANE, F32), (4 * LANE, BF16),
              (4 * LANE, F32), (LANE, F32)]
    return pl.pallas_call(
        _inproj_even_kernel,
        grid=(bsz, t // tm),
        in_specs=_row_specs(tm, d, bsz, n_lat_tiles) + [_const_spec(w), tab, tab] + [_const_spec(a) for a in consts],
        out_specs=[pl.BlockSpec((1, tm, n), lambda b, i: (b, i, 0)) for n, _ in widths],
        out_shape=[jax.ShapeDtypeStruct((bsz, t, n), dt) for n, dt in widths],
        compiler_params=_cparams(("parallel", "parallel")),
        name="inproj_even",
    )(x, mod, g.reshape(1, d), w, cos, sin, *consts)


def _inproj_odd_kernel(x_ref, mod_ref, g_ref, w_ref, cos_ref, sin_ref, cosd_ref, sind_ref, bd_ref, gqc_ref, gkc_ref,
                       gcq_ref, wuq_ref, gqd_ref, gckv_ref, wkn_ref, wvd_ref, gkd_ref,
                       qc_ref, kc_ref, vc_ref, qd_ref, kd_ref, vd_ref):
    z = jnp.dot(_modulated(x_ref, mod_ref, g_ref), w_ref[...], preferred_element_type=F32)
    lane = lax.broadcasted_iota(jnp.int32, (z.shape[0], LANE), 1)
    even = (lane & 1) == 0
    cos, sin, bd = cos_ref[...], sin_ref[...], bd_ref[...]
    cosd, sind = cosd_ref[...], sind_ref[...]
    c_scale = C_HEAD_DIM ** -0.5 * LOG2E
    d_scale = (D_NOPE + D_ROPE) ** -0.5 * LOG2E
    for c in range(4):
        sl = slice(c * LANE, (c + 1) * LANE)
        qc_ref[0, :, sl] = (_rope(_head64_norm(_blk(z, c), bd, gqc_ref[...]), cos, sin, even) * c_scale).astype(BF16)
        kc_ref[0, :, sl] = _rope(_head64_norm(_blk(z, 4 + c), bd, gkc_ref[...]), cos, sin, even).astype(BF16)
    vc_ref[0] = z[:, 8 * LANE:12 * LANE].astype(BF16)
    cq = z[:, 12 * LANE:14 * LANE]
    cq = cq * lax.rsqrt(jnp.mean(cq * cq, axis=-1, keepdims=True) + EPS) * gcq_ref[...]
    qd = jnp.dot(cq.astype(BF16), wuq_ref[...], preferred_element_type=F32)
    ckv = _blk(z, 14)
    ckv = (ckv * lax.rsqrt(jnp.mean(ckv * ckv, axis=-1, keepdims=True) + EPS) * gckv_ref[...]).astype(BF16)
    kn = jnp.dot(ckv, wkn_ref[...], preferred_element_type=F32)
    vd_ref[0] = jnp.dot(ckv, wvd_ref[...], preferred_element_type=F32).astype(BF16)
    kr = pltpu.roll(_blk(z, 15), D_NOPE, 1)
    hd = D_NOPE + D_ROPE
    for c in range(4):
        sl = slice(c * LANE, (c + 1) * LANE)
        qd_ref[0, :, sl] = (_rope(_lane_norm(_blk(qd, c), hd, gqd_ref[...]), cosd, sind, even) * d_scale).astype(BF16)
        kd_ref[0, :, sl] = _rope(_lane_norm(_blk(kn, c) + kr, hd, gkd_ref[...]), cosd, sind, even).astype(BF16)


def _inproj_odd(x, mod, g, w, cos, sin, cosd, sind, consts, n_lat_tiles):
    bsz, t, d = x.shape
    tm = ROW_TILE
    tab = pl.BlockSpec((tm, LANE), lambda b, i: (i, 0))
    return pl.pallas_call(
        _inproj_odd_kernel,
        grid=(bsz, t // tm),
        in_specs=(_row_specs(tm, d, bsz, n_lat_tiles) + [_const_spec(w), tab, tab, tab, tab]
                  + [_const_spec(a) for a in consts]),
        out_specs=[pl.BlockSpec((1, tm, HALF), lambda b, i: (b, i, 0)) for _ in range(6)],
        out_shape=[jax.ShapeDtypeStruct((bsz, t, HALF), BF16) for _ in range(6)],
        compiler_params=_cparams(("parallel", "parallel")),
        name="inproj_odd",
    )(x, mod, g.reshape(1, d), w, cos, sin, cosd, sind, *consts)


def _attn_kernel(*refs, mode, tq, tk, n_tiles):
    if mode == "diff":
        q_ref, k_ref, v_ref, lam_ref, gsub_ref, o_ref, q_sc, s_sc, m_sc, a_sc, l_sc, acc_sc = refs
    else:
        q_ref, k_ref, v_ref = refs[:3]
        o_ref, q_sc, s_sc, m_sc, a_sc, l_sc, acc_sc = refs[-7:]
    rows = q_sc.shape[0]
    n_blk = tk // LANE
    lo = lax.broadcasted_iota(jnp.int32, (tq, LANE), 1) < HEAD64

    if mode == "mla":
        q_sc[...] = q_ref[0]
    else:
        for p in range(q_ref.shape[2] // LANE):
            qb = _blk(q_ref[0], p)
            q_sc[2 * p * tq:(2 * p + 1) * tq] = jnp.where(lo, qb, jnp.zeros_like(qb))
            q_sc[(2 * p + 1) * tq:(2 * p + 2) * tq] = jnp.where(lo, jnp.zeros_like(qb), qb)

    def score_stage(j, slot):
        start = pl.multiple_of(j * tk, LANE)
        s = lax.dot_general(q_sc[...], k_ref[0, pl.ds(start, tk), :], (((1,), (1,)), ((), ())),
                            preferred_element_type=F32)
        s_sc[slot] = s
        m_tile = jnp.max(functools.reduce(jnp.maximum, [_blk(s, c) for c in range(n_blk)]), axis=1, keepdims=True)
        m_prev = m_sc[1 - slot]
        m_new = jnp.maximum(m_prev, m_tile)
        m_sc[slot] = m_new
        a_sc[slot] = jnp.exp2(m_prev - m_new)

    def value_stage(j, slot):
        start = pl.multiple_of(j * tk, LANE)
        m = m_sc[slot]
        alpha = a_sc[slot]
        p_blocks = [jnp.exp2((s_sc[slot, :, c * LANE:(c + 1) * LANE] - m).astype(BF16)) for c in range(n_blk)]
        l_sc[...] = alpha * l_sc[...] + functools.reduce(jnp.add, p_blocks).astype(F32)
        p = jnp.concatenate(p_blocks, axis=1)
        acc_sc[...] = alpha * acc_sc[...] + jnp.dot(p, v_ref[0, pl.ds(start, tk), :], preferred_element_type=F32)

    m_sc[1] = jnp.full((rows, LANE), NEG_BIG, F32)
    l_sc[...] = jnp.zeros((rows, LANE), F32)
    acc_sc[...] = jnp.zeros((rows, LANE), F32)
    score_stage(0, 0)
    peel = (n_tiles - 1) % ATTN_UNROLL
    for j in range(peel):
        value_stage(j, j % 2)
        score_stage(j + 1, (j + 1) % 2)

    def body(t, carry):
        for u in range(ATTN_UNROLL):
            j = ATTN_UNROLL * t + peel + u
            value_stage(j, (peel + u) % 2)
            score_stage(j + 1, (peel + u + 1) % 2)
        return carry

    lax.fori_loop(0, (n_tiles - 1) // ATTN_UNROLL, body, 0)
    value_stage(n_tiles - 1, (n_tiles - 1) % 2)

    o = acc_sc[...] / jnp.sum(l_sc[...], axis=1, keepdims=True)
    head = lambda h: o[h * tq:(h + 1) * tq]
    if mode == "gqa":
        for p in range(o_ref.shape[2] // LANE):
            o_ref[0, :, p * LANE:(p + 1) * LANE] = jnp.where(lo, head(2 * p), head(2 * p + 1)).astype(o_ref.dtype)
    elif mode == "diff":
        dlt = head(0) - lam_ref[...] * head(1)
        o_ref[0] = (dlt * lax.rsqrt(jnp.mean(dlt * dlt, axis=1, keepdims=True) + EPS) * gsub_ref[...]).astype(o_ref.dtype)
    else:
        o_ref[0] = o.astype(o_ref.dtype)


def _attn(q, k, v, *, mode, tq, tk, q_rows, k_rows, extra=(), into=None):
    bsz, t_q, wq = q.shape
    qw = 2 * LANE if mode == "gqa" else LANE
    q0, nq = q_rows
    k0, nk = k_rows
    tq, tk = min(tq, nq), min(tk, nk)
    rows = {"gqa": 4, "diff": 2, "mla": 1}[mode] * tq
    assert q0 % tq == 0 and nq % tq == 0 and k0 % nk == 0 and nk % tk == 0
    qmap = lambda b, g, i: (b, q0 // tq + i, g)
    kmap = lambda b, g, i: (b, k0 // nk, g)
    in_specs = [pl.BlockSpec((1, tq, qw), qmap), pl.BlockSpec((1, nk, LANE), kmap), pl.BlockSpec((1, nk, LANE), kmap)]
    in_specs += [pl.BlockSpec(a.shape, lambda b, g, i: (0, 0)) for a in extra]
    args = [q, k, v, *extra]
    aliases = {}
    if into is not None:
        in_specs.append(pl.BlockSpec(memory_space=pl.ANY))
        aliases = {len(args): 0}
        args.append(into)

    def kern(*refs):
        if into is not None:
            refs = refs[:len(args) - 1] + refs[len(args):]
        _attn_kernel(*refs, mode=mode, tq=tq, tk=tk, n_tiles=nk // tk)

    return pl.pallas_call(
        kern,
        grid=(bsz, wq // qw, nq // tq),
        in_specs=in_specs,
        out_specs=pl.BlockSpec((1, tq, qw), qmap),
        out_shape=jax.ShapeDtypeStruct((bsz, t_q, wq), BF16),
        scratch_shapes=[pltpu.VMEM((rows, LANE), BF16), pltpu.VMEM((2, rows, tk), F32),
                        pltpu.VMEM((2, rows, LANE), F32), pltpu.VMEM((2, rows, LANE), F32),
                        pltpu.VMEM((rows, LANE), F32), pltpu.VMEM((rows, LANE), F32)],
        input_output_aliases=aliases,
        compiler_params=_cparams(("parallel", "parallel", "parallel")),
        name="attn_" + mode,
    )(*args)


def _mlstm_kernel(qf_ref, pf_ref, nf_ref, vf_ref, gf_ref, qb_ref, pb_ref, nb_ref, vb_ref, gb_ref, cw_ref, cb_ref,
                  hf_ref, hb_ref, c_sc, m_sc, *, n_lat_chunks, n_chunks):
    L, dh = B_CHUNK, B_HEAD_DIM
    c = pl.program_id(1)
    n_ctx_chunks = n_chunks - n_lat_chunks
    fwd_chunk = jnp.where(c < n_ctx_chunks, n_lat_chunks + c, c - n_ctx_chunks)
    bwd_chunk = n_chunks - 1 - c

    @pl.when(c == 0)
    def _():
        c_sc[...] = jnp.zeros(c_sc.shape, F32)
        m_sc[...] = jnp.zeros(m_sc.shape, F32)

    row = lax.broadcasted_iota(jnp.int32, (L, L), 0)
    col = lax.broadcasted_iota(jnp.int32, (L, L), 1)
    upper = (row <= col).astype(F32)
    lower = (col <= row).astype(F32)
    rid = lax.broadcasted_iota(jnp.int32, (L, 1), 0)
    glane = lax.broadcasted_iota(jnp.int32, (L, LANE), 1)
    is_f = (glane // B_HEADS) % 2 == 1
    ones_col = (lax.broadcasted_iota(jnp.int32, (L, dh), 1) == 0).astype(BF16)
    hi = lax.Precision.HIGHEST

    def direction(qk_ref, p_ref, n_ref, v_ref, g_ref, h_ref, chunk, reverse):
        first = (chunk == 0) | (chunk == n_lat_chunks)
        last = (chunk == n_lat_chunks - 1) | (chunk == n_chunks - 1)
        x = qk_ref[0]
        prev_row = jnp.where(first, 0.0, p_ref[0, SUBLANE - 1:SUBLANE, :])
        next_row = jnp.where(last, 0.0, n_ref[0, 0:1, :])
        x_prev = jnp.where(rid == 0, prev_row, pltpu.roll(x, 1, 0))
        x_next = jnp.where(rid == L - 1, next_row, pltpu.roll(x, L - 1, 0))
        y = x_prev * cw_ref[0:1, :] + x * cw_ref[1:2, :] + x_next * cw_ref[2:3, :] + cb_ref[...]
        y = y * jax.nn.sigmoid(y)
        g = g_ref[0]
        g = jnp.where(is_f, jnp.minimum(g, 0.0) - jnp.log(1.0 + jnp.exp(-jnp.abs(g))), g)
        g_t = jnp.transpose(g)
        cum_r = jnp.dot(g_t, upper, preferred_element_type=F32, precision=hi)
        cum_c = jnp.dot(lower, g, preferred_element_type=F32, precision=hi)
        if reverse:
            cum_r = cum_r[:, L - 1:L] - cum_r + g_t
            cum_c = cum_c[L - 1:L, :] - cum_c + g
        gi = 2 * B_HEADS if reverse else 0
        mask = (col >= row) if reverse else (col <= row)
        for h in range(B_HEADS):
            idx = (B_HEADS if reverse else 0) + h
            q = _blk(y, h).astype(BF16)
            k = _blk(y, B_HEADS + h) * (dh ** -0.5)
            v_ext = jnp.concatenate([_blk(v_ref[0], h), ones_col], axis=1)
            i_r = g_t[gi + h:gi + h + 1, :]
            b_r = cum_r[gi + B_HEADS + h:gi + B_HEADS + h + 1, :]
            i_c = g[:, gi + h:gi + h + 1]
            b_c = cum_c[:, gi + B_HEADS + h:gi + B_HEADS + h + 1]
            m_prev = m_sc[idx][:, 0:1]
            c_ext = c_sc[idx]

            log_w = jnp.where(mask, b_c - b_r + i_r, -jnp.inf)
            log_inter = b_c + m_prev
            m_row = jnp.maximum(log_inter, jnp.max(log_w, axis=1, keepdims=True))
            w_intra = jnp.exp(log_w - m_row)
            a_inter = jnp.exp(log_inter - m_row)
            sc = lax.dot_general(q, k.astype(BF16), (((1,), (1,)), ((), ())), preferred_element_type=F32) * w_intra
            q_c = jnp.dot(q, c_ext.astype(BF16), preferred_element_type=F32)
            s_v = jnp.dot(sc.astype(BF16), v_ext, preferred_element_type=F32)
            num = a_inter * q_c[:, :dh] + s_v[:, :dh]
            den = a_inter * q_c[:, dh:dh + 1] + s_v[:, dh:dh + 1]
            h_ref[0, :, h * dh:(h + 1) * dh] = num / jnp.maximum(jnp.abs(den), jnp.exp(-m_row))

            b_last = b_r[:, 0:1] if reverse else b_r[:, L - 1:L]
            m_new = jnp.maximum(b_last + m_prev, jnp.max(b_last - b_r + i_r, axis=1, keepdims=True))
            g_c = jnp.exp(b_last - b_c + i_c - m_new)
            decay = jnp.exp(b_last + m_prev - m_new)
            upd = jnp.dot(jnp.transpose(k).astype(BF16), (g_c * v_ext.astype(F32)).astype(BF16),
                          preferred_element_type=F32)
            c_sc[idx] = decay * c_ext + upd
            m_sc[idx] = jnp.broadcast_to(m_new, (1, LANE))

    direction(qf_ref, pf_ref, nf_ref, vf_ref, gf_ref, hf_ref, fwd_chunk, False)
    direction(qb_ref, pb_ref, nb_ref, vb_ref, gb_ref, hb_ref, bwd_chunk, True)


def _mlstm(qk, vb, gates, conv_w, conv_b, n_lat):
    bsz, t, _ = qk.shape
    L = B_CHUNK
    n_chunks, n_lat_chunks = t // L, n_lat // L
    n_ctx_chunks = n_chunks - n_lat_chunks
    per = L // SUBLANE
    fwd = lambda c: jnp.where(c < n_ctx_chunks, n_lat_chunks + c, c - n_ctx_chunks)
    bwd = lambda c: n_chunks - 1 - c

    def specs(order):
        return [
            pl.BlockSpec((1, L, 2 * HALF), lambda b, c: (b, order(c), 0)),
            pl.BlockSpec((1, SUBLANE, 2 * HALF), lambda b, c: (b, jnp.maximum(order(c) * per - 1, 0), 0)),
            pl.BlockSpec((1, SUBLANE, 2 * HALF), lambda b, c: (b, jnp.minimum((order(c) + 1) * per, t // SUBLANE - 1), 0)),
            pl.BlockSpec((1, L, HALF), lambda b, c: (b, order(c), 0)),
            pl.BlockSpec((1, L, LANE), lambda b, c: (b, order(c), 0)),
        ]

    const = lambda a: pl.BlockSpec(a.shape, lambda b, c: (0, 0))
    return pl.pallas_call(
        functools.partial(_mlstm_kernel, n_lat_chunks=n_lat_chunks, n_chunks=n_chunks),
        grid=(bsz, n_chunks),
        in_specs=specs(fwd) + specs(bwd) + [const(conv_w), const(conv_b)],
        out_specs=[pl.BlockSpec((1, L, HALF), lambda b, c: (b, fwd(c), 0)),
                   pl.BlockSpec((1, L, HALF), lambda b, c: (b, bwd(c), 0))],
        out_shape=[jax.ShapeDtypeStruct((bsz, t, HALF), F32)] * 2,
        scratch_shapes=[pltpu.VMEM((2 * B_HEADS, B_HEAD_DIM, 2 * B_HEAD_DIM), F32),
                        pltpu.VMEM((2 * B_HEADS, 1, LANE), F32)],
        compiler_params=_cparams(("parallel", "arbitrary")),
        name="mlstm",
    )(qk, qk, qk, vb, gates, qk, qk, qk, vb, gates, conv_w, conv_b)


def _route(logits, bias):
    per = N_EXPERTS // N_GROUPS
    lane = lax.broadcasted_iota(jnp.int32, logits.shape, 1)
    lane_f = lane.astype(F32)
    scores = jax.nn.sigmoid(logits)
    sel = scores + bias

    def first_max(a):
        m = jnp.max(a, axis=1, keepdims=True)
        return m, jnp.min(jnp.where(a == m, lane_f, float(N_EXPERTS)), axis=1, keepdims=True)

    best, g_idx = None, None
    for g in range(N_GROUPS):
        a = jnp.where(lane // per == g, sel, -jnp.inf)
        m1, i1 = first_max(a)
        top2 = m1 + jnp.max(jnp.where(lane_f == i1, -jnp.inf, a), axis=1, keepdims=True)
        if g == 0:
            best, g_idx = top2, jnp.zeros_like(top2)
        else:
            better = top2 > best
            best, g_idx = jnp.where(better, top2, best), jnp.where(better, float(g), g_idx)
    a = jnp.where((lane // per).astype(F32) == g_idx, sel, -jnp.inf)
    _, e1 = first_max(a)
    _, e2 = first_max(jnp.where(lane_f == e1, -jnp.inf, a))
    s1 = jnp.sum(jnp.where(lane_f == e1, scores, 0.0), axis=1, keepdims=True)
    s2 = jnp.sum(jnp.where(lane_f == e2, scores, 0.0), axis=1, keepdims=True)
    return e1, e2, s1 / (s1 + s2), s2 / (s1 + s2)


def _out_tail(mix, x_ref, mod_ref, g_ref, w_ref, wrh_ref, wrl_ref, br_ref, xo_ref, f_ref, rt_ref):
    o = jnp.dot(mix, w_ref[...], preferred_element_type=F32)
    xn = x_ref[0] + mod_ref[0, 0:1, :] * o
    xo_ref[0] = xn
    y = xn * lax.rsqrt(jnp.mean(xn * xn, axis=-1, keepdims=True) + EPS) * g_ref[...]
    f = y * (1.0 + mod_ref[0, 2:3, :]) + mod_ref[0, 1:2, :]
    f_hi = f.astype(BF16)
    f_ref[0] = f_hi
    f_lo = (f - f_hi.astype(F32)).astype(BF16)
    logits = (jnp.dot(f_hi, wrh_ref[...], preferred_element_type=F32)
              + jnp.dot(f_lo, wrh_ref[...], preferred_element_type=F32)
              + jnp.dot(f_hi, wrl_ref[...], preferred_element_type=F32))
    cols = _route(logits, br_ref[...])
    out_lane = lax.broadcasted_iota(jnp.int32, (logits.shape[0], LANE), 1)
    rt_ref[0] = functools.reduce(lambda acc, ic: jnp.where(out_lane == ic[0], ic[1], acc), enumerate(cols),
                                 jnp.zeros((logits.shape[0], LANE), F32))


def _outproj_even_kernel(oa_ref, hf_ref, hb_ref, ob_ref, gh_ref, *tail):
    parts = [oa_ref[0]]
    for h in range(B_HEADS):
        hs = _blk(hf_ref[0], h) + _blk(hb_ref[0], h)
        hn = hs * lax.rsqrt(jnp.mean(hs * hs, axis=1, keepdims=True) + EPS) * _blk(gh_ref[...], h)
        parts.append((hn * jax.nn.sigmoid(_blk(ob_ref[0], h))).astype(BF16))
    _out_tail(jnp.concatenate(parts, axis=1), *tail)


def _outproj_odd_kernel(oc_ref, od_ref, *tail):
    _out_tail(jnp.concatenate([oc_ref[0], od_ref[0]], axis=1), *tail)


def _outproj(kern, heads, consts, x, mod, g, w, wr_hi, wr_lo, b_router, n_rows, n_lat_tiles):
    bsz, _, d = x.shape
    tm = ROW_TILE
    row = lambda b, i: (b, i, 0)
    return pl.pallas_call(
        kern,
        grid=(bsz, n_rows // tm),
        in_specs=([pl.BlockSpec((1, tm, HALF), row) for _ in heads] + [_const_spec(a) for a in consts]
                  + [pl.BlockSpec((1, tm, d), row),
                     pl.BlockSpec((1, 3, d), lambda b, i: (jnp.where(i >= n_lat_tiles, bsz, b), 0, 0)),
                     pl.BlockSpec((1, d), lambda b, i: (0, 0)), _const_spec(w), _const_spec(wr_hi),
                     _const_spec(wr_lo), _const_spec(b_router)]),
        out_specs=[pl.BlockSpec((1, tm, d), row), pl.BlockSpec((1, tm, d), row), pl.BlockSpec((1, tm, LANE), row)],
        out_shape=[jax.ShapeDtypeStruct((bsz, n_rows, d), F32), jax.ShapeDtypeStruct((bsz, n_rows, d), BF16),
                   jax.ShapeDtypeStruct((bsz, n_rows, LANE), F32)],
        compiler_params=_cparams(("parallel", "parallel")),
        name="outproj",
    )(*heads, *consts, x, mod, g.reshape(1, d), w, wr_hi, wr_lo, b_router)


def _expert_kernel(blk_e_ref, n_used_ref, x_ref, w1_ref, w3_ref, w2_ref, y_ref, w1_sc, w3_sc, w2_sc):
    i = pl.program_id(0)

    @pl.when((i == 0) | (blk_e_ref[i] != blk_e_ref[jnp.maximum(i - 1, 0)]))
    def _():
        w1_sc[...] = w1_ref[0, 0].astype(BF16)
        w3_sc[...] = w3_ref[0, 0].astype(BF16)
        w2_sc[...] = w2_ref[0, 0].astype(BF16)

    @pl.when(i < n_used_ref[0])
    def _():
        xb = x_ref[...]
        h1 = jnp.dot(xb, w1_sc[...], preferred_element_type=F32)
        h3 = jnp.dot(xb, w3_sc[...], preferred_element_type=F32)
        a = (h1 * jax.nn.sigmoid(h1) * h3).astype(BF16)
        y_ref[...] = jnp.dot(a, w2_sc[...], preferred_element_type=F32).astype(y_ref.dtype)

    @pl.when(i >= n_used_ref[0])
    def _():
        y_ref[...] = jnp.zeros(y_ref.shape, y_ref.dtype)


def _experts(xin, blk_e, n_used, layer, w1, w3, w2):
    n_rows, d = xin.shape
    de = w1.shape[-1]
    nb = n_rows // MOE_ROWS
    grid_spec = pltpu.PrefetchScalarGridSpec(
        num_scalar_prefetch=2,
        grid=(nb,),
        in_specs=[
            pl.BlockSpec((MOE_ROWS, d), lambda i, be, nu: (i, 0)),
            pl.BlockSpec((1, 1, d, de), lambda i, be, nu: (layer, be[i], 0, 0)),
            pl.BlockSpec((1, 1, d, de), lambda i, be, nu: (layer, be[i], 0, 0)),
            pl.BlockSpec((1, 1, de, d), lambda i, be, nu: (layer, be[i], 0, 0)),
        ],
        out_specs=pl.BlockSpec((MOE_ROWS, d), lambda i, be, nu: (i, 0)),
        scratch_shapes=[pltpu.VMEM((d, de), BF16), pltpu.VMEM((d, de), BF16), pltpu.VMEM((de, d), BF16)],
    )
    return pl.pallas_call(
        _expert_kernel,
        grid_spec=grid_spec,
        out_shape=jax.ShapeDtypeStruct((n_rows, d), BF16),
        compiler_params=_cparams(("arbitrary",)),
        name="experts",
    )(blk_e, n_used, xin, w1, w3, w2)


def _moe(f_tok, route, layer, w1, w3, w2):
    n_tok, d = f_tok.shape
    e1, e2 = route[:, 0].astype(jnp.int32), route[:, 1].astype(jnp.int32)
    gate = route[:, 2:4]

    n_assign = n_tok * TOP_K
    e_flat = jnp.concatenate([e1, e2]).astype(jnp.int32)
    assert n_assign % MOE_ROWS == 0
    onehot = (e_flat[:, None] == jnp.arange(N_EXPERTS, dtype=jnp.int32)[None, :]).astype(F32)
    tri = jnp.tril(jnp.ones((MOE_ROWS, MOE_ROWS), F32))
    within = jnp.einsum('ij,bjk->bik', tri, onehot.reshape(-1, MOE_ROWS, N_EXPERTS), precision=lax.Precision.HIGHEST)
    totals = within[:, -1, :]
    before = jnp.cumsum(totals, axis=0) - totals
    csum = (within + before[:, None, :]).reshape(n_assign, N_EXPERTS)
    counts = jnp.sum(totals, axis=0).astype(jnp.int32)
    rank = jnp.sum(csum * onehot, axis=1).astype(jnp.int32) - 1
    padded = (counts + MOE_ROWS - 1) // MOE_ROWS * MOE_ROWS
    pend = jnp.cumsum(padded)
    pstart = pend - padded
    dest = pstart[e_flat] + rank
    n_rows = -(-n_assign // MOE_ROWS) * MOE_ROWS + N_EXPERTS * MOE_ROWS
    nb = n_rows // MOE_ROWS
    blk_row = jnp.arange(nb, dtype=jnp.int32) * MOE_ROWS
    blk_e = jnp.minimum(jnp.sum((pend[None, :] <= blk_row[:, None]).astype(jnp.int32), axis=1), N_EXPERTS - 1)
    order = jnp.argsort(e_flat, stable=True).astype(jnp.int32)
    row = jnp.arange(n_rows, dtype=jnp.int32)
    row_e = jnp.repeat(blk_e, MOE_ROWS)
    pos = row - pstart[row_e]
    src = jnp.minimum((jnp.cumsum(counts) - counts)[row_e] + pos, n_assign - 1)
    row_tok = jnp.where(pos < counts[row_e], order[src], row) % n_tok
    xin = f_tok[row_tok]
    n_used = (pend[-1:] // MOE_ROWS).astype(jnp.int32)
    y = _experts(xin, blk_e, n_used, layer, w1, w3, w2)
    return y[dest[:n_tok]], y[dest[n_tok:]], gate


def _combine_kernel(x_ref, y0_ref, y1_ref, gate_ref, g2_ref, o_ref):
    gate = gate_ref[0]
    y = gate[:, 0:1] * y0_ref[0].astype(F32) + gate[:, 1:2] * y1_ref[0].astype(F32)
    o_ref[0] = x_ref[0] + g2_ref[0] * y


def _combine(x, y0, y1, gate, g2, n_lat_tiles):
    bsz, n_rows, d = x.shape
    tm = ROW_TILE
    row = lambda b, i: (b, i, 0)
    return pl.pallas_call(
        _combine_kernel,
        grid=(bsz, n_rows // tm),
        in_specs=[pl.BlockSpec((1, tm, d), row), pl.BlockSpec((1, tm, d), row), pl.BlockSpec((1, tm, d), row),
                  pl.BlockSpec((1, tm, TOP_K), row),
                  pl.BlockSpec((1, 1, d), lambda b, i: (jnp.where(i >= n_lat_tiles, bsz, b), 0, 0))],
        out_specs=pl.BlockSpec((1, tm, d), row),
        out_shape=jax.ShapeDtypeStruct((bsz, n_rows, d), F32),
        compiler_params=_cparams(("parallel", "parallel")),
        name="combine",
    )(x, y0, y1, gate, g2)


def _rope_lanes(n_lat, n_ctx, rot_dim):
    rows = n_lat // GRID_W
    r = jnp.repeat(jnp.arange(rows), GRID_W)
    c = jnp.tile(jnp.arange(GRID_W), rows)
    n_freq = rot_dim // 4
    inv = ROPE_THETA ** (-jnp.arange(n_freq, dtype=F32) / n_freq)
    ang = jnp.concatenate([r[:, None] * inv, c[:, None] * inv], axis=-1)
    cos = jnp.repeat(jnp.cos(ang), 2, axis=1)
    sin = jnp.repeat(jnp.sin(ang), 2, axis=1) * jnp.tile(jnp.array([-1.0, 1.0], F32), rot_dim // 2)
    pad = ((0, n_ctx), (0, 0))
    return jnp.pad(cos, pad, constant_values=1.0), jnp.pad(sin, pad)


def _pad_cols(w, n):
    return jnp.pad(w, ((0, 0), (0, n - w.shape[1])))


def _pad_row(v, n):
    return jnp.pad(v, (0, n - v.shape[0])).reshape(1, n)


def _round_up(n, m):
    return -(-n // m) * m


def _head_blocks(w, n_heads, width, start, size):
    w = w.reshape(w.shape[0], n_heads, width)[:, :, start:start + size]
    return jnp.pad(w, ((0, 0), (0, 0), (0, LANE - size))).reshape(w.shape[0], n_heads * LANE)


def kernel(x, c, ctx, c_ctx, w_ada, b_ada, g_mix, g_ffn, e_w_in, e_w_out, e_g_q, e_g_k, e_conv_w, e_conv_b, e_b_gates, e_g_h, o_w_in, o_w_out, o_g_qc, o_g_kc, o_lam, o_g_sub, o_g_cq, o_w_uq, o_g_ckv, o_w_ukv, o_g_qd, o_g_kd, w_router, b_router, w1, w3, w2):
    bsz, n_lat, d = x.shape
    n_ctx = ctx.shape[1]
    t = n_lat + n_ctx
    depth = w_ada.shape[0]
    assert n_lat % ROW_TILE == 0 and n_ctx % ROW_TILE == 0 and n_ctx % B_CHUNK == 0
    n_lat_tiles = n_lat // ROW_TILE
    hd = D_NOPE + D_ROPE

    cos64, sin64 = _rope_lanes(n_lat, n_ctx, A_HEAD_DIM)
    cos64, sin64 = jnp.tile(cos64, (1, 2)), jnp.tile(sin64, (1, 2))
    cos_d, sin_d = _rope_lanes(n_lat, n_ctx, D_ROPE)
    cos_d = jnp.pad(cos_d, ((0, 0), (D_NOPE, LANE - hd)), constant_values=1.0)
    sin_d = jnp.pad(sin_d, ((0, 0), (D_NOPE, LANE - hd)))
    half = jnp.arange(LANE) // HEAD64
    bd64 = (half[:, None] == half[None, :]).astype(BF16)
    wr_hi = w_router.astype(BF16)
    wr_lo = (w_router.astype(F32) - wr_hi.astype(F32)).astype(BF16)
    cond = jnp.concatenate([c, c_ctx[None, :]], axis=0)
    x_all = jnp.concatenate([x, ctx], axis=1)

    for layer in range(depth):
        last = layer == depth - 1
        j = layer // 2
        mod = (jax.nn.silu(cond) @ w_ada[layer] + b_ada[layer]).reshape(bsz + 1, 6, d)
        sh1, sc1, g1, sh2, sc2, g2 = [mod[:, i] for i in range(6)]
        mod_in = jnp.stack([sh1, sc1], axis=1)
        mod_out = jnp.stack([g1, sh2, sc2], axis=1)
        n_rows = n_lat if last else t
        if layer % 2 == 0:
            w_in = _pad_cols(e_w_in[j], _round_up(e_w_in.shape[-1], LANE)).astype(BF16)
            q, kd, vd, qk, vb, ob, gates = _inproj_even(
                x_all, mod_in, g_mix[layer], w_in, cos64, sin64, bd64, jnp.tile(e_g_q[j], 2).reshape(1, LANE),
                jnp.tile(e_g_k[j], 2).reshape(1, LANE), _pad_row(e_b_gates[j], LANE), n_lat_tiles)
            oa = _attn(q, kd, vd, mode="gqa", tq=ATTN_ROWS // 4, tk=KEY_TILE, q_rows=(0, n_lat), k_rows=(0, t))
            if not last:
                oa = _attn(q, kd, vd, mode="gqa", tq=ATTN_ROWS // 4, tk=KEY_TILE, q_rows=(n_lat, n_ctx), k_rows=(n_lat, n_ctx),
                           into=oa)
            hf, hb = _mlstm(qk, vb, gates, e_conv_w[j], e_conv_b[j].reshape(1, -1), n_lat)
            heads, consts, kern = [oa, hf, hb, ob], [e_g_h[j].reshape(1, HALF)], _outproj_even_kernel
            w_out = e_w_out[j]
        else:
            assert last, "the odd mixer is only wired as the final layer"
            lam_init = LAM_INIT_BASE[0] - LAM_INIT_BASE[1] * math.exp(-LAM_INIT_BASE[2] * layer)
            lq1, lk1, lq2, lk2 = o_lam[j].astype(F32)
            lam = jnp.exp(jnp.sum(lq1 * lk1)) - jnp.exp(jnp.sum(lq2 * lk2)) + lam_init
            w_in = _pad_cols(o_w_in[j], _round_up(o_w_in.shape[-1], LANE)).astype(BF16)
            consts = [bd64, jnp.tile(o_g_qc[j], 2).reshape(1, LANE), jnp.tile(o_g_kc[j], 2).reshape(1, LANE),
                      o_g_cq[j].reshape(1, D_Q_LORA), _head_blocks(o_w_uq[j], D_HEADS, hd, 0, hd).astype(BF16),
                      _pad_row(o_g_qd[j], LANE), o_g_ckv[j].reshape(1, D_KV_LORA),
                      _head_blocks(o_w_ukv[j], D_HEADS, D_NOPE + D_V_DIM, 0, D_NOPE).astype(BF16),
                      _head_blocks(o_w_ukv[j], D_HEADS, D_NOPE + D_V_DIM, D_NOPE, D_V_DIM).astype(BF16),
                      _pad_row(o_g_kd[j], LANE)]
            qc, kc, vc, qd, kdl, vdl = _inproj_odd(x_all, mod_in, g_mix[layer], w_in, cos64, sin64, cos_d, sin_d,
                                                   consts, n_lat_tiles)
            extra = (jnp.full((1, LANE), lam, F32), (o_g_sub[j] * (1.0 - lam_init)).reshape(1, LANE))
            oc = _attn(qc, kc, vc, mode="diff", tq=ATTN_ROWS // 2, tk=KEY_TILE, q_rows=(0, n_lat), k_rows=(0, t), extra=extra)
            od = _attn(qd, kdl, vdl, mode="mla", tq=ATTN_ROWS, tk=KEY_TILE, q_rows=(0, n_lat), k_rows=(0, t))
            heads, consts, kern = [oc, od], [], _outproj_odd_kernel
            w_out = o_w_out[j]
        x_new, f, route = _outproj(kern, heads, consts, x_all, mod_out, g_ffn[layer], w_out.astype(BF16), wr_hi, wr_lo,
                                   b_router.astype(F32).reshape(1, N_EXPERTS), n_rows, n_lat_tiles)
        y0, y1, gate = _moe(f.reshape(bsz * n_rows, d), route.reshape(bsz * n_rows, LANE), layer, w1, w3, w2)
        x_all = _combine(x_new, y0.reshape(bsz, n_rows, d), y1.reshape(bsz, n_rows, d),
                         gate.reshape(bsz, n_rows, TOP_K), g2.reshape(bsz + 1, 1, d), n_lat_tiles)
    return x_all
```

```python
import functools
import math

import jax
import jax.numpy as jnp
import numpy as np
from jax import lax
from jax.experimental import pallas as pl
from jax.experimental.pallas import tpu as pltpu

F32 = jnp.float32
BF16 = jnp.bfloat16
EPS = 1e-6
LOG2E = 1.4426950408889634
NEG_BIG = -1e30
GRID_W = 64
ROPE_THETA = 10000.0
LAM_INIT_BASE = (0.8, 0.6, 0.3)

A_HEAD_DIM, A_HEADS, A_KV_HEADS = 64, 8, 2
B_HEAD_DIM, B_HEADS, B_CONV, B_CHUNK = 128, 4, 3, 128
C_HEAD_DIM, C_V_DIM, C_HEADS = 64, 128, 4
D_HEADS, D_Q_LORA, D_KV_LORA, D_NOPE, D_ROPE, D_V_DIM = 4, 256, 128, 64, 32, 128
N_EXPERTS, N_GROUPS, TOP_K, D_EXPERT = 32, 4, 2, 512
HALF = 512
HEAD64 = 64

LANE = 128
SUBLANE = 8
ROW_TILE = 256
MOE_ROWS = 512
KEY_TILE = 1280
ATTN_ROWS = 2048
ATTN_UNROLL = 2
VMEM_LIMIT = 60 * 1024 * 1024


def _cparams(sem):
    return pltpu.CompilerParams(dimension_semantics=sem, vmem_limit_bytes=VMEM_LIMIT)


def _blk(a, i):
    return a[:, i * LANE:(i + 1) * LANE]


def _modulated(x_ref, mod_ref, g_ref):
    x = x_ref[0]
    y = x * lax.rsqrt(jnp.mean(x * x, axis=-1, keepdims=True) + EPS) * g_ref[...]
    return (y * (1.0 + mod_ref[0, 1:2, :]) + mod_ref[0, 0:1, :]).astype(BF16)


def _pairswap(b, even):
    return jnp.where(even, pltpu.roll(b, LANE - 1, 1), pltpu.roll(b, 1, 1))


def _rope(b, cos, sin, even):
    return b * cos + _pairswap(b, even) * sin


def _head64_norm(b, bd, gain):
    ss = jnp.dot((b * b).astype(BF16), bd, preferred_element_type=F32)
    return b * lax.rsqrt(ss * (1.0 / HEAD64) + EPS) * gain


def _lane_norm(b, n_real, gain):
    ss = jnp.sum(b * b, axis=1, keepdims=True)
    return b * lax.rsqrt(ss * (1.0 / n_real) + EPS) * gain


def _inproj_even_kernel(x_ref, mod_ref, g_ref, w_ref, cos_ref, sin_ref, bd_ref, gq_ref, gk_ref, bg_ref,
                        q_ref, k_ref, v_ref, qk_ref, vb_ref, ob_ref, gt_ref):
    z = jnp.dot(_modulated(x_ref, mod_ref, g_ref), w_ref[...], preferred_element_type=F32)
    lane = lax.broadcasted_iota(jnp.int32, (z.shape[0], LANE), 1)
    even = (lane & 1) == 0
    lo = lane < HEAD64
    cos, sin, bd = cos_ref[...], sin_ref[...], bd_ref[...]
    q_scale = A_HEAD_DIM ** -0.5 * LOG2E
    for c in range(4):
        qb = _rope(_head64_norm(_blk(z, c), bd, gq_ref[...]), cos, sin, even) * q_scale
        q_ref[0, :, c * LANE:(c + 1) * LANE] = qb.astype(BF16)
    kn = _rope(_head64_norm(_blk(z, 4), bd, gk_ref[...]), cos, sin, even)
    for src, ref in ((kn, k_ref), (_blk(z, 5), v_ref)):
        sw = pltpu.roll(src, HEAD64, 1)
        ref[0] = jnp.concatenate([jnp.where(lo, src, sw), jnp.where(lo, sw, src)], axis=1).astype(BF16)
    qk_ref[0] = z[:, 6 * LANE:14 * LANE]
    vb_ref[0] = z[:, 14 * LANE:18 * LANE].astype(BF16)
    ob_ref[0] = z[:, 18 * LANE:22 * LANE]
    gt_ref[0] = _blk(z, 22) + bg_ref[...]


def _row_specs(tm, d, bsz, n_lat_tiles):
    return [pl.BlockSpec((1, tm, d), lambda b, i: (b, i, 0)),
            pl.BlockSpec((1, 2, d), lambda b, i: (jnp.where(i >= n_lat_tiles, bsz, b), 0, 0)),
            pl.BlockSpec((1, d), lambda b, i: (0, 0))]


def _const_spec(a):
    return pl.BlockSpec(a.shape, lambda b, i: (0,) * a.ndim)


def _inproj_even(x, mod, g, w, cos, sin, bd, gq, gk, bg, n_lat_tiles):
    bsz, t, d = x.shape
    tm = ROW_TILE
    tab = pl.BlockSpec((tm, LANE), lambda b, i: (i, 0))
    consts = [bd, gq, gk, bg]
    widths = [(4 * LANE, BF16), (2 * LANE, BF16), (2 * LANE, BF16), (8 * LANE, F32), (4 * LANE, BF16),
              (4 * LANE, F32), (LANE, F32)]
    return pl.pallas_call(
        _inproj_even_kernel,
        grid=(bsz, t // tm),
        in_specs=_row_specs(tm, d, bsz, n_lat_tiles) + [_const_spec(w), tab, tab] + [_const_spec(a) for a in consts],
        out_specs=[pl.BlockSpec((1, tm, n), lambda b, i: (b, i, 0)) for n, _ in widths],
        out_shape=[jax.ShapeDtypeStruct((bsz, t, n), dt) for n, dt in widths],
        compiler_params=_cparams(("parallel", "parallel")),
        name="inproj_even",
    )(x, mod, g.reshape(1, d), w, cos, sin, *consts)


def _inproj_odd_kernel(x_ref, mod_ref, g_ref, w_ref, cos_ref, sin_ref, cosd_ref, sind_ref, bd_ref, gqc_ref, gkc_ref,
                       gcq_ref, wuq_ref, gqd_ref, gckv_ref, wkn_ref, wvd_ref, gkd_ref,
                       qc_ref, kc_ref, vc_ref, qd_ref, kd_ref, vd_ref):
    z = jnp.dot(_modulated(x_ref, mod_ref, g_ref), w_ref[...], preferred_element_type=F32)
    lane = lax.broadcasted_iota(jnp.int32, (z.shape[0], LANE), 1)
    even = (lane & 1) == 0
    cos, sin, bd = cos_ref[...], sin_ref[...], bd_ref[...]
    cosd, sind = cosd_ref[...], sind_ref[...]
    c_scale = C_HEAD_DIM ** -0.5 * LOG2E
    d_scale = (D_NOPE + D_ROPE) ** -0.5 * LOG2E
    for c in range(4):
        sl = slice(c * LANE, (c + 1) * LANE)
        qc_ref[0, :, sl] = (_rope(_head64_norm(_blk(z, c), bd, gqc_ref[...]), cos, sin, even) * c_scale).astype(BF16)
        kc_ref[0, :, sl] = _rope(_head64_norm(_blk(z, 4 + c), bd, gkc_ref[...]), cos, sin, even).astype(BF16)
    vc_ref[0] = z[:, 8 * LANE:12 * LANE].astype(BF16)
    cq = z[:, 12 * LANE:14 * LANE]
    cq = cq * lax.rsqrt(jnp.mean(cq * cq, axis=-1, keepdims=True) + EPS) * gcq_ref[...]
    qd = jnp.dot(cq.astype(BF16), wuq_ref[...], preferred_element_type=F32)
    ckv = _blk(z, 14)
    ckv = (ckv * lax.rsqrt(jnp.mean(ckv * ckv, axis=-1, keepdims=True) + EPS) * gckv_ref[...]).astype(BF16)
    kn = jnp.dot(ckv, wkn_ref[...], preferred_element_type=F32)
    vd_ref[0] = jnp.dot(ckv, wvd_ref[...], preferred_element_type=F32).astype(BF16)
    kr = pltpu.roll(_blk(z, 15), D_NOPE, 1)
    hd = D_NOPE + D_ROPE
    for c in range(4):
        sl = slice(c * LANE, (c + 1) * LANE)
        qd_ref[0, :, sl] = (_rope(_lane_norm(_blk(qd, c), hd, gqd_ref[...]), cosd, sind, even) * d_scale).astype(BF16)
        kd_ref[0, :, sl] = _rope(_lane_norm(_blk(kn, c) + kr, hd, gkd_ref[...]), cosd, sind, even).astype(BF16)


def _inproj_odd(x, mod, g, w, cos, sin, cosd, sind, consts, n_lat_tiles):
    bsz, t, d = x.shape
    tm = ROW_TILE
    tab = pl.BlockSpec((tm, LANE), lambda b, i: (i, 0))
    return pl.pallas_call(
        _inproj_odd_kernel,
        grid=(bsz, t // tm),
        in_specs=(_row_specs(tm, d, bsz, n_lat_tiles) + [_const_spec(w), tab, tab, tab, tab]
                  + [_const_spec(a) for a in consts]),
        out_specs=[pl.BlockSpec((1, tm, HALF), lambda b, i: (b, i, 0)) for _ in range(6)],
        out_shape=[jax.ShapeDtypeStruct((bsz, t, HALF), BF16) for _ in range(6)],
        compiler_params=_cparams(("parallel", "parallel")),
        name="inproj_odd",
    )(x, mod, g.reshape(1, d), w, cos, sin, cosd, sind, *consts)


def _attn_kernel(*refs, mode, tq, tk, n_tiles):
    if mode == "diff":
        q_ref, k_ref, v_ref, lam_ref, gsub_ref, o_ref, q_sc, s_sc, m_sc, a_sc, l_sc, acc_sc = refs
    else:
        q_ref, k_ref, v_ref = refs[:3]
        o_ref, q_sc, s_sc, m_sc, a_sc, l_sc, acc_sc = refs[-7:]
    rows = q_sc.shape[0]
    n_blk = tk // LANE
    lo = lax.broadcasted_iota(jnp.int32, (tq, LANE), 1) < HEAD64

    if mode == "mla":
        q_sc[...] = q_ref[0]
    else:
        for p in range(q_ref.shape[2] // LANE):
            qb = _blk(q_ref[0], p)
            q_sc[2 * p * tq:(2 * p + 1) * tq] = jnp.where(lo, qb, jnp.zeros_like(qb))
            q_sc[(2 * p + 1) * tq:(2 * p + 2) * tq] = jnp.where(lo, jnp.zeros_like(qb), qb)

    def score_stage(j, slot):
        start = pl.multiple_of(j * tk, LANE)
        s = lax.dot_general(q_sc[...], k_ref[0, pl.ds(start, tk), :], (((1,), (1,)), ((), ())),
                            preferred_element_type=F32)
        s_sc[slot] = s
        m_tile = jnp.max(functools.reduce(jnp.maximum, [_blk(s, c) for c in range(n_blk)]), axis=1, keepdims=True)
        m_prev = m_sc[1 - slot]
        m_new = jnp.maximum(m_prev, m_tile)
        m_sc[slot] = m_new
        a_sc[slot] = jnp.exp2(m_prev - m_new)

    def value_stage(j, slot):
        start = pl.multiple_of(j * tk, LANE)
        m = m_sc[slot]
        alpha = a_sc[slot]
        p_blocks = [jnp.exp2((s_sc[slot, :, c * LANE:(c + 1) * LANE] - m).astype(BF16)) for c in range(n_blk)]
        l_sc[...] = alpha * l_sc[...] + functools.reduce(jnp.add, p_blocks).astype(F32)
        p = jnp.concatenate(p_blocks, axis=1)
        acc_sc[...] = alpha * acc_sc[...] + jnp.dot(p, v_ref[0, pl.ds(start, tk), :], preferred_element_type=F32)

    m_sc[1] = jnp.full((rows, LANE), NEG_BIG, F32)
    l_sc[...] = jnp.zeros((rows, LANE), F32)
    acc_sc[...] = jnp.zeros((rows, LANE), F32)
    score_stage(0, 0)
    peel = (n_tiles - 1) % ATTN_UNROLL
    for j in range(peel):
        value_stage(j, j % 2)
        score_stage(j + 1, (j + 1) % 2)

    def body(t, carry):
        for u in range(ATTN_UNROLL):
            j = ATTN_UNROLL * t + peel + u
            value_stage(j, (peel + u) % 2)
            score_stage(j + 1, (peel + u + 1) % 2)
        return carry

    lax.fori_loop(0, (n_tiles - 1) // ATTN_UNROLL, body, 0)
    value_stage(n_tiles - 1, (n_tiles - 1) % 2)

    o = acc_sc[...] / jnp.sum(l_sc[...], axis=1, keepdims=True)
    head = lambda h: o[h * tq:(h + 1) * tq]
    if mode == "gqa":
        for p in range(o_ref.shape[2] // LANE):
            o_ref[0, :, p * LANE:(p + 1) * LANE] = jnp.where(lo, head(2 * p), head(2 * p + 1)).astype(o_ref.dtype)
    elif mode == "diff":
        dlt = head(0) - lam_ref[...] * head(1)
        o_ref[0] = (dlt * lax.rsqrt(jnp.mean(dlt * dlt, axis=1, keepdims=True) + EPS) * gsub_ref[...]).astype(o_ref.dtype)
    else:
        o_ref[0] = o.astype(o_ref.dtype)


def _attn(q, k, v, *, mode, tq, tk, q_rows, k_rows, extra=(), into=None):
    bsz, t_q, wq = q.shape
    qw = 2 * LANE if mode == "gqa" else LANE
    q0, nq = q_rows
    k0, nk = k_rows
    tq, tk = min(tq, nq), min(tk, nk)
    rows = {"gqa": 4, "diff": 2, "mla": 1}[mode] * tq
    assert q0 % tq == 0 and nq % tq == 0 and k0 % nk == 0 and nk % tk == 0
    qmap = lambda b, g, i: (b, q0 // tq + i, g)
    kmap = lambda b, g, i: (b, k0 // nk, g)
    in_specs = [pl.BlockSpec((1, tq, qw), qmap), pl.BlockSpec((1, nk, LANE), kmap), pl.BlockSpec((1, nk, LANE), kmap)]
    in_specs += [pl.BlockSpec(a.shape, lambda b, g, i: (0, 0)) for a in extra]
    args = [q, k, v, *extra]
    aliases = {}
    if into is not None:
        in_specs.append(pl.BlockSpec(memory_space=pl.ANY))
        aliases = {len(args): 0}
        args.append(into)

    def kern(*refs):
        if into is not None:
            refs = refs[:len(args) - 1] + refs[len(args):]
        _attn_kernel(*refs, mode=mode, tq=tq, tk=tk, n_tiles=nk // tk)

    return pl.pallas_call(
        kern,
        grid=(bsz, wq // qw, nq // tq),
        in_specs=in_specs,
        out_specs=pl.BlockSpec((1, tq, qw), qmap),
        out_shape=jax.ShapeDtypeStruct((bsz, t_q, wq), BF16),
        scratch_shapes=[pltpu.VMEM((rows, LANE), BF16), pltpu.VMEM((2, rows, tk), F32),
                        pltpu.VMEM((2, rows, LANE), F32), pltpu.VMEM((2, rows, LANE), F32),
                        pltpu.VMEM((rows, LANE), F32), pltpu.VMEM((rows, LANE), F32)],
        input_output_aliases=aliases,
        compiler_params=_cparams(("parallel", "parallel", "parallel")),
        name="attn_" + mode,
    )(*args)


def _mlstm_kernel(qf_ref, pf_ref, nf_ref, vf_ref, gf_ref, qb_ref, pb_ref, nb_ref, vb_ref, gb_ref, cw_ref, cb_ref,
                  hf_ref, hb_ref, c_sc, m_sc, *, n_lat_chunks, n_chunks):
    L, dh = B_CHUNK, B_HEAD_DIM
    c = pl.program_id(1)
    n_ctx_chunks = n_chunks - n_lat_chunks
    fwd_chunk = jnp.where(c < n_ctx_chunks, n_lat_chunks + c, c - n_ctx_chunks)
    bwd_chunk = n_chunks - 1 - c

    @pl.when(c == 0)
    def _():
        c_sc[...] = jnp.zeros(c_sc.shape, F32)
        m_sc[...] = jnp.zeros(m_sc.shape, F32)

    row = lax.broadcasted_iota(jnp.int32, (L, L), 0)
    col = lax.broadcasted_iota(jnp.int32, (L, L), 1)
    upper = (row <= col).astype(F32)
    lower = (col <= row).astype(F32)
    rid = lax.broadcasted_iota(jnp.int32, (L, 1), 0)
    glane = lax.broadcasted_iota(jnp.int32, (L, LANE), 1)
    is_f = (glane // B_HEADS) % 2 == 1
    ones_col = (lax.broadcasted_iota(jnp.int32, (L, dh), 1) == 0).astype(BF16)
    hi = lax.Precision.HIGHEST

    def direction(qk_ref, p_ref, n_ref, v_ref, g_ref, h_ref, chunk, reverse):
        first = (chunk == 0) | (chunk == n_lat_chunks)
        last = (chunk == n_lat_chunks - 1) | (chunk == n_chunks - 1)
        x = qk_ref[0]
        prev_row = jnp.where(first, 0.0, p_ref[0, SUBLANE - 1:SUBLANE, :])
        next_row = jnp.where(last, 0.0, n_ref[0, 0:1, :])
        x_prev = jnp.where(rid == 0, prev_row, pltpu.roll(x, 1, 0))
        x_next = jnp.where(rid == L - 1, next_row, pltpu.roll(x, L - 1, 0))
        y = x_prev * cw_ref[0:1, :] + x * cw_ref[1:2, :] + x_next * cw_ref[2:3, :] + cb_ref[...]
        y = y * jax.nn.sigmoid(y)
        g = g_ref[0]
        g = jnp.where(is_f, jnp.minimum(g, 0.0) - jnp.log(1.0 + jnp.exp(-jnp.abs(g))), g)
        g_t = jnp.transpose(g)
        cum_r = jnp.dot(g_t, upper, preferred_element_type=F32, precision=hi)
        cum_c = jnp.dot(lower, g, preferred_element_type=F32, precision=hi)
        if reverse:
            cum_r = cum_r[:, L - 1:L] - cum_r + g_t
            cum_c = cum_c[L - 1:L, :] - cum_c + g
        gi = 2 * B_HEADS if reverse else 0
        mask = (col >= row) if reverse else (col <= row)
        for h in range(B_HEADS):
            idx = (B_HEADS if reverse else 0) + h
            q = _blk(y, h).astype(BF16)
            k = _blk(y, B_HEADS + h) * (dh ** -0.5)
            v_ext = jnp.concatenate([_blk(v_ref[0], h), ones_col], axis=1)
            i_r = g_t[gi + h:gi + h + 1, :]
            b_r = cum_r[gi + B_HEADS + h:gi + B_HEADS + h + 1, :]
            i_c = g[:, gi + h:gi + h + 1]
            b_c = cum_c[:, gi + B_HEADS + h:gi + B_HEADS + h + 1]
            m_prev = m_sc[idx][:, 0:1]
            c_ext = c_sc[idx]

            log_w = jnp.where(mask, b_c - b_r + i_r, -jnp.inf)
            log_inter = b_c + m_prev
            m_row = jnp.maximum(log_inter, jnp.max(log_w, axis=1, keepdims=True))
            w_intra = jnp.exp(log_w - m_row)
            a_inter = jnp.exp(log_inter - m_row)
            sc = lax.dot_general(q, k.astype(BF16), (((1,), (1,)), ((), ())), preferred_element_type=F32) * w_intra
            q_c = jnp.dot(q, c_ext.astype(BF16), preferred_element_type=F32)
            s_v = jnp.dot(sc.astype(BF16), v_ext, preferred_element_type=F32)
            num = a_inter * q_c[:, :dh] + s_v[:, :dh]
            den = a_inter * q_c[:, dh:dh + 1] + s_v[:, dh:dh + 1]
            h_ref[0, :, h * dh:(h + 1) * dh] = num / jnp.maximum(jnp.abs(den), jnp.exp(-m_row))

            b_last = b_r[:, 0:1] if reverse else b_r[:, L - 1:L]
            m_new = jnp.maximum(b_last + m_prev, jnp.max(b_last - b_r + i_r, axis=1, keepdims=True))
            g_c = jnp.exp(b_last - b_c + i_c - m_new)
            decay = jnp.exp(b_last + m_prev - m_new)
            upd = jnp.dot(jnp.transpose(k).astype(BF16), (g_c * v_ext.astype(F32)).astype(BF16),
                          preferred_element_type=F32)
            c_sc[idx] = decay * c_ext + upd
            m_sc[idx] = jnp.broadcast_to(m_new, (1, LANE))

    direction(qf_ref, pf_ref, nf_ref, vf_ref, gf_ref, hf_ref, fwd_chunk, False)
    direction(qb_ref, pb_ref, nb_ref, vb_ref, gb_ref, hb_ref, bwd_chunk, True)


def _mlstm(qk, vb, gates, conv_w, conv_b, n_lat):
    bsz, t, _ = qk.shape
    L = B_CHUNK
    n_chunks, n_lat_chunks = t // L, n_lat // L
    n_ctx_chunks = n_chunks - n_lat_chunks
    per = L // SUBLANE
    fwd = lambda c: jnp.where(c < n_ctx_chunks, n_lat_chunks + c, c - n_ctx_chunks)
    bwd = lambda c: n_chunks - 1 - c

    def specs(order):
        return [
            pl.BlockSpec((1, L, 2 * HALF), lambda b, c: (b, order(c), 0)),
            pl.BlockSpec((1, SUBLANE, 2 * HALF), lambda b, c: (b, jnp.maximum(order(c) * per - 1, 0), 0)),
            pl.BlockSpec((1, SUBLANE, 2 * HALF), lambda b, c: (b, jnp.minimum((order(c) + 1) * per, t // SUBLANE - 1), 0)),
            pl.BlockSpec((1, L, HALF), lambda b, c: (b, order(c), 0)),
            pl.BlockSpec((1, L, LANE), lambda b, c: (b, order(c), 0)),
        ]

    const = lambda a: pl.BlockSpec(a.shape, lambda b, c: (0, 0))
    return pl.pallas_call(
        functools.partial(_mlstm_kernel, n_lat_chunks=n_lat_chunks, n_chunks=n_chunks),
        grid=(bsz, n_chunks),
        in_specs=specs(fwd) + specs(bwd) + [const(conv_w), const(conv_b)],
        out_specs=[pl.BlockSpec((1, L, HALF), lambda b, c: (b, fwd(c), 0)),
                   pl.BlockSpec((1, L, HALF), lambda b, c: (b, bwd(c), 0))],
        out_shape=[jax.ShapeDtypeStruct((bsz, t, HALF), F32)] * 2,
        scratch_shapes=[pltpu.VMEM((2 * B_HEADS, B_HEAD_DIM, 2 * B_HEAD_DIM), F32),
                        pltpu.VMEM((2 * B_HEADS, 1, LANE), F32)],
        compiler_params=_cparams(("parallel", "arbitrary")),
        name="mlstm",
    )(qk, qk, qk, vb, gates, qk, qk, qk, vb, gates, conv_w, conv_b)


def _route(logits, bias):
    per = N_EXPERTS // N_GROUPS
    lane = lax.broadcasted_iota(jnp.int32, logits.shape, 1)
    lane_f = lane.astype(F32)
    scores = jax.nn.sigmoid(logits)
    sel = scores + bias

    def first_max(a):
        m = jnp.max(a, axis=1, keepdims=True)
        return m, jnp.min(jnp.where(a == m, lane_f, float(N_EXPERTS)), axis=1, keepdims=True)

    best, g_idx = None, None
    for g in range(N_GROUPS):
        a = jnp.where(lane // per == g, sel, -jnp.inf)
        m1, i1 = first_max(a)
        top2 = m1 + jnp.max(jnp.where(lane_f == i1, -jnp.inf, a), axis=1, keepdims=True)
        if g == 0:
            best, g_idx = top2, jnp.zeros_like(top2)
        else:
            better = top2 > best
            best, g_idx = jnp.where(better, top2, best), jnp.where(better, float(g), g_idx)
    a = jnp.where((lane // per).astype(F32) == g_idx, sel, -jnp.inf)
    _, e1 = first_max(a)
    _, e2 = first_max(jnp.where(lane_f == e1, -jnp.inf, a))
    s1 = jnp.sum(jnp.where(lane_f == e1, scores, 0.0), axis=1, keepdims=True)
    s2 = jnp.sum(jnp.where(lane_f == e2, scores, 0.0), axis=1, keepdims=True)
    return e1, e2, s1 / (s1 + s2), s2 / (s1 + s2)


def _out_tail(mix, x_ref, mod_ref, g_ref, w_ref, wrh_ref, wrl_ref, br_ref, xo_ref, f_ref, rt_ref):
    o = jnp.dot(mix, w_ref[...], preferred_element_type=F32)
    xn = x_ref[0] + mod_ref[0, 0:1, :] * o
    xo_ref[0] = xn
    y = xn * lax.rsqrt(jnp.mean(xn * xn, axis=-1, keepdims=True) + EPS) * g_ref[...]
    f = y * (1.0 + mod_ref[0, 2:3, :]) + mod_ref[0, 1:2, :]
    f_hi = f.astype(BF16)
    f_ref[0] = f_hi
    f_lo = (f - f_hi.astype(F32)).astype(BF16)
    logits = (jnp.dot(f_hi, wrh_ref[...], preferred_element_type=F32)
              + jnp.dot(f_lo, wrh_ref[...], preferred_element_type=F32)
              + jnp.dot(f_hi, wrl_ref[...], preferred_element_type=F32))
    cols = _route(logits, br_ref[...])
    out_lane = lax.broadcasted_iota(jnp.int32, (logits.shape[0], LANE), 1)
    rec = functools.reduce(lambda acc, ic: jnp.where(out_lane == ic[0], ic[1], acc), enumerate(cols),
                           jnp.zeros((logits.shape[0], LANE), F32))
    rt_ref[0] = jnp.transpose(rec)[:SUBLANE]


def _outproj_even_kernel(oa_ref, hf_ref, hb_ref, ob_ref, gh_ref, *tail):
    parts = [oa_ref[0]]
    for h in range(B_HEADS):
        hs = _blk(hf_ref[0], h) + _blk(hb_ref[0], h)
        hn = hs * lax.rsqrt(jnp.mean(hs * hs, axis=1, keepdims=True) + EPS) * _blk(gh_ref[...], h)
        parts.append((hn * jax.nn.sigmoid(_blk(ob_ref[0], h))).astype(BF16))
    _out_tail(jnp.concatenate(parts, axis=1), *tail)


def _outproj_odd_kernel(oc_ref, od_ref, *tail):
    _out_tail(jnp.concatenate([oc_ref[0], od_ref[0]], axis=1), *tail)


def _outproj(kern, heads, consts, x, mod, g, w, wr_hi, wr_lo, b_router, n_rows, n_lat_tiles):
    bsz, _, d = x.shape
    tm = ROW_TILE
    row = lambda b, i: (b, i, 0)
    return pl.pallas_call(
        kern,
        grid=(bsz, n_rows // tm),
        in_specs=([pl.BlockSpec((1, tm, HALF), row) for _ in heads] + [_const_spec(a) for a in consts]
                  + [pl.BlockSpec((1, tm, d), row),
                     pl.BlockSpec((1, 3, d), lambda b, i: (jnp.where(i >= n_lat_tiles, bsz, b), 0, 0)),
                     pl.BlockSpec((1, d), lambda b, i: (0, 0)), _const_spec(w), _const_spec(wr_hi),
                     _const_spec(wr_lo), _const_spec(b_router)]),
        out_specs=[pl.BlockSpec((1, tm, d), row), pl.BlockSpec((1, tm, d), row),
                   pl.BlockSpec((1, SUBLANE, tm), lambda b, i: (b, 0, i))],
        out_shape=[jax.ShapeDtypeStruct((bsz, n_rows, d), F32), jax.ShapeDtypeStruct((bsz, n_rows, d), BF16),
                   jax.ShapeDtypeStruct((bsz, SUBLANE, n_rows), F32)],
        compiler_params=_cparams(("parallel", "parallel")),
        name="outproj",
    )(*heads, *consts, x, mod, g.reshape(1, d), w, wr_hi, wr_lo, b_router)


def _expert_kernel(blk_e_ref, n_used_ref, x_ref, w1_ref, w3_ref, w2_ref, y_ref, w1_sc, w3_sc, w2_sc):
    i = pl.program_id(0)

    @pl.when((i == 0) | (blk_e_ref[i] != blk_e_ref[jnp.maximum(i - 1, 0)]))
    def _():
        w1_sc[...] = w1_ref[0, 0].astype(BF16)
        w3_sc[...] = w3_ref[0, 0].astype(BF16)
        w2_sc[...] = w2_ref[0, 0].astype(BF16)

    @pl.when(i < n_used_ref[0])
    def _():
        xb = x_ref[...]
        h1 = jnp.dot(xb, w1_sc[...], preferred_element_type=F32)
        h3 = jnp.dot(xb, w3_sc[...], preferred_element_type=F32)
        a = (h1 * jax.nn.sigmoid(h1) * h3).astype(BF16)
        y_ref[...] = jnp.dot(a, w2_sc[...], preferred_element_type=F32).astype(y_ref.dtype)

    @pl.when(i >= n_used_ref[0])
    def _():
        y_ref[...] = jnp.zeros(y_ref.shape, y_ref.dtype)


def _experts(xin, blk_e, n_used, layer, w1, w3, w2):
    n_rows, d = xin.shape
    de = w1.shape[-1]
    nb = n_rows // MOE_ROWS
    grid_spec = pltpu.PrefetchScalarGridSpec(
        num_scalar_prefetch=2,
        grid=(nb,),
        in_specs=[
            pl.BlockSpec((MOE_ROWS, d), lambda i, be, nu: (i, 0)),
            pl.BlockSpec((1, 1, d, de), lambda i, be, nu: (layer, be[i], 0, 0)),
            pl.BlockSpec((1, 1, d, de), lambda i, be, nu: (layer, be[i], 0, 0)),
            pl.BlockSpec((1, 1, de, d), lambda i, be, nu: (layer, be[i], 0, 0)),
        ],
        out_specs=pl.BlockSpec((MOE_ROWS, d), lambda i, be, nu: (i, 0)),
        scratch_shapes=[pltpu.VMEM((d, de), BF16), pltpu.VMEM((d, de), BF16), pltpu.VMEM((de, d), BF16)],
    )
    return pl.pallas_call(
        _expert_kernel,
        grid_spec=grid_spec,
        out_shape=jax.ShapeDtypeStruct((n_rows, d), BF16),
        compiler_params=_cparams(("arbitrary",)),
        name="experts",
    )(blk_e, n_used, xin, w1, w3, w2)


def _moe(f_tok, route, layer, w1, w3, w2):
    n_tok, d = f_tok.shape
    e1, e2 = route[:, 0].reshape(-1).astype(jnp.int32), route[:, 1].reshape(-1).astype(jnp.int32)
    gate = jnp.stack([route[:, 2].reshape(-1), route[:, 3].reshape(-1)], axis=-1)

    n_assign = n_tok * TOP_K
    e_flat = jnp.concatenate([e1, e2]).astype(jnp.int32)
    assert n_assign % MOE_ROWS == 0
    onehot = (e_flat[:, None] == jnp.arange(N_EXPERTS, dtype=jnp.int32)[None, :]).astype(F32)
    tri = jnp.tril(jnp.ones((MOE_ROWS, MOE_ROWS), F32))
    within = jnp.einsum('ij,bjk->bik', tri, onehot.reshape(-1, MOE_ROWS, N_EXPERTS), precision=lax.Precision.HIGHEST)
    totals = within[:, -1, :]
    before = jnp.cumsum(totals, axis=0) - totals
    csum = (within + before[:, None, :]).reshape(n_assign, N_EXPERTS)
    counts = jnp.sum(totals, axis=0).astype(jnp.int32)
    rank = jnp.sum(csum * onehot, axis=1).astype(jnp.int32) - 1
    padded = (counts + MOE_ROWS - 1) // MOE_ROWS * MOE_ROWS
    pend = jnp.cumsum(padded)
    pstart = pend - padded
    dest = pstart[e_flat] + rank
    n_rows = -(-n_assign // MOE_ROWS) * MOE_ROWS + N_EXPERTS * MOE_ROWS
    nb = n_rows // MOE_ROWS
    blk_row = jnp.arange(nb, dtype=jnp.int32) * MOE_ROWS
    blk_e = jnp.minimum(jnp.sum((pend[None, :] <= blk_row[:, None]).astype(jnp.int32), axis=1), N_EXPERTS - 1)
    order = jnp.argsort(e_flat, stable=True).astype(jnp.int32)
    row = jnp.arange(n_rows, dtype=jnp.int32)
    row_e = jnp.repeat(blk_e, MOE_ROWS)
    pos = row - pstart[row_e]
    src = jnp.minimum((jnp.cumsum(counts) - counts)[row_e] + pos, n_assign - 1)
    row_tok = jnp.where(pos < counts[row_e], order[src], row) % n_tok
    xin = f_tok[row_tok]
    n_used = (pend[-1:] // MOE_ROWS).astype(jnp.int32)
    y = _experts(xin, blk_e, n_used, layer, w1, w3, w2)
    return y[dest[:n_tok]], y[dest[n_tok:]], gate


def _combine_kernel(x_ref, y0_ref, y1_ref, gate_ref, g2_ref, o_ref):
    gate = gate_ref[0]
    y = gate[:, 0:1] * y0_ref[0].astype(F32) + gate[:, 1:2] * y1_ref[0].astype(F32)
    o_ref[0] = x_ref[0] + g2_ref[0] * y


def _combine(x, y0, y1, gate, g2, n_lat_tiles):
    bsz, n_rows, d = x.shape
    tm = ROW_TILE
    row = lambda b, i: (b, i, 0)
    return pl.pallas_call(
        _combine_kernel,
        grid=(bsz, n_rows // tm),
        in_specs=[pl.BlockSpec((1, tm, d), row), pl.BlockSpec((1, tm, d), row), pl.BlockSpec((1, tm, d), row),
                  pl.BlockSpec((1, tm, TOP_K), row),
                  pl.BlockSpec((1, 1, d), lambda b, i: (jnp.where(i >= n_lat_tiles, bsz, b), 0, 0))],
        out_specs=pl.BlockSpec((1, tm, d), row),
        out_shape=jax.ShapeDtypeStruct((bsz, n_rows, d), F32),
        compiler_params=_cparams(("parallel", "parallel")),
        name="combine",
    )(x, y0, y1, gate, g2)


def _rope_lanes(n_lat, n_ctx, rot_dim):
    rows = n_lat // GRID_W
    r = jnp.repeat(jnp.arange(rows), GRID_W)
    c = jnp.tile(jnp.arange(GRID_W), rows)
    n_freq = rot_dim // 4
    inv = ROPE_THETA ** (-jnp.arange(n_freq, dtype=F32) / n_freq)
    ang = jnp.concatenate([r[:, None] * inv, c[:, None] * inv], axis=-1)
    cos = jnp.repeat(jnp.cos(ang), 2, axis=1)
    sin = jnp.repeat(jnp.sin(ang), 2, axis=1) * jnp.tile(jnp.array([-1.0, 1.0], F32), rot_dim // 2)
    pad = ((0, n_ctx), (0, 0))
    return jnp.pad(cos, pad, constant_values=1.0), jnp.pad(sin, pad)


def _pad_cols(w, n):
    return jnp.pad(w, ((0, 0), (0, n - w.shape[1])))


def _pad_row(v, n):
    return jnp.pad(v, (0, n - v.shape[0])).reshape(1, n)


def _round_up(n, m):
    return -(-n // m) * m


def _head_blocks(w, n_heads, width, start, size):
    w = w.reshape(w.shape[0], n_heads, width)[:, :, start:start + size]
    return jnp.pad(w, ((0, 0), (0, 0), (0, LANE - size))).reshape(w.shape[0], n_heads * LANE)


def kernel(x, c, ctx, c_ctx, w_ada, b_ada, g_mix, g_ffn, e_w_in, e_w_out, e_g_q, e_g_k, e_conv_w, e_conv_b, e_b_gates, e_g_h, o_w_in, o_w_out, o_g_qc, o_g_kc, o_lam, o_g_sub, o_g_cq, o_w_uq, o_g_ckv, o_w_ukv, o_g_qd, o_g_kd, w_router, b_router, w1, w3, w2):
    bsz, n_lat, d = x.shape
    n_ctx = ctx.shape[1]
    t = n_lat + n_ctx
    depth = w_ada.shape[0]
    assert n_lat % ROW_TILE == 0 and n_ctx % ROW_TILE == 0 and n_ctx % B_CHUNK == 0
    n_lat_tiles = n_lat // ROW_TILE
    hd = D_NOPE + D_ROPE

    cos64, sin64 = _rope_lanes(n_lat, n_ctx, A_HEAD_DIM)
    cos64, sin64 = jnp.tile(cos64, (1, 2)), jnp.tile(sin64, (1, 2))
    cos_d, sin_d = _rope_lanes(n_lat, n_ctx, D_ROPE)
    cos_d = jnp.pad(cos_d, ((0, 0), (D_NOPE, LANE - hd)), constant_values=1.0)
    sin_d = jnp.pad(sin_d, ((0, 0), (D_NOPE, LANE - hd)))
    half = jnp.arange(LANE) // HEAD64
    bd64 = (half[:, None] == half[None, :]).astype(BF16)
    wr_hi = w_router.astype(BF16)
    wr_lo = (w_router.astype(F32) - wr_hi.astype(F32)).astype(BF16)
    cond = jnp.concatenate([c, c_ctx[None, :]], axis=0)
    x_all = jnp.concatenate([x, ctx], axis=1)

    for layer in range(depth):
        last = layer == depth - 1
        j = layer // 2
        mod = (jax.nn.silu(cond) @ w_ada[layer] + b_ada[layer]).reshape(bsz + 1, 6, d)
        sh1, sc1, g1, sh2, sc2, g2 = [mod[:, i] for i in range(6)]
        mod_in = jnp.stack([sh1, sc1], axis=1)
        mod_out = jnp.stack([g1, sh2, sc2], axis=1)
        n_rows = n_lat if last else t
        if layer % 2 == 0:
            w_in = _pad_cols(e_w_in[j], _round_up(e_w_in.shape[-1], LANE)).astype(BF16)
            q, kd, vd, qk, vb, ob, gates = _inproj_even(
                x_all, mod_in, g_mix[layer], w_in, cos64, sin64, bd64, jnp.tile(e_g_q[j], 2).reshape(1, LANE),
                jnp.tile(e_g_k[j], 2).reshape(1, LANE), _pad_row(e_b_gates[j], LANE), n_lat_tiles)
            oa = _attn(q, kd, vd, mode="gqa", tq=ATTN_ROWS // 4, tk=KEY_TILE, q_rows=(0, n_lat), k_rows=(0, t))
            if not last:
                oa = _attn(q, kd, vd, mode="gqa", tq=ATTN_ROWS // 4, tk=KEY_TILE, q_rows=(n_lat, n_ctx), k_rows=(n_lat, n_ctx),
                           into=oa)
            hf, hb = _mlstm(qk, vb, gates, e_conv_w[j], e_conv_b[j].reshape(1, -1), n_lat)
            heads, consts, kern = [oa, hf, hb, ob], [e_g_h[j].reshape(1, HALF)], _outproj_even_kernel
            w_out = e_w_out[j]
        else:
            assert last, "the odd mixer is only wired as the final layer"
            lam_init = LAM_INIT_BASE[0] - LAM_INIT_BASE[1] * math.exp(-LAM_INIT_BASE[2] * layer)
            lq1, lk1, lq2, lk2 = o_lam[j].astype(F32)
            lam = jnp.exp(jnp.sum(lq1 * lk1)) - jnp.exp(jnp.sum(lq2 * lk2)) + lam_init
            w_in = _pad_cols(o_w_in[j], _round_up(o_w_in.shape[-1], LANE)).astype(BF16)
            consts = [bd64, jnp.tile(o_g_qc[j], 2).reshape(1, LANE), jnp.tile(o_g_kc[j], 2).reshape(1, LANE),
                      o_g_cq[j].reshape(1, D_Q_LORA), _head_blocks(o_w_uq[j], D_HEADS, hd, 0, hd).astype(BF16),
                      _pad_row(o_g_qd[j], LANE), o_g_ckv[j].reshape(1, D_KV_LORA),
                      _head_blocks(o_w_ukv[j], D_HEADS, D_NOPE + D_V_DIM, 0, D_NOPE).astype(BF16),
                      _head_blocks(o_w_ukv[j], D_HEADS, D_NOPE + D_V_DIM, D_NOPE, D_V_DIM).astype(BF16),
                      _pad_row(o_g_kd[j], LANE)]
            qc, kc, vc, qd, kdl, vdl = _inproj_odd(x_all, mod_in, g_mix[layer], w_in, cos64, sin64, cos_d, sin_d,
                                                   consts, n_lat_tiles)
            extra = (jnp.full((1, LANE), lam, F32), (o_g_sub[j] * (1.0 - lam_init)).reshape(1, LANE))
            oc = _attn(qc, kc, vc, mode="diff", tq=ATTN_ROWS // 2, tk=KEY_TILE, q_rows=(0, n_lat), k_rows=(0, t), extra=extra)
            od = _attn(qd, kdl, vdl, mode="mla", tq=ATTN_ROWS, tk=KEY_TILE, q_rows=(0, n_lat), k_rows=(0, t))
            heads, consts, kern = [oc, od], [], _outproj_odd_kernel
            w_out = o_w_out[j]
        x_new, f, route = _outproj(kern, heads, consts, x_all, mod_out, g_ffn[layer], w_out.astype(BF16), wr_hi, wr_lo,
                                   b_router.astype(F32).reshape(1, N_EXPERTS), n_rows, n_lat_tiles)
        y0, y1, gate = _moe(f.reshape(bsz * n_rows, d), route, layer, w1, w3, w2)
        x_all = _combine(x_new, y0.reshape(bsz, n_rows, d), y1.reshape(bsz, n_rows, d),
                         gate.reshape(bsz, n_rows, TOP_K), g2.reshape(bsz + 1, 1, d), n_lat_tiles)
    return x_all
```

```python
import functools
import math

import jax
import jax.numpy as jnp
import numpy as np
from jax import lax
from jax.experimental import pallas as pl
from jax.experimental.pallas import tpu as pltpu

F32 = jnp.float32
BF16 = jnp.bfloat16
EPS = 1e-6
LOG2E = 1.4426950408889634
NEG_BIG = -1e30
GRID_W = 64
ROPE_THETA = 10000.0
LAM_INIT_BASE = (0.8, 0.6, 0.3)

A_HEAD_DIM, A_HEADS, A_KV_HEADS = 64, 8, 2
B_HEAD_DIM, B_HEADS, B_CONV, B_CHUNK = 128, 4, 3, 128
C_HEAD_DIM, C_V_DIM, C_HEADS = 64, 128, 4
D_HEADS, D_Q_LORA, D_KV_LORA, D_NOPE, D_ROPE, D_V_DIM = 4, 256, 128, 64, 32, 128
N_EXPERTS, N_GROUPS, TOP_K, D_EXPERT = 32, 4, 2, 512
HALF = 512
HEAD64 = 64

LANE = 128
SUBLANE = 8
ROW_TILE = 256
MOE_ROWS = 512
KEY_TILE = 1280
ATTN_ROWS = 2048
ATTN_UNROLL = 2
VMEM_LIMIT = 60 * 1024 * 1024


def _cparams(sem):
    return pltpu.CompilerParams(dimension_semantics=sem, vmem_limit_bytes=VMEM_LIMIT)


def _blk(a, i):
    return a[:, i * LANE:(i + 1) * LANE]


def _modulated(x_ref, mod_ref, g_ref):
    x = x_ref[0]
    y = x * lax.rsqrt(jnp.mean(x * x, axis=-1, keepdims=True) + EPS) * g_ref[...]
    return (y * (1.0 + mod_ref[0, 1:2, :]) + mod_ref[0, 0:1, :]).astype(BF16)


def _pairswap(b, even):
    return jnp.where(even, pltpu.roll(b, LANE - 1, 1), pltpu.roll(b, 1, 1))


def _rope(b, cos, sin, even):
    return b * cos + _pairswap(b, even) * sin


def _head64_norm(b, bd, gain):
    ss = jnp.dot((b * b).astype(BF16), bd, preferred_element_type=F32)
    return b * lax.rsqrt(ss * (1.0 / HEAD64) + EPS) * gain


def _lane_norm(b, n_real, gain):
    ss = jnp.sum(b * b, axis=1, keepdims=True)
    return b * lax.rsqrt(ss * (1.0 / n_real) + EPS) * gain


def _inproj_even_kernel(x_ref, mod_ref, g_ref, w_ref, cos_ref, sin_ref, bd_ref, gq_ref, gk_ref, bg_ref,
                        q_ref, k_ref, v_ref, qk_ref, vb_ref, ob_ref, gt_ref):
    z = jnp.dot(_modulated(x_ref, mod_ref, g_ref), w_ref[...], preferred_element_type=F32)
    lane = lax.broadcasted_iota(jnp.int32, (z.shape[0], LANE), 1)
    even = (lane & 1) == 0
    lo = lane < HEAD64
    cos, sin, bd = cos_ref[...], sin_ref[...], bd_ref[...]
    q_scale = A_HEAD_DIM ** -0.5 * LOG2E
    for c in range(4):
        qb = _rope(_head64_norm(_blk(z, c), bd, gq_ref[...]), cos, sin, even) * q_scale
        q_ref[0, :, c * LANE:(c + 1) * LANE] = qb.astype(BF16)
    kn = _rope(_head64_norm(_blk(z, 4), bd, gk_ref[...]), cos, sin, even)
    sw = pltpu.roll(kn, HEAD64, 1)
    k_ref[0] = jnp.concatenate([jnp.where(lo, kn, sw), jnp.where(lo, sw, kn)], axis=1).astype(BF16)
    va = _blk(z, 5)
    one = (lane == HEAD64).astype(F32)
    v_ref[0] = jnp.concatenate([jnp.where(lo, va, one), jnp.where(lo, pltpu.roll(va, HEAD64, 1), one)],
                               axis=1).astype(BF16)
    qk_ref[0] = z[:, 6 * LANE:14 * LANE]
    vb_ref[0] = z[:, 14 * LANE:18 * LANE].astype(BF16)
    ob_ref[0] = z[:, 18 * LANE:22 * LANE]
    gt_ref[0] = _blk(z, 22) + bg_ref[...]


def _row_specs(tm, d, bsz, n_lat_tiles):
    return [pl.BlockSpec((1, tm, d), lambda b, i: (b, i, 0)),
            pl.BlockSpec((1, 2, d), lambda b, i: (jnp.where(i >= n_lat_tiles, bsz, b), 0, 0)),
            pl.BlockSpec((1, d), lambda b, i: (0, 0))]


def _const_spec(a):
    return pl.BlockSpec(a.shape, lambda b, i: (0,) * a.ndim)


def _inproj_even(x, mod, g, w, cos, sin, bd, gq, gk, bg, n_lat_tiles):
    bsz, t, d = x.shape
    tm = ROW_TILE
    tab = pl.BlockSpec((tm, LANE), lambda b, i: (i, 0))
    consts = [bd, gq, gk, bg]
    widths = [(4 * LANE, BF16), (2 * LANE, BF16), (2 * LANE, BF16), (8 * LANE, F32), (4 * LANE, BF16),
              (4 * LANE, F32), (LANE, F32)]
    return pl.pallas_call(
        _inproj_even_kernel,
        grid=(bsz, t // tm),
        in_specs=_row_specs(tm, d, bsz, n_lat_tiles) + [_const_spec(w), tab, tab] + [_const_spec(a) for a in consts],
        out_specs=[pl.BlockSpec((1, tm, n), lambda b, i: (b, i, 0)) for n, _ in widths],
        out_shape=[jax.ShapeDtypeStruct((bsz, t, n), dt) for n, dt in widths],
        compiler_params=_cparams(("parallel", "parallel")),
        name="inproj_even",
    )(x, mod, g.reshape(1, d), w, cos, sin, *consts)


def _inproj_odd_kernel(x_ref, mod_ref, g_ref, w_ref, cos_ref, sin_ref, cosd_ref, sind_ref, bd_ref, gqc_ref, gkc_ref,
                       gcq_ref, wuq_ref, gqd_ref, gckv_ref, wkn_ref, wvd_ref, gkd_ref,
                       qc_ref, kc_ref, vc_ref, qd_ref, kd_ref, vd_ref):
    z = jnp.dot(_modulated(x_ref, mod_ref, g_ref), w_ref[...], preferred_element_type=F32)
    lane = lax.broadcasted_iota(jnp.int32, (z.shape[0], LANE), 1)
    even = (lane & 1) == 0
    cos, sin, bd = cos_ref[...], sin_ref[...], bd_ref[...]
    cosd, sind = cosd_ref[...], sind_ref[...]
    c_scale = C_HEAD_DIM ** -0.5 * LOG2E
    d_scale = (D_NOPE + D_ROPE) ** -0.5 * LOG2E
    for c in range(4):
        sl = slice(c * LANE, (c + 1) * LANE)
        qc_ref[0, :, sl] = (_rope(_head64_norm(_blk(z, c), bd, gqc_ref[...]), cos, sin, even) * c_scale).astype(BF16)
        kc_ref[0, :, sl] = _rope(_head64_norm(_blk(z, 4 + c), bd, gkc_ref[...]), cos, sin, even).astype(BF16)
    vc_ref[0] = z[:, 8 * LANE:12 * LANE].astype(BF16)
    cq = z[:, 12 * LANE:14 * LANE]
    cq = cq * lax.rsqrt(jnp.mean(cq * cq, axis=-1, keepdims=True) + EPS) * gcq_ref[...]
    qd = jnp.dot(cq.astype(BF16), wuq_ref[...], preferred_element_type=F32)
    ckv = _blk(z, 14)
    ckv = (ckv * lax.rsqrt(jnp.mean(ckv * ckv, axis=-1, keepdims=True) + EPS) * gckv_ref[...]).astype(BF16)
    kn = jnp.dot(ckv, wkn_ref[...], preferred_element_type=F32)
    vd_ref[0] = jnp.dot(ckv, wvd_ref[...], preferred_element_type=F32).astype(BF16)
    kr = pltpu.roll(_blk(z, 15), D_NOPE, 1)
    hd = D_NOPE + D_ROPE
    for c in range(4):
        sl = slice(c * LANE, (c + 1) * LANE)
        qd_ref[0, :, sl] = (_rope(_lane_norm(_blk(qd, c), hd, gqd_ref[...]), cosd, sind, even) * d_scale).astype(BF16)
        kd_ref[0, :, sl] = _rope(_lane_norm(_blk(kn, c) + kr, hd, gkd_ref[...]), cosd, sind, even).astype(BF16)


def _inproj_odd(x, mod, g, w, cos, sin, cosd, sind, consts, n_lat_tiles):
    bsz, t, d = x.shape
    tm = ROW_TILE
    tab = pl.BlockSpec((tm, LANE), lambda b, i: (i, 0))
    return pl.pallas_call(
        _inproj_odd_kernel,
        grid=(bsz, t // tm),
        in_specs=(_row_specs(tm, d, bsz, n_lat_tiles) + [_const_spec(w), tab, tab, tab, tab]
                  + [_const_spec(a) for a in consts]),
        out_specs=[pl.BlockSpec((1, tm, HALF), lambda b, i: (b, i, 0)) for _ in range(6)],
        out_shape=[jax.ShapeDtypeStruct((bsz, t, HALF), BF16) for _ in range(6)],
        compiler_params=_cparams(("parallel", "parallel")),
        name="inproj_odd",
    )(x, mod, g.reshape(1, d), w, cos, sin, cosd, sind, *consts)


def _attn_kernel(*refs, mode, tq, tk, n_tiles):
    if mode == "diff":
        q_ref, k_ref, v_ref, lam_ref, gsub_ref, o_ref, q_sc, s_sc, m_sc, a_sc, l_sc, acc_sc = refs
    else:
        q_ref, k_ref, v_ref = refs[:3]
        o_ref, q_sc, s_sc, m_sc, a_sc, l_sc, acc_sc = refs[-7:]
    rows = q_sc.shape[0]
    n_blk = tk // LANE
    lo = lax.broadcasted_iota(jnp.int32, (tq, LANE), 1) < HEAD64

    if mode == "mla":
        q_sc[...] = q_ref[0]
    else:
        for p in range(q_ref.shape[2] // LANE):
            qb = _blk(q_ref[0], p)
            q_sc[2 * p * tq:(2 * p + 1) * tq] = jnp.where(lo, qb, jnp.zeros_like(qb))
            q_sc[(2 * p + 1) * tq:(2 * p + 2) * tq] = jnp.where(lo, jnp.zeros_like(qb), qb)

    def score_stage(j, slot):
        start = pl.multiple_of(j * tk, LANE)
        s = lax.dot_general(q_sc[...], k_ref[0, pl.ds(start, tk), :], (((1,), (1,)), ((), ())),
                            preferred_element_type=F32)
        s_sc[slot] = s
        m_tile = jnp.max(functools.reduce(jnp.maximum, [_blk(s, c) for c in range(n_blk)]), axis=1, keepdims=True)
        m_prev = m_sc[1 - slot]
        m_new = jnp.maximum(m_prev, m_tile)
        m_sc[slot] = m_new
        a_sc[slot] = jnp.exp2(m_prev - m_new)

    def value_stage(j, slot):
        start = pl.multiple_of(j * tk, LANE)
        m = m_sc[slot]
        alpha = a_sc[slot]
        p_blocks = [jnp.exp2((s_sc[slot, :, c * LANE:(c + 1) * LANE] - m).astype(BF16)) for c in range(n_blk)]
        if mode != "gqa":
            l_sc[...] = alpha * l_sc[...] + functools.reduce(jnp.add, p_blocks).astype(F32)
        p = jnp.concatenate(p_blocks, axis=1)
        acc_sc[...] = alpha * acc_sc[...] + jnp.dot(p, v_ref[0, pl.ds(start, tk), :], preferred_element_type=F32)

    m_sc[1] = jnp.full((rows, LANE), NEG_BIG, F32)
    l_sc[...] = jnp.zeros((rows, LANE), F32)
    acc_sc[...] = jnp.zeros((rows, LANE), F32)
    score_stage(0, 0)
    peel = (n_tiles - 1) % ATTN_UNROLL
    for j in range(peel):
        value_stage(j, j % 2)
        score_stage(j + 1, (j + 1) % 2)

    def body(t, carry):
        for u in range(ATTN_UNROLL):
            j = ATTN_UNROLL * t + peel + u
            value_stage(j, (peel + u) % 2)
            score_stage(j + 1, (peel + u + 1) % 2)
        return carry

    lax.fori_loop(0, (n_tiles - 1) // ATTN_UNROLL, body, 0)
    value_stage(n_tiles - 1, (n_tiles - 1) % 2)

    acc = acc_sc[...]
    o = acc / (acc[:, HEAD64:HEAD64 + 1] if mode == "gqa" else jnp.sum(l_sc[...], axis=1, keepdims=True))
    head = lambda h: o[h * tq:(h + 1) * tq]
    if mode == "gqa":
        for p in range(o_ref.shape[2] // LANE):
            o_ref[0, :, p * LANE:(p + 1) * LANE] = jnp.where(
                lo, head(2 * p), pltpu.roll(head(2 * p + 1), HEAD64, 1)).astype(o_ref.dtype)
    elif mode == "diff":
        dlt = head(0) - lam_ref[...] * head(1)
        o_ref[0] = (dlt * lax.rsqrt(jnp.mean(dlt * dlt, axis=1, keepdims=True) + EPS) * gsub_ref[...]).astype(o_ref.dtype)
    else:
        o_ref[0] = o.astype(o_ref.dtype)


def _attn(q, k, v, *, mode, tq, tk, q_rows, k_rows, extra=(), into=None):
    bsz, t_q, wq = q.shape
    qw = 2 * LANE if mode == "gqa" else LANE
    q0, nq = q_rows
    k0, nk = k_rows
    tq, tk = min(tq, nq), min(tk, nk)
    rows = {"gqa": 4, "diff": 2, "mla": 1}[mode] * tq
    assert q0 % tq == 0 and nq % tq == 0 and k0 % nk == 0 and nk % tk == 0
    qmap = lambda b, g, i: (b, q0 // tq + i, g)
    kmap = lambda b, g, i: (b, k0 // nk, g)
    in_specs = [pl.BlockSpec((1, tq, qw), qmap), pl.BlockSpec((1, nk, LANE), kmap), pl.BlockSpec((1, nk, LANE), kmap)]
    in_specs += [pl.BlockSpec(a.shape, lambda b, g, i: (0, 0)) for a in extra]
    args = [q, k, v, *extra]
    aliases = {}
    if into is not None:
        in_specs.append(pl.BlockSpec(memory_space=pl.ANY))
        aliases = {len(args): 0}
        args.append(into)

    def kern(*refs):
        if into is not None:
            refs = refs[:len(args) - 1] + refs[len(args):]
        _attn_kernel(*refs, mode=mode, tq=tq, tk=tk, n_tiles=nk // tk)

    return pl.pallas_call(
        kern,
        grid=(bsz, wq // qw, nq // tq),
        in_specs=in_specs,
        out_specs=pl.BlockSpec((1, tq, qw), qmap),
        out_shape=jax.ShapeDtypeStruct((bsz, t_q, wq), BF16),
        scratch_shapes=[pltpu.VMEM((rows, LANE), BF16), pltpu.VMEM((2, rows, tk), F32),
                        pltpu.VMEM((2, rows, LANE), F32), pltpu.VMEM((2, rows, LANE), F32),
                        pltpu.VMEM((rows, LANE), F32), pltpu.VMEM((rows, LANE), F32)],
        input_output_aliases=aliases,
        compiler_params=_cparams(("parallel", "parallel", "parallel")),
        name="attn_" + mode,
    )(*args)


def _mlstm_kernel(qf_ref, pf_ref, nf_ref, vf_ref, gf_ref, qb_ref, pb_ref, nb_ref, vb_ref, gb_ref, cw_ref, cb_ref,
                  hf_ref, hb_ref, c_sc, m_sc, *, n_lat_chunks, n_chunks):
    L, dh = B_CHUNK, B_HEAD_DIM
    c = pl.program_id(1)
    n_ctx_chunks = n_chunks - n_lat_chunks
    fwd_chunk = jnp.where(c < n_ctx_chunks, n_lat_chunks + c, c - n_ctx_chunks)
    bwd_chunk = n_chunks - 1 - c

    @pl.when(c == 0)
    def _():
        c_sc[...] = jnp.zeros(c_sc.shape, F32)
        m_sc[...] = jnp.zeros(m_sc.shape, F32)

    row = lax.broadcasted_iota(jnp.int32, (L, L), 0)
    col = lax.broadcasted_iota(jnp.int32, (L, L), 1)
    upper = (row <= col).astype(F32)
    lower = (col <= row).astype(F32)
    rid = lax.broadcasted_iota(jnp.int32, (L, 1), 0)
    glane = lax.broadcasted_iota(jnp.int32, (L, LANE), 1)
    is_f = (glane // B_HEADS) % 2 == 1
    ones_col = (lax.broadcasted_iota(jnp.int32, (L, dh), 1) == 0).astype(BF16)
    hi = lax.Precision.HIGHEST

    def direction(qk_ref, p_ref, n_ref, v_ref, g_ref, h_ref, chunk, reverse):
        first = (chunk == 0) | (chunk == n_lat_chunks)
        last = (chunk == n_lat_chunks - 1) | (chunk == n_chunks - 1)
        x = qk_ref[0]
        prev_row = jnp.where(first, 0.0, p_ref[0, SUBLANE - 1:SUBLANE, :])
        next_row = jnp.where(last, 0.0, n_ref[0, 0:1, :])
        x_prev = jnp.where(rid == 0, prev_row, pltpu.roll(x, 1, 0))
        x_next = jnp.where(rid == L - 1, next_row, pltpu.roll(x, L - 1, 0))
        y = x_prev * cw_ref[0:1, :] + x * cw_ref[1:2, :] + x_next * cw_ref[2:3, :] + cb_ref[...]
        y = y * jax.nn.sigmoid(y)
        g = g_ref[0]
        g = jnp.where(is_f, jnp.minimum(g, 0.0) - jnp.log(1.0 + jnp.exp(-jnp.abs(g))), g)
        g_t = jnp.transpose(g)
        cum_r = jnp.dot(g_t, upper, preferred_element_type=F32, precision=hi)
        cum_c = jnp.dot(lower, g, preferred_element_type=F32, precision=hi)
        if reverse:
            cum_r = cum_r[:, L - 1:L] - cum_r + g_t
            cum_c = cum_c[L - 1:L, :] - cum_c + g
        gi = 2 * B_HEADS if reverse else 0
        mask = (col >= row) if reverse else (col <= row)
        for h in range(B_HEADS):
            idx = (B_HEADS if reverse else 0) + h
            q = _blk(y, h).astype(BF16)
            k = _blk(y, B_HEADS + h) * (dh ** -0.5)
            v_ext = jnp.concatenate([_blk(v_ref[0], h), ones_col], axis=1)
            i_r = g_t[gi + h:gi + h + 1, :]
            b_r = cum_r[gi + B_HEADS + h:gi + B_HEADS + h + 1, :]
            i_c = g[:, gi + h:gi + h + 1]
            b_c = cum_c[:, gi + B_HEADS + h:gi + B_HEADS + h + 1]
            m_prev = m_sc[idx][:, 0:1]
            c_ext = c_sc[idx]

            log_w = jnp.where(mask, b_c - b_r + i_r, -jnp.inf)
            log_inter = b_c + m_prev
            m_row = jnp.maximum(log_inter, jnp.max(log_w, axis=1, keepdims=True))
            w_intra = jnp.exp(log_w - m_row)
            a_inter = jnp.exp(log_inter - m_row)
            sc = lax.dot_general(q, k.astype(BF16), (((1,), (1,)), ((), ())), preferred_element_type=F32) * w_intra
            q_c = jnp.dot(q, c_ext.astype(BF16), preferred_element_type=F32)
            s_v = jnp.dot(sc.astype(BF16), v_ext, preferred_element_type=F32)
            num = a_inter * q_c[:, :dh] + s_v[:, :dh]
            den = a_inter * q_c[:, dh:dh + 1] + s_v[:, dh:dh + 1]
            h_ref[0, :, h * dh:(h + 1) * dh] = num / jnp.maximum(jnp.abs(den), jnp.exp(-m_row))

            b_last = b_r[:, 0:1] if reverse else b_r[:, L - 1:L]
            m_new = jnp.maximum(b_last + m_prev, jnp.max(b_last - b_r + i_r, axis=1, keepdims=True))
            g_c = jnp.exp(b_last - b_c + i_c - m_new)
            decay = jnp.exp(b_last + m_prev - m_new)
            upd = jnp.dot(jnp.transpose(k).astype(BF16), (g_c * v_ext.astype(F32)).astype(BF16),
                          preferred_element_type=F32)
            c_sc[idx] = decay * c_ext + upd
            m_sc[idx] = jnp.broadcast_to(m_new, (1, LANE))

    direction(qf_ref, pf_ref, nf_ref, vf_ref, gf_ref, hf_ref, fwd_chunk, False)
    direction(qb_ref, pb_ref, nb_ref, vb_ref, gb_ref, hb_ref, bwd_chunk, True)


def _mlstm(qk, vb, gates, conv_w, conv_b, n_lat):
    bsz, t, _ = qk.shape
    L = B_CHUNK
    n_chunks, n_lat_chunks = t // L, n_lat // L
    n_ctx_chunks = n_chunks - n_lat_chunks
    per = L // SUBLANE
    fwd = lambda c: jnp.where(c < n_ctx_chunks, n_lat_chunks + c, c - n_ctx_chunks)
    bwd = lambda c: n_chunks - 1 - c

    def specs(order):
        return [
            pl.BlockSpec((1, L, 2 * HALF), lambda b, c: (b, order(c), 0)),
            pl.BlockSpec((1, SUBLANE, 2 * HALF), lambda b, c: (b, jnp.maximum(order(c) * per - 1, 0), 0)),
            pl.BlockSpec((1, SUBLANE, 2 * HALF), lambda b, c: (b, jnp.minimum((order(c) + 1) * per, t // SUBLANE - 1), 0)),
            pl.BlockSpec((1, L, HALF), lambda b, c: (b, order(c), 0)),
            pl.BlockSpec((1, L, LANE), lambda b, c: (b, order(c), 0)),
        ]

    const = lambda a: pl.BlockSpec(a.shape, lambda b, c: (0, 0))
    return pl.pallas_call(
        functools.partial(_mlstm_kernel, n_lat_chunks=n_lat_chunks, n_chunks=n_chunks),
        grid=(bsz, n_chunks),
        in_specs=specs(fwd) + specs(bwd) + [const(conv_w), const(conv_b)],
        out_specs=[pl.BlockSpec((1, L, HALF), lambda b, c: (b, fwd(c), 0)),
                   pl.BlockSpec((1, L, HALF), lambda b, c: (b, bwd(c), 0))],
        out_shape=[jax.ShapeDtypeStruct((bsz, t, HALF), F32)] * 2,
        scratch_shapes=[pltpu.VMEM((2 * B_HEADS, B_HEAD_DIM, 2 * B_HEAD_DIM), F32),
                        pltpu.VMEM((2 * B_HEADS, 1, LANE), F32)],
        compiler_params=_cparams(("parallel", "arbitrary")),
        name="mlstm",
    )(qk, qk, qk, vb, gates, qk, qk, qk, vb, gates, conv_w, conv_b)


def _route(logits, bias):
    per = N_EXPERTS // N_GROUPS
    lane = lax.broadcasted_iota(jnp.int32, logits.shape, 1)
    lane_f = lane.astype(F32)
    scores = jax.nn.sigmoid(logits)
    sel = scores + bias

    def first_max(a):
        m = jnp.max(a, axis=1, keepdims=True)
        return m, jnp.min(jnp.where(a == m, lane_f, float(N_EXPERTS)), axis=1, keepdims=True)

    best, g_idx = None, None
    for g in range(N_GROUPS):
        a = jnp.where(lane // per == g, sel, -jnp.inf)
        m1, i1 = first_max(a)
        top2 = m1 + jnp.max(jnp.where(lane_f == i1, -jnp.inf, a), axis=1, keepdims=True)
        if g == 0:
            best, g_idx = top2, jnp.zeros_like(top2)
        else:
            better = top2 > best
            best, g_idx = jnp.where(better, top2, best), jnp.where(better, float(g), g_idx)
    a = jnp.where((lane // per).astype(F32) == g_idx, sel, -jnp.inf)
    _, e1 = first_max(a)
    _, e2 = first_max(jnp.where(lane_f == e1, -jnp.inf, a))
    s1 = jnp.sum(jnp.where(lane_f == e1, scores, 0.0), axis=1, keepdims=True)
    s2 = jnp.sum(jnp.where(lane_f == e2, scores, 0.0), axis=1, keepdims=True)
    return e1, e2, s1 / (s1 + s2), s2 / (s1 + s2)


def _out_tail(mix, x_ref, mod_ref, g_ref, w_ref, wrh_ref, wrl_ref, br_ref, xo_ref, f_ref, rt_ref):
    o = jnp.dot(mix, w_ref[...], preferred_element_type=F32)
    xn = x_ref[0] + mod_ref[0, 0:1, :] * o
    xo_ref[0] = xn
    y = xn * lax.rsqrt(jnp.mean(xn * xn, axis=-1, keepdims=True) + EPS) * g_ref[...]
    f = y * (1.0 + mod_ref[0, 2:3, :]) + mod_ref[0, 1:2, :]
    f_hi = f.astype(BF16)
    f_ref[0] = f_hi
    f_lo = (f - f_hi.astype(F32)).astype(BF16)
    logits = (jnp.dot(f_hi, wrh_ref[...], preferred_element_type=F32)
              + jnp.dot(f_lo, wrh_ref[...], preferred_element_type=F32)
              + jnp.dot(f_hi, wrl_ref[...], preferred_element_type=F32))
    cols = _route(logits, br_ref[...])
    out_lane = lax.broadcasted_iota(jnp.int32, (logits.shape[0], LANE), 1)
    rec = functools.reduce(lambda acc, ic: jnp.where(out_lane == ic[0], ic[1], acc), enumerate(cols),
                           jnp.zeros((logits.shape[0], LANE), F32))
    rt_ref[0] = jnp.transpose(rec)[:SUBLANE]


def _outproj_even_kernel(oa_ref, hf_ref, hb_ref, ob_ref, gh_ref, *tail):
    parts = [oa_ref[0]]
    for h in range(B_HEADS):
        hs = _blk(hf_ref[0], h) + _blk(hb_ref[0], h)
        hn = hs * lax.rsqrt(jnp.mean(hs * hs, axis=1, keepdims=True) + EPS) * _blk(gh_ref[...], h)
        parts.append((hn * jax.nn.sigmoid(_blk(ob_ref[0], h))).astype(BF16))
    _out_tail(jnp.concatenate(parts, axis=1), *tail)


def _outproj_odd_kernel(oc_ref, od_ref, *tail):
    _out_tail(jnp.concatenate([oc_ref[0], od_ref[0]], axis=1), *tail)


def _outproj(kern, heads, consts, x, mod, g, w, wr_hi, wr_lo, b_router, n_rows, n_lat_tiles):
    bsz, _, d = x.shape
    tm = ROW_TILE
    row = lambda b, i: (b, i, 0)
    return pl.pallas_call(
        kern,
        grid=(bsz, n_rows // tm),
        in_specs=([pl.BlockSpec((1, tm, HALF), row) for _ in heads] + [_const_spec(a) for a in consts]
                  + [pl.BlockSpec((1, tm, d), row),
                     pl.BlockSpec((1, 3, d), lambda b, i: (jnp.where(i >= n_lat_tiles, bsz, b), 0, 0)),
                     pl.BlockSpec((1, d), lambda b, i: (0, 0)), _const_spec(w), _const_spec(wr_hi),
                     _const_spec(wr_lo), _const_spec(b_router)]),
        out_specs=[pl.BlockSpec((1, tm, d), row), pl.BlockSpec((1, tm, d), row),
                   pl.BlockSpec((1, SUBLANE, tm), lambda b, i: (b, 0, i))],
        out_shape=[jax.ShapeDtypeStruct((bsz, n_rows, d), F32), jax.ShapeDtypeStruct((bsz, n_rows, d), BF16),
                   jax.ShapeDtypeStruct((bsz, SUBLANE, n_rows), F32)],
        compiler_params=_cparams(("parallel", "parallel")),
        name="outproj",
    )(*heads, *consts, x, mod, g.reshape(1, d), w, wr_hi, wr_lo, b_router)


def _expert_kernel(blk_e_ref, n_used_ref, x_ref, w1_ref, w3_ref, w2_ref, y_ref, w1_sc, w3_sc, w2_sc):
    i = pl.program_id(0)

    @pl.when((i == 0) | (blk_e_ref[i] != blk_e_ref[jnp.maximum(i - 1, 0)]))
    def _():
        w1_sc[...] = w1_ref[0, 0].astype(BF16)
        w3_sc[...] = w3_ref[0, 0].astype(BF16)
        w2_sc[...] = w2_ref[0, 0].astype(BF16)

    @pl.when(i < n_used_ref[0])
    def _():
        xb = x_ref[...]
        h1 = jnp.dot(xb, w1_sc[...], preferred_element_type=F32)
        h3 = jnp.dot(xb, w3_sc[...], preferred_element_type=F32)
        a = (h1 * jax.nn.sigmoid(h1) * h3).astype(BF16)
        y_ref[...] = jnp.dot(a, w2_sc[...], preferred_element_type=F32).astype(y_ref.dtype)

    @pl.when(i >= n_used_ref[0])
    def _():
        y_ref[...] = jnp.zeros(y_ref.shape, y_ref.dtype)


def _experts(xin, blk_e, n_used, layer, w1, w3, w2):
    n_rows, d = xin.shape
    de = w1.shape[-1]
    nb = n_rows // MOE_ROWS
    grid_spec = pltpu.PrefetchScalarGridSpec(
        num_scalar_prefetch=2,
        grid=(nb,),
        in_specs=[
            pl.BlockSpec((MOE_ROWS, d), lambda i, be, nu: (i, 0)),
            pl.BlockSpec((1, 1, d, de), lambda i, be, nu: (layer, be[i], 0, 0)),
            pl.BlockSpec((1, 1, d, de), lambda i, be, nu: (layer, be[i], 0, 0)),
            pl.BlockSpec((1, 1, de, d), lambda i, be, nu: (layer, be[i], 0, 0)),
        ],
        out_specs=pl.BlockSpec((MOE_ROWS, d), lambda i, be, nu: (i, 0)),
        scratch_shapes=[pltpu.VMEM((d, de), BF16), pltpu.VMEM((d, de), BF16), pltpu.VMEM((de, d), BF16)],
    )
    return pl.pallas_call(
        _expert_kernel,
        grid_spec=grid_spec,
        out_shape=jax.ShapeDtypeStruct((n_rows, d), BF16),
        compiler_params=_cparams(("arbitrary",)),
        name="experts",
    )(blk_e, n_used, xin, w1, w3, w2)


def _moe(f_tok, route, layer, w1, w3, w2):
    n_tok, d = f_tok.shape
    e1, e2 = route[:, 0].reshape(-1).astype(jnp.int32), route[:, 1].reshape(-1).astype(jnp.int32)
    gate = jnp.stack([route[:, 2].reshape(-1), route[:, 3].reshape(-1)], axis=-1)

    n_assign = n_tok * TOP_K
    e_flat = jnp.concatenate([e1, e2]).astype(jnp.int32)
    assert n_assign % MOE_ROWS == 0
    onehot = (e_flat[:, None] == jnp.arange(N_EXPERTS, dtype=jnp.int32)[None, :]).astype(F32)
    tri = jnp.tril(jnp.ones((MOE_ROWS, MOE_ROWS), F32))
    within = jnp.einsum('ij,bjk->bik', tri, onehot.reshape(-1, MOE_ROWS, N_EXPERTS), precision=lax.Precision.HIGHEST)
    totals = within[:, -1, :]
    before = jnp.cumsum(totals, axis=0) - totals
    csum = (within + before[:, None, :]).reshape(n_assign, N_EXPERTS)
    counts = jnp.sum(totals, axis=0).astype(jnp.int32)
    rank = jnp.sum(csum * onehot, axis=1).astype(jnp.int32) - 1
    padded = (counts + MOE_ROWS - 1) // MOE_ROWS * MOE_ROWS
    pend = jnp.cumsum(padded)
    pstart = pend - padded
    dest = pstart[e_flat] + rank
    n_rows = -(-n_assign // MOE_ROWS) * MOE_ROWS + N_EXPERTS * MOE_ROWS
    nb = n_rows // MOE_ROWS
    blk_row = jnp.arange(nb, dtype=jnp.int32) * MOE_ROWS
    blk_e = jnp.minimum(jnp.sum((pend[None, :] <= blk_row[:, None]).astype(jnp.int32), axis=1), N_EXPERTS - 1)
    order = jnp.argsort(e_flat, stable=True).astype(jnp.int32)
    row = jnp.arange(n_rows, dtype=jnp.int32)
    row_e = jnp.repeat(blk_e, MOE_ROWS)
    pos = row - pstart[row_e]
    src = jnp.minimum((jnp.cumsum(counts) - counts)[row_e] + pos, n_assign - 1)
    row_tok = jnp.where(pos < counts[row_e], order[src], row) % n_tok
    xin = f_tok[row_tok]
    n_used = (pend[-1:] // MOE_ROWS).astype(jnp.int32)
    y = _experts(xin, blk_e, n_used, layer, w1, w3, w2)
    return y[dest[:n_tok]], y[dest[n_tok:]], gate


def _combine_kernel(x_ref, y0_ref, y1_ref, gate_ref, g2_ref, o_ref):
    gate = gate_ref[0]
    y = gate[:, 0:1] * y0_ref[0].astype(F32) + gate[:, 1:2] * y1_ref[0].astype(F32)
    o_ref[0] = x_ref[0] + g2_ref[0] * y


def _combine(x, y0, y1, gate, g2, n_lat_tiles):
    bsz, n_rows, d = x.shape
    tm = ROW_TILE
    row = lambda b, i: (b, i, 0)
    return pl.pallas_call(
        _combine_kernel,
        grid=(bsz, n_rows // tm),
        in_specs=[pl.BlockSpec((1, tm, d), row), pl.BlockSpec((1, tm, d), row), pl.BlockSpec((1, tm, d), row),
                  pl.BlockSpec((1, tm, TOP_K), row),
                  pl.BlockSpec((1, 1, d), lambda b, i: (jnp.where(i >= n_lat_tiles, bsz, b), 0, 0))],
        out_specs=pl.BlockSpec((1, tm, d), row),
        out_shape=jax.ShapeDtypeStruct((bsz, n_rows, d), F32),
        compiler_params=_cparams(("parallel", "parallel")),
        name="combine",
    )(x, y0, y1, gate, g2)


def _rope_lanes(n_lat, n_ctx, rot_dim):
    rows = n_lat // GRID_W
    r = jnp.repeat(jnp.arange(rows), GRID_W)
    c = jnp.tile(jnp.arange(GRID_W), rows)
    n_freq = rot_dim // 4
    inv = ROPE_THETA ** (-jnp.arange(n_freq, dtype=F32) / n_freq)
    ang = jnp.concatenate([r[:, None] * inv, c[:, None] * inv], axis=-1)
    cos = jnp.repeat(jnp.cos(ang), 2, axis=1)
    sin = jnp.repeat(jnp.sin(ang), 2, axis=1) * jnp.tile(jnp.array([-1.0, 1.0], F32), rot_dim // 2)
    pad = ((0, n_ctx), (0, 0))
    return jnp.pad(cos, pad, constant_values=1.0), jnp.pad(sin, pad)


def _pad_cols(w, n):
    return jnp.pad(w, ((0, 0), (0, n - w.shape[1])))


def _pad_row(v, n):
    return jnp.pad(v, (0, n - v.shape[0])).reshape(1, n)


def _round_up(n, m):
    return -(-n // m) * m


def _head_blocks(w, n_heads, width, start, size):
    w = w.reshape(w.shape[0], n_heads, width)[:, :, start:start + size]
    return jnp.pad(w, ((0, 0), (0, 0), (0, LANE - size))).reshape(w.shape[0], n_heads * LANE)


def kernel(x, c, ctx, c_ctx, w_ada, b_ada, g_mix, g_ffn, e_w_in, e_w_out, e_g_q, e_g_k, e_conv_w, e_conv_b, e_b_gates, e_g_h, o_w_in, o_w_out, o_g_qc, o_g_kc, o_lam, o_g_sub, o_g_cq, o_w_uq, o_g_ckv, o_w_ukv, o_g_qd, o_g_kd, w_router, b_router, w1, w3, w2):
    bsz, n_lat, d = x.shape
    n_ctx = ctx.shape[1]
    t = n_lat + n_ctx
    depth = w_ada.shape[0]
    assert n_lat % ROW_TILE == 0 and n_ctx % ROW_TILE == 0 and n_ctx % B_CHUNK == 0
    n_lat_tiles = n_lat // ROW_TILE
    hd = D_NOPE + D_ROPE

    cos64, sin64 = _rope_lanes(n_lat, n_ctx, A_HEAD_DIM)
    cos64, sin64 = jnp.tile(cos64, (1, 2)), jnp.tile(sin64, (1, 2))
    cos_d, sin_d = _rope_lanes(n_lat, n_ctx, D_ROPE)
    cos_d = jnp.pad(cos_d, ((0, 0), (D_NOPE, LANE - hd)), constant_values=1.0)
    sin_d = jnp.pad(sin_d, ((0, 0), (D_NOPE, LANE - hd)))
    half = jnp.arange(LANE) // HEAD64
    bd64 = (half[:, None] == half[None, :]).astype(BF16)
    wr_hi = w_router.astype(BF16)
    wr_lo = (w_router.astype(F32) - wr_hi.astype(F32)).astype(BF16)
    cond = jnp.concatenate([c, c_ctx[None, :]], axis=0)
    x_all = jnp.concatenate([x, ctx], axis=1)

    for layer in range(depth):
        last = layer == depth - 1
        j = layer // 2
        mod = (jax.nn.silu(cond) @ w_ada[layer] + b_ada[layer]).reshape(bsz + 1, 6, d)
        sh1, sc1, g1, sh2, sc2, g2 = [mod[:, i] for i in range(6)]
        mod_in = jnp.stack([sh1, sc1], axis=1)
        mod_out = jnp.stack([g1, sh2, sc2], axis=1)
        n_rows = n_lat if last else t
        if layer % 2 == 0:
            w_in = _pad_cols(e_w_in[j], _round_up(e_w_in.shape[-1], LANE)).astype(BF16)
            q, kd, vd, qk, vb, ob, gates = _inproj_even(
                x_all, mod_in, g_mix[layer], w_in, cos64, sin64, bd64, jnp.tile(e_g_q[j], 2).reshape(1, LANE),
                jnp.tile(e_g_k[j], 2).reshape(1, LANE), _pad_row(e_b_gates[j], LANE), n_lat_tiles)
            oa = _attn(q, kd, vd, mode="gqa", tq=ATTN_ROWS // 4, tk=KEY_TILE, q_rows=(0, n_lat), k_rows=(0, t))
            if not last:
                oa = _attn(q, kd, vd, mode="gqa", tq=ATTN_ROWS // 4, tk=KEY_TILE, q_rows=(n_lat, n_ctx), k_rows=(n_lat, n_ctx),
                           into=oa)
            hf, hb = _mlstm(qk, vb, gates, e_conv_w[j], e_conv_b[j].reshape(1, -1), n_lat)
            heads, consts, kern = [oa, hf, hb, ob], [e_g_h[j].reshape(1, HALF)], _outproj_even_kernel
            w_out = e_w_out[j]
        else:
            assert last, "the odd mixer is only wired as the final layer"
            lam_init = LAM_INIT_BASE[0] - LAM_INIT_BASE[1] * math.exp(-LAM_INIT_BASE[2] * layer)
            lq1, lk1, lq2, lk2 = o_lam[j].astype(F32)
            lam = jnp.exp(jnp.sum(lq1 * lk1)) - jnp.exp(jnp.sum(lq2 * lk2)) + lam_init
            w_in = _pad_cols(o_w_in[j], _round_up(o_w_in.shape[-1], LANE)).astype(BF16)
            consts = [bd64, jnp.tile(o_g_qc[j], 2).reshape(1, LANE), jnp.tile(o_g_kc[j], 2).reshape(1, LANE),
                      o_g_cq[j].reshape(1, D_Q_LORA), _head_blocks(o_w_uq[j], D_HEADS, hd, 0, hd).astype(BF16),
                      _pad_row(o_g_qd[j], LANE), o_g_ckv[j].reshape(1, D_KV_LORA),
                      _head_blocks(o_w_ukv[j], D_HEADS, D_NOPE + D_V_DIM, 0, D_NOPE).astype(BF16),
                      _head_blocks(o_w_ukv[j], D_HEADS, D_NOPE + D_V_DIM, D_NOPE, D_V_DIM).astype(BF16),
                      _pad_row(o_g_kd[j], LANE)]
            qc, kc, vc, qd, kdl, vdl = _inproj_odd(x_all, mod_in, g_mix[layer], w_in, cos64, sin64, cos_d, sin_d,
                                                   consts, n_lat_tiles)
            extra = (jnp.full((1, LANE), lam, F32), (o_g_sub[j] * (1.0 - lam_init)).reshape(1, LANE))
            oc = _attn(qc, kc, vc, mode="diff", tq=ATTN_ROWS // 2, tk=KEY_TILE, q_rows=(0, n_lat), k_rows=(0, t), extra=extra)
            od = _attn(qd, kdl, vdl, mode="mla", tq=ATTN_ROWS, tk=KEY_TILE, q_rows=(0, n_lat), k_rows=(0, t))
            heads, consts, kern = [oc, od], [], _outproj_odd_kernel
            w_out = o_w_out[j]
        x_new, f, route = _outproj(kern, heads, consts, x_all, mod_out, g_ffn[layer], w_out.astype(BF16), wr_hi, wr_lo,
                                   b_router.astype(F32).reshape(1, N_EXPERTS), n_rows, n_lat_tiles)
        y0, y1, gate = _moe(f.reshape(bsz * n_rows, d), route, layer, w1, w3, w2)
        x_all = _combine(x_new, y0.reshape(bsz, n_rows, d), y1.reshape(bsz, n_rows, d),
                         gate.reshape(bsz, n_rows, TOP_K), g2.reshape(bsz + 1, 1, d), n_lat_tiles)
    return x_all
```

```python
import functools
import math

import jax
import jax.numpy as jnp
import numpy as np
from jax import lax
from jax.experimental import pallas as pl
from jax.experimental.pallas import tpu as pltpu

F32 = jnp.float32
BF16 = jnp.bfloat16
EPS = 1e-6
LOG2E = 1.4426950408889634
NEG_BIG = -1e30
GRID_W = 64
ROPE_THETA = 10000.0
LAM_INIT_BASE = (0.8, 0.6, 0.3)

A_HEAD_DIM, A_HEADS, A_KV_HEADS = 64, 8, 2
B_HEAD_DIM, B_HEADS, B_CONV, B_CHUNK = 128, 4, 3, 128
C_HEAD_DIM, C_V_DIM, C_HEADS = 64, 128, 4
D_HEADS, D_Q_LORA, D_KV_LORA, D_NOPE, D_ROPE, D_V_DIM = 4, 256, 128, 64, 32, 128
N_EXPERTS, N_GROUPS, TOP_K, D_EXPERT = 32, 4, 2, 512
HALF = 512
HEAD64 = 64

LANE = 128
SUBLANE = 8
ROW_TILE = 256
MOE_ROWS = 512
KEY_TILE = 1280
ATTN_ROWS = 2048
ATTN_UNROLL = 2
VMEM_LIMIT = 60 * 1024 * 1024


def _cparams(sem):
    return pltpu.CompilerParams(dimension_semantics=sem, vmem_limit_bytes=VMEM_LIMIT)


def _blk(a, i):
    return a[:, i * LANE:(i + 1) * LANE]


def _modulated(x_ref, mod_ref, g_ref):
    x = x_ref[0]
    y = x * lax.rsqrt(jnp.mean(x * x, axis=-1, keepdims=True) + EPS) * g_ref[...]
    return (y * (1.0 + mod_ref[0, 1:2, :]) + mod_ref[0, 0:1, :]).astype(BF16)


def _pairswap(b, even):
    return jnp.where(even, pltpu.roll(b, LANE - 1, 1), pltpu.roll(b, 1, 1))


def _rope(b, cos, sin, even):
    return b * cos + _pairswap(b, even) * sin


def _head64_norm(b, bd, gain):
    ss = jnp.dot((b * b).astype(BF16), bd, preferred_element_type=F32)
    return b * lax.rsqrt(ss * (1.0 / HEAD64) + EPS) * gain


def _lane_norm(b, n_real, gain):
    ss = jnp.sum(b * b, axis=1, keepdims=True)
    return b * lax.rsqrt(ss * (1.0 / n_real) + EPS) * gain


def _inproj_even_kernel(x_ref, mod_ref, g_ref, w_ref, cos_ref, sin_ref, bd_ref, gq_ref, gk_ref, bg_ref,
                        q_ref, k_ref, v_ref, qk_ref, vb_ref, ob_ref, gt_ref):
    z = jnp.dot(_modulated(x_ref, mod_ref, g_ref), w_ref[...], preferred_element_type=F32)
    lane = lax.broadcasted_iota(jnp.int32, (z.shape[0], LANE), 1)
    even = (lane & 1) == 0
    lo = lane < HEAD64
    cos, sin, bd = cos_ref[...], sin_ref[...], bd_ref[...]
    q_scale = A_HEAD_DIM ** -0.5 * LOG2E
    for c in range(4):
        qb = _rope(_head64_norm(_blk(z, c), bd, gq_ref[...]), cos, sin, even) * q_scale
        q_ref[0, :, c * LANE:(c + 1) * LANE] = qb.astype(BF16)
    kn = _rope(_head64_norm(_blk(z, 4), bd, gk_ref[...]), cos, sin, even)
    sw = pltpu.roll(kn, HEAD64, 1)
    k_ref[0] = jnp.concatenate([jnp.where(lo, kn, sw), jnp.where(lo, sw, kn)], axis=1).astype(BF16)
    va = _blk(z, 5)
    one = (lane == HEAD64).astype(F32)
    v_ref[0] = jnp.concatenate([jnp.where(lo, va, one), jnp.where(lo, pltpu.roll(va, HEAD64, 1), one)],
                               axis=1).astype(BF16)
    qk_ref[0] = z[:, 6 * LANE:14 * LANE]
    vb_ref[0] = z[:, 14 * LANE:18 * LANE].astype(BF16)
    ob_ref[0] = z[:, 18 * LANE:22 * LANE]
    gt_ref[0] = _blk(z, 22) + bg_ref[...]


def _row_specs(tm, d, bsz, n_lat_tiles):
    return [pl.BlockSpec((1, tm, d), lambda b, i: (b, i, 0)),
            pl.BlockSpec((1, 2, d), lambda b, i: (jnp.where(i >= n_lat_tiles, bsz, b), 0, 0)),
            pl.BlockSpec((1, d), lambda b, i: (0, 0))]


def _const_spec(a):
    return pl.BlockSpec(a.shape, lambda b, i: (0,) * a.ndim)


def _inproj_even(x, mod, g, w, cos, sin, bd, gq, gk, bg, n_lat_tiles):
    bsz, t, d = x.shape
    tm = ROW_TILE
    tab = pl.BlockSpec((tm, LANE), lambda b, i: (i, 0))
    consts = [bd, gq, gk, bg]
    widths = [(4 * LANE, BF16), (2 * LANE, BF16), (2 * LANE, BF16), (8 * LANE, F32), (4 * LANE, BF16),
              (4 * LANE, F32), (LANE, F32)]
    return pl.pallas_call(
        _inproj_even_kernel,
        grid=(bsz, t // tm),
        in_specs=_row_specs(tm, d, bsz, n_lat_tiles) + [_const_spec(w), tab, tab] + [_const_spec(a) for a in consts],
        out_specs=[pl.BlockSpec((1, tm, n), lambda b, i: (b, i, 0)) for n, _ in widths],
        out_shape=[jax.ShapeDtypeStruct((bsz, t, n), dt) for n, dt in widths],
        compiler_params=_cparams(("parallel", "parallel")),
        name="inproj_even",
    )(x, mod, g.reshape(1, d), w, cos, sin, *consts)


def _inproj_odd_kernel(x_ref, mod_ref, g_ref, w_ref, cos_ref, sin_ref, cosd_ref, sind_ref, bd_ref, gqc_ref, gkc_ref,
                       gcq_ref, wuq_ref, gqd_ref, gckv_ref, wkn_ref, wvd_ref, gkd_ref,
                       qc_ref, kc_ref, vc_ref, qd_ref, kd_ref, vd_ref):
    z = jnp.dot(_modulated(x_ref, mod_ref, g_ref), w_ref[...], preferred_element_type=F32)
    lane = lax.broadcasted_iota(jnp.int32, (z.shape[0], LANE), 1)
    even = (lane & 1) == 0
    cos, sin, bd = cos_ref[...], sin_ref[...], bd_ref[...]
    cosd, sind = cosd_ref[...], sind_ref[...]
    c_scale = C_HEAD_DIM ** -0.5 * LOG2E
    d_scale = (D_NOPE + D_ROPE) ** -0.5 * LOG2E
    for c in range(4):
        sl = slice(c * LANE, (c + 1) * LANE)
        qc_ref[0, :, sl] = (_rope(_head64_norm(_blk(z, c), bd, gqc_ref[...]), cos, sin, even) * c_scale).astype(BF16)
        kc_ref[0, :, sl] = _rope(_head64_norm(_blk(z, 4 + c), bd, gkc_ref[...]), cos, sin, even).astype(BF16)
    vc_ref[0] = z[:, 8 * LANE:12 * LANE].astype(BF16)
    cq = z[:, 12 * LANE:14 * LANE]
    cq = cq * lax.rsqrt(jnp.mean(cq * cq, axis=-1, keepdims=True) + EPS) * gcq_ref[...]
    qd = jnp.dot(cq.astype(BF16), wuq_ref[...], preferred_element_type=F32)
    ckv = _blk(z, 14)
    ckv = (ckv * lax.rsqrt(jnp.mean(ckv * ckv, axis=-1, keepdims=True) + EPS) * gckv_ref[...]).astype(BF16)
    kn = jnp.dot(ckv, wkn_ref[...], preferred_element_type=F32)
    vd_ref[0] = jnp.dot(ckv, wvd_ref[...], preferred_element_type=F32).astype(BF16)
    kr = pltpu.roll(_blk(z, 15), D_NOPE, 1)
    hd = D_NOPE + D_ROPE
    for c in range(4):
        sl = slice(c * LANE, (c + 1) * LANE)
        qd_ref[0, :, sl] = (_rope(_lane_norm(_blk(qd, c), hd, gqd_ref[...]), cosd, sind, even) * d_scale).astype(BF16)
        kd_ref[0, :, sl] = _rope(_lane_norm(_blk(kn, c) + kr, hd, gkd_ref[...]), cosd, sind, even).astype(BF16)


def _inproj_odd(x, mod, g, w, cos, sin, cosd, sind, consts, n_lat_tiles):
    bsz, t, d = x.shape
    tm = ROW_TILE
    tab = pl.BlockSpec((tm, LANE), lambda b, i: (i, 0))
    return pl.pallas_call(
        _inproj_odd_kernel,
        grid=(bsz, t // tm),
        in_specs=(_row_specs(tm, d, bsz, n_lat_tiles) + [_const_spec(w), tab, tab, tab, tab]
                  + [_const_spec(a) for a in consts]),
        out_specs=[pl.BlockSpec((1, tm, HALF), lambda b, i: (b, i, 0)) for _ in range(6)],
        out_shape=[jax.ShapeDtypeStruct((bsz, t, HALF), BF16) for _ in range(6)],
        compiler_params=_cparams(("parallel", "parallel")),
        name="inproj_odd",
    )(x, mod, g.reshape(1, d), w, cos, sin, cosd, sind, *consts)


def _attn_kernel(*refs, mode, tq, tk, n_tiles):
    if mode == "diff":
        q_ref, k_ref, v_ref, lam_ref, gsub_ref, o_ref, q_sc, s_sc, m_sc, a_sc, l_sc, acc_sc = refs
    else:
        q_ref, k_ref, v_ref = refs[:3]
        o_ref, q_sc, s_sc, m_sc, a_sc, l_sc, acc_sc = refs[-7:]
    rows = q_sc.shape[0]
    n_blk = tk // LANE
    lo = lax.broadcasted_iota(jnp.int32, (tq, LANE), 1) < HEAD64

    if mode == "mla":
        q_sc[...] = q_ref[0]
    else:
        for p in range(q_ref.shape[2] // LANE):
            qb = _blk(q_ref[0], p)
            q_sc[2 * p * tq:(2 * p + 1) * tq] = jnp.where(lo, qb, jnp.zeros_like(qb))
            q_sc[(2 * p + 1) * tq:(2 * p + 2) * tq] = jnp.where(lo, jnp.zeros_like(qb), qb)

    def score_stage(j, slot):
        start = pl.multiple_of(j * tk, LANE)
        s = lax.dot_general(q_sc[...], k_ref[0, pl.ds(start, tk), :], (((1,), (1,)), ((), ())),
                            preferred_element_type=F32)
        s_sc[slot] = s
        m_tile = jnp.max(functools.reduce(jnp.maximum, [_blk(s, c) for c in range(n_blk)]), axis=1, keepdims=True)
        m_prev = m_sc[1 - slot]
        m_new = jnp.maximum(m_prev, m_tile)
        m_sc[slot] = m_new
        a_sc[slot] = jnp.exp2(m_prev - m_new)

    def value_stage(j, slot):
        start = pl.multiple_of(j * tk, LANE)
        m = m_sc[slot]
        alpha = a_sc[slot]
        p_blocks = [jnp.exp2((s_sc[slot, :, c * LANE:(c + 1) * LANE] - m).astype(BF16)) for c in range(n_blk)]
        if mode != "gqa":
            l_sc[...] = alpha * l_sc[...] + functools.reduce(jnp.add, p_blocks).astype(F32)
        p = jnp.concatenate(p_blocks, axis=1)
        acc_sc[...] = alpha * acc_sc[...] + jnp.dot(p, v_ref[0, pl.ds(start, tk), :], preferred_element_type=F32)

    m_sc[1] = jnp.full((rows, LANE), NEG_BIG, F32)
    l_sc[...] = jnp.zeros((rows, LANE), F32)
    acc_sc[...] = jnp.zeros((rows, LANE), F32)
    score_stage(0, 0)
    peel = (n_tiles - 1) % ATTN_UNROLL
    for j in range(peel):
        value_stage(j, j % 2)
        score_stage(j + 1, (j + 1) % 2)

    def body(t, carry):
        for u in range(ATTN_UNROLL):
            j = ATTN_UNROLL * t + peel + u
            value_stage(j, (peel + u) % 2)
            score_stage(j + 1, (peel + u + 1) % 2)
        return carry

    lax.fori_loop(0, (n_tiles - 1) // ATTN_UNROLL, body, 0)
    value_stage(n_tiles - 1, (n_tiles - 1) % 2)

    acc = acc_sc[...]
    o = acc / (acc[:, HEAD64:HEAD64 + 1] if mode == "gqa" else jnp.sum(l_sc[...], axis=1, keepdims=True))
    head = lambda h: o[h * tq:(h + 1) * tq]
    if mode == "gqa":
        for p in range(o_ref.shape[2] // LANE):
            o_ref[0, :, p * LANE:(p + 1) * LANE] = jnp.where(
                lo, head(2 * p), pltpu.roll(head(2 * p + 1), HEAD64, 1)).astype(o_ref.dtype)
    elif mode == "diff":
        dlt = head(0) - lam_ref[...] * head(1)
        o_ref[0] = (dlt * lax.rsqrt(jnp.mean(dlt * dlt, axis=1, keepdims=True) + EPS) * gsub_ref[...]).astype(o_ref.dtype)
    else:
        o_ref[0] = o.astype(o_ref.dtype)


def _attn(q, k, v, *, mode, tq, tk, q_rows, k_rows, extra=(), into=None):
    bsz, t_q, wq = q.shape
    qw = 2 * LANE if mode == "gqa" else LANE
    q0, nq = q_rows
    k0, nk = k_rows
    tq, tk = min(tq, nq), min(tk, nk)
    rows = {"gqa": 4, "diff": 2, "mla": 1}[mode] * tq
    assert q0 % tq == 0 and nq % tq == 0 and k0 % nk == 0 and nk % tk == 0
    qmap = lambda b, g, i: (b, q0 // tq + i, g)
    kmap = lambda b, g, i: (b, k0 // nk, g)
    in_specs = [pl.BlockSpec((1, tq, qw), qmap), pl.BlockSpec((1, nk, LANE), kmap), pl.BlockSpec((1, nk, LANE), kmap)]
    in_specs += [pl.BlockSpec(a.shape, lambda b, g, i: (0, 0)) for a in extra]
    args = [q, k, v, *extra]
    aliases = {}
    if into is not None:
        in_specs.append(pl.BlockSpec(memory_space=pl.ANY))
        aliases = {len(args): 0}
        args.append(into)

    def kern(*refs):
        if into is not None:
            refs = refs[:len(args) - 1] + refs[len(args):]
        _attn_kernel(*refs, mode=mode, tq=tq, tk=tk, n_tiles=nk // tk)

    return pl.pallas_call(
        kern,
        grid=(bsz, wq // qw, nq // tq),
        in_specs=in_specs,
        out_specs=pl.BlockSpec((1, tq, qw), qmap),
        out_shape=jax.ShapeDtypeStruct((bsz, t_q, wq), BF16),
        scratch_shapes=[pltpu.VMEM((rows, LANE), BF16), pltpu.VMEM((2, rows, tk), F32),
                        pltpu.VMEM((2, rows, LANE), F32), pltpu.VMEM((2, rows, LANE), F32),
                        pltpu.VMEM((rows, LANE), F32), pltpu.VMEM((rows, LANE), F32)],
        input_output_aliases=aliases,
        compiler_params=_cparams(("parallel", "parallel", "parallel")),
        name="attn_" + mode,
    )(*args)


def _mlstm_kernel(qf_ref, pf_ref, nf_ref, vf_ref, gf_ref, qb_ref, pb_ref, nb_ref, vb_ref, gb_ref, cw_ref, cb_ref,
                  hf_ref, hb_ref, c_sc, m_sc, *, n_lat_chunks, n_chunks):
    L, dh = B_CHUNK, B_HEAD_DIM
    c = pl.program_id(1)
    n_ctx_chunks = n_chunks - n_lat_chunks
    fwd_chunk = jnp.where(c < n_ctx_chunks, n_lat_chunks + c, c - n_ctx_chunks)
    bwd_chunk = n_chunks - 1 - c

    @pl.when(c == 0)
    def _():
        c_sc[...] = jnp.zeros(c_sc.shape, F32)
        m_sc[...] = jnp.zeros(m_sc.shape, F32)

    row = lax.broadcasted_iota(jnp.int32, (L, L), 0)
    col = lax.broadcasted_iota(jnp.int32, (L, L), 1)
    upper = (row <= col).astype(F32)
    lower = (col <= row).astype(F32)
    rid = lax.broadcasted_iota(jnp.int32, (L, 1), 0)
    glane = lax.broadcasted_iota(jnp.int32, (L, LANE), 1)
    is_f = (glane // B_HEADS) % 2 == 1
    ones_col = (lax.broadcasted_iota(jnp.int32, (L, dh), 1) == 0).astype(BF16)
    hi = lax.Precision.HIGHEST

    def direction(qk_ref, p_ref, n_ref, v_ref, g_ref, h_ref, chunk, reverse):
        first = (chunk == 0) | (chunk == n_lat_chunks)
        last = (chunk == n_lat_chunks - 1) | (chunk == n_chunks - 1)
        x = qk_ref[0]
        prev_row = jnp.where(first, 0.0, p_ref[0, SUBLANE - 1:SUBLANE, :])
        next_row = jnp.where(last, 0.0, n_ref[0, 0:1, :])
        x_prev = jnp.where(rid == 0, prev_row, pltpu.roll(x, 1, 0))
        x_next = jnp.where(rid == L - 1, next_row, pltpu.roll(x, L - 1, 0))
        y = x_prev * cw_ref[0:1, :] + x * cw_ref[1:2, :] + x_next * cw_ref[2:3, :] + cb_ref[...]
        y = y * jax.nn.sigmoid(y)
        g = g_ref[0]
        g = jnp.where(is_f, jnp.minimum(g, 0.0) - jnp.log(1.0 + jnp.exp(-jnp.abs(g))), g)
        g_t = jnp.transpose(g)
        cum_r = jnp.dot(g_t, upper, preferred_element_type=F32, precision=hi)
        cum_c = jnp.dot(lower, g, preferred_element_type=F32, precision=hi)
        if reverse:
            cum_r = cum_r[:, L - 1:L] - cum_r + g_t
            cum_c = cum_c[L - 1:L, :] - cum_c + g
        gi = 2 * B_HEADS if reverse else 0
        mask = (col >= row) if reverse else (col <= row)
        for h in range(B_HEADS):
            idx = (B_HEADS if reverse else 0) + h
            q = _blk(y, h).astype(BF16)
            k = _blk(y, B_HEADS + h) * (dh ** -0.5)
            v_ext = jnp.concatenate([_blk(v_ref[0], h), ones_col], axis=1)
            i_r = g_t[gi + h:gi + h + 1, :]
            b_r = cum_r[gi + B_HEADS + h:gi + B_HEADS + h + 1, :]
            i_c = g[:, gi + h:gi + h + 1]
            b_c = cum_c[:, gi + B_HEADS + h:gi + B_HEADS + h + 1]
            m_prev = m_sc[idx][:, 0:1]
            c_ext = c_sc[idx]

            log_w = jnp.where(mask, b_c - b_r + i_r, -jnp.inf)
            log_inter = b_c + m_prev
            m_row = jnp.maximum(log_inter, jnp.max(log_w, axis=1, keepdims=True))
            w_intra = jnp.exp(log_w - m_row)
            a_inter = jnp.exp(log_inter - m_row)
            sc = lax.dot_general(q, k.astype(BF16), (((1,), (1,)), ((), ())), preferred_element_type=F32) * w_intra
            q_c = jnp.dot(q, c_ext.astype(BF16), preferred_element_type=F32)
            s_v = jnp.dot(sc.astype(BF16), v_ext, preferred_element_type=F32)
            num = a_inter * q_c[:, :dh] + s_v[:, :dh]
            den = a_inter * q_c[:, dh:dh + 1] + s_v[:, dh:dh + 1]
            h_ref[0, :, h * dh:(h + 1) * dh] = num / jnp.maximum(jnp.abs(den), jnp.exp(-m_row))

            b_last = b_r[:, 0:1] if reverse else b_r[:, L - 1:L]
            m_new = jnp.maximum(b_last + m_prev, jnp.max(b_last - b_r + i_r, axis=1, keepdims=True))
            g_c = jnp.exp(b_last - b_c + i_c - m_new)
            decay = jnp.exp(b_last + m_prev - m_new)
            upd = jnp.dot(jnp.transpose(k).astype(BF16), (g_c * v_ext.astype(F32)).astype(BF16),
                          preferred_element_type=F32)
            c_sc[idx] = decay * c_ext + upd
            m_sc[idx] = jnp.broadcast_to(m_new, (1, LANE))

    direction(qf_ref, pf_ref, nf_ref, vf_ref, gf_ref, hf_ref, fwd_chunk, False)
    direction(qb_ref, pb_ref, nb_ref, vb_ref, gb_ref, hb_ref, bwd_chunk, True)


def _mlstm(qk, vb, gates, conv_w, conv_b, n_lat):
    bsz, t, _ = qk.shape
    L = B_CHUNK
    n_chunks, n_lat_chunks = t // L, n_lat // L
    n_ctx_chunks = n_chunks - n_lat_chunks
    per = L // SUBLANE
    fwd = lambda c: jnp.where(c < n_ctx_chunks, n_lat_chunks + c, c - n_ctx_chunks)
    bwd = lambda c: n_chunks - 1 - c

    def specs(order):
        return [
            pl.BlockSpec((1, L, 2 * HALF), lambda b, c: (b, order(c), 0)),
            pl.BlockSpec((1, SUBLANE, 2 * HALF), lambda b, c: (b, jnp.maximum(order(c) * per - 1, 0), 0)),
            pl.BlockSpec((1, SUBLANE, 2 * HALF), lambda b, c: (b, jnp.minimum((order(c) + 1) * per, t // SUBLANE - 1), 0)),
            pl.BlockSpec((1, L, HALF), lambda b, c: (b, order(c), 0)),
            pl.BlockSpec((1, L, LANE), lambda b, c: (b, order(c), 0)),
        ]

    const = lambda a: pl.BlockSpec(a.shape, lambda b, c: (0, 0))
    return pl.pallas_call(
        functools.partial(_mlstm_kernel, n_lat_chunks=n_lat_chunks, n_chunks=n_chunks),
        grid=(bsz, n_chunks),
        in_specs=specs(fwd) + specs(bwd) + [const(conv_w), const(conv_b)],
        out_specs=[pl.BlockSpec((1, L, HALF), lambda b, c: (b, fwd(c), 0)),
                   pl.BlockSpec((1, L, HALF), lambda b, c: (b, bwd(c), 0))],
        out_shape=[jax.ShapeDtypeStruct((bsz, t, HALF), F32)] * 2,
        scratch_shapes=[pltpu.VMEM((2 * B_HEADS, B_HEAD_DIM, 2 * B_HEAD_DIM), F32),
                        pltpu.VMEM((2 * B_HEADS, 1, LANE), F32)],
        compiler_params=_cparams(("parallel", "arbitrary")),
        name="mlstm",
    )(qk, qk, qk, vb, gates, qk, qk, qk, vb, gates, conv_w, conv_b)


def _route(logits_t, bias):
    per = N_EXPERTS // N_GROUPS
    idx = lax.broadcasted_iota(jnp.int32, logits_t.shape, 0)
    idx_f = idx.astype(F32)
    scores = jax.nn.sigmoid(logits_t)
    sel = scores + bias

    def first_max(a):
        m = jnp.max(a, axis=0, keepdims=True)
        return m, jnp.min(jnp.where(a == m, idx_f, float(N_EXPERTS)), axis=0, keepdims=True)

    loc_f = lax.broadcasted_iota(jnp.int32, (per, logits_t.shape[1]), 0).astype(F32)
    best, g_idx = None, None
    for g in range(N_GROUPS):
        a = sel[g * per:(g + 1) * per]
        m1 = jnp.max(a, axis=0, keepdims=True)
        i1 = jnp.min(jnp.where(a == m1, loc_f, float(per)), axis=0, keepdims=True)
        top2 = m1 + jnp.max(jnp.where(loc_f == i1, -jnp.inf, a), axis=0, keepdims=True)
        if g == 0:
            best, g_idx = top2, jnp.zeros_like(top2)
        else:
            better = top2 > best
            best, g_idx = jnp.where(better, top2, best), jnp.where(better, float(g), g_idx)
    a = jnp.where((idx // per).astype(F32) == g_idx, sel, -jnp.inf)
    _, e1 = first_max(a)
    _, e2 = first_max(jnp.where(idx_f == e1, -jnp.inf, a))
    s1 = jnp.sum(jnp.where(idx_f == e1, scores, 0.0), axis=0, keepdims=True)
    s2 = jnp.sum(jnp.where(idx_f == e2, scores, 0.0), axis=0, keepdims=True)
    return e1, e2, s1 / (s1 + s2), s2 / (s1 + s2)


def _out_tail(mix, x_ref, mod_ref, g_ref, w_ref, wrh_ref, wrl_ref, br_ref, xo_ref, f_ref, rt_ref):
    o = jnp.dot(mix, w_ref[...], preferred_element_type=F32)
    xn = x_ref[0] + mod_ref[0, 0:1, :] * o
    xo_ref[0] = xn
    y = xn * lax.rsqrt(jnp.mean(xn * xn, axis=-1, keepdims=True) + EPS) * g_ref[...]
    f = y * (1.0 + mod_ref[0, 2:3, :]) + mod_ref[0, 1:2, :]
    f_hi = f.astype(BF16)
    f_ref[0] = f_hi
    f_lo = (f - f_hi.astype(F32)).astype(BF16)
    logits = (jnp.dot(f_hi, wrh_ref[...], preferred_element_type=F32)
              + jnp.dot(f_lo, wrh_ref[...], preferred_element_type=F32)
              + jnp.dot(f_hi, wrl_ref[...], preferred_element_type=F32))
    rows = _route(jnp.transpose(logits)[:N_EXPERTS], br_ref[...])
    row_id = lax.broadcasted_iota(jnp.int32, (SUBLANE, logits.shape[0]), 0)
    rt_ref[0] = functools.reduce(lambda rec, kr: jnp.where(row_id == kr[0], kr[1], rec), enumerate(rows),
                                 jnp.zeros((SUBLANE, logits.shape[0]), F32))


def _outproj_even_kernel(oa_ref, hf_ref, hb_ref, ob_ref, gh_ref, *tail):
    parts = [oa_ref[0]]
    for h in range(B_HEADS):
        hs = _blk(hf_ref[0], h) + _blk(hb_ref[0], h)
        hn = hs * lax.rsqrt(jnp.mean(hs * hs, axis=1, keepdims=True) + EPS) * _blk(gh_ref[...], h)
        parts.append((hn * jax.nn.sigmoid(_blk(ob_ref[0], h))).astype(BF16))
    _out_tail(jnp.concatenate(parts, axis=1), *tail)


def _outproj_odd_kernel(oc_ref, od_ref, *tail):
    _out_tail(jnp.concatenate([oc_ref[0], od_ref[0]], axis=1), *tail)


def _outproj(kern, heads, consts, x, mod, g, w, wr_hi, wr_lo, b_router, n_rows, n_lat_tiles):
    bsz, _, d = x.shape
    tm = ROW_TILE
    row = lambda b, i: (b, i, 0)
    return pl.pallas_call(
        kern,
        grid=(bsz, n_rows // tm),
        in_specs=([pl.BlockSpec((1, tm, HALF), row) for _ in heads] + [_const_spec(a) for a in consts]
                  + [pl.BlockSpec((1, tm, d), row),
                     pl.BlockSpec((1, 3, d), lambda b, i: (jnp.where(i >= n_lat_tiles, bsz, b), 0, 0)),
                     pl.BlockSpec((1, d), lambda b, i: (0, 0)), _const_spec(w), _const_spec(wr_hi),
                     _const_spec(wr_lo), _const_spec(b_router)]),
        out_specs=[pl.BlockSpec((1, tm, d), row), pl.BlockSpec((1, tm, d), row),
                   pl.BlockSpec((1, SUBLANE, tm), lambda b, i: (b, 0, i))],
        out_shape=[jax.ShapeDtypeStruct((bsz, n_rows, d), F32), jax.ShapeDtypeStruct((bsz, n_rows, d), BF16),
                   jax.ShapeDtypeStruct((bsz, SUBLANE, n_rows), F32)],
        compiler_params=_cparams(("parallel", "parallel")),
        name="outproj",
    )(*heads, *consts, x, mod, g.reshape(1, d), w, wr_hi, wr_lo, b_router)


def _expert_kernel(blk_e_ref, n_used_ref, x_ref, w1_ref, w3_ref, w2_ref, y_ref, w1_sc, w3_sc, w2_sc):
    i = pl.program_id(0)

    @pl.when((i == 0) | (blk_e_ref[i] != blk_e_ref[jnp.maximum(i - 1, 0)]))
    def _():
        w1_sc[...] = w1_ref[0, 0].astype(BF16)
        w3_sc[...] = w3_ref[0, 0].astype(BF16)
        w2_sc[...] = w2_ref[0, 0].astype(BF16)

    @pl.when(i < n_used_ref[0])
    def _():
        xb = x_ref[...]
        h1 = jnp.dot(xb, w1_sc[...], preferred_element_type=F32)
        h3 = jnp.dot(xb, w3_sc[...], preferred_element_type=F32)
        a = (h1 * jax.nn.sigmoid(h1) * h3).astype(BF16)
        y_ref[...] = jnp.dot(a, w2_sc[...], preferred_element_type=F32).astype(y_ref.dtype)

    @pl.when(i >= n_used_ref[0])
    def _():
        y_ref[...] = jnp.zeros(y_ref.shape, y_ref.dtype)


def _experts(xin, blk_e, n_used, layer, w1, w3, w2):
    n_rows, d = xin.shape
    de = w1.shape[-1]
    nb = n_rows // MOE_ROWS
    grid_spec = pltpu.PrefetchScalarGridSpec(
        num_scalar_prefetch=2,
        grid=(nb,),
        in_specs=[
            pl.BlockSpec((MOE_ROWS, d), lambda i, be, nu: (i, 0)),
            pl.BlockSpec((1, 1, d, de), lambda i, be, nu: (layer, be[i], 0, 0)),
            pl.BlockSpec((1, 1, d, de), lambda i, be, nu: (layer, be[i], 0, 0)),
            pl.BlockSpec((1, 1, de, d), lambda i, be, nu: (layer, be[i], 0, 0)),
        ],
        out_specs=pl.BlockSpec((MOE_ROWS, d), lambda i, be, nu: (i, 0)),
        scratch_shapes=[pltpu.VMEM((d, de), BF16), pltpu.VMEM((d, de), BF16), pltpu.VMEM((de, d), BF16)],
    )
    return pl.pallas_call(
        _expert_kernel,
        grid_spec=grid_spec,
        out_shape=jax.ShapeDtypeStruct((n_rows, d), BF16),
        compiler_params=_cparams(("arbitrary",)),
        name="experts",
    )(blk_e, n_used, xin, w1, w3, w2)


def _moe(f_tok, route, layer, w1, w3, w2):
    n_tok, d = f_tok.shape
    e1, e2 = route[:, 0].reshape(-1).astype(jnp.int32), route[:, 1].reshape(-1).astype(jnp.int32)
    gate = jnp.stack([route[:, 2].reshape(-1), route[:, 3].reshape(-1)], axis=-1)

    n_assign = n_tok * TOP_K
    e_flat = jnp.concatenate([e1, e2]).astype(jnp.int32)
    assert n_assign % MOE_ROWS == 0
    onehot = (e_flat[:, None] == jnp.arange(N_EXPERTS, dtype=jnp.int32)[None, :]).astype(F32)
    tri = jnp.tril(jnp.ones((MOE_ROWS, MOE_ROWS), F32))
    within = jnp.einsum('ij,bjk->bik', tri, onehot.reshape(-1, MOE_ROWS, N_EXPERTS), precision=lax.Precision.HIGHEST)
    totals = within[:, -1, :]
    before = jnp.cumsum(totals, axis=0) - totals
    csum = (within + before[:, None, :]).reshape(n_assign, N_EXPERTS)
    counts = jnp.sum(totals, axis=0).astype(jnp.int32)
    rank = jnp.sum(csum * onehot, axis=1).astype(jnp.int32) - 1
    padded = (counts + MOE_ROWS - 1) // MOE_ROWS * MOE_ROWS
    pend = jnp.cumsum(padded)
    pstart = pend - padded
    dest = pstart[e_flat] + rank
    n_rows = -(-n_assign // MOE_ROWS) * MOE_ROWS + N_EXPERTS * MOE_ROWS
    nb = n_rows // MOE_ROWS
    blk_row = jnp.arange(nb, dtype=jnp.int32) * MOE_ROWS
    blk_e = jnp.minimum(jnp.sum((pend[None, :] <= blk_row[:, None]).astype(jnp.int32), axis=1), N_EXPERTS - 1)
    order = jnp.argsort(e_flat, stable=True).astype(jnp.int32)
    row = jnp.arange(n_rows, dtype=jnp.int32)
    row_e = jnp.repeat(blk_e, MOE_ROWS)
    pos = row - pstart[row_e]
    src = jnp.minimum((jnp.cumsum(counts) - counts)[row_e] + pos, n_assign - 1)
    row_tok = jnp.where(pos < counts[row_e], order[src], row) % n_tok
    xin = f_tok[row_tok]
    n_used = (pend[-1:] // MOE_ROWS).astype(jnp.int32)
    y = _experts(xin, blk_e, n_used, layer, w1, w3, w2)
    return y[dest[:n_tok]], y[dest[n_tok:]], gate


def _combine_kernel(x_ref, y0_ref, y1_ref, gate_ref, g2_ref, o_ref):
    gate = gate_ref[0]
    y = gate[:, 0:1] * y0_ref[0].astype(F32) + gate[:, 1:2] * y1_ref[0].astype(F32)
    o_ref[0] = x_ref[0] + g2_ref[0] * y


def _combine(x, y0, y1, gate, g2, n_lat_tiles):
    bsz, n_rows, d = x.shape
    tm = ROW_TILE
    row = lambda b, i: (b, i, 0)
    return pl.pallas_call(
        _combine_kernel,
        grid=(bsz, n_rows // tm),
        in_specs=[pl.BlockSpec((1, tm, d), row), pl.BlockSpec((1, tm, d), row), pl.BlockSpec((1, tm, d), row),
                  pl.BlockSpec((1, tm, TOP_K), row),
                  pl.BlockSpec((1, 1, d), lambda b, i: (jnp.where(i >= n_lat_tiles, bsz, b), 0, 0))],
        out_specs=pl.BlockSpec((1, tm, d), row),
        out_shape=jax.ShapeDtypeStruct((bsz, n_rows, d), F32),
        compiler_params=_cparams(("parallel", "parallel")),
        name="combine",
    )(x, y0, y1, gate, g2)


def _rope_lanes(n_lat, n_ctx, rot_dim):
    rows = n_lat // GRID_W
    r = jnp.repeat(jnp.arange(rows), GRID_W)
    c = jnp.tile(jnp.arange(GRID_W), rows)
    n_freq = rot_dim // 4
    inv = ROPE_THETA ** (-jnp.arange(n_freq, dtype=F32) / n_freq)
    ang = jnp.concatenate([r[:, None] * inv, c[:, None] * inv], axis=-1)
    cos = jnp.repeat(jnp.cos(ang), 2, axis=1)
    sin = jnp.repeat(jnp.sin(ang), 2, axis=1) * jnp.tile(jnp.array([-1.0, 1.0], F32), rot_dim // 2)
    pad = ((0, n_ctx), (0, 0))
    return jnp.pad(cos, pad, constant_values=1.0), jnp.pad(sin, pad)


def _pad_cols(w, n):
    return jnp.pad(w, ((0, 0), (0, n - w.shape[1])))


def _pad_row(v, n):
    return jnp.pad(v, (0, n - v.shape[0])).reshape(1, n)


def _round_up(n, m):
    return -(-n // m) * m


def _head_blocks(w, n_heads, width, start, size):
    w = w.reshape(w.shape[0], n_heads, width)[:, :, start:start + size]
    return jnp.pad(w, ((0, 0), (0, 0), (0, LANE - size))).reshape(w.shape[0], n_heads * LANE)


def kernel(x, c, ctx, c_ctx, w_ada, b_ada, g_mix, g_ffn, e_w_in, e_w_out, e_g_q, e_g_k, e_conv_w, e_conv_b, e_b_gates, e_g_h, o_w_in, o_w_out, o_g_qc, o_g_kc, o_lam, o_g_sub, o_g_cq, o_w_uq, o_g_ckv, o_w_ukv, o_g_qd, o_g_kd, w_router, b_router, w1, w3, w2):
    bsz, n_lat, d = x.shape
    n_ctx = ctx.shape[1]
    t = n_lat + n_ctx
    depth = w_ada.shape[0]
    assert n_lat % ROW_TILE == 0 and n_ctx % ROW_TILE == 0 and n_ctx % B_CHUNK == 0
    n_lat_tiles = n_lat // ROW_TILE
    hd = D_NOPE + D_ROPE

    cos64, sin64 = _rope_lanes(n_lat, n_ctx, A_HEAD_DIM)
    cos64, sin64 = jnp.tile(cos64, (1, 2)), jnp.tile(sin64, (1, 2))
    cos_d, sin_d = _rope_lanes(n_lat, n_ctx, D_ROPE)
    cos_d = jnp.pad(cos_d, ((0, 0), (D_NOPE, LANE - hd)), constant_values=1.0)
    sin_d = jnp.pad(sin_d, ((0, 0), (D_NOPE, LANE - hd)))
    half = jnp.arange(LANE) // HEAD64
    bd64 = (half[:, None] == half[None, :]).astype(BF16)
    w_router_p = _pad_cols(w_router.astype(F32), LANE)
    wr_hi = w_router_p.astype(BF16)
    wr_lo = (w_router_p - wr_hi.astype(F32)).astype(BF16)
    cond = jnp.concatenate([c, c_ctx[None, :]], axis=0)
    x_all = jnp.concatenate([x, ctx], axis=1)

    for layer in range(depth):
        last = layer == depth - 1
        j = layer // 2
        mod = (jax.nn.silu(cond) @ w_ada[layer] + b_ada[layer]).reshape(bsz + 1, 6, d)
        sh1, sc1, g1, sh2, sc2, g2 = [mod[:, i] for i in range(6)]
        mod_in = jnp.stack([sh1, sc1], axis=1)
        mod_out = jnp.stack([g1, sh2, sc2], axis=1)
        n_rows = n_lat if last else t
        if layer % 2 == 0:
            w_in = _pad_cols(e_w_in[j], _round_up(e_w_in.shape[-1], LANE)).astype(BF16)
            q, kd, vd, qk, vb, ob, gates = _inproj_even(
                x_all, mod_in, g_mix[layer], w_in, cos64, sin64, bd64, jnp.tile(e_g_q[j], 2).reshape(1, LANE),
                jnp.tile(e_g_k[j], 2).reshape(1, LANE), _pad_row(e_b_gates[j], LANE), n_lat_tiles)
            oa = _attn(q, kd, vd, mode="gqa", tq=ATTN_ROWS // 4, tk=KEY_TILE, q_rows=(0, n_lat), k_rows=(0, t))
            if not last:
                oa = _attn(q, kd, vd, mode="gqa", tq=ATTN_ROWS // 4, tk=KEY_TILE, q_rows=(n_lat, n_ctx), k_rows=(n_lat, n_ctx),
                           into=oa)
            hf, hb = _mlstm(qk, vb, gates, e_conv_w[j], e_conv_b[j].reshape(1, -1), n_lat)
            heads, consts, kern = [oa, hf, hb, ob], [e_g_h[j].reshape(1, HALF)], _outproj_even_kernel
            w_out = e_w_out[j]
        else:
            assert last, "the odd mixer is only wired as the final layer"
            lam_init = LAM_INIT_BASE[0] - LAM_INIT_BASE[1] * math.exp(-LAM_INIT_BASE[2] * layer)
            lq1, lk1, lq2, lk2 = o_lam[j].astype(F32)
            lam = jnp.exp(jnp.sum(lq1 * lk1)) - jnp.exp(jnp.sum(lq2 * lk2)) + lam_init
            w_in = _pad_cols(o_w_in[j], _round_up(o_w_in.shape[-1], LANE)).astype(BF16)
            consts = [bd64, jnp.tile(o_g_qc[j], 2).reshape(1, LANE), jnp.tile(o_g_kc[j], 2).reshape(1, LANE),
                      o_g_cq[j].reshape(1, D_Q_LORA), _head_blocks(o_w_uq[j], D_HEADS, hd, 0, hd).astype(BF16),
                      _pad_row(o_g_qd[j], LANE), o_g_ckv[j].reshape(1, D_KV_LORA),
                      _head_blocks(o_w_ukv[j], D_HEADS, D_NOPE + D_V_DIM, 0, D_NOPE).astype(BF16),
                      _head_blocks(o_w_ukv[j], D_HEADS, D_NOPE + D_V_DIM, D_NOPE, D_V_DIM).astype(BF16),
                      _pad_row(o_g_kd[j], LANE)]
            qc, kc, vc, qd, kdl, vdl = _inproj_odd(x_all, mod_in, g_mix[layer], w_in, cos64, sin64, cos_d, sin_d,
                                                   consts, n_lat_tiles)
            extra = (jnp.full((1, LANE), lam, F32), (o_g_sub[j] * (1.0 - lam_init)).reshape(1, LANE))
            oc = _attn(qc, kc, vc, mode="diff", tq=ATTN_ROWS // 2, tk=KEY_TILE, q_rows=(0, n_lat), k_rows=(0, t), extra=extra)
            od = _attn(qd, kdl, vdl, mode="mla", tq=ATTN_ROWS, tk=KEY_TILE, q_rows=(0, n_lat), k_rows=(0, t))
            heads, consts, kern = [oc, od], [], _outproj_odd_kernel
            w_out = o_w_out[j]
        x_new, f, route = _outproj(kern, heads, consts, x_all, mod_out, g_ffn[layer], w_out.astype(BF16), wr_hi, wr_lo,
                                   jnp.broadcast_to(b_router.astype(F32)[:, None], (N_EXPERTS, ROW_TILE)), n_rows, n_lat_tiles)
        y0, y1, gate = _moe(f.reshape(bsz * n_rows, d), route, layer, w1, w3, w2)
        x_all = _combine(x_new, y0.reshape(bsz, n_rows, d), y1.reshape(bsz, n_rows, d),
                         gate.reshape(bsz, n_rows, TOP_K), g2.reshape(bsz + 1, 1, d), n_lat_tiles)
    return x_all
```

```python
import functools
import math

import jax
import jax.numpy as jnp
import numpy as np
from jax import lax
from jax.experimental import pallas as pl
from jax.experimental.pallas import tpu as pltpu

F32 = jnp.float32
BF16 = jnp.bfloat16
EPS = 1e-6
LOG2E = 1.4426950408889634
NEG_BIG = -1e30
GRID_W = 64
ROPE_THETA = 10000.0
LAM_INIT_BASE = (0.8, 0.6, 0.3)

A_HEAD_DIM, A_HEADS, A_KV_HEADS = 64, 8, 2
B_HEAD_DIM, B_HEADS, B_CONV, B_CHUNK = 128, 4, 3, 128
C_HEAD_DIM, C_V_DIM, C_HEADS = 64, 128, 4
D_HEADS, D_Q_LORA, D_KV_LORA, D_NOPE, D_ROPE, D_V_DIM = 4, 256, 128, 64, 32, 128
N_EXPERTS, N_GROUPS, TOP_K, D_EXPERT = 32, 4, 2, 512
HALF = 512
HEAD64 = 64

LANE = 128
SUBLANE = 8
ROW_TILE = 256
MOE_ROWS = 512
KEY_TILE = 1280
ATTN_ROWS = 2048
ATTN_UNROLL = 2
VMEM_LIMIT = 60 * 1024 * 1024


def _cparams(sem):
    return pltpu.CompilerParams(dimension_semantics=sem, vmem_limit_bytes=VMEM_LIMIT)


def _blk(a, i):
    return a[:, i * LANE:(i + 1) * LANE]


def _modulated(x_ref, mod_ref, g_ref):
    x = x_ref[0]
    y = x * lax.rsqrt(jnp.mean(x * x, axis=-1, keepdims=True) + EPS) * g_ref[...]
    return (y * (1.0 + mod_ref[0, 1:2, :]) + mod_ref[0, 0:1, :]).astype(BF16)


def _pairswap(b, even):
    return jnp.where(even, pltpu.roll(b, LANE - 1, 1), pltpu.roll(b, 1, 1))


def _rope(b, cos, sin, even):
    return b * cos + _pairswap(b, even) * sin


def _head64_norm(b, bd, gain):
    ss = jnp.dot((b * b).astype(BF16), bd, preferred_element_type=F32)
    return b * lax.rsqrt(ss * (1.0 / HEAD64) + EPS) * gain


def _lane_norm(b, n_real, gain):
    ss = jnp.sum(b * b, axis=1, keepdims=True)
    return b * lax.rsqrt(ss * (1.0 / n_real) + EPS) * gain


def _inproj_even_kernel(x_ref, mod_ref, g_ref, w_ref, cos_ref, sin_ref, bd_ref, gq_ref, gk_ref, bg_ref,
                        q_ref, k_ref, v_ref, qk_ref, vb_ref, ob_ref, gt_ref):
    z = jnp.dot(_modulated(x_ref, mod_ref, g_ref), w_ref[...], preferred_element_type=F32)
    lane = lax.broadcasted_iota(jnp.int32, (z.shape[0], LANE), 1)
    even = (lane & 1) == 0
    lo = lane < HEAD64
    cos, sin, bd = cos_ref[...], sin_ref[...], bd_ref[...]
    q_scale = A_HEAD_DIM ** -0.5 * LOG2E
    for c in range(4):
        qb = _rope(_head64_norm(_blk(z, c), bd, gq_ref[...]), cos, sin, even) * q_scale
        q_ref[0, :, c * LANE:(c + 1) * LANE] = qb.astype(BF16)
    kn = _rope(_head64_norm(_blk(z, 4), bd, gk_ref[...]), cos, sin, even)
    sw = pltpu.roll(kn, HEAD64, 1)
    k_ref[0] = jnp.concatenate([jnp.where(lo, kn, sw), jnp.where(lo, sw, kn)], axis=1).astype(BF16)
    va = _blk(z, 5)
    one = (lane == HEAD64).astype(F32)
    v_ref[0] = jnp.concatenate([jnp.where(lo, va, one), jnp.where(lo, pltpu.roll(va, HEAD64, 1), one)],
                               axis=1).astype(BF16)
    qk_ref[0] = z[:, 6 * LANE:14 * LANE]
    vb_ref[0] = z[:, 14 * LANE:18 * LANE].astype(BF16)
    ob_ref[0] = z[:, 18 * LANE:22 * LANE]
    gt_ref[0] = _blk(z, 22) + bg_ref[...]


def _row_specs(tm, d, bsz, n_lat_tiles):
    return [pl.BlockSpec((1, tm, d), lambda b, i: (b, i, 0)),
            pl.BlockSpec((1, 2, d), lambda b, i: (jnp.where(i >= n_lat_tiles, bsz, b), 0, 0)),
            pl.BlockSpec((1, d), lambda b, i: (0, 0))]


def _const_spec(a):
    return pl.BlockSpec(a.shape, lambda b, i: (0,) * a.ndim)


def _inproj_even(x, mod, g, w, cos, sin, bd, gq, gk, bg, n_lat_tiles):
    bsz, t, d = x.shape
    tm = ROW_TILE
    tab = pl.BlockSpec((tm, LANE), lambda b, i: (i, 0))
    consts = [bd, gq, gk, bg]
    widths = [(4 * LANE, BF16), (2 * LANE, BF16), (2 * LANE, BF16), (8 * LANE, F32), (4 * LANE, BF16),
              (4 * LANE, F32), (LANE, F32)]
    return pl.pallas_call(
        _inproj_even_kernel,
        grid=(bsz, t // tm),
        in_specs=_row_specs(tm, d, bsz, n_lat_tiles) + [_const_spec(w), tab, tab] + [_const_spec(a) for a in consts],
        out_specs=[pl.BlockSpec((1, tm, n), lambda b, i: (b, i, 0)) for n, _ in widths],
        out_shape=[jax.ShapeDtypeStruct((bsz, t, n), dt) for n, dt in widths],
        compiler_params=_cparams(("parallel", "parallel")),
        name="inproj_even",
    )(x, mod, g.reshape(1, d), w, cos, sin, *consts)


def _inproj_odd_kernel(x_ref, mod_ref, g_ref, w_ref, cos_ref, sin_ref, cosd_ref, sind_ref, bd_ref, gqc_ref, gkc_ref,
                       gcq_ref, wuq_ref, gqd_ref, gckv_ref, wkn_ref, wvd_ref, gkd_ref,
                       qc_ref, kc_ref, vc_ref, qd_ref, kd_ref, vd_ref):
    z = jnp.dot(_modulated(x_ref, mod_ref, g_ref), w_ref[...], preferred_element_type=F32)
    lane = lax.broadcasted_iota(jnp.int32, (z.shape[0], LANE), 1)
    even = (lane & 1) == 0
    cos, sin, bd = cos_ref[...], sin_ref[...], bd_ref[...]
    cosd, sind = cosd_ref[...], sind_ref[...]
    c_scale = C_HEAD_DIM ** -0.5 * LOG2E
    d_scale = (D_NOPE + D_ROPE) ** -0.5 * LOG2E
    for c in range(4):
        sl = slice(c * LANE, (c + 1) * LANE)
        qc_ref[0, :, sl] = (_rope(_head64_norm(_blk(z, c), bd, gqc_ref[...]), cos, sin, even) * c_scale).astype(BF16)
        kc_ref[0, :, sl] = _rope(_head64_norm(_blk(z, 4 + c), bd, gkc_ref[...]), cos, sin, even).astype(BF16)
    vc_ref[0] = z[:, 8 * LANE:12 * LANE].astype(BF16)
    cq = z[:, 12 * LANE:14 * LANE]
    cq = cq * lax.rsqrt(jnp.mean(cq * cq, axis=-1, keepdims=True) + EPS) * gcq_ref[...]
    qd = jnp.dot(cq.astype(BF16), wuq_ref[...], preferred_element_type=F32)
    ckv = _blk(z, 14)
    ckv = (ckv * lax.rsqrt(jnp.mean(ckv * ckv, axis=-1, keepdims=True) + EPS) * gckv_ref[...]).astype(BF16)
    kn = jnp.dot(ckv, wkn_ref[...], preferred_element_type=F32)
    vd_ref[0] = jnp.dot(ckv, wvd_ref[...], preferred_element_type=F32).astype(BF16)
    kr = pltpu.roll(_blk(z, 15), D_NOPE, 1)
    hd = D_NOPE + D_ROPE
    for c in range(4):
        sl = slice(c * LANE, (c + 1) * LANE)
        qd_ref[0, :, sl] = (_rope(_lane_norm(_blk(qd, c), hd, gqd_ref[...]), cosd, sind, even) * d_scale).astype(BF16)
        kd_ref[0, :, sl] = _rope(_lane_norm(_blk(kn, c) + kr, hd, gkd_ref[...]), cosd, sind, even).astype(BF16)


def _inproj_odd(x, mod, g, w, cos, sin, cosd, sind, consts, n_lat_tiles):
    bsz, t, d = x.shape
    tm = ROW_TILE
    tab = pl.BlockSpec((tm, LANE), lambda b, i: (i, 0))
    return pl.pallas_call(
        _inproj_odd_kernel,
        grid=(bsz, t // tm),
        in_specs=(_row_specs(tm, d, bsz, n_lat_tiles) + [_const_spec(w), tab, tab, tab, tab]
                  + [_const_spec(a) for a in consts]),
        out_specs=[pl.BlockSpec((1, tm, HALF), lambda b, i: (b, i, 0)) for _ in range(6)],
        out_shape=[jax.ShapeDtypeStruct((bsz, t, HALF), BF16) for _ in range(6)],
        compiler_params=_cparams(("parallel", "parallel")),
        name="inproj_odd",
    )(x, mod, g.reshape(1, d), w, cos, sin, cosd, sind, *consts)


def _attn_kernel(*refs, mode, tq, tk, n_tiles):
    if mode == "diff":
        q_ref, k_ref, v_ref, lam_ref, gsub_ref, o_ref, q_sc, s_sc, m_sc, a_sc, l_sc, acc_sc = refs
    else:
        q_ref, k_ref, v_ref = refs[:3]
        o_ref, q_sc, s_sc, m_sc, a_sc, l_sc, acc_sc = refs[-7:]
    rows = q_sc.shape[0]
    n_blk = tk // LANE
    lo = lax.broadcasted_iota(jnp.int32, (tq, LANE), 1) < HEAD64

    if mode == "mla":
        q_sc[...] = q_ref[0]
    else:
        for p in range(q_ref.shape[2] // LANE):
            qb = _blk(q_ref[0], p)
            q_sc[2 * p * tq:(2 * p + 1) * tq] = jnp.where(lo, qb, jnp.zeros_like(qb))
            q_sc[(2 * p + 1) * tq:(2 * p + 2) * tq] = jnp.where(lo, jnp.zeros_like(qb), qb)

    def score_stage(j, slot):
        start = pl.multiple_of(j * tk, LANE)
        s = lax.dot_general(q_sc[...], k_ref[0, pl.ds(start, tk), :], (((1,), (1,)), ((), ())),
                            preferred_element_type=F32)
        s_sc[slot] = s
        m_tile = jnp.max(functools.reduce(jnp.maximum, [_blk(s, c) for c in range(n_blk)]), axis=1, keepdims=True)
        m_prev = m_sc[1 - slot]
        m_new = jnp.maximum(m_prev, m_tile)
        m_sc[slot] = m_new
        a_sc[slot] = jnp.exp2(m_prev - m_new)

    def value_stage(j, slot):
        start = pl.multiple_of(j * tk, LANE)
        m = m_sc[slot]
        alpha = a_sc[slot]
        p_blocks = [jnp.exp2((s_sc[slot, :, c * LANE:(c + 1) * LANE] - m).astype(BF16)) for c in range(n_blk)]
        if mode != "gqa":
            l_sc[...] = alpha * l_sc[...] + functools.reduce(jnp.add, p_blocks).astype(F32)
        p = jnp.concatenate(p_blocks, axis=1)
        acc_sc[...] = alpha * acc_sc[...] + jnp.dot(p, v_ref[0, pl.ds(start, tk), :], preferred_element_type=F32)

    m_sc[1] = jnp.full((rows, LANE), NEG_BIG, F32)
    l_sc[...] = jnp.zeros((rows, LANE), F32)
    acc_sc[...] = jnp.zeros((rows, LANE), F32)
    score_stage(0, 0)
    peel = (n_tiles - 1) % ATTN_UNROLL
    for j in range(peel):
        value_stage(j, j % 2)
        score_stage(j + 1, (j + 1) % 2)

    def body(t, carry):
        for u in range(ATTN_UNROLL):
            j = ATTN_UNROLL * t + peel + u
            value_stage(j, (peel + u) % 2)
            score_stage(j + 1, (peel + u + 1) % 2)
        return carry

    lax.fori_loop(0, (n_tiles - 1) // ATTN_UNROLL, body, 0)
    value_stage(n_tiles - 1, (n_tiles - 1) % 2)

    acc = acc_sc[...]
    o = acc / (acc[:, HEAD64:HEAD64 + 1] if mode == "gqa" else jnp.sum(l_sc[...], axis=1, keepdims=True))
    head = lambda h: o[h * tq:(h + 1) * tq]
    if mode == "gqa":
        for p in range(o_ref.shape[2] // LANE):
            o_ref[0, :, p * LANE:(p + 1) * LANE] = jnp.where(
                lo, head(2 * p), pltpu.roll(head(2 * p + 1), HEAD64, 1)).astype(o_ref.dtype)
    elif mode == "diff":
        dlt = head(0) - lam_ref[...] * head(1)
        o_ref[0] = (dlt * lax.rsqrt(jnp.mean(dlt * dlt, axis=1, keepdims=True) + EPS) * gsub_ref[...]).astype(o_ref.dtype)
    else:
        o_ref[0] = o.astype(o_ref.dtype)


def _attn(q, k, v, *, mode, tq, tk, q_rows, k_rows, extra=(), into=None):
    bsz, t_q, wq = q.shape
    qw = 2 * LANE if mode == "gqa" else LANE
    q0, nq = q_rows
    k0, nk = k_rows
    tq, tk = min(tq, nq), min(tk, nk)
    rows = {"gqa": 4, "diff": 2, "mla": 1}[mode] * tq
    assert q0 % tq == 0 and nq % tq == 0 and k0 % nk == 0 and nk % tk == 0
    qmap = lambda b, g, i: (b, q0 // tq + i, g)
    kmap = lambda b, g, i: (b, k0 // nk, g)
    in_specs = [pl.BlockSpec((1, tq, qw), qmap), pl.BlockSpec((1, nk, LANE), kmap), pl.BlockSpec((1, nk, LANE), kmap)]
    in_specs += [pl.BlockSpec(a.shape, lambda b, g, i: (0, 0)) for a in extra]
    args = [q, k, v, *extra]
    aliases = {}
    if into is not None:
        in_specs.append(pl.BlockSpec(memory_space=pl.ANY))
        aliases = {len(args): 0}
        args.append(into)

    def kern(*refs):
        if into is not None:
            refs = refs[:len(args) - 1] + refs[len(args):]
        _attn_kernel(*refs, mode=mode, tq=tq, tk=tk, n_tiles=nk // tk)

    return pl.pallas_call(
        kern,
        grid=(bsz, wq // qw, nq // tq),
        in_specs=in_specs,
        out_specs=pl.BlockSpec((1, tq, qw), qmap),
        out_shape=jax.ShapeDtypeStruct((bsz, t_q, wq), BF16),
        scratch_shapes=[pltpu.VMEM((rows, LANE), BF16), pltpu.VMEM((2, rows, tk), F32),
                        pltpu.VMEM((2, rows, LANE), F32), pltpu.VMEM((2, rows, LANE), F32),
                        pltpu.VMEM((rows, LANE), F32), pltpu.VMEM((rows, LANE), F32)],
        input_output_aliases=aliases,
        compiler_params=_cparams(("parallel", "parallel", "parallel")),
        name="attn_" + mode,
    )(*args)


def _mlstm_kernel(qf_ref, pf_ref, nf_ref, vf_ref, gf_ref, qb_ref, pb_ref, nb_ref, vb_ref, gb_ref, cw_ref, cb_ref,
                  hf_ref, hb_ref, c_sc, m_sc, *, n_lat_chunks, n_chunks):
    L, dh = B_CHUNK, B_HEAD_DIM
    c = pl.program_id(1)
    n_ctx_chunks = n_chunks - n_lat_chunks
    fwd_chunk = jnp.where(c < n_ctx_chunks, n_lat_chunks + c, c - n_ctx_chunks)
    bwd_chunk = n_chunks - 1 - c

    @pl.when(c == 0)
    def _():
        c_sc[...] = jnp.zeros(c_sc.shape, F32)
        m_sc[...] = jnp.zeros(m_sc.shape, F32)

    row = lax.broadcasted_iota(jnp.int32, (L, L), 0)
    col = lax.broadcasted_iota(jnp.int32, (L, L), 1)
    upper = (row <= col).astype(F32)
    lower = (col <= row).astype(F32)
    rid = lax.broadcasted_iota(jnp.int32, (L, 1), 0)
    glane = lax.broadcasted_iota(jnp.int32, (L, LANE), 1)
    is_f = (glane // B_HEADS) % 2 == 1
    ones_col = (lax.broadcasted_iota(jnp.int32, (L, dh), 1) == 0).astype(BF16)
    hi = lax.Precision.HIGHEST

    def direction(qk_ref, p_ref, n_ref, v_ref, g_ref, h_ref, chunk, reverse):
        first = (chunk == 0) | (chunk == n_lat_chunks)
        last = (chunk == n_lat_chunks - 1) | (chunk == n_chunks - 1)
        x = qk_ref[0]
        prev_row = jnp.where(first, 0.0, p_ref[0, SUBLANE - 1:SUBLANE, :])
        next_row = jnp.where(last, 0.0, n_ref[0, 0:1, :])
        x_prev = jnp.where(rid == 0, prev_row, pltpu.roll(x, 1, 0))
        x_next = jnp.where(rid == L - 1, next_row, pltpu.roll(x, L - 1, 0))
        y = x_prev * cw_ref[0:1, :] + x * cw_ref[1:2, :] + x_next * cw_ref[2:3, :] + cb_ref[...]
        y = y * jax.nn.sigmoid(y)
        g = g_ref[0]
        g = jnp.where(is_f, jnp.minimum(g, 0.0) - jnp.log(1.0 + jnp.exp(-jnp.abs(g))), g)
        g_t = jnp.transpose(g)
        cum_r = jnp.dot(g_t, upper, preferred_element_type=F32, precision=hi)
        cum_c = jnp.dot(lower, g, preferred_element_type=F32, precision=hi)
        if reverse:
            cum_r = cum_r[:, L - 1:L] - cum_r + g_t
            cum_c = cum_c[L - 1:L, :] - cum_c + g
        gi = 2 * B_HEADS if reverse else 0
        mask = (col >= row) if reverse else (col <= row)
        for h in range(B_HEADS):
            idx = (B_HEADS if reverse else 0) + h
            q = _blk(y, h).astype(BF16)
            k = _blk(y, B_HEADS + h) * (dh ** -0.5)
            v_ext = jnp.concatenate([_blk(v_ref[0], h), ones_col], axis=1)
            i_r = g_t[gi + h:gi + h + 1, :]
            b_r = cum_r[gi + B_HEADS + h:gi + B_HEADS + h + 1, :]
            i_c = g[:, gi + h:gi + h + 1]
            b_c = cum_c[:, gi + B_HEADS + h:gi + B_HEADS + h + 1]
            m_prev = m_sc[idx][:, 0:1]
            c_ext = c_sc[idx]

            log_w = jnp.where(mask, b_c - b_r + i_r, -jnp.inf)
            log_inter = b_c + m_prev
            m_row = jnp.maximum(log_inter, jnp.max(log_w, axis=1, keepdims=True))
            w_intra = jnp.exp(log_w - m_row)
            a_inter = jnp.exp(log_inter - m_row)
            sc = lax.dot_general(q, k.astype(BF16), (((1,), (1,)), ((), ())), preferred_element_type=F32) * w_intra
            q_c = jnp.dot(q, c_ext.astype(BF16), preferred_element_type=F32)
            s_v = jnp.dot(sc.astype(BF16), v_ext, preferred_element_type=F32)
            num = a_inter * q_c[:, :dh] + s_v[:, :dh]
            den = a_inter * q_c[:, dh:dh + 1] + s_v[:, dh:dh + 1]
            h_ref[0, :, h * dh:(h + 1) * dh] = num / jnp.maximum(jnp.abs(den), jnp.exp(-m_row))

            b_last = b_r[:, 0:1] if reverse else b_r[:, L - 1:L]
            m_new = jnp.maximum(b_last + m_prev, jnp.max(b_last - b_r + i_r, axis=1, keepdims=True))
            g_c = jnp.exp(b_last - b_c + i_c - m_new)
            decay = jnp.exp(b_last + m_prev - m_new)
            upd = jnp.dot(jnp.transpose(k).astype(BF16), (g_c * v_ext.astype(F32)).astype(BF16),
                          preferred_element_type=F32)
            c_sc[idx] = decay * c_ext + upd
            m_sc[idx] = jnp.broadcast_to(m_new, (1, LANE))

    direction(qf_ref, pf_ref, nf_ref, vf_ref, gf_ref, hf_ref, fwd_chunk, False)
    direction(qb_ref, pb_ref, nb_ref, vb_ref, gb_ref, hb_ref, bwd_chunk, True)


def _mlstm(qk, vb, gates, conv_w, conv_b, n_lat):
    bsz, t, _ = qk.shape
    L = B_CHUNK
    n_chunks, n_lat_chunks = t // L, n_lat // L
    n_ctx_chunks = n_chunks - n_lat_chunks
    per = L // SUBLANE
    fwd = lambda c: jnp.where(c < n_ctx_chunks, n_lat_chunks + c, c - n_ctx_chunks)
    bwd = lambda c: n_chunks - 1 - c

    def specs(order):
        return [
            pl.BlockSpec((1, L, 2 * HALF), lambda b, c: (b, order(c), 0)),
            pl.BlockSpec((1, SUBLANE, 2 * HALF), lambda b, c: (b, jnp.maximum(order(c) * per - 1, 0), 0)),
            pl.BlockSpec((1, SUBLANE, 2 * HALF), lambda b, c: (b, jnp.minimum((order(c) + 1) * per, t // SUBLANE - 1), 0)),
            pl.BlockSpec((1, L, HALF), lambda b, c: (b, order(c), 0)),
            pl.BlockSpec((1, L, LANE), lambda b, c: (b, order(c), 0)),
        ]

    const = lambda a: pl.BlockSpec(a.shape, lambda b, c: (0, 0))
    return pl.pallas_call(
        functools.partial(_mlstm_kernel, n_lat_chunks=n_lat_chunks, n_chunks=n_chunks),
        grid=(bsz, n_chunks),
        in_specs=specs(fwd) + specs(bwd) + [const(conv_w), const(conv_b)],
        out_specs=[pl.BlockSpec((1, L, HALF), lambda b, c: (b, fwd(c), 0)),
                   pl.BlockSpec((1, L, HALF), lambda b, c: (b, bwd(c), 0))],
        out_shape=[jax.ShapeDtypeStruct((bsz, t, HALF), F32)] * 2,
        scratch_shapes=[pltpu.VMEM((2 * B_HEADS, B_HEAD_DIM, 2 * B_HEAD_DIM), F32),
                        pltpu.VMEM((2 * B_HEADS, 1, LANE), F32)],
        compiler_params=_cparams(("parallel", "arbitrary")),
        name="mlstm",
    )(qk, qk, qk, vb, gates, qk, qk, qk, vb, gates, conv_w, conv_b)


def _route(logits_t, bias):
    per = N_EXPERTS // N_GROUPS
    idx = lax.broadcasted_iota(jnp.int32, logits_t.shape, 0)
    idx_f = idx.astype(F32)
    scores = jax.nn.sigmoid(logits_t)
    sel = scores + bias

    def first_max(a):
        m = jnp.max(a, axis=0, keepdims=True)
        return m, jnp.min(jnp.where(a == m, idx_f, float(N_EXPERTS)), axis=0, keepdims=True)

    loc_f = lax.broadcasted_iota(jnp.int32, (per, logits_t.shape[1]), 0).astype(F32)
    best, g_idx = None, None
    for g in range(N_GROUPS):
        a = sel[g * per:(g + 1) * per]
        m1 = jnp.max(a, axis=0, keepdims=True)
        i1 = jnp.min(jnp.where(a == m1, loc_f, float(per)), axis=0, keepdims=True)
        top2 = m1 + jnp.max(jnp.where(loc_f == i1, -jnp.inf, a), axis=0, keepdims=True)
        if g == 0:
            best, g_idx = top2, jnp.zeros_like(top2)
        else:
            better = top2 > best
            best, g_idx = jnp.where(better, top2, best), jnp.where(better, float(g), g_idx)
    a = jnp.where((idx // per).astype(F32) == g_idx, sel, -jnp.inf)
    _, e1 = first_max(a)
    _, e2 = first_max(jnp.where(idx_f == e1, -jnp.inf, a))
    s1 = jnp.sum(jnp.where(idx_f == e1, scores, 0.0), axis=0, keepdims=True)
    s2 = jnp.sum(jnp.where(idx_f == e2, scores, 0.0), axis=0, keepdims=True)
    return e1, e2, s1 / (s1 + s2), s2 / (s1 + s2)


def _out_tail(mix, x_ref, mod_ref, g_ref, w_ref, wrh_ref, wrl_ref, br_ref, xo_ref, f_ref, rt_ref):
    o = jnp.dot(mix, w_ref[...], preferred_element_type=F32)
    xn = x_ref[0] + mod_ref[0, 0:1, :] * o
    xo_ref[0] = xn
    y = xn * lax.rsqrt(jnp.mean(xn * xn, axis=-1, keepdims=True) + EPS) * g_ref[...]
    f = y * (1.0 + mod_ref[0, 2:3, :]) + mod_ref[0, 1:2, :]
    f_hi = f.astype(BF16)
    f_ref[0] = f_hi
    f_lo = (f - f_hi.astype(F32)).astype(BF16)
    logits = (jnp.dot(f_hi, wrh_ref[...], preferred_element_type=F32)
              + jnp.dot(f_lo, wrh_ref[...], preferred_element_type=F32)
              + jnp.dot(f_hi, wrl_ref[...], preferred_element_type=F32))
    rows = _route(jnp.transpose(logits)[:N_EXPERTS], br_ref[...])
    row_id = lax.broadcasted_iota(jnp.int32, (SUBLANE, logits.shape[0]), 0)
    rt_ref[0] = functools.reduce(lambda rec, kr: jnp.where(row_id == kr[0], kr[1], rec), enumerate(rows),
                                 jnp.zeros((SUBLANE, logits.shape[0]), F32))


def _outproj_even_kernel(oa_ref, hf_ref, hb_ref, ob_ref, gh_ref, *tail):
    parts = [oa_ref[0]]
    for h in range(B_HEADS):
        hs = _blk(hf_ref[0], h) + _blk(hb_ref[0], h)
        hn = hs * lax.rsqrt(jnp.mean(hs * hs, axis=1, keepdims=True) + EPS) * _blk(gh_ref[...], h)
        parts.append((hn * jax.nn.sigmoid(_blk(ob_ref[0], h))).astype(BF16))
    _out_tail(jnp.concatenate(parts, axis=1), *tail)


def _outproj_odd_kernel(oc_ref, od_ref, *tail):
    _out_tail(jnp.concatenate([oc_ref[0], od_ref[0]], axis=1), *tail)


def _outproj(kern, heads, consts, x, mod, g, w, wr_hi, wr_lo, b_router, n_rows, n_lat_tiles):
    bsz, _, d = x.shape
    tm = ROW_TILE
    row = lambda b, i: (b, i, 0)
    return pl.pallas_call(
        kern,
        grid=(bsz, n_rows // tm),
        in_specs=([pl.BlockSpec((1, tm, HALF), row) for _ in heads] + [_const_spec(a) for a in consts]
                  + [pl.BlockSpec((1, tm, d), row),
                     pl.BlockSpec((1, 3, d), lambda b, i: (jnp.where(i >= n_lat_tiles, bsz, b), 0, 0)),
                     pl.BlockSpec((1, d), lambda b, i: (0, 0)), _const_spec(w), _const_spec(wr_hi),
                     _const_spec(wr_lo), _const_spec(b_router)]),
        out_specs=[pl.BlockSpec((1, tm, d), row), pl.BlockSpec((1, tm, d), row),
                   pl.BlockSpec((1, SUBLANE, tm), lambda b, i: (b, 0, i))],
        out_shape=[jax.ShapeDtypeStruct((bsz, n_rows, d), F32), jax.ShapeDtypeStruct((bsz, n_rows, d), BF16),
                   jax.ShapeDtypeStruct((bsz, SUBLANE, n_rows), F32)],
        compiler_params=_cparams(("parallel", "parallel")),
        name="outproj",
    )(*heads, *consts, x, mod, g.reshape(1, d), w, wr_hi, wr_lo, b_router)


def _expert_kernel(blk_e_ref, n_used_ref, x_ref, w1_ref, w3_ref, w2_ref, y_ref, w1_sc, w3_sc, w2_sc):
    i = pl.program_id(0)

    @pl.when((i == 0) | (blk_e_ref[i] != blk_e_ref[jnp.maximum(i - 1, 0)]))
    def _():
        w1_sc[...] = w1_ref[0, 0].astype(BF16)
        w3_sc[...] = w3_ref[0, 0].astype(BF16)
        w2_sc[...] = w2_ref[0, 0].astype(BF16)

    @pl.when(i < n_used_ref[0])
    def _():
        xb = x_ref[...]
        h1 = jnp.dot(xb, w1_sc[...], preferred_element_type=F32)
        h3 = jnp.dot(xb, w3_sc[...], preferred_element_type=F32)
        a = (h1 * jax.nn.sigmoid(h1) * h3).astype(BF16)
        y_ref[...] = jnp.dot(a, w2_sc[...], preferred_element_type=F32).astype(y_ref.dtype)

    @pl.when(i >= n_used_ref[0])
    def _():
        y_ref[...] = jnp.zeros(y_ref.shape, y_ref.dtype)


def _experts(xin, blk_e, n_used, layer, w1, w3, w2):
    n_rows, d = xin.shape
    de = w1.shape[-1]
    nb = n_rows // MOE_ROWS
    grid_spec = pltpu.PrefetchScalarGridSpec(
        num_scalar_prefetch=2,
        grid=(nb,),
        in_specs=[
            pl.BlockSpec((MOE_ROWS, d), lambda i, be, nu: (i, 0)),
            pl.BlockSpec((1, 1, d, de), lambda i, be, nu: (layer, be[i], 0, 0)),
            pl.BlockSpec((1, 1, d, de), lambda i, be, nu: (layer, be[i], 0, 0)),
            pl.BlockSpec((1, 1, de, d), lambda i, be, nu: (layer, be[i], 0, 0)),
        ],
        out_specs=pl.BlockSpec((MOE_ROWS, d), lambda i, be, nu: (i, 0)),
        scratch_shapes=[pltpu.VMEM((d, de), BF16), pltpu.VMEM((d, de), BF16), pltpu.VMEM((de, d), BF16)],
    )
    return pl.pallas_call(
        _expert_kernel,
        grid_spec=grid_spec,
        out_shape=jax.ShapeDtypeStruct((n_rows, d), BF16),
        compiler_params=_cparams(("arbitrary",)),
        name="experts",
    )(blk_e, n_used, xin, w1, w3, w2)


def _moe(f_tok, route, layer, w1, w3, w2):
    n_tok, d = f_tok.shape
    e1, e2 = route[:, 0].reshape(-1).astype(jnp.int32), route[:, 1].reshape(-1).astype(jnp.int32)
    gate = jnp.stack([route[:, 2].reshape(-1), route[:, 3].reshape(-1)], axis=-1)

    n_assign = n_tok * TOP_K
    e_flat = jnp.concatenate([e1, e2]).astype(jnp.int32)
    assert n_assign % MOE_ROWS == 0
    onehot = (e_flat[:, None] == jnp.arange(N_EXPERTS, dtype=jnp.int32)[None, :]).astype(F32)
    tri = jnp.tril(jnp.ones((MOE_ROWS, MOE_ROWS), F32))
    within = jnp.einsum('ij,bjk->bik', tri, onehot.reshape(-1, MOE_ROWS, N_EXPERTS), precision=lax.Precision.HIGHEST)
    totals = within[:, -1, :]
    before = jnp.cumsum(totals, axis=0) - totals
    csum = (within + before[:, None, :]).reshape(n_assign, N_EXPERTS)
    counts = jnp.sum(totals, axis=0).astype(jnp.int32)
    rank = jnp.sum(csum * onehot, axis=1).astype(jnp.int32) - 1
    padded = (counts + MOE_ROWS - 1) // MOE_ROWS * MOE_ROWS
    pend = jnp.cumsum(padded)
    pstart = pend - padded
    dest = pstart[e_flat] + rank
    n_rows = -(-n_assign // MOE_ROWS) * MOE_ROWS + N_EXPERTS * MOE_ROWS
    nb = n_rows // MOE_ROWS
    blk_row = jnp.arange(nb, dtype=jnp.int32) * MOE_ROWS
    blk_e = jnp.minimum(jnp.sum((pend[None, :] <= blk_row[:, None]).astype(jnp.int32), axis=1), N_EXPERTS - 1)
    order = jnp.argsort(e_flat, stable=True).astype(jnp.int32)
    row = jnp.arange(n_rows, dtype=jnp.int32)
    row_e = jnp.repeat(blk_e, MOE_ROWS)
    pos = row - pstart[row_e]
    src = jnp.minimum((jnp.cumsum(counts) - counts)[row_e] + pos, n_assign - 1)
    row_tok = jnp.where(pos < counts[row_e], order[src], row) % n_tok
    xin = f_tok[row_tok]
    n_used = (pend[-1:] // MOE_ROWS).astype(jnp.int32)
    y = _experts(xin, blk_e, n_used, layer, w1, w3, w2)
    return y[dest[:n_tok]], y[dest[n_tok:]], gate


def _combine_kernel(x_ref, y0_ref, y1_ref, gate_ref, g2_ref, o_ref):
    gate = gate_ref[0]
    y = gate[:, 0:1] * y0_ref[0].astype(F32) + gate[:, 1:2] * y1_ref[0].astype(F32)
    o_ref[0] = x_ref[0] + g2_ref[0] * y


def _combine(x, y0, y1, gate, g2, n_lat_tiles):
    bsz, n_rows, d = x.shape
    tm = ROW_TILE
    row = lambda b, i: (b, i, 0)
    return pl.pallas_call(
        _combine_kernel,
        grid=(bsz, n_rows // tm),
        in_specs=[pl.BlockSpec((1, tm, d), row), pl.BlockSpec((1, tm, d), row), pl.BlockSpec((1, tm, d), row),
                  pl.BlockSpec((1, tm, TOP_K), row),
                  pl.BlockSpec((1, 1, d), lambda b, i: (jnp.where(i >= n_lat_tiles, bsz, b), 0, 0))],
        out_specs=pl.BlockSpec((1, tm, d), row),
        out_shape=jax.ShapeDtypeStruct((bsz, n_rows, d), F32),
        compiler_params=_cparams(("parallel", "parallel")),
        name="combine",
    )(x, y0, y1, gate, g2)


def _rope_lanes(n_lat, n_ctx, rot_dim):
    rows = n_lat // GRID_W
    r = jnp.repeat(jnp.arange(rows), GRID_W)
    c = jnp.tile(jnp.arange(GRID_W), rows)
    n_freq = rot_dim // 4
    inv = ROPE_THETA ** (-jnp.arange(n_freq, dtype=F32) / n_freq)
    ang = jnp.concatenate([r[:, None] * inv, c[:, None] * inv], axis=-1)
    cos = jnp.repeat(jnp.cos(ang), 2, axis=1)
    sin = jnp.repeat(jnp.sin(ang), 2, axis=1) * jnp.tile(jnp.array([-1.0, 1.0], F32), rot_dim // 2)
    pad = ((0, n_ctx), (0, 0))
    return jnp.pad(cos, pad, constant_values=1.0), jnp.pad(sin, pad)


def _pad_cols(w, n):
    return jnp.pad(w, ((0, 0), (0, n - w.shape[1])))


def _pad_row(v, n):
    return jnp.pad(v, (0, n - v.shape[0])).reshape(1, n)


def _split_bf16(w):
    hi = lax.bitcast_convert_type(lax.bitcast_convert_type(w, jnp.uint32) & jnp.uint32(0xFFFF0000), F32)
    return hi.astype(BF16), (w - hi).astype(BF16)


def _round_up(n, m):
    return -(-n // m) * m


def _head_blocks(w, n_heads, width, start, size):
    w = w.reshape(w.shape[0], n_heads, width)[:, :, start:start + size]
    return jnp.pad(w, ((0, 0), (0, 0), (0, LANE - size))).reshape(w.shape[0], n_heads * LANE)


def kernel(x, c, ctx, c_ctx, w_ada, b_ada, g_mix, g_ffn, e_w_in, e_w_out, e_g_q, e_g_k, e_conv_w, e_conv_b, e_b_gates, e_g_h, o_w_in, o_w_out, o_g_qc, o_g_kc, o_lam, o_g_sub, o_g_cq, o_w_uq, o_g_ckv, o_w_ukv, o_g_qd, o_g_kd, w_router, b_router, w1, w3, w2):
    bsz, n_lat, d = x.shape
    n_ctx = ctx.shape[1]
    t = n_lat + n_ctx
    depth = w_ada.shape[0]
    assert n_lat % ROW_TILE == 0 and n_ctx % ROW_TILE == 0 and n_ctx % B_CHUNK == 0
    n_lat_tiles = n_lat // ROW_TILE
    hd = D_NOPE + D_ROPE

    cos64, sin64 = _rope_lanes(n_lat, n_ctx, A_HEAD_DIM)
    cos64, sin64 = jnp.tile(cos64, (1, 2)), jnp.tile(sin64, (1, 2))
    cos_d, sin_d = _rope_lanes(n_lat, n_ctx, D_ROPE)
    cos_d = jnp.pad(cos_d, ((0, 0), (D_NOPE, LANE - hd)), constant_values=1.0)
    sin_d = jnp.pad(sin_d, ((0, 0), (D_NOPE, LANE - hd)))
    half = jnp.arange(LANE) // HEAD64
    bd64 = (half[:, None] == half[None, :]).astype(BF16)
    wr_hi, wr_lo = _split_bf16(_pad_cols(w_router.astype(F32), LANE))
    cond = jnp.concatenate([c, c_ctx[None, :]], axis=0)
    x_all = jnp.concatenate([x, ctx], axis=1)

    for layer in range(depth):
        last = layer == depth - 1
        j = layer // 2
        mod = (jax.nn.silu(cond) @ w_ada[layer] + b_ada[layer]).reshape(bsz + 1, 6, d)
        sh1, sc1, g1, sh2, sc2, g2 = [mod[:, i] for i in range(6)]
        mod_in = jnp.stack([sh1, sc1], axis=1)
        mod_out = jnp.stack([g1, sh2, sc2], axis=1)
        n_rows = n_lat if last else t
        if layer % 2 == 0:
            w_in = _pad_cols(e_w_in[j], _round_up(e_w_in.shape[-1], LANE)).astype(BF16)
            q, kd, vd, qk, vb, ob, gates = _inproj_even(
                x_all, mod_in, g_mix[layer], w_in, cos64, sin64, bd64, jnp.tile(e_g_q[j], 2).reshape(1, LANE),
                jnp.tile(e_g_k[j], 2).reshape(1, LANE), _pad_row(e_b_gates[j], LANE), n_lat_tiles)
            oa = _attn(q, kd, vd, mode="gqa", tq=ATTN_ROWS // 4, tk=KEY_TILE, q_rows=(0, n_lat), k_rows=(0, t))
            if not last:
                oa = _attn(q, kd, vd, mode="gqa", tq=ATTN_ROWS // 4, tk=KEY_TILE, q_rows=(n_lat, n_ctx), k_rows=(n_lat, n_ctx),
                           into=oa)
            hf, hb = _mlstm(qk, vb, gates, e_conv_w[j], e_conv_b[j].reshape(1, -1), n_lat)
            heads, consts, kern = [oa, hf, hb, ob], [e_g_h[j].reshape(1, HALF)], _outproj_even_kernel
            w_out = e_w_out[j]
        else:
            assert last, "the odd mixer is only wired as the final layer"
            lam_init = LAM_INIT_BASE[0] - LAM_INIT_BASE[1] * math.exp(-LAM_INIT_BASE[2] * layer)
            lq1, lk1, lq2, lk2 = o_lam[j].astype(F32)
            lam = jnp.exp(jnp.sum(lq1 * lk1)) - jnp.exp(jnp.sum(lq2 * lk2)) + lam_init
            w_in = _pad_cols(o_w_in[j], _round_up(o_w_in.shape[-1], LANE)).astype(BF16)
            consts = [bd64, jnp.tile(o_g_qc[j], 2).reshape(1, LANE), jnp.tile(o_g_kc[j], 2).reshape(1, LANE),
                      o_g_cq[j].reshape(1, D_Q_LORA), _head_blocks(o_w_uq[j], D_HEADS, hd, 0, hd).astype(BF16),
                      _pad_row(o_g_qd[j], LANE), o_g_ckv[j].reshape(1, D_KV_LORA),
                      _head_blocks(o_w_ukv[j], D_HEADS, D_NOPE + D_V_DIM, 0, D_NOPE).astype(BF16),
                      _head_blocks(o_w_ukv[j], D_HEADS, D_NOPE + D_V_DIM, D_NOPE, D_V_DIM).astype(BF16),
                      _pad_row(o_g_kd[j], LANE)]
            qc, kc, vc, qd, kdl, vdl = _inproj_odd(x_all, mod_in, g_mix[layer], w_in, cos64, sin64, cos_d, sin_d,
                                                   consts, n_lat_tiles)
            extra = (jnp.full((1, LANE), lam, F32), (o_g_sub[j] * (1.0 - lam_init)).reshape(1, LANE))
            oc = _attn(qc, kc, vc, mode="diff", tq=ATTN_ROWS // 2, tk=KEY_TILE, q_rows=(0, n_lat), k_rows=(0, t), extra=extra)
            od = _attn(qd, kdl, vdl, mode="mla", tq=ATTN_ROWS, tk=KEY_TILE, q_rows=(0, n_lat), k_rows=(0, t))
            heads, consts, kern = [oc, od], [], _outproj_odd_kernel
            w_out = o_w_out[j]
        x_new, f, route = _outproj(kern, heads, consts, x_all, mod_out, g_ffn[layer], w_out.astype(BF16), wr_hi, wr_lo,
                                   jnp.broadcast_to(b_router.astype(F32)[:, None], (N_EXPERTS, ROW_TILE)), n_rows, n_lat_tiles)
        y0, y1, gate = _moe(f.reshape(bsz * n_rows, d), route, layer, w1, w3, w2)
        x_all = _combine(x_new, y0.reshape(bsz, n_rows, d), y1.reshape(bsz, n_rows, d),
                         gate.reshape(bsz, n_rows, TOP_K), g2.reshape(bsz + 1, 1, d), n_lat_tiles)
    return x_all
```

```python
import functools
import math

import jax
import jax.numpy as jnp
import numpy as np
from jax import lax
from jax.experimental import pallas as pl
from jax.experimental.pallas import tpu as pltpu

F32 = jnp.float32
BF16 = jnp.bfloat16
EPS = 1e-6
LOG2E = 1.4426950408889634
NEG_BIG = -1e30
GRID_W = 64
ROPE_THETA = 10000.0
LAM_INIT_BASE = (0.8, 0.6, 0.3)

A_HEAD_DIM, A_HEADS, A_KV_HEADS = 64, 8, 2
B_HEAD_DIM, B_HEADS, B_CONV, B_CHUNK = 128, 4, 3, 128
C_HEAD_DIM, C_V_DIM, C_HEADS = 64, 128, 4
D_HEADS, D_Q_LORA, D_KV_LORA, D_NOPE, D_ROPE, D_V_DIM = 4, 256, 128, 64, 32, 128
N_EXPERTS, N_GROUPS, TOP_K, D_EXPERT = 32, 4, 2, 512
HALF = 512
HEAD64 = 64

LANE = 128
SUBLANE = 8
ROW_TILE = 256
MOE_ROWS = 512
KEY_TILE = 1280
ATTN_ROWS = 2048
ATTN_UNROLL = 2
VMEM_LIMIT = 60 * 1024 * 1024


def _cparams(sem):
    return pltpu.CompilerParams(dimension_semantics=sem, vmem_limit_bytes=VMEM_LIMIT)


def _blk(a, i):
    return a[:, i * LANE:(i + 1) * LANE]


def _modulated(x_ref, mod_ref, g_ref):
    x = x_ref[0]
    y = x * lax.rsqrt(jnp.mean(x * x, axis=-1, keepdims=True) + EPS) * g_ref[...]
    return (y * (1.0 + mod_ref[0, 1:2, :]) + mod_ref[0, 0:1, :]).astype(BF16)


def _pairswap(b, even):
    return jnp.where(even, pltpu.roll(b, LANE - 1, 1), pltpu.roll(b, 1, 1))


def _rope(b, cos, sin, even):
    return b * cos + _pairswap(b, even) * sin


def _head64_norm(b, bd, gain):
    ss = jnp.dot((b * b).astype(BF16), bd, preferred_element_type=F32)
    return b * lax.rsqrt(ss * (1.0 / HEAD64) + EPS) * gain


def _lane_norm(b, n_real, gain):
    ss = jnp.sum(b * b, axis=1, keepdims=True)
    return b * lax.rsqrt(ss * (1.0 / n_real) + EPS) * gain


def _inproj_even_kernel(x_ref, mod_ref, g_ref, w_ref, cos_ref, sin_ref, bd_ref, gq_ref, gk_ref, bg_ref,
                        q_ref, k_ref, v_ref, qk_ref, vb_ref, ob_ref, gt_ref):
    z = jnp.dot(_modulated(x_ref, mod_ref, g_ref), w_ref[...], preferred_element_type=F32)
    lane = lax.broadcasted_iota(jnp.int32, (z.shape[0], LANE), 1)
    even = (lane & 1) == 0
    lo = lane < HEAD64
    cos, sin, bd = cos_ref[...], sin_ref[...], bd_ref[...]
    q_scale = A_HEAD_DIM ** -0.5 * LOG2E
    for c in range(4):
        qb = _rope(_head64_norm(_blk(z, c), bd, gq_ref[...]), cos, sin, even) * q_scale
        q_ref[0, :, c * LANE:(c + 1) * LANE] = qb.astype(BF16)
    kn = _rope(_head64_norm(_blk(z, 4), bd, gk_ref[...]), cos, sin, even)
    sw = pltpu.roll(kn, HEAD64, 1)
    k_ref[0] = jnp.concatenate([jnp.where(lo, kn, sw), jnp.where(lo, sw, kn)], axis=1).astype(BF16)
    va = _blk(z, 5)
    one = (lane == HEAD64).astype(F32)
    v_ref[0] = jnp.concatenate([jnp.where(lo, va, one), jnp.where(lo, pltpu.roll(va, HEAD64, 1), one)],
                               axis=1).astype(BF16)
    qk_ref[0] = z[:, 6 * LANE:14 * LANE]
    vb_ref[0] = z[:, 14 * LANE:18 * LANE].astype(BF16)
    ob_ref[0] = z[:, 18 * LANE:22 * LANE]
    gt_ref[0] = _blk(z, 22) + bg_ref[...]


def _row_specs(tm, d, bsz, n_lat_tiles):
    return [pl.BlockSpec((1, tm, d), lambda b, i: (b, i, 0)),
            pl.BlockSpec((1, 2, d), lambda b, i: (jnp.where(i >= n_lat_tiles, bsz, b), 0, 0)),
            pl.BlockSpec((1, d), lambda b, i: (0, 0))]


def _const_spec(a):
    return pl.BlockSpec(a.shape, lambda b, i: (0,) * a.ndim)


def _inproj_even(x, mod, g, w, cos, sin, bd, gq, gk, bg, n_lat_tiles):
    bsz, t, d = x.shape
    tm = ROW_TILE
    tab = pl.BlockSpec((tm, LANE), lambda b, i: (i, 0))
    consts = [bd, gq, gk, bg]
    widths = [(4 * LANE, BF16), (2 * LANE, BF16), (2 * LANE, BF16), (8 * LANE, F32), (4 * LANE, BF16),
              (4 * LANE, F32), (LANE, F32)]
    return pl.pallas_call(
        _inproj_even_kernel,
        grid=(bsz, t // tm),
        in_specs=_row_specs(tm, d, bsz, n_lat_tiles) + [_const_spec(w), tab, tab] + [_const_spec(a) for a in consts],
        out_specs=[pl.BlockSpec((1, tm, n), lambda b, i: (b, i, 0)) for n, _ in widths],
        out_shape=[jax.ShapeDtypeStruct((bsz, t, n), dt) for n, dt in widths],
        compiler_params=_cparams(("parallel", "parallel")),
        name="inproj_even",
    )(x, mod, g.reshape(1, d), w, cos, sin, *consts)


def _inproj_odd_kernel(x_ref, mod_ref, g_ref, w_ref, cos_ref, sin_ref, cosd_ref, sind_ref, bd_ref, gqc_ref, gkc_ref,
                       gcq_ref, wuq_ref, gqd_ref, gckv_ref, wkn_ref, wvd_ref, gkd_ref,
                       qc_ref, kc_ref, vc_ref, qd_ref, kd_ref, vd_ref):
    z = jnp.dot(_modulated(x_ref, mod_ref, g_ref), w_ref[...], preferred_element_type=F32)
    lane = lax.broadcasted_iota(jnp.int32, (z.shape[0], LANE), 1)
    even = (lane & 1) == 0
    cos, sin, bd = cos_ref[...], sin_ref[...], bd_ref[...]
    cosd, sind = cosd_ref[...], sind_ref[...]
    c_scale = C_HEAD_DIM ** -0.5 * LOG2E
    d_scale = (D_NOPE + D_ROPE) ** -0.5 * LOG2E
    for c in range(4):
        sl = slice(c * LANE, (c + 1) * LANE)
        qc_ref[0, :, sl] = (_rope(_head64_norm(_blk(z, c), bd, gqc_ref[...]), cos, sin, even) * c_scale).astype(BF16)
        kc_ref[0, :, sl] = _rope(_head64_norm(_blk(z, 4 + c), bd, gkc_ref[...]), cos, sin, even).astype(BF16)
    vc_ref[0] = z[:, 8 * LANE:12 * LANE].astype(BF16)
    cq = z[:, 12 * LANE:14 * LANE]
    cq = cq * lax.rsqrt(jnp.mean(cq * cq, axis=-1, keepdims=True) + EPS) * gcq_ref[...]
    qd = jnp.dot(cq.astype(BF16), wuq_ref[...], preferred_element_type=F32)
    ckv = _blk(z, 14)
    ckv = (ckv * lax.rsqrt(jnp.mean(ckv * ckv, axis=-1, keepdims=True) + EPS) * gckv_ref[...]).astype(BF16)
    kn = jnp.dot(ckv, wkn_ref[...], preferred_element_type=F32)
    vd_ref[0] = jnp.dot(ckv, wvd_ref[...], preferred_element_type=F32).astype(BF16)
    kr = pltpu.roll(_blk(z, 15), D_NOPE, 1)
    hd = D_NOPE + D_ROPE
    for c in range(4):
        sl = slice(c * LANE, (c + 1) * LANE)
        qd_ref[0, :, sl] = (_rope(_lane_norm(_blk(qd, c), hd, gqd_ref[...]), cosd, sind, even) * d_scale).astype(BF16)
        kd_ref[0, :, sl] = _rope(_lane_norm(_blk(kn, c) + kr, hd, gkd_ref[...]), cosd, sind, even).astype(BF16)


def _inproj_odd(x, mod, g, w, cos, sin, cosd, sind, consts, n_lat_tiles):
    bsz, t, d = x.shape
    tm = ROW_TILE
    tab = pl.BlockSpec((tm, LANE), lambda b, i: (i, 0))
    return pl.pallas_call(
        _inproj_odd_kernel,
        grid=(bsz, t // tm),
        in_specs=(_row_specs(tm, d, bsz, n_lat_tiles) + [_const_spec(w), tab, tab, tab, tab]
                  + [_const_spec(a) for a in consts]),
        out_specs=[pl.BlockSpec((1, tm, HALF), lambda b, i: (b, i, 0)) for _ in range(6)],
        out_shape=[jax.ShapeDtypeStruct((bsz, t, HALF), BF16) for _ in range(6)],
        compiler_params=_cparams(("parallel", "parallel")),
        name="inproj_odd",
    )(x, mod, g.reshape(1, d), w, cos, sin, cosd, sind, *consts)


def _attn_kernel(*refs, mode, tq, tk, n_tiles):
    if mode == "diff":
        q_ref, k_ref, v_ref, lam_ref, gsub_ref, o_ref, q_sc, s_sc, m_sc, a_sc, l_sc, acc_sc = refs
    else:
        q_ref, k_ref, v_ref = refs[:3]
        o_ref, q_sc, s_sc, m_sc, a_sc, l_sc, acc_sc = refs[-7:]
    rows = q_sc.shape[0]
    n_blk = tk // LANE
    lo = lax.broadcasted_iota(jnp.int32, (tq, LANE), 1) < HEAD64

    if mode == "mla":
        q_sc[...] = q_ref[0]
    else:
        for p in range(q_ref.shape[2] // LANE):
            qb = _blk(q_ref[0], p)
            q_sc[2 * p * tq:(2 * p + 1) * tq] = jnp.where(lo, qb, jnp.zeros_like(qb))
            q_sc[(2 * p + 1) * tq:(2 * p + 2) * tq] = jnp.where(lo, jnp.zeros_like(qb), qb)

    def score_stage(j, slot):
        start = pl.multiple_of(j * tk, LANE)
        s = lax.dot_general(q_sc[...], k_ref[0, pl.ds(start, tk), :], (((1,), (1,)), ((), ())),
                            preferred_element_type=F32)
        s_sc[slot] = s
        m_tile = jnp.max(functools.reduce(jnp.maximum, [_blk(s, c) for c in range(n_blk)]), axis=1, keepdims=True)
        m_prev = m_sc[1 - slot]
        m_new = jnp.maximum(m_prev, m_tile)
        m_sc[slot] = m_new
        a_sc[slot] = jnp.exp2(m_prev - m_new)

    def value_stage(j, slot):
        start = pl.multiple_of(j * tk, LANE)
        m = m_sc[slot]
        alpha = a_sc[slot]
        p_blocks = [jnp.exp2((s_sc[slot, :, c * LANE:(c + 1) * LANE] - m).astype(BF16)) for c in range(n_blk)]
        if mode != "gqa":
            l_sc[...] = alpha * l_sc[...] + functools.reduce(jnp.add, p_blocks).astype(F32)
        p = jnp.concatenate(p_blocks, axis=1)
        acc_sc[...] = alpha * acc_sc[...] + jnp.dot(p, v_ref[0, pl.ds(start, tk), :], preferred_element_type=F32)

    m_sc[1] = jnp.full((rows, LANE), NEG_BIG, F32)
    l_sc[...] = jnp.zeros((rows, LANE), F32)
    acc_sc[...] = jnp.zeros((rows, LANE), F32)
    score_stage(0, 0)
    peel = (n_tiles - 1) % ATTN_UNROLL
    for j in range(peel):
        value_stage(j, j % 2)
        score_stage(j + 1, (j + 1) % 2)

    def body(t, carry):
        for u in range(ATTN_UNROLL):
            j = ATTN_UNROLL * t + peel + u
            value_stage(j, (peel + u) % 2)
            score_stage(j + 1, (peel + u + 1) % 2)
        return carry

    lax.fori_loop(0, (n_tiles - 1) // ATTN_UNROLL, body, 0)
    value_stage(n_tiles - 1, (n_tiles - 1) % 2)

    acc = acc_sc[...]
    o = acc / (acc[:, HEAD64:HEAD64 + 1] if mode == "gqa" else jnp.sum(l_sc[...], axis=1, keepdims=True))
    head = lambda h: o[h * tq:(h + 1) * tq]
    if mode == "gqa":
        for p in range(o_ref.shape[2] // LANE):
            o_ref[0, :, p * LANE:(p + 1) * LANE] = jnp.where(
                lo, head(2 * p), pltpu.roll(head(2 * p + 1), HEAD64, 1)).astype(o_ref.dtype)
    elif mode == "diff":
        dlt = head(0) - lam_ref[...] * head(1)
        o_ref[0] = (dlt * lax.rsqrt(jnp.mean(dlt * dlt, axis=1, keepdims=True) + EPS) * gsub_ref[...]).astype(o_ref.dtype)
    else:
        o_ref[0] = o.astype(o_ref.dtype)


def _attn(q, k, v, *, mode, tq, tk, q_rows, k_rows, extra=(), into=None, out_rows=None):
    bsz, _, wq = q.shape
    qw = 2 * LANE if mode == "gqa" else LANE
    q0, nq = q_rows
    k0, nk = k_rows
    tq, tk = min(tq, nq), min(tk, nk)
    rows = {"gqa": 4, "diff": 2, "mla": 1}[mode] * tq
    assert q0 % tq == 0 and nq % tq == 0 and k0 % nk == 0 and nk % tk == 0
    qmap = lambda b, g, i: (b, q0 // tq + i, g)
    kmap = lambda b, g, i: (b, k0 // nk, g)
    in_specs = [pl.BlockSpec((1, tq, qw), qmap), pl.BlockSpec((1, nk, LANE), kmap), pl.BlockSpec((1, nk, LANE), kmap)]
    in_specs += [pl.BlockSpec(a.shape, lambda b, g, i: (0, 0)) for a in extra]
    args = [q, k, v, *extra]
    aliases = {}
    if into is not None:
        in_specs.append(pl.BlockSpec(memory_space=pl.ANY))
        aliases = {len(args): 0}
        args.append(into)

    def kern(*refs):
        if into is not None:
            refs = refs[:len(args) - 1] + refs[len(args):]
        _attn_kernel(*refs, mode=mode, tq=tq, tk=tk, n_tiles=nk // tk)

    return pl.pallas_call(
        kern,
        grid=(bsz, wq // qw, nq // tq),
        in_specs=in_specs,
        out_specs=pl.BlockSpec((1, tq, qw), qmap),
        out_shape=jax.ShapeDtypeStruct((bsz, out_rows or q_rows[0] + q_rows[1], wq), BF16),
        scratch_shapes=[pltpu.VMEM((rows, LANE), BF16), pltpu.VMEM((2, rows, tk), F32),
                        pltpu.VMEM((2, rows, LANE), F32), pltpu.VMEM((2, rows, LANE), F32),
                        pltpu.VMEM((rows, LANE), F32), pltpu.VMEM((rows, LANE), F32)],
        input_output_aliases=aliases,
        compiler_params=_cparams(("parallel", "parallel", "parallel")),
        name="attn_" + mode,
    )(*args)


def _mlstm_kernel(qf_ref, pf_ref, nf_ref, vf_ref, gf_ref, qb_ref, pb_ref, nb_ref, vb_ref, gb_ref, cw_ref, cb_ref,
                  hf_ref, hb_ref, c_sc, m_sc, *, n_lat_chunks, n_chunks):
    L, dh = B_CHUNK, B_HEAD_DIM
    c = pl.program_id(1)
    n_ctx_chunks = n_chunks - n_lat_chunks
    fwd_chunk = jnp.where(c < n_ctx_chunks, n_lat_chunks + c, c - n_ctx_chunks)
    bwd_chunk = n_chunks - 1 - c

    @pl.when(c == 0)
    def _():
        c_sc[...] = jnp.zeros(c_sc.shape, F32)
        m_sc[...] = jnp.zeros(m_sc.shape, F32)

    row = lax.broadcasted_iota(jnp.int32, (L, L), 0)
    col = lax.broadcasted_iota(jnp.int32, (L, L), 1)
    upper = (row <= col).astype(F32)
    lower = (col <= row).astype(F32)
    rid = lax.broadcasted_iota(jnp.int32, (L, 1), 0)
    glane = lax.broadcasted_iota(jnp.int32, (L, LANE), 1)
    is_f = (glane // B_HEADS) % 2 == 1
    ones_col = (lax.broadcasted_iota(jnp.int32, (L, dh), 1) == 0).astype(BF16)
    hi = lax.Precision.HIGHEST

    def direction(qk_ref, p_ref, n_ref, v_ref, g_ref, h_ref, chunk, reverse):
        first = (chunk == 0) | (chunk == n_lat_chunks)
        last = (chunk == n_lat_chunks - 1) | (chunk == n_chunks - 1)
        x = qk_ref[0]
        prev_row = jnp.where(first, 0.0, p_ref[0, SUBLANE - 1:SUBLANE, :])
        next_row = jnp.where(last, 0.0, n_ref[0, 0:1, :])
        x_prev = jnp.where(rid == 0, prev_row, pltpu.roll(x, 1, 0))
        x_next = jnp.where(rid == L - 1, next_row, pltpu.roll(x, L - 1, 0))
        y = x_prev * cw_ref[0:1, :] + x * cw_ref[1:2, :] + x_next * cw_ref[2:3, :] + cb_ref[...]
        y = y * jax.nn.sigmoid(y)
        g = g_ref[0]
        g = jnp.where(is_f, jnp.minimum(g, 0.0) - jnp.log(1.0 + jnp.exp(-jnp.abs(g))), g)
        g_t = jnp.transpose(g)
        cum_r = jnp.dot(g_t, upper, preferred_element_type=F32, precision=hi)
        cum_c = jnp.dot(lower, g, preferred_element_type=F32, precision=hi)
        if reverse:
            cum_r = cum_r[:, L - 1:L] - cum_r + g_t
            cum_c = cum_c[L - 1:L, :] - cum_c + g
        gi = 2 * B_HEADS if reverse else 0
        mask = (col >= row) if reverse else (col <= row)
        for h in range(B_HEADS):
            idx = (B_HEADS if reverse else 0) + h
            q = _blk(y, h).astype(BF16)
            k = _blk(y, B_HEADS + h) * (dh ** -0.5)
            v_ext = jnp.concatenate([_blk(v_ref[0], h), ones_col], axis=1)
            i_r = g_t[gi + h:gi + h + 1, :]
            b_r = cum_r[gi + B_HEADS + h:gi + B_HEADS + h + 1, :]
            i_c = g[:, gi + h:gi + h + 1]
            b_c = cum_c[:, gi + B_HEADS + h:gi + B_HEADS + h + 1]
            m_prev = m_sc[idx][:, 0:1]
            c_ext = c_sc[idx]

            log_w = jnp.where(mask, b_c - b_r + i_r, -jnp.inf)
            log_inter = b_c + m_prev
            m_row = jnp.maximum(log_inter, jnp.max(log_w, axis=1, keepdims=True))
            w_intra = jnp.exp(log_w - m_row)
            a_inter = jnp.exp(log_inter - m_row)
            sc = lax.dot_general(q, k.astype(BF16), (((1,), (1,)), ((), ())), preferred_element_type=F32) * w_intra
            q_c = jnp.dot(q, c_ext.astype(BF16), preferred_element_type=F32)
            s_v = jnp.dot(sc.astype(BF16), v_ext, preferred_element_type=F32)
            num = a_inter * q_c[:, :dh] + s_v[:, :dh]
            den = a_inter * q_c[:, dh:dh + 1] + s_v[:, dh:dh + 1]
            h_ref[0, :, h * dh:(h + 1) * dh] = num / jnp.maximum(jnp.abs(den), jnp.exp(-m_row))

            b_last = b_r[:, 0:1] if reverse else b_r[:, L - 1:L]
            m_new = jnp.maximum(b_last + m_prev, jnp.max(b_last - b_r + i_r, axis=1, keepdims=True))
            g_c = jnp.exp(b_last - b_c + i_c - m_new)
            decay = jnp.exp(b_last + m_prev - m_new)
            upd = jnp.dot(jnp.transpose(k).astype(BF16), (g_c * v_ext.astype(F32)).astype(BF16),
                          preferred_element_type=F32)
            c_sc[idx] = decay * c_ext + upd
            m_sc[idx] = jnp.broadcast_to(m_new, (1, LANE))

    direction(qf_ref, pf_ref, nf_ref, vf_ref, gf_ref, hf_ref, fwd_chunk, False)
    direction(qb_ref, pb_ref, nb_ref, vb_ref, gb_ref, hb_ref, bwd_chunk, True)


def _mlstm(qk, vb, gates, conv_w, conv_b, n_lat):
    bsz, t, _ = qk.shape
    L = B_CHUNK
    n_chunks, n_lat_chunks = t // L, n_lat // L
    n_ctx_chunks = n_chunks - n_lat_chunks
    per = L // SUBLANE
    fwd = lambda c: jnp.where(c < n_ctx_chunks, n_lat_chunks + c, c - n_ctx_chunks)
    bwd = lambda c: n_chunks - 1 - c

    def specs(order):
        return [
            pl.BlockSpec((1, L, 2 * HALF), lambda b, c: (b, order(c), 0)),
            pl.BlockSpec((1, SUBLANE, 2 * HALF), lambda b, c: (b, jnp.maximum(order(c) * per - 1, 0), 0)),
            pl.BlockSpec((1, SUBLANE, 2 * HALF), lambda b, c: (b, jnp.minimum((order(c) + 1) * per, t // SUBLANE - 1), 0)),
            pl.BlockSpec((1, L, HALF), lambda b, c: (b, order(c), 0)),
            pl.BlockSpec((1, L, LANE), lambda b, c: (b, order(c), 0)),
        ]

    const = lambda a: pl.BlockSpec(a.shape, lambda b, c: (0, 0))
    return pl.pallas_call(
        functools.partial(_mlstm_kernel, n_lat_chunks=n_lat_chunks, n_chunks=n_chunks),
        grid=(bsz, n_chunks),
        in_specs=specs(fwd) + specs(bwd) + [const(conv_w), const(conv_b)],
        out_specs=[pl.BlockSpec((1, L, HALF), lambda b, c: (b, fwd(c), 0)),
                   pl.BlockSpec((1, L, HALF), lambda b, c: (b, bwd(c), 0))],
        out_shape=[jax.ShapeDtypeStruct((bsz, t, HALF), F32)] * 2,
        scratch_shapes=[pltpu.VMEM((2 * B_HEADS, B_HEAD_DIM, 2 * B_HEAD_DIM), F32),
                        pltpu.VMEM((2 * B_HEADS, 1, LANE), F32)],
        compiler_params=_cparams(("parallel", "arbitrary")),
        name="mlstm",
    )(qk, qk, qk, vb, gates, qk, qk, qk, vb, gates, conv_w, conv_b)


def _route(logits_t, bias):
    per = N_EXPERTS // N_GROUPS
    idx = lax.broadcasted_iota(jnp.int32, logits_t.shape, 0)
    idx_f = idx.astype(F32)
    scores = jax.nn.sigmoid(logits_t)
    sel = scores + bias

    def first_max(a):
        m = jnp.max(a, axis=0, keepdims=True)
        return m, jnp.min(jnp.where(a == m, idx_f, float(N_EXPERTS)), axis=0, keepdims=True)

    loc_f = lax.broadcasted_iota(jnp.int32, (per, logits_t.shape[1]), 0).astype(F32)
    best, g_idx = None, None
    for g in range(N_GROUPS):
        a = sel[g * per:(g + 1) * per]
        m1 = jnp.max(a, axis=0, keepdims=True)
        i1 = jnp.min(jnp.where(a == m1, loc_f, float(per)), axis=0, keepdims=True)
        top2 = m1 + jnp.max(jnp.where(loc_f == i1, -jnp.inf, a), axis=0, keepdims=True)
        if g == 0:
            best, g_idx = top2, jnp.zeros_like(top2)
        else:
            better = top2 > best
            best, g_idx = jnp.where(better, top2, best), jnp.where(better, float(g), g_idx)
    a = jnp.where((idx // per).astype(F32) == g_idx, sel, -jnp.inf)
    _, e1 = first_max(a)
    _, e2 = first_max(jnp.where(idx_f == e1, -jnp.inf, a))
    s1 = jnp.sum(jnp.where(idx_f == e1, scores, 0.0), axis=0, keepdims=True)
    s2 = jnp.sum(jnp.where(idx_f == e2, scores, 0.0), axis=0, keepdims=True)
    return e1, e2, s1 / (s1 + s2), s2 / (s1 + s2)


def _out_tail(mix, x_ref, mod_ref, g_ref, w_ref, wrh_ref, wrl_ref, br_ref, xo_ref, f_ref, rt_ref):
    o = jnp.dot(mix, w_ref[...], preferred_element_type=F32)
    xn = x_ref[0] + mod_ref[0, 0:1, :] * o
    xo_ref[0] = xn
    y = xn * lax.rsqrt(jnp.mean(xn * xn, axis=-1, keepdims=True) + EPS) * g_ref[...]
    f = y * (1.0 + mod_ref[0, 2:3, :]) + mod_ref[0, 1:2, :]
    f_hi = f.astype(BF16)
    f_ref[0] = f_hi
    f_lo = (f - f_hi.astype(F32)).astype(BF16)
    logits = (jnp.dot(f_hi, wrh_ref[...], preferred_element_type=F32)
              + jnp.dot(f_lo, wrh_ref[...], preferred_element_type=F32)
              + jnp.dot(f_hi, wrl_ref[...], preferred_element_type=F32))
    rows = _route(jnp.transpose(logits)[:N_EXPERTS], br_ref[...])
    row_id = lax.broadcasted_iota(jnp.int32, (SUBLANE, logits.shape[0]), 0)
    rt_ref[0] = functools.reduce(lambda rec, kr: jnp.where(row_id == kr[0], kr[1], rec), enumerate(rows),
                                 jnp.zeros((SUBLANE, logits.shape[0]), F32))


def _outproj_even_kernel(oa_ref, hf_ref, hb_ref, ob_ref, gh_ref, *tail):
    parts = [oa_ref[0]]
    for h in range(B_HEADS):
        hs = _blk(hf_ref[0], h) + _blk(hb_ref[0], h)
        hn = hs * lax.rsqrt(jnp.mean(hs * hs, axis=1, keepdims=True) + EPS) * _blk(gh_ref[...], h)
        parts.append((hn * jax.nn.sigmoid(_blk(ob_ref[0], h))).astype(BF16))
    _out_tail(jnp.concatenate(parts, axis=1), *tail)


def _outproj_odd_kernel(oc_ref, od_ref, *tail):
    _out_tail(jnp.concatenate([oc_ref[0], od_ref[0]], axis=1), *tail)


def _outproj(kern, heads, consts, x, mod, g, w, wr_hi, wr_lo, b_router, n_rows, n_lat_tiles):
    bsz, _, d = x.shape
    tm = ROW_TILE
    row = lambda b, i: (b, i, 0)
    return pl.pallas_call(
        kern,
        grid=(bsz, n_rows // tm),
        in_specs=([pl.BlockSpec((1, tm, HALF), row) for _ in heads] + [_const_spec(a) for a in consts]
                  + [pl.BlockSpec((1, tm, d), row),
                     pl.BlockSpec((1, 3, d), lambda b, i: (jnp.where(i >= n_lat_tiles, bsz, b), 0, 0)),
                     pl.BlockSpec((1, d), lambda b, i: (0, 0)), _const_spec(w), _const_spec(wr_hi),
                     _const_spec(wr_lo), _const_spec(b_router)]),
        out_specs=[pl.BlockSpec((1, tm, d), row), pl.BlockSpec((1, tm, d), row),
                   pl.BlockSpec((1, SUBLANE, tm), lambda b, i: (b, 0, i))],
        out_shape=[jax.ShapeDtypeStruct((bsz, n_rows, d), F32), jax.ShapeDtypeStruct((bsz, n_rows, d), BF16),
                   jax.ShapeDtypeStruct((bsz, SUBLANE, n_rows), F32)],
        compiler_params=_cparams(("parallel", "parallel")),
        name="outproj",
    )(*heads, *consts, x, mod, g.reshape(1, d), w, wr_hi, wr_lo, b_router)


def _expert_kernel(blk_e_ref, n_used_ref, x_ref, w1_ref, w3_ref, w2_ref, y_ref, w1_sc, w3_sc, w2_sc):
    i = pl.program_id(0)

    @pl.when((i == 0) | (blk_e_ref[i] != blk_e_ref[jnp.maximum(i - 1, 0)]))
    def _():
        w1_sc[...] = w1_ref[0, 0].astype(BF16)
        w3_sc[...] = w3_ref[0, 0].astype(BF16)
        w2_sc[...] = w2_ref[0, 0].astype(BF16)

    @pl.when(i < n_used_ref[0])
    def _():
        xb = x_ref[...]
        h1 = jnp.dot(xb, w1_sc[...], preferred_element_type=F32)
        h3 = jnp.dot(xb, w3_sc[...], preferred_element_type=F32)
        a = (h1 * jax.nn.sigmoid(h1) * h3).astype(BF16)
        y_ref[...] = jnp.dot(a, w2_sc[...], preferred_element_type=F32).astype(y_ref.dtype)

    @pl.when(i >= n_used_ref[0])
    def _():
        y_ref[...] = jnp.zeros(y_ref.shape, y_ref.dtype)


def _experts(xin, blk_e, n_used, layer, w1, w3, w2):
    n_rows, d = xin.shape
    de = w1.shape[-1]
    nb = n_rows // MOE_ROWS
    grid_spec = pltpu.PrefetchScalarGridSpec(
        num_scalar_prefetch=2,
        grid=(nb,),
        in_specs=[
            pl.BlockSpec((MOE_ROWS, d), lambda i, be, nu: (i, 0)),
            pl.BlockSpec((1, 1, d, de), lambda i, be, nu: (layer, be[i], 0, 0)),
            pl.BlockSpec((1, 1, d, de), lambda i, be, nu: (layer, be[i], 0, 0)),
            pl.BlockSpec((1, 1, de, d), lambda i, be, nu: (layer, be[i], 0, 0)),
        ],
        out_specs=pl.BlockSpec((MOE_ROWS, d), lambda i, be, nu: (i, 0)),
        scratch_shapes=[pltpu.VMEM((d, de), BF16), pltpu.VMEM((d, de), BF16), pltpu.VMEM((de, d), BF16)],
    )
    return pl.pallas_call(
        _expert_kernel,
        grid_spec=grid_spec,
        out_shape=jax.ShapeDtypeStruct((n_rows, d), BF16),
        compiler_params=_cparams(("arbitrary",)),
        name="experts",
    )(blk_e, n_used, xin, w1, w3, w2)


def _moe(f_tok, route, layer, w1, w3, w2):
    n_tok, d = f_tok.shape
    e1, e2 = route[:, 0].reshape(-1).astype(jnp.int32), route[:, 1].reshape(-1).astype(jnp.int32)
    gate = jnp.stack([route[:, 2].reshape(-1), route[:, 3].reshape(-1)], axis=-1)

    n_assign = n_tok * TOP_K
    e_flat = jnp.concatenate([e1, e2]).astype(jnp.int32)
    assert n_assign % MOE_ROWS == 0
    onehot = (e_flat[:, None] == jnp.arange(N_EXPERTS, dtype=jnp.int32)[None, :]).astype(F32)
    tri = jnp.tril(jnp.ones((MOE_ROWS, MOE_ROWS), F32))
    within = jnp.einsum('ij,bjk->bik', tri, onehot.reshape(-1, MOE_ROWS, N_EXPERTS), precision=lax.Precision.HIGHEST)
    totals = within[:, -1, :]
    before = jnp.cumsum(totals, axis=0) - totals
    csum = (within + before[:, None, :]).reshape(n_assign, N_EXPERTS)
    counts = jnp.sum(totals, axis=0).astype(jnp.int32)
    rank = jnp.sum(csum * onehot, axis=1).astype(jnp.int32) - 1
    padded = (counts + MOE_ROWS - 1) // MOE_ROWS * MOE_ROWS
    pend = jnp.cumsum(padded)
    pstart = pend - padded
    dest = pstart[e_flat] + rank
    n_rows = -(-n_assign // MOE_ROWS) * MOE_ROWS + N_EXPERTS * MOE_ROWS
    nb = n_rows // MOE_ROWS
    blk_row = jnp.arange(nb, dtype=jnp.int32) * MOE_ROWS
    blk_e = jnp.minimum(jnp.sum((pend[None, :] <= blk_row[:, None]).astype(jnp.int32), axis=1), N_EXPERTS - 1)
    order = jnp.argsort(e_flat, stable=True).astype(jnp.int32)
    row = jnp.arange(n_rows, dtype=jnp.int32)
    row_e = jnp.repeat(blk_e, MOE_ROWS)
    pos = row - pstart[row_e]
    src = jnp.minimum((jnp.cumsum(counts) - counts)[row_e] + pos, n_assign - 1)
    row_tok = jnp.where(pos < counts[row_e], order[src], row) % n_tok
    xin = f_tok[row_tok]
    n_used = (pend[-1:] // MOE_ROWS).astype(jnp.int32)
    y = _experts(xin, blk_e, n_used, layer, w1, w3, w2)
    return y[dest[:n_tok]], y[dest[n_tok:]], gate


def _combine_kernel(x_ref, y0_ref, y1_ref, gate_ref, g2_ref, o_ref):
    gate = gate_ref[0]
    y = gate[:, 0:1] * y0_ref[0].astype(F32) + gate[:, 1:2] * y1_ref[0].astype(F32)
    o_ref[0] = x_ref[0] + g2_ref[0] * y


def _combine(x, y0, y1, gate, g2, n_lat_tiles):
    bsz, n_rows, d = x.shape
    tm = ROW_TILE
    row = lambda b, i: (b, i, 0)
    return pl.pallas_call(
        _combine_kernel,
        grid=(bsz, n_rows // tm),
        in_specs=[pl.BlockSpec((1, tm, d), row), pl.BlockSpec((1, tm, d), row), pl.BlockSpec((1, tm, d), row),
                  pl.BlockSpec((1, tm, TOP_K), row),
                  pl.BlockSpec((1, 1, d), lambda b, i: (jnp.where(i >= n_lat_tiles, bsz, b), 0, 0))],
        out_specs=pl.BlockSpec((1, tm, d), row),
        out_shape=jax.ShapeDtypeStruct((bsz, n_rows, d), F32),
        compiler_params=_cparams(("parallel", "parallel")),
        name="combine",
    )(x, y0, y1, gate, g2)


def _rope_lanes(n_lat, n_ctx, rot_dim):
    rows = n_lat // GRID_W
    r = jnp.repeat(jnp.arange(rows), GRID_W)
    c = jnp.tile(jnp.arange(GRID_W), rows)
    n_freq = rot_dim // 4
    inv = ROPE_THETA ** (-jnp.arange(n_freq, dtype=F32) / n_freq)
    ang = jnp.concatenate([r[:, None] * inv, c[:, None] * inv], axis=-1)
    cos = jnp.repeat(jnp.cos(ang), 2, axis=1)
    sin = jnp.repeat(jnp.sin(ang), 2, axis=1) * jnp.tile(jnp.array([-1.0, 1.0], F32), rot_dim // 2)
    pad = ((0, n_ctx), (0, 0))
    return jnp.pad(cos, pad, constant_values=1.0), jnp.pad(sin, pad)


def _pad_cols(w, n):
    return jnp.pad(w, ((0, 0), (0, n - w.shape[1])))


def _pad_row(v, n):
    return jnp.pad(v, (0, n - v.shape[0])).reshape(1, n)


def _split_bf16(w):
    hi = lax.bitcast_convert_type(lax.bitcast_convert_type(w, jnp.uint32) & jnp.uint32(0xFFFF0000), F32)
    return hi.astype(BF16), (w - hi).astype(BF16)


def _round_up(n, m):
    return -(-n // m) * m


def _head_blocks(w, n_heads, width, start, size):
    w = w.reshape(w.shape[0], n_heads, width)[:, :, start:start + size]
    return jnp.pad(w, ((0, 0), (0, 0), (0, LANE - size))).reshape(w.shape[0], n_heads * LANE)


def kernel(x, c, ctx, c_ctx, w_ada, b_ada, g_mix, g_ffn, e_w_in, e_w_out, e_g_q, e_g_k, e_conv_w, e_conv_b, e_b_gates, e_g_h, o_w_in, o_w_out, o_g_qc, o_g_kc, o_lam, o_g_sub, o_g_cq, o_w_uq, o_g_ckv, o_w_ukv, o_g_qd, o_g_kd, w_router, b_router, w1, w3, w2):
    bsz, n_lat, d = x.shape
    n_ctx = ctx.shape[1]
    t = n_lat + n_ctx
    depth = w_ada.shape[0]
    assert n_lat % ROW_TILE == 0 and n_ctx % ROW_TILE == 0 and n_ctx % B_CHUNK == 0
    n_lat_tiles = n_lat // ROW_TILE
    hd = D_NOPE + D_ROPE

    cos64, sin64 = _rope_lanes(n_lat, n_ctx, A_HEAD_DIM)
    cos64, sin64 = jnp.tile(cos64, (1, 2)), jnp.tile(sin64, (1, 2))
    cos_d, sin_d = _rope_lanes(n_lat, n_ctx, D_ROPE)
    cos_d = jnp.pad(cos_d, ((0, 0), (D_NOPE, LANE - hd)), constant_values=1.0)
    sin_d = jnp.pad(sin_d, ((0, 0), (D_NOPE, LANE - hd)))
    half = jnp.arange(LANE) // HEAD64
    bd64 = (half[:, None] == half[None, :]).astype(BF16)
    wr_hi, wr_lo = _split_bf16(_pad_cols(w_router.astype(F32), LANE))
    cond = jnp.concatenate([c, c_ctx[None, :]], axis=0)
    x_all = jnp.concatenate([x, ctx], axis=1)

    for layer in range(depth):
        last = layer == depth - 1
        j = layer // 2
        mod = (jax.nn.silu(cond) @ w_ada[layer] + b_ada[layer]).reshape(bsz + 1, 6, d)
        sh1, sc1, g1, sh2, sc2, g2 = [mod[:, i] for i in range(6)]
        mod_in = jnp.stack([sh1, sc1], axis=1)
        mod_out = jnp.stack([g1, sh2, sc2], axis=1)
        n_rows = n_lat if last else t
        if layer % 2 == 0:
            w_in = _pad_cols(e_w_in[j], _round_up(e_w_in.shape[-1], LANE)).astype(BF16)
            q, kd, vd, qk, vb, ob, gates = _inproj_even(
                x_all, mod_in, g_mix[layer], w_in, cos64, sin64, bd64, jnp.tile(e_g_q[j], 2).reshape(1, LANE),
                jnp.tile(e_g_k[j], 2).reshape(1, LANE), _pad_row(e_b_gates[j], LANE), n_lat_tiles)
            oa = _attn(q, kd, vd, mode="gqa", tq=ATTN_ROWS // 4, tk=KEY_TILE, q_rows=(0, n_lat), k_rows=(0, t),
                       out_rows=n_rows)
            if not last:
                oa = _attn(q, kd, vd, mode="gqa", tq=ATTN_ROWS // 4, tk=KEY_TILE, q_rows=(n_lat, n_ctx), k_rows=(n_lat, n_ctx),
                           into=oa)
            hf, hb = _mlstm(qk, vb, gates, e_conv_w[j], e_conv_b[j].reshape(1, -1), n_lat)
            heads, consts, kern = [oa, hf, hb, ob], [e_g_h[j].reshape(1, HALF)], _outproj_even_kernel
            w_out = e_w_out[j]
        else:
            assert last, "the odd mixer is only wired as the final layer"
            lam_init = LAM_INIT_BASE[0] - LAM_INIT_BASE[1] * math.exp(-LAM_INIT_BASE[2] * layer)
            lq1, lk1, lq2, lk2 = o_lam[j].astype(F32)
            lam = jnp.exp(jnp.sum(lq1 * lk1)) - jnp.exp(jnp.sum(lq2 * lk2)) + lam_init
            w_in = _pad_cols(o_w_in[j], _round_up(o_w_in.shape[-1], LANE)).astype(BF16)
            consts = [bd64, jnp.tile(o_g_qc[j], 2).reshape(1, LANE), jnp.tile(o_g_kc[j], 2).reshape(1, LANE),
                      o_g_cq[j].reshape(1, D_Q_LORA), _head_blocks(o_w_uq[j], D_HEADS, hd, 0, hd).astype(BF16),
                      _pad_row(o_g_qd[j], LANE), o_g_ckv[j].reshape(1, D_KV_LORA),
                      _head_blocks(o_w_ukv[j], D_HEADS, D_NOPE + D_V_DIM, 0, D_NOPE).astype(BF16),
                      _head_blocks(o_w_ukv[j], D_HEADS, D_NOPE + D_V_DIM, D_NOPE, D_V_DIM).astype(BF16),
                      _pad_row(o_g_kd[j], LANE)]
            qc, kc, vc, qd, kdl, vdl = _inproj_odd(x_all, mod_in, g_mix[layer], w_in, cos64, sin64, cos_d, sin_d,
                                                   consts, n_lat_tiles)
            extra = (jnp.full((1, LANE), lam, F32), (o_g_sub[j] * (1.0 - lam_init)).reshape(1, LANE))
            oc = _attn(qc, kc, vc, mode="diff", tq=ATTN_ROWS // 2, tk=KEY_TILE, q_rows=(0, n_lat), k_rows=(0, t), extra=extra)
            od = _attn(qd, kdl, vdl, mode="mla", tq=ATTN_ROWS, tk=KEY_TILE, q_rows=(0, n_lat), k_rows=(0, t))
            heads, consts, kern = [oc, od], [], _outproj_odd_kernel
            w_out = o_w_out[j]
        x_new, f, route = _outproj(kern, heads, consts, x_all, mod_out, g_ffn[layer], w_out.astype(BF16), wr_hi, wr_lo,
                                   jnp.broadcast_to(b_router.astype(F32)[:, None], (N_EXPERTS, ROW_TILE)), n_rows, n_lat_tiles)
        y0, y1, gate = _moe(f.reshape(bsz * n_rows, d), route, layer, w1, w3, w2)
        x_all = _combine(x_new, y0.reshape(bsz, n_rows, d), y1.reshape(bsz, n_rows, d),
                         gate.reshape(bsz, n_rows, TOP_K), g2.reshape(bsz + 1, 1, d), n_lat_tiles)
    return x_all
```
